```python
import math
import jax, jax.numpy as jnp
from jax import lax
import numpy as np

D_MODEL = 1024
BATCH = 32
SEQ = 2048
DEPTH = 1

GRID_W = 64
CTX_LEN = 256
EPS = 1e-6
N_HEADS = 8
HEAD_DIM = D_MODEL // 16
QK_W = N_HEADS * 2 * HEAD_DIM
V_W = N_HEADS * 2 * HEAD_DIM
ATTN_SCALE = HEAD_DIM ** -0.5
ROT_FREQS = HEAD_DIM // 4
ROPE_BASE = 10000.0
Q_BLOCK = 128
LAMBDA_STD = 0.1
F_GROUPS = 4
F_GROUP_DIM = D_MODEL // 8
F_W = F_GROUPS * F_GROUP_DIM
Q_OFF = 0
K_OFF = Q_OFF + QK_W
V_OFF = K_OFF + QK_W
F_OFF = V_OFF + V_W
GA_OFF = F_OFF + F_W
GF_OFF = GA_OFF + D_MODEL
PROJ_W = GF_OFF + D_MODEL
N_GROUPS = 4
EXPERTS_PER_GROUP = 4
N_EXPERTS = N_GROUPS * EXPERTS_PER_GROUP
TOP_K_IN_GROUP = 2
EXPERT_HIDDEN = D_MODEL // 2

kernel_name = 'hybrid_diffattn_fnet_hmoe_dit'


def rmsnorm(x, g):
    x32 = x.astype(jnp.float32)
    y = x32 * lax.rsqrt(jnp.mean(x32 * x32, axis=-1, keepdims=True) + EPS)
    return (y * g.astype(jnp.float32)).astype(x.dtype)


def modulate(h, shift, scale):
    return h * (1 + scale) + shift


def axial_angles(rows):
    inv = 1.0 / (ROPE_BASE ** (jnp.arange(ROT_FREQS, dtype=jnp.float32) / ROT_FREQS))
    row = jnp.broadcast_to(jnp.arange(rows, dtype=jnp.float32)[:, None], (rows, GRID_W)).reshape(-1)
    col = jnp.broadcast_to(jnp.arange(GRID_W, dtype=jnp.float32)[None, :], (rows, GRID_W)).reshape(-1)
    return row[:, None] * inv[None, :], col[:, None] * inv[None, :]


def rotate(x, ang):
    cos = jnp.cos(ang)[:, None, None, :].astype(x.dtype)
    sin = jnp.sin(ang)[:, None, None, :].astype(x.dtype)
    x1, x2 = x[..., :ROT_FREQS], x[..., ROT_FREQS:]
    return jnp.concatenate([x1 * cos - x2 * sin, x2 * cos + x1 * sin], axis=-1)


def axial_rope(x, ang_row, ang_col):
    half = 2 * ROT_FREQS
    return jnp.concatenate([rotate(x[..., :half], ang_row), rotate(x[..., half:], ang_col)], axis=-1)


def split_proj(p):
    b, n = p.shape[:2]
    q = p[..., Q_OFF:K_OFF].reshape(b, n, N_HEADS, 2, HEAD_DIM)
    k = p[..., K_OFF:V_OFF].reshape(b, n, N_HEADS, 2, HEAD_DIM)
    v = p[..., V_OFF:F_OFF].reshape(b, n, N_HEADS, 2 * HEAD_DIM)
    f = p[..., F_OFF:GA_OFF]
    ga = p[..., GA_OFF:GF_OFF]
    gf = p[..., GF_OFF:PROJ_W]
    return q, k, v, f, ga, gf


def diff_attend(q, k, v, lam):
    s = jnp.einsum('bqhid,bkhid->bhiqk', q, k, preferred_element_type=jnp.float32) * ATTN_SCALE
    p = jax.nn.softmax(s, axis=-1)
    a = p[:, :, 0] - lam * p[:, :, 1]
    return jnp.einsum('bhqk,bkhe->bqhe', a.astype(v.dtype), v)


def blocked_latent_attention(q, k_all, v_all, lam):
    b, n = q.shape[:2]
    qb = q.reshape(b, n // Q_BLOCK, Q_BLOCK, N_HEADS, 2, HEAD_DIM).transpose(1, 0, 2, 3, 4, 5)
    out = lax.map(lambda qblk: diff_attend(qblk, k_all, v_all, lam), qb)
    return out.transpose(1, 0, 2, 3, 4).reshape(b, n, N_HEADS, 2 * HEAD_DIM)


def merge_branches(heads, f, ga, gf, subln_g, lam_init, w_ao, w_fo, w_o):
    b, n = heads.shape[:2]
    attn = (rmsnorm(heads, subln_g) * (1.0 - lam_init)).reshape(b, n, V_W)
    fg = f.reshape(b, n, F_GROUPS, F_GROUP_DIM).astype(jnp.float32)
    four = jnp.fft.fftn(fg, axes=(1, 3), norm='ortho').real.astype(f.dtype).reshape(b, n, F_W)
    y = jax.nn.sigmoid(ga) * (attn @ w_ao) + jax.nn.sigmoid(gf) * (four @ w_fo)
    return y @ w_o


def hier_moe(h, w_rg, b_rg, w_re, b_re, w1, w3, w2):
    b, n, d = h.shape
    t = h.reshape(-1, d)
    lg = (t @ w_rg + b_rg).astype(jnp.float32)
    g_sel = jnp.argmax(lg, axis=-1)
    w_grp = jnp.take_along_axis(jax.nn.softmax(lg, axis=-1), g_sel[:, None], axis=-1)
    le = (t @ w_re + b_re).astype(jnp.float32).reshape(-1, N_GROUPS, EXPERTS_PER_GROUP)
    le_g = jnp.take_along_axis(le, g_sel[:, None, None], axis=1)[:, 0]
    top_v, top_i = lax.top_k(le_g, TOP_K_IN_GROUP)
    w_top = jax.nn.softmax(top_v, axis=-1) * w_grp
    eid = g_sel[:, None] * EXPERTS_PER_GROUP + top_i
    gates = jnp.einsum('tk,tke->et', w_top, jax.nn.one_hot(eid, N_EXPERTS, dtype=jnp.float32)).astype(t.dtype)

    def expert_step(acc, xs):
        w1e, w3e, w2e, ge = xs
        y = (jax.nn.silu(t @ w1e) * (t @ w3e)) @ w2e
        return acc + ge[:, None] * y, None

    acc, _ = lax.scan(expert_step, jnp.zeros_like(t), (w1, w3, w2, gates))
    return acc.reshape(b, n, d)


def setup_inputs(seed: int = 0) -> dict:
    key = jax.random.key(seed)
    ks = jax.random.split(key, 32)
    f32 = jnp.float32
    d = D_MODEL

    def nrm(k, shape, scale):
        return jax.random.normal(k, shape, f32) * scale

    return {
        'x': nrm(ks[0], (BATCH, SEQ, d), 1.0),
        'c': nrm(ks[1], (BATCH, d), 1.0),
        'ctx': nrm(ks[2], (BATCH, CTX_LEN, d), 1.0),
        'c_ctx': nrm(ks[3], (d,), 1.0),
        'w_mod': nrm(ks[4], (DEPTH, d, 6 * d), 0.5 * d ** -0.5),
        'b_mod': nrm(ks[5], (DEPTH, 6 * d), 0.02),
        'norm1_g': 1.0 + nrm(ks[6], (DEPTH, d), 0.02),
        'norm2_g': 1.0 + nrm(ks[7], (DEPTH, d), 0.02),
        'w_in': nrm(ks[8], (DEPTH, d, PROJ_W), d ** -0.5),
        'lam_q1': nrm(ks[9], (DEPTH, HEAD_DIM), LAMBDA_STD),
        'lam_k1': nrm(ks[10], (DEPTH, HEAD_DIM), LAMBDA_STD),
        'lam_q2': nrm(ks[11], (DEPTH, HEAD_DIM), LAMBDA_STD),
        'lam_k2': nrm(ks[12], (DEPTH, HEAD_DIM), LAMBDA_STD),
        'subln_g': 1.0 + nrm(ks[13], (DEPTH, 2 * HEAD_DIM), 0.02),
        'w_attn_out': nrm(ks[14], (DEPTH, V_W, d), V_W ** -0.5),
        'w_four_out': nrm(ks[15], (DEPTH, F_W, d), F_W ** -0.5),
        'w_out': nrm(ks[16], (DEPTH, d, d), d ** -0.5),
        'w_router_group': nrm(ks[17], (DEPTH, d, N_GROUPS), d ** -0.5),
        'b_router_group': nrm(ks[18], (DEPTH, N_GROUPS), 0.01),
        'w_router_expert': nrm(ks[19], (DEPTH, d, N_EXPERTS), d ** -0.5),
        'b_router_expert': nrm(ks[20], (DEPTH, N_EXPERTS), 0.01),
        'w_exp_gate': nrm(ks[21], (DEPTH, N_EXPERTS, d, EXPERT_HIDDEN), d ** -0.5),
        'w_exp_up': nrm(ks[22], (DEPTH, N_EXPERTS, d, EXPERT_HIDDEN), d ** -0.5),
        'w_exp_down': nrm(ks[23], (DEPTH, N_EXPERTS, EXPERT_HIDDEN, d), EXPERT_HIDDEN ** -0.5),
        'final_g': 1.0 + nrm(ks[24], (d,), 0.02),
    }


def reference(x, c, ctx, c_ctx, w_mod, b_mod, norm1_g, norm2_g, w_in, lam_q1, lam_k1, lam_q2, lam_k2,
              subln_g, w_attn_out, w_four_out, w_out, w_router_group, b_router_group, w_router_expert,
              b_router_expert, w_exp_gate, w_exp_up, w_exp_down, final_g):
    n = x.shape[1]
    rows = n // GRID_W
    ang_row, ang_col = axial_angles(rows)
    x_lat, x_ctx = x, ctx
    for l in range(DEPTH):
        update_ctx = l < DEPTH - 1
        lam_init = 0.8 - 0.6 * math.exp(-0.3 * l)
        lam = (jnp.exp(jnp.sum(lam_q1[l].astype(jnp.float32) * lam_k1[l].astype(jnp.float32)))
               - jnp.exp(jnp.sum(lam_q2[l].astype(jnp.float32) * lam_k2[l].astype(jnp.float32)))
               + lam_init)
        sh1, sc1, g1, sh2, sc2, g2 = [m[:, None, :] for m in jnp.split(jax.nn.silu(c) @ w_mod[l] + b_mod[l], 6, axis=-1)]
        csh1, csc1, cg1, csh2, csc2, cg2 = jnp.split(jax.nn.silu(c_ctx) @ w_mod[l] + b_mod[l], 6, axis=-1)

        h_c = modulate(rmsnorm(x_ctx, norm1_g[l]), csh1, csc1)
        if update_ctx:
            q_c, k_c, v_c, f_c, ga_c, gf_c = split_proj(h_c @ w_in[l])
        else:
            kv_c = h_c @ w_in[l][:, K_OFF:F_OFF]
            k_c = kv_c[..., :QK_W].reshape(x_ctx.shape[0], x_ctx.shape[1], N_HEADS, 2, HEAD_DIM)
            v_c = kv_c[..., QK_W:].reshape(x_ctx.shape[0], x_ctx.shape[1], N_HEADS, 2 * HEAD_DIM)
        h_l = modulate(rmsnorm(x_lat, norm1_g[l]), sh1, sc1)
        q_l, k_l, v_l, f_l, ga_l, gf_l = split_proj(h_l @ w_in[l])
        q_l = axial_rope(q_l, ang_row, ang_col)
        k_l = axial_rope(k_l, ang_row, ang_col)
        k_all = jnp.concatenate([k_c, k_l], axis=1)
        v_all = jnp.concatenate([v_c, v_l], axis=1)
        heads_l = blocked_latent_attention(q_l, k_all, v_all, lam)
        mix_l = merge_branches(heads_l, f_l, ga_l, gf_l, subln_g[l], lam_init, w_attn_out[l], w_four_out[l], w_out[l])
        x_lat = x_lat + g1 * mix_l
        if update_ctx:
            heads_c = diff_attend(q_c, k_c, v_c, lam)
            mix_c = merge_branches(heads_c, f_c, ga_c, gf_c, subln_g[l], lam_init, w_attn_out[l], w_four_out[l], w_out[l])
            x_ctx = x_ctx + cg1 * mix_c

        h2_l = modulate(rmsnorm(x_lat, norm2_g[l]), sh2, sc2)
        x_lat = x_lat + g2 * hier_moe(h2_l, w_router_group[l], b_router_group[l], w_router_expert[l],
                                      b_router_expert[l], w_exp_gate[l], w_exp_up[l], w_exp_down[l])
        if update_ctx:
            h2_c = modulate(rmsnorm(x_ctx, norm2_g[l]), csh2, csc2)
            x_ctx = x_ctx + cg2 * hier_moe(h2_c, w_router_group[l], b_router_group[l], w_router_expert[l],
                                          b_router_expert[l], w_exp_gate[l], w_exp_up[l], w_exp_down[l])
    return rmsnorm(x_lat, final_g)
```

```python
import functools
import math

import jax
import jax.numpy as jnp
import numpy as np
from jax import lax
from jax.experimental import pallas as pl
from jax.experimental.pallas import tpu as pltpu

D_MODEL = 1024
GRID_W = 64
EPS = 1e-6
N_HEADS = 8
HEAD_DIM = 64
HEAD_W = 2 * HEAD_DIM
ROT_FREQS = HEAD_DIM // 4
ROPE_BASE = 10000.0
F_GROUPS = 4
F_GROUP_DIM = 128
F_W = F_GROUPS * F_GROUP_DIM
N_GROUPS = 4
EXPERTS_PER_GROUP = 4
N_EXPERTS = N_GROUPS * EXPERTS_PER_GROUP
EXPERT_HIDDEN = 512
LAM_INIT = 0.8 - 0.6 * math.exp(-0.3 * 0)

REF_Q, REF_K, REF_V, REF_F, REF_GA, REF_GF, REF_END = 0, 1024, 2048, 3072, 3584, 4608, 5632
P_Q, P_K, P_V, P_GA, P_GF, P_F, P_W = 0, 1024, 2048, 3072, 4096, 5120, 5632

LANES = 128
VMEM_LIMIT_BYTES = 56 * 1024 * 1024

PROJ_TM = 512
PROJ_CW = 512
ATTN_TQ = 512
FNET_TR = 512
MERGE_TM = 512
MOE_TM = 512
MOD_ROWS = 40
ROUTER_W = LANES
GROUP_LANE0 = N_EXPERTS

_BF = jnp.bfloat16
_F32 = jnp.float32


def _cparams(*sem):
    return pltpu.CompilerParams(dimension_semantics=sem, vmem_limit_bytes=VMEM_LIMIT_BYTES)


def _const_spec(shape):
    return pl.BlockSpec(shape, lambda *_: (0,) * len(shape), pipeline_mode=pl.Buffered(1))


def _rms(x):
    return x * lax.rsqrt(jnp.mean(x * x, axis=-1, keepdims=True) + EPS)


def _sigmoid(x):
    return 1.0 / (1.0 + jnp.exp(-x))


def _mod_kernel(cc_ref, w_ref, b_ref, o_ref):
    cc = cc_ref[...]
    s = cc * _sigmoid(cc)
    o_ref[...] = jnp.dot(s, w_ref[...], preferred_element_type=_F32,
                         precision=lax.Precision.HIGHEST) + b_ref[...]


def _modulation(cc, w_mod, b_mod):
    n = w_mod.shape[1]
    bn = D_MODEL
    return pl.pallas_call(
        _mod_kernel,
        grid=(n // bn,),
        in_specs=[pl.BlockSpec((MOD_ROWS, D_MODEL), lambda j: (0, 0)),
                  pl.BlockSpec((D_MODEL, bn), lambda j: (0, j)),
                  pl.BlockSpec((1, bn), lambda j: (0, j))],
        out_specs=pl.BlockSpec((MOD_ROWS, bn), lambda j: (0, j)),
        out_shape=jax.ShapeDtypeStruct((MOD_ROWS, n), _F32),
        compiler_params=_cparams("arbitrary"),
        name="modulation",
    )(cc, w_mod, b_mod)


def _rope(acc, cos_ref, sin_ref):
    cos = cos_ref[...]
    sin = sin_ref[...]
    lane = lax.broadcasted_iota(jnp.int32, (1, LANES), 1)
    first_half = (lane % (2 * ROT_FREQS)) < ROT_FREQS
    outs = []
    for s in range(acc.shape[1] // LANES):
        xs = acc[:, s * LANES:(s + 1) * LANES]
        partner = jnp.where(first_half,
                            pltpu.roll(xs, LANES - ROT_FREQS, 1),
                            pltpu.roll(xs, ROT_FREQS, 1))
        outs.append(xs * cos + partner * sin)
    return jnp.concatenate(outs, axis=1)


def _inproj_kernel(x_ref, mod_ref, g_ref, w_ref, cos_ref, sin_ref, o_ref, *, chunk_kinds):
    x = x_ref[0]
    shift = mod_ref[0, 0:1, :]
    scale = mod_ref[0, 1:2, :]
    h = (_rms(x) * g_ref[...]) * (1.0 + scale) + shift
    hb = h.astype(_BF)
    for j, kind in enumerate(chunk_kinds):
        cols = slice(j * PROJ_CW, (j + 1) * PROJ_CW)
        acc = jnp.dot(hb, w_ref[:, cols], preferred_element_type=_F32)
        if kind == "rope":
            acc = _rope(acc, cos_ref, sin_ref)
        elif kind == "sigmoid":
            acc = _sigmoid(acc)
        o_ref[0, :, cols] = acc.astype(_BF)


def _in_projection(x, mod3, mod_row_of_batch, g, w, cos_t, sin_t, chunk_kinds, tm):
    b, n, d = x.shape
    width = w.shape[1]
    assert width == len(chunk_kinds) * PROJ_CW and n % tm == 0
    return pl.pallas_call(
        functools.partial(_inproj_kernel, chunk_kinds=chunk_kinds),
        grid=(b, n // tm),
        in_specs=[pl.BlockSpec((1, tm, d), lambda bi, i: (bi, i, 0)),
                  pl.BlockSpec((1, 6, d), lambda bi, i: (mod_row_of_batch(bi), 0, 0)),
                  pl.BlockSpec((1, d), lambda bi, i: (0, 0)),
                  _const_spec((d, width)),
                  pl.BlockSpec((tm, LANES), lambda bi, i: (i, 0)),
                  pl.BlockSpec((tm, LANES), lambda bi, i: (i, 0))],
        out_specs=pl.BlockSpec((1, tm, width), lambda bi, i: (bi, i, 0)),
        out_shape=jax.ShapeDtypeStruct((b, n, width), _BF),
        compiler_params=_cparams("parallel", "arbitrary"),
        name="in_projection",
    )(x, mod3, g, w, cos_t, sin_t)


def _attn_kernel(lam_ref, q_ref, kl_ref, vl_ref, kc_ref, vc_ref, sg_ref, o_ref):
    q = q_ref[0]
    kl, vl, kc, vc = kl_ref[0], vl_ref[0], kc_ref[0], vc_ref[0]
    lane = lax.broadcasted_iota(jnp.int32, (1, HEAD_W), 1)
    nt = (((1,), (1,)), ((), ()))

    def softmax_map(qm):
        sc = lax.dot_general(qm, kc, nt, preferred_element_type=_F32)
        sl = lax.dot_general(qm, kl, nt, preferred_element_type=_F32)
        m = jnp.maximum(jnp.max(sc, axis=-1, keepdims=True), jnp.max(sl, axis=-1, keepdims=True))
        ec = jnp.exp(sc - m)
        el = jnp.exp(sl - m)
        denom = jnp.sum(ec, axis=-1, keepdims=True) + jnp.sum(el, axis=-1, keepdims=True)
        o = (jnp.dot(ec.astype(_BF), vc, preferred_element_type=_F32)
             + jnp.dot(el.astype(_BF), vl, preferred_element_type=_F32))
        return o / denom

    zero = jnp.zeros_like(q)
    o1 = softmax_map(jnp.where(lane < HEAD_DIM, q, zero))
    o2 = softmax_map(jnp.where(lane >= HEAD_DIM, q, zero))
    heads = o1 - lam_ref[0] * o2
    o_ref[0] = (_rms(heads) * (sg_ref[...] * (1.0 - LAM_INIT))).astype(_BF)


def _attention(lam, p, kvc, subln_g, tq):
    b, n, _ = p.shape
    nc = kvc.shape[1]
    qb, kb, vb = P_Q // HEAD_W, P_K // HEAD_W, P_V // HEAD_W
    return pl.pallas_call(
        _attn_kernel,
        grid=(b, N_HEADS, n // tq),
        in_specs=[pl.BlockSpec(memory_space=pltpu.SMEM),
                  pl.BlockSpec((1, tq, HEAD_W), lambda bi, h, i: (bi, i, qb + h)),
                  pl.BlockSpec((1, n, HEAD_W), lambda bi, h, i: (bi, 0, kb + h)),
                  pl.BlockSpec((1, n, HEAD_W), lambda bi, h, i: (bi, 0, vb + h)),
                  pl.BlockSpec((1, nc, HEAD_W), lambda bi, h, i: (bi, 0, h)),
                  pl.BlockSpec((1, nc, HEAD_W), lambda bi, h, i: (bi, 0, N_HEADS + h)),
                  pl.BlockSpec((1, HEAD_W), lambda bi, h, i: (0, 0))],
        out_specs=pl.BlockSpec((1, tq, HEAD_W), lambda bi, h, i: (bi, i, h)),
        out_shape=jax.ShapeDtypeStruct((b, n, N_HEADS * HEAD_W), _BF),
        compiler_params=_cparams("parallel", "parallel", "arbitrary"),
        name="diff_attention",
    )(lam, p, p, p, kvc, kvc, subln_g)


def _fnet_kernel(f_ref, cs_ch_ref, cs_seq_ref, o_ref, xcs_ref, *, n):
    for g in range(F_GROUPS):
        cols = slice(g * F_GROUP_DIM, (g + 1) * F_GROUP_DIM)
        t = jnp.dot(f_ref[0, :, cols], cs_ch_ref[...], preferred_element_type=_F32)
        xcs_ref[0:n, cols] = t[:, :F_GROUP_DIM].astype(_BF)
        xcs_ref[n:2 * n, cols] = t[:, F_GROUP_DIM:].astype(_BF)
    ortho = 1.0 / math.sqrt(n * F_GROUP_DIM)
    for r in range(n // FNET_TR):
        rows = slice(r * FNET_TR, (r + 1) * FNET_TR)
        y = jnp.dot(cs_seq_ref[rows, :], xcs_ref[...], preferred_element_type=_F32)
        o_ref[0, rows, :] = (y * ortho).astype(_BF)


def _fnet(p, cs_ch, cs_seq):
    b, n, _ = p.shape
    return pl.pallas_call(
        functools.partial(_fnet_kernel, n=n),
        grid=(b,),
        in_specs=[pl.BlockSpec((1, n, F_W), lambda bi: (bi, 0, P_F // F_W)),
                  _const_spec((F_GROUP_DIM, 2 * F_GROUP_DIM)),
                  _const_spec((n, 2 * n))],
        out_specs=pl.BlockSpec((1, n, F_W), lambda bi: (bi, 0, 0)),
        out_shape=jax.ShapeDtypeStruct((b, n, F_W), _BF),
        scratch_shapes=[pltpu.VMEM((2 * n, F_W), _BF)],
        compiler_params=_cparams("arbitrary"),
        name="fnet_dft",
    )(p, cs_ch, cs_seq)


def _route(logits):
    lane = lax.broadcasted_iota(jnp.int32, logits.shape, 1)
    neg = jnp.float32(-jnp.inf)
    big = jnp.int32(ROUTER_W)

    def first_argmax(v):
        m = jnp.max(v, axis=-1, keepdims=True)
        idx = jnp.min(jnp.where(v == m, lane, big), axis=-1, keepdims=True)
        return m, idx

    lg = jnp.where((lane >= GROUP_LANE0) & (lane < GROUP_LANE0 + N_GROUPS), logits, neg)
    mg, ig = first_argmax(lg)
    w_grp = 1.0 / jnp.sum(jnp.exp(lg - mg), axis=-1, keepdims=True)
    g_sel = ig - GROUP_LANE0
    le = jnp.where((lane < N_EXPERTS) & (jnp.right_shift(lane, 2) == g_sel), logits, neg)
    v1, i1 = first_argmax(le)
    le2 = jnp.where(lane == i1, neg, le)
    v2, i2 = first_argmax(le2)
    e2 = jnp.exp(v2 - v1)
    w1 = w_grp / (1.0 + e2)
    w2 = w_grp * e2 / (1.0 + e2)
    return jnp.where(lane == i1, w1, 0.0) + jnp.where(lane == i2, w2, 0.0)


def _merge_kernel(hd_ref, fo_ref, ga_ref, gf_ref, x_ref, mod_ref, g2n_ref, wao_ref, wfo_ref,
                  wo_ref, wrh_ref, wrl_ref, br_ref, x1_ref, h2_ref, gates_ref):
    a = jnp.dot(hd_ref[0], wao_ref[...], preferred_element_type=_F32)
    ff = jnp.dot(fo_ref[0], wfo_ref[...], preferred_element_type=_F32)
    y = ga_ref[0].astype(_F32) * a + gf_ref[0].astype(_F32) * ff
    mix = jnp.dot(y.astype(_BF), wo_ref[...], preferred_element_type=_F32)
    x1 = x_ref[0] + mod_ref[0, 2:3, :] * mix
    x1_ref[0] = x1
    h2 = (_rms(x1) * g2n_ref[...]) * (1.0 + mod_ref[0, 4:5, :]) + mod_ref[0, 3:4, :]
    h2_hi = h2.astype(_BF)
    h2_lo = (h2 - h2_hi.astype(_F32)).astype(_BF)
    h2_ref[0] = h2_hi
    logits = (jnp.dot(h2_hi, wrh_ref[...], preferred_element_type=_F32)
              + jnp.dot(h2_lo, wrh_ref[...], preferred_element_type=_F32)
              + jnp.dot(h2_hi, wrl_ref[...], preferred_element_type=_F32)) + br_ref[...]
    gates_ref[0] = _route(logits)


def _merge(heads, four, p, x, mod3, norm2_g, w_ao, w_fo, w_o, wr_hi, wr_lo, b_r, tm):
    b, n, d = x.shape
    tok = lambda w: pl.BlockSpec((1, tm, w), lambda bi, i: (bi, i, 0))
    return pl.pallas_call(
        _merge_kernel,
        grid=(b, n // tm),
        in_specs=[tok(d), tok(F_W),
                  pl.BlockSpec((1, tm, d), lambda bi, i: (bi, i, P_GA // D_MODEL)),
                  pl.BlockSpec((1, tm, d), lambda bi, i: (bi, i, P_GF // D_MODEL)),
                  tok(d),
                  pl.BlockSpec((1, 6, d), lambda bi, i: (bi, 0, 0)),
                  pl.BlockSpec((1, d), lambda bi, i: (0, 0)),
                  _const_spec((d, d)), _const_spec((F_W, d)), _const_spec((d, d)),
                  _const_spec((d, ROUTER_W)), _const_spec((d, ROUTER_W)),
                  pl.BlockSpec((1, ROUTER_W), lambda bi, i: (0, 0))],
        out_specs=[tok(d), tok(d), tok(ROUTER_W)],
        out_shape=[jax.ShapeDtypeStruct((b, n, d), _F32),
                   jax.ShapeDtypeStruct((b, n, d), _BF),
                   jax.ShapeDtypeStruct((b, n, ROUTER_W), _F32)],
        compiler_params=_cparams("parallel", "arbitrary"),
        name="merge_router",
    )(heads, four, p, p, x, mod3, norm2_g, w_ao, w_fo, w_o, wr_hi, wr_lo, b_r)


def _moe_kernel(t_ref, gates_ref, w1_ref, w3_ref, w2_ref, x1_ref, mod_ref, fg_ref, o_ref,
                acc_ref, hid_ref):
    g = pl.program_id(1)
    t = t_ref[...]
    gates = gates_ref[...]
    lane = lax.broadcasted_iota(jnp.int32, gates.shape, 1)
    for e in range(EXPERTS_PER_GROUP):
        gate = jnp.sum(jnp.where(lane == g * EXPERTS_PER_GROUP + e, gates, 0.0),
                       axis=-1, keepdims=True)
        a = jnp.dot(t, w1_ref[e], preferred_element_type=_F32)
        u = jnp.dot(t, w3_ref[e], preferred_element_type=_F32)
        hid = (a * _sigmoid(a)) * u * gate
        hid_ref[:, e * EXPERT_HIDDEN:(e + 1) * EXPERT_HIDDEN] = hid.astype(_BF)
    w2 = w2_ref[...].reshape(EXPERTS_PER_GROUP * EXPERT_HIDDEN, D_MODEL)
    y = jnp.dot(hid_ref[...], w2, preferred_element_type=_F32)

    @pl.when(g == 0)
    def _():
        acc_ref[...] = y

    @pl.when(g != 0)
    def _():
        acc_ref[...] += y

    @pl.when(g == N_GROUPS - 1)
    def _():
        xo = x1_ref[...] + mod_ref[0, 5:6, :] * acc_ref[...]
        o_ref[...] = _rms(xo) * fg_ref[...]


def _moe(h2, gates, w1, w3, w2, x1, mod3, final_g, seq, tm):
    t, d = h2.shape
    tiles_per_batch = seq // tm
    row = lambda w: pl.BlockSpec((tm, w), lambda i, g: (i, 0))
    grp = lambda a, b_: pl.BlockSpec((EXPERTS_PER_GROUP, a, b_), lambda i, g: (g, 0, 0))
    return pl.pallas_call(
        _moe_kernel,
        grid=(t // tm, N_GROUPS),
        in_specs=[row(d), row(ROUTER_W),
                  grp(d, EXPERT_HIDDEN), grp(d, EXPERT_HIDDEN), grp(EXPERT_HIDDEN, d),
                  row(d),
                  pl.BlockSpec((1, 6, d), lambda i, g: (i // tiles_per_batch, 0, 0)),
                  pl.BlockSpec((1, d), lambda i, g: (0, 0))],
        out_specs=row(d),
        out_shape=jax.ShapeDtypeStruct((t, d), _F32),
        scratch_shapes=[pltpu.VMEM((tm, d), _F32),
                        pltpu.VMEM((tm, EXPERTS_PER_GROUP * EXPERT_HIDDEN), _BF)],
        compiler_params=_cparams("parallel", "arbitrary"),
        name="moe_experts",
    )(h2, gates, w1, w3, w2, x1, mod3, final_g)


def _rope_tables(n):
    inv = (1.0 / (ROPE_BASE ** (np.arange(ROT_FREQS, dtype=np.float32) / ROT_FREQS))).astype(np.float32)
    pos = np.arange(n)
    row = (pos // GRID_W).astype(np.float32)[:, None] * inv[None, :]
    col = (pos % GRID_W).astype(np.float32)[:, None] * inv[None, :]
    cos64 = np.concatenate([np.cos(row), np.cos(row), np.cos(col), np.cos(col)], axis=1)
    sin64 = np.concatenate([-np.sin(row), np.sin(row), -np.sin(col), np.sin(col)], axis=1)
    tile = lambda a: np.tile(a.astype(np.float32), (1, LANES // HEAD_DIM))
    return jnp.asarray(tile(cos64)), jnp.asarray(tile(sin64))


def _dft_cos_sin(n):
    k = np.arange(n, dtype=np.int64)
    ang = (2.0 * np.pi / n) * ((k[:, None] * k[None, :]) % n).astype(np.float64)
    return np.cos(ang), np.sin(ang)


def _dft_tables(n):
    c_ch, s_ch = _dft_cos_sin(F_GROUP_DIM)
    c_seq, s_seq = _dft_cos_sin(n)
    cs_ch = np.concatenate([c_ch, s_ch], axis=1).astype(np.float32)
    cs_seq = np.concatenate([c_seq, -s_seq], axis=1).astype(np.float32)
    return jnp.asarray(cs_ch.astype(_BF)), jnp.asarray(cs_seq.astype(_BF))


def kernel(x, c, ctx, c_ctx, w_mod, b_mod, norm1_g, norm2_g, w_in, lam_q1, lam_k1, lam_q2, lam_k2,
           subln_g, w_attn_out, w_four_out, w_out, w_router_group, b_router_group, w_router_expert,
           b_router_expert, w_exp_gate, w_exp_up, w_exp_down, final_g):
    b, n, d = x.shape
    assert w_mod.shape[0] == 1, "depth-1 stack"
    assert b + 1 <= MOD_ROWS

    cc = jnp.concatenate([c, c_ctx[None, :], jnp.zeros((MOD_ROWS - b - 1, d), _F32)], axis=0)
    mod3 = _modulation(cc, w_mod[0], b_mod).reshape(MOD_ROWS, 6, d)

    lam = (jnp.exp(jnp.sum(lam_q1[0] * lam_k1[0])) - jnp.exp(jnp.sum(lam_q2[0] * lam_k2[0]))
           + LAM_INIT).reshape(1).astype(_F32)

    w = w_in[0]
    scale = HEAD_DIM ** -0.5
    w_lat = jnp.concatenate([w[:, REF_Q:REF_K] * scale, w[:, REF_K:REF_F],
                             w[:, REF_GA:REF_END], w[:, REF_F:REF_GA]], axis=1).astype(_BF)
    w_ctx = w[:, REF_K:REF_F].astype(_BF)
    lat_kinds = ("rope",) * ((P_V - P_Q) // PROJ_CW) + ("plain",) * ((P_GA - P_V) // PROJ_CW) \
        + ("sigmoid",) * ((P_F - P_GA) // PROJ_CW) + ("plain",) * ((P_W - P_F) // PROJ_CW)
    ctx_kinds = ("plain",) * (w_ctx.shape[1] // PROJ_CW)

    cos_t, sin_t = _rope_tables(n)
    p = _in_projection(x, mod3, lambda bi: bi, norm1_g, w_lat, cos_t, sin_t, lat_kinds, PROJ_TM)
    kvc = _in_projection(ctx, mod3, lambda bi: b, norm1_g, w_ctx, cos_t, sin_t, ctx_kinds,
                         ctx.shape[1])

    heads = _attention(lam, p, kvc, subln_g, ATTN_TQ)
    cs_ch, cs_seq = _dft_tables(n)
    four = _fnet(p, cs_ch, cs_seq)

    w_r = jnp.concatenate([w_router_expert[0], w_router_group[0],
                           jnp.zeros((d, ROUTER_W - N_EXPERTS - N_GROUPS), _F32)], axis=1)
    b_r = jnp.concatenate([b_router_expert[0], b_router_group[0],
                           jnp.zeros((ROUTER_W - N_EXPERTS - N_GROUPS,), _F32)])[None, :]
    wr_hi = w_r.astype(_BF)
    wr_lo = (w_r - wr_hi.astype(_F32)).astype(_BF)
    x1, h2, gates = _merge(heads, four, p, x, mod3, norm2_g, w_attn_out[0].astype(_BF),
                           w_four_out[0].astype(_BF), w_out[0].astype(_BF), wr_hi, wr_lo, b_r,
                           MERGE_TM)

    out = _moe(h2.reshape(b * n, d), gates.reshape(b * n, ROUTER_W),
               w_exp_gate[0].astype(_BF), w_exp_up[0].astype(_BF), w_exp_down[0].astype(_BF),
               x1.reshape(b * n, d), mod3, final_g[None, :], n, MOE_TM)
    return out.reshape(b, n, d)
```

```python
import functools
import math

import jax
import jax.numpy as jnp
import numpy as np
from jax import lax
from jax.experimental import pallas as pl
from jax.experimental.pallas import tpu as pltpu

D_MODEL = 1024
GRID_W = 64
EPS = 1e-6
N_HEADS = 8
HEAD_DIM = 64
HEAD_W = 2 * HEAD_DIM
ROT_FREQS = HEAD_DIM // 4
ROPE_BASE = 10000.0
F_GROUPS = 4
F_GROUP_DIM = 128
F_W = F_GROUPS * F_GROUP_DIM
N_GROUPS = 4
EXPERTS_PER_GROUP = 4
N_EXPERTS = N_GROUPS * EXPERTS_PER_GROUP
EXPERT_HIDDEN = 512
LAM_INIT = 0.8 - 0.6 * math.exp(-0.3 * 0)

REF_Q, REF_K, REF_V, REF_F, REF_GA, REF_GF, REF_END = 0, 1024, 2048, 3072, 3584, 4608, 5632
P_Q, P_K, P_V, P_GA, P_GF, P_F, P_W = 0, 1024, 2048, 3072, 4096, 5120, 5632

LANES = 128
VMEM_LIMIT_BYTES = 56 * 1024 * 1024

PROJ_TM = 512
PROJ_CW = 512
ATTN_TQ = 256
FNET_TR = 512
MERGE_TM = 512
MOE_TM = 512
MOD_ROWS = 40
ROUTER_W = LANES
GROUP_LANE0 = N_EXPERTS

_BF = jnp.bfloat16
_F32 = jnp.float32


def _cparams(*sem):
    return pltpu.CompilerParams(dimension_semantics=sem, vmem_limit_bytes=VMEM_LIMIT_BYTES)


def _const_spec(shape):
    return pl.BlockSpec(shape, lambda *_: (0,) * len(shape), pipeline_mode=pl.Buffered(1))


def _rms(x):
    return x * lax.rsqrt(jnp.mean(x * x, axis=-1, keepdims=True) + EPS)


def _sigmoid(x):
    return 1.0 / (1.0 + jnp.exp(-x))


def _mod_kernel(cc_ref, w_ref, b_ref, o_ref):
    cc = cc_ref[...]
    s = cc * _sigmoid(cc)
    o_ref[...] = jnp.dot(s, w_ref[...], preferred_element_type=_F32,
                         precision=lax.Precision.HIGHEST) + b_ref[...]


def _modulation(cc, w_mod, b_mod):
    n = w_mod.shape[1]
    bn = D_MODEL
    return pl.pallas_call(
        _mod_kernel,
        grid=(n // bn,),
        in_specs=[pl.BlockSpec((MOD_ROWS, D_MODEL), lambda j: (0, 0)),
                  pl.BlockSpec((D_MODEL, bn), lambda j: (0, j)),
                  pl.BlockSpec((1, bn), lambda j: (0, j))],
        out_specs=pl.BlockSpec((MOD_ROWS, bn), lambda j: (0, j)),
        out_shape=jax.ShapeDtypeStruct((MOD_ROWS, n), _F32),
        compiler_params=_cparams("arbitrary"),
        name="modulation",
    )(cc, w_mod, b_mod)


def _rope(acc, cos_ref, sin_ref):
    cos = cos_ref[...]
    sin = sin_ref[...]
    lane = lax.broadcasted_iota(jnp.int32, (1, LANES), 1)
    first_half = (lane % (2 * ROT_FREQS)) < ROT_FREQS
    outs = []
    for s in range(acc.shape[1] // LANES):
        xs = acc[:, s * LANES:(s + 1) * LANES]
        partner = jnp.where(first_half,
                            pltpu.roll(xs, LANES - ROT_FREQS, 1),
                            pltpu.roll(xs, ROT_FREQS, 1))
        outs.append(xs * cos + partner * sin)
    return jnp.concatenate(outs, axis=1)


def _inproj_kernel(x_ref, mod_ref, g_ref, w_ref, cos_ref, sin_ref, o_ref, *, chunk_kinds):
    x = x_ref[0]
    shift = mod_ref[0, 0:1, :]
    scale = mod_ref[0, 1:2, :]
    h = (_rms(x) * g_ref[...]) * (1.0 + scale) + shift
    hb = h.astype(_BF)
    for j, kind in enumerate(chunk_kinds):
        cols = slice(j * PROJ_CW, (j + 1) * PROJ_CW)
        acc = jnp.dot(hb, w_ref[:, cols], preferred_element_type=_F32)
        if kind == "rope":
            acc = _rope(acc, cos_ref, sin_ref)
        elif kind == "sigmoid":
            acc = _sigmoid(acc)
        o_ref[0, :, cols] = acc.astype(_BF)


def _in_projection(x, mod3, mod_row_of_batch, g, w, cos_t, sin_t, chunk_kinds, tm):
    b, n, d = x.shape
    width = w.shape[1]
    assert width == len(chunk_kinds) * PROJ_CW and n % tm == 0
    return pl.pallas_call(
        functools.partial(_inproj_kernel, chunk_kinds=chunk_kinds),
        grid=(b, n // tm),
        in_specs=[pl.BlockSpec((1, tm, d), lambda bi, i: (bi, i, 0)),
                  pl.BlockSpec((1, 6, d), lambda bi, i: (mod_row_of_batch(bi), 0, 0)),
                  pl.BlockSpec((1, d), lambda bi, i: (0, 0)),
                  _const_spec((d, width)),
                  pl.BlockSpec((tm, LANES), lambda bi, i: (i, 0)),
                  pl.BlockSpec((tm, LANES), lambda bi, i: (i, 0))],
        out_specs=pl.BlockSpec((1, tm, width), lambda bi, i: (bi, i, 0)),
        out_shape=jax.ShapeDtypeStruct((b, n, width), _BF),
        compiler_params=_cparams("parallel", "arbitrary"),
        name="in_projection",
    )(x, mod3, g, w, cos_t, sin_t)


def _attn_kernel(lam_ref, q_ref, kl_ref, vl_ref, kc_ref, vc_ref, sg_ref, o_ref,
                 kcat_ref, vcat_ref, s_even_ref, s_odd_ref, *, tq):
    nc, n = kc_ref.shape[1], kl_ref.shape[1]
    kcat_ref[0:nc, :] = kc_ref[0]
    kcat_ref[nc:nc + n, :] = kl_ref[0]
    vcat_ref[0:nc, :] = vc_ref[0]
    vcat_ref[nc:nc + n, :] = vl_ref[0]
    lane = lax.broadcasted_iota(jnp.int32, (1, HEAD_W), 1)
    map_lanes = (lane < HEAD_DIM, lane >= HEAD_DIM)
    nt = (((1,), (1,)), ((), ()))
    bufs = (s_even_ref, s_odd_ref)
    lam = lam_ref[0]
    post_scale = sg_ref[...] * (1.0 - LAM_INIT)

    def scores(j):
        q = q_ref[0, j * tq:(j + 1) * tq, :]
        for mp in range(2):
            qm = jnp.where(map_lanes[mp], q, jnp.zeros_like(q))
            bufs[j % 2][mp] = lax.dot_general(qm, kcat_ref[...], nt, preferred_element_type=_F32)

    def finish(j):
        outs = []
        for mp in range(2):
            s = bufs[j % 2][mp]
            e = jnp.exp(s - jnp.max(s, axis=-1, keepdims=True))
            denom = jnp.sum(e, axis=-1, keepdims=True)
            o = jnp.dot(e.astype(_BF), vcat_ref[...], preferred_element_type=_F32)
            outs.append(o / denom)
        heads = outs[0] - lam * outs[1]
        o_ref[0, j * tq:(j + 1) * tq, :] = (_rms(heads) * post_scale).astype(_BF)

    n_sub = n // tq
    scores(0)
    for j in range(n_sub):
        if j + 1 < n_sub:
            scores(j + 1)
        finish(j)


def _attention(lam, p, kvc, subln_g, tq):
    b, n, _ = p.shape
    nc = kvc.shape[1]
    qb, kb, vb = P_Q // HEAD_W, P_K // HEAD_W, P_V // HEAD_W
    seq = lambda blk: pl.BlockSpec((1, n, HEAD_W), lambda bi, h: (bi, 0, blk + h))
    return pl.pallas_call(
        functools.partial(_attn_kernel, tq=tq),
        grid=(b, N_HEADS),
        in_specs=[pl.BlockSpec(memory_space=pltpu.SMEM),
                  seq(qb), seq(kb), seq(vb),
                  pl.BlockSpec((1, nc, HEAD_W), lambda bi, h: (bi, 0, h)),
                  pl.BlockSpec((1, nc, HEAD_W), lambda bi, h: (bi, 0, N_HEADS + h)),
                  pl.BlockSpec((1, HEAD_W), lambda bi, h: (0, 0))],
        out_specs=seq(0),
        out_shape=jax.ShapeDtypeStruct((b, n, N_HEADS * HEAD_W), _BF),
        scratch_shapes=[pltpu.VMEM((nc + n, HEAD_W), _BF), pltpu.VMEM((nc + n, HEAD_W), _BF),
                        pltpu.VMEM((2, tq, nc + n), _F32), pltpu.VMEM((2, tq, nc + n), _F32)],
        compiler_params=_cparams("parallel", "arbitrary"),
        name="diff_attention",
    )(lam, p, p, p, kvc, kvc, subln_g)


def _fnet_kernel(f_ref, cs_ch_ref, cs_seq_ref, o_ref, xcs_ref, *, n):
    for g in range(F_GROUPS):
        cols = slice(g * F_GROUP_DIM, (g + 1) * F_GROUP_DIM)
        t = jnp.dot(f_ref[0, :, cols], cs_ch_ref[...], preferred_element_type=_F32)
        xcs_ref[0:n, cols] = t[:, :F_GROUP_DIM].astype(_BF)
        xcs_ref[n:2 * n, cols] = t[:, F_GROUP_DIM:].astype(_BF)
    ortho = 1.0 / math.sqrt(n * F_GROUP_DIM)
    for r in range(n // FNET_TR):
        rows = slice(r * FNET_TR, (r + 1) * FNET_TR)
        y = jnp.dot(cs_seq_ref[rows, :], xcs_ref[...], preferred_element_type=_F32)
        o_ref[0, rows, :] = (y * ortho).astype(_BF)


def _fnet(p, cs_ch, cs_seq):
    b, n, _ = p.shape
    return pl.pallas_call(
        functools.partial(_fnet_kernel, n=n),
        grid=(b,),
        in_specs=[pl.BlockSpec((1, n, F_W), lambda bi: (bi, 0, P_F // F_W)),
                  _const_spec((F_GROUP_DIM, 2 * F_GROUP_DIM)),
                  _const_spec((n, 2 * n))],
        out_specs=pl.BlockSpec((1, n, F_W), lambda bi: (bi, 0, 0)),
        out_shape=jax.ShapeDtypeStruct((b, n, F_W), _BF),
        scratch_shapes=[pltpu.VMEM((2 * n, F_W), _BF)],
        compiler_params=_cparams("arbitrary"),
        name="fnet_dft",
    )(p, cs_ch, cs_seq)


def _route(logits):
    lane = lax.broadcasted_iota(jnp.int32, logits.shape, 1)
    neg = jnp.float32(-jnp.inf)
    big = jnp.int32(ROUTER_W)

    def first_argmax(v):
        m = jnp.max(v, axis=-1, keepdims=True)
        idx = jnp.min(jnp.where(v == m, lane, big), axis=-1, keepdims=True)
        return m, idx

    lg = jnp.where((lane >= GROUP_LANE0) & (lane < GROUP_LANE0 + N_GROUPS), logits, neg)
    mg, ig = first_argmax(lg)
    w_grp = 1.0 / jnp.sum(jnp.exp(lg - mg), axis=-1, keepdims=True)
    g_sel = ig - GROUP_LANE0
    le = jnp.where((lane < N_EXPERTS) & (jnp.right_shift(lane, 2) == g_sel), logits, neg)
    v1, i1 = first_argmax(le)
    le2 = jnp.where(lane == i1, neg, le)
    v2, i2 = first_argmax(le2)
    e2 = jnp.exp(v2 - v1)
    w1 = w_grp / (1.0 + e2)
    w2 = w_grp * e2 / (1.0 + e2)
    return jnp.where(lane == i1, w1, 0.0) + jnp.where(lane == i2, w2, 0.0)


def _merge_kernel(hd_ref, fo_ref, ga_ref, gf_ref, x_ref, mod_ref, g2n_ref, wao_ref, wfo_ref,
                  wo_ref, wrh_ref, wrl_ref, br_ref, x1_ref, h2_ref, gates_ref):
    a = jnp.dot(hd_ref[0], wao_ref[...], preferred_element_type=_F32)
    ff = jnp.dot(fo_ref[0], wfo_ref[...], preferred_element_type=_F32)
    y = ga_ref[0].astype(_F32) * a + gf_ref[0].astype(_F32) * ff
    mix = jnp.dot(y.astype(_BF), wo_ref[...], preferred_element_type=_F32)
    x1 = x_ref[0] + mod_ref[0, 2:3, :] * mix
    x1_ref[0] = x1
    h2 = (_rms(x1) * g2n_ref[...]) * (1.0 + mod_ref[0, 4:5, :]) + mod_ref[0, 3:4, :]
    h2_hi = h2.astype(_BF)
    h2_lo = (h2 - h2_hi.astype(_F32)).astype(_BF)
    h2_ref[0] = h2_hi
    logits = (jnp.dot(h2_hi, wrh_ref[...], preferred_element_type=_F32)
              + jnp.dot(h2_lo, wrh_ref[...], preferred_element_type=_F32)
              + jnp.dot(h2_hi, wrl_ref[...], preferred_element_type=_F32)) + br_ref[...]
    gates_ref[0] = _route(logits)


def _merge(heads, four, p, x, mod3, norm2_g, w_ao, w_fo, w_o, wr_hi, wr_lo, b_r, tm):
    b, n, d = x.shape
    tok = lambda w: pl.BlockSpec((1, tm, w), lambda bi, i: (bi, i, 0))
    return pl.pallas_call(
        _merge_kernel,
        grid=(b, n // tm),
        in_specs=[tok(d), tok(F_W),
                  pl.BlockSpec((1, tm, d), lambda bi, i: (bi, i, P_GA // D_MODEL)),
                  pl.BlockSpec((1, tm, d), lambda bi, i: (bi, i, P_GF // D_MODEL)),
                  tok(d),
                  pl.BlockSpec((1, 6, d), lambda bi, i: (bi, 0, 0)),
                  pl.BlockSpec((1, d), lambda bi, i: (0, 0)),
                  _const_spec((d, d)), _const_spec((F_W, d)), _const_spec((d, d)),
                  _const_spec((d, ROUTER_W)), _const_spec((d, ROUTER_W)),
                  pl.BlockSpec((1, ROUTER_W), lambda bi, i: (0, 0))],
        out_specs=[tok(d), tok(d), tok(ROUTER_W)],
        out_shape=[jax.ShapeDtypeStruct((b, n, d), _F32),
                   jax.ShapeDtypeStruct((b, n, d), _BF),
                   jax.ShapeDtypeStruct((b, n, ROUTER_W), _F32)],
        compiler_params=_cparams("parallel", "arbitrary"),
        name="merge_router",
    )(heads, four, p, p, x, mod3, norm2_g, w_ao, w_fo, w_o, wr_hi, wr_lo, b_r)


def _moe_kernel(t_ref, gates_ref, w1_ref, w3_ref, w2_ref, x1_ref, mod_ref, fg_ref, o_ref,
                acc_ref, hid_ref):
    g = pl.program_id(1)
    t = t_ref[...]
    gates = gates_ref[...]
    lane = lax.broadcasted_iota(jnp.int32, gates.shape, 1)
    for e in range(EXPERTS_PER_GROUP):
        gate = jnp.sum(jnp.where(lane == g * EXPERTS_PER_GROUP + e, gates, 0.0),
                       axis=-1, keepdims=True)
        a = jnp.dot(t, w1_ref[e], preferred_element_type=_F32)
        u = jnp.dot(t, w3_ref[e], preferred_element_type=_F32)
        hid = (a * _sigmoid(a)) * u * gate
        hid_ref[:, e * EXPERT_HIDDEN:(e + 1) * EXPERT_HIDDEN] = hid.astype(_BF)
    w2 = w2_ref[...].reshape(EXPERTS_PER_GROUP * EXPERT_HIDDEN, D_MODEL)
    y = jnp.dot(hid_ref[...], w2, preferred_element_type=_F32)

    @pl.when(g == 0)
    def _():
        acc_ref[...] = y

    @pl.when(g != 0)
    def _():
        acc_ref[...] += y

    @pl.when(g == N_GROUPS - 1)
    def _():
        xo = x1_ref[...] + mod_ref[0, 5:6, :] * acc_ref[...]
        o_ref[...] = _rms(xo) * fg_ref[...]


def _moe(h2, gates, w1, w3, w2, x1, mod3, final_g, seq, tm):
    t, d = h2.shape
    tiles_per_batch = seq // tm
    row = lambda w: pl.BlockSpec((tm, w), lambda i, g: (i, 0))
    grp = lambda a, b_: pl.BlockSpec((EXPERTS_PER_GROUP, a, b_), lambda i, g: (g, 0, 0))
    return pl.pallas_call(
        _moe_kernel,
        grid=(t // tm, N_GROUPS),
        in_specs=[row(d), row(ROUTER_W),
                  grp(d, EXPERT_HIDDEN), grp(d, EXPERT_HIDDEN), grp(EXPERT_HIDDEN, d),
                  row(d),
                  pl.BlockSpec((1, 6, d), lambda i, g: (i // tiles_per_batch, 0, 0)),
                  pl.BlockSpec((1, d), lambda i, g: (0, 0))],
        out_specs=row(d),
        out_shape=jax.ShapeDtypeStruct((t, d), _F32),
        scratch_shapes=[pltpu.VMEM((tm, d), _F32),
                        pltpu.VMEM((tm, EXPERTS_PER_GROUP * EXPERT_HIDDEN), _BF)],
        compiler_params=_cparams("parallel", "arbitrary"),
        name="moe_experts",
    )(h2, gates, w1, w3, w2, x1, mod3, final_g)


def _rope_tables(n):
    inv = (1.0 / (ROPE_BASE ** (np.arange(ROT_FREQS, dtype=np.float32) / ROT_FREQS))).astype(np.float32)
    pos = np.arange(n)
    row = (pos // GRID_W).astype(np.float32)[:, None] * inv[None, :]
    col = (pos % GRID_W).astype(np.float32)[:, None] * inv[None, :]
    cos64 = np.concatenate([np.cos(row), np.cos(row), np.cos(col), np.cos(col)], axis=1)
    sin64 = np.concatenate([-np.sin(row), np.sin(row), -np.sin(col), np.sin(col)], axis=1)
    tile = lambda a: np.tile(a.astype(np.float32), (1, LANES // HEAD_DIM))
    return jnp.asarray(tile(cos64)), jnp.asarray(tile(sin64))


def _dft_cos_sin(n):
    k = np.arange(n, dtype=np.int64)
    ang = (2.0 * np.pi / n) * ((k[:, None] * k[None, :]) % n).astype(np.float64)
    return np.cos(ang), np.sin(ang)


def _dft_tables(n):
    c_ch, s_ch = _dft_cos_sin(F_GROUP_DIM)
    c_seq, s_seq = _dft_cos_sin(n)
    cs_ch = np.concatenate([c_ch, s_ch], axis=1).astype(np.float32)
    cs_seq = np.concatenate([c_seq, -s_seq], axis=1).astype(np.float32)
    return jnp.asarray(cs_ch.astype(_BF)), jnp.asarray(cs_seq.astype(_BF))


def kernel(x, c, ctx, c_ctx, w_mod, b_mod, norm1_g, norm2_g, w_in, lam_q1, lam_k1, lam_q2, lam_k2,
           subln_g, w_attn_out, w_four_out, w_out, w_router_group, b_router_group, w_router_expert,
           b_router_expert, w_exp_gate, w_exp_up, w_exp_down, final_g):
    b, n, d = x.shape
    assert w_mod.shape[0] == 1, "depth-1 stack"
    assert b + 1 <= MOD_ROWS

    cc = jnp.concatenate([c, c_ctx[None, :], jnp.zeros((MOD_ROWS - b - 1, d), _F32)], axis=0)
    mod3 = _modulation(cc, w_mod[0], b_mod).reshape(MOD_ROWS, 6, d)

    lam = (jnp.exp(jnp.sum(lam_q1[0] * lam_k1[0])) - jnp.exp(jnp.sum(lam_q2[0] * lam_k2[0]))
           + LAM_INIT).reshape(1).astype(_F32)

    w = w_in[0]
    scale = HEAD_DIM ** -0.5
    w_lat = jnp.concatenate([w[:, REF_Q:REF_K] * scale, w[:, REF_K:REF_F],
                             w[:, REF_GA:REF_END], w[:, REF_F:REF_GA]], axis=1).astype(_BF)
    w_ctx = w[:, REF_K:REF_F].astype(_BF)
    lat_kinds = ("rope",) * ((P_V - P_Q) // PROJ_CW) + ("plain",) * ((P_GA - P_V) // PROJ_CW) \
        + ("sigmoid",) * ((P_F - P_GA) // PROJ_CW) + ("plain",) * ((P_W - P_F) // PROJ_CW)
    ctx_kinds = ("plain",) * (w_ctx.shape[1] // PROJ_CW)

    cos_t, sin_t = _rope_tables(n)
    p = _in_projection(x, mod3, lambda bi: bi, norm1_g, w_lat, cos_t, sin_t, lat_kinds, PROJ_TM)
    kvc = _in_projection(ctx, mod3, lambda bi: b, norm1_g, w_ctx, cos_t, sin_t, ctx_kinds,
                         ctx.shape[1])

    heads = _attention(lam, p, kvc, subln_g, ATTN_TQ)
    cs_ch, cs_seq = _dft_tables(n)
    four = _fnet(p, cs_ch, cs_seq)

    w_r = jnp.concatenate([w_router_expert[0], w_router_group[0],
                           jnp.zeros((d, ROUTER_W - N_EXPERTS - N_GROUPS), _F32)], axis=1)
    b_r = jnp.concatenate([b_router_expert[0], b_router_group[0],
                           jnp.zeros((ROUTER_W - N_EXPERTS - N_GROUPS,), _F32)])[None, :]
    wr_hi = w_r.astype(_BF)
    wr_lo = (w_r - wr_hi.astype(_F32)).astype(_BF)
    x1, h2, gates = _merge(heads, four, p, x, mod3, norm2_g, w_attn_out[0].astype(_BF),
                           w_four_out[0].astype(_BF), w_out[0].astype(_BF), wr_hi, wr_lo, b_r,
                           MERGE_TM)

    out = _moe(h2.reshape(b * n, d), gates.reshape(b * n, ROUTER_W),
               w_exp_gate[0].astype(_BF), w_exp_up[0].astype(_BF), w_exp_down[0].astype(_BF),
               x1.reshape(b * n, d), mod3, final_g[None, :], n, MOE_TM)
    return out.reshape(b, n, d)
```

```python
import functools
import math

import jax
import jax.numpy as jnp
import numpy as np
from jax import lax
from jax.experimental import pallas as pl
from jax.experimental.pallas import tpu as pltpu

D_MODEL = 1024
GRID_W = 64
EPS = 1e-6
N_HEADS = 8
HEAD_DIM = 64
HEAD_W = 2 * HEAD_DIM
ROT_FREQS = HEAD_DIM // 4
ROPE_BASE = 10000.0
F_GROUPS = 4
F_GROUP_DIM = 128
F_W = F_GROUPS * F_GROUP_DIM
N_GROUPS = 4
EXPERTS_PER_GROUP = 4
LOG2_EXPERTS_PER_GROUP = 2
N_EXPERTS = N_GROUPS * EXPERTS_PER_GROUP
PAIRS_PER_GROUP = EXPERTS_PER_GROUP * (EXPERTS_PER_GROUP - 1) // 2
N_CLASSES = N_GROUPS * PAIRS_PER_GROUP
EXPERT_HIDDEN = 512
LAM_INIT = 0.8 - 0.6 * math.exp(-0.3 * 0)

REF_Q, REF_K, REF_V, REF_F, REF_GA, REF_GF, REF_END = 0, 1024, 2048, 3072, 3584, 4608, 5632
P_Q, P_K, P_V, P_GA, P_GF, P_F, P_W = 0, 1024, 2048, 3072, 4096, 5120, 5632

LANES = 128
SUBLANES = 8
VMEM_LIMIT_BYTES = 56 * 1024 * 1024

PROJ_TM = 512
PROJ_CW = 512
ATTN_TQ = 256
FNET_TR = 512
MERGE_TM = 512
MOE_TM = 256
DISPATCH_TM = 512
COMBINE_TM = 512
DMA_ISSUE_GROUP = 8
MOD_ROWS = 40
ROUTER_W = LANES
GROUP_LANE0 = N_EXPERTS
H2_SLAB_ROWS = D_MODEL // 2 // LANES
ROUTE_SLAB_ROW = H2_SLAB_ROWS

_BF = jnp.bfloat16
_F32 = jnp.float32


def _cparams(*sem):
    return pltpu.CompilerParams(dimension_semantics=sem, vmem_limit_bytes=VMEM_LIMIT_BYTES)


def _const_spec(shape):
    return pl.BlockSpec(shape, lambda *_: (0,) * len(shape), pipeline_mode=pl.Buffered(1))


def _rms(x):
    return x * lax.rsqrt(jnp.mean(x * x, axis=-1, keepdims=True) + EPS)


def _sigmoid(x):
    return 1.0 / (1.0 + jnp.exp(-x))


def _mod_kernel(cc_ref, w_ref, b_ref, o_ref):
    cc = cc_ref[...]
    s = cc * _sigmoid(cc)
    o_ref[...] = jnp.dot(s, w_ref[...], preferred_element_type=_F32,
                         precision=lax.Precision.HIGHEST) + b_ref[...]


def _modulation(cc, w_mod, b_mod):
    n = w_mod.shape[1]
    bn = D_MODEL
    return pl.pallas_call(
        _mod_kernel,
        grid=(n // bn,),
        in_specs=[pl.BlockSpec((MOD_ROWS, D_MODEL), lambda j: (0, 0)),
                  pl.BlockSpec((D_MODEL, bn), lambda j: (0, j)),
                  pl.BlockSpec((1, bn), lambda j: (0, j))],
        out_specs=pl.BlockSpec((MOD_ROWS, bn), lambda j: (0, j)),
        out_shape=jax.ShapeDtypeStruct((MOD_ROWS, n), _F32),
        compiler_params=_cparams("arbitrary"),
        name="modulation",
    )(cc, w_mod, b_mod)


def _rope(acc, cos_ref, sin_ref):
    cos = cos_ref[...]
    sin = sin_ref[...]
    lane = lax.broadcasted_iota(jnp.int32, (1, LANES), 1)
    first_half = (lane % (2 * ROT_FREQS)) < ROT_FREQS
    outs = []
    for s in range(acc.shape[1] // LANES):
        xs = acc[:, s * LANES:(s + 1) * LANES]
        partner = jnp.where(first_half,
                            pltpu.roll(xs, LANES - ROT_FREQS, 1),
                            pltpu.roll(xs, ROT_FREQS, 1))
        outs.append(xs * cos + partner * sin)
    return jnp.concatenate(outs, axis=1)


def _inproj_kernel(x_ref, mod_ref, g_ref, w_ref, cos_ref, sin_ref, o_ref, *, chunk_kinds):
    x = x_ref[0]
    shift = mod_ref[0, 0:1, :]
    scale = mod_ref[0, 1:2, :]
    h = (_rms(x) * g_ref[...]) * (1.0 + scale) + shift
    hb = h.astype(_BF)
    for j, kind in enumerate(chunk_kinds):
        cols = slice(j * PROJ_CW, (j + 1) * PROJ_CW)
        acc = jnp.dot(hb, w_ref[:, cols], preferred_element_type=_F32)
        if kind == "rope":
            acc = _rope(acc, cos_ref, sin_ref)
        elif kind == "sigmoid":
            acc = _sigmoid(acc)
        o_ref[0, :, cols] = acc.astype(_BF)


def _in_projection(x, mod3, mod_row_of_batch, g, w, cos_t, sin_t, chunk_kinds, tm):
    b, n, d = x.shape
    width = w.shape[1]
    assert width == len(chunk_kinds) * PROJ_CW and n % tm == 0
    return pl.pallas_call(
        functools.partial(_inproj_kernel, chunk_kinds=chunk_kinds),
        grid=(b, n // tm),
        in_specs=[pl.BlockSpec((1, tm, d), lambda bi, i: (bi, i, 0)),
                  pl.BlockSpec((1, 6, d), lambda bi, i: (mod_row_of_batch(bi), 0, 0)),
                  pl.BlockSpec((1, d), lambda bi, i: (0, 0)),
                  _const_spec((d, width)),
                  pl.BlockSpec((tm, LANES), lambda bi, i: (i, 0)),
                  pl.BlockSpec((tm, LANES), lambda bi, i: (i, 0))],
        out_specs=pl.BlockSpec((1, tm, width), lambda bi, i: (bi, i, 0)),
        out_shape=jax.ShapeDtypeStruct((b, n, width), _BF),
        compiler_params=_cparams("parallel", "arbitrary"),
        name="in_projection",
    )(x, mod3, g, w, cos_t, sin_t)


def _attn_kernel(lam_ref, q_ref, kl_ref, vl_ref, kc_ref, vc_ref, sg_ref, o_ref,
                 kcat_ref, vcat_ref, s_even_ref, s_odd_ref, *, tq):
    nc, n = kc_ref.shape[1], kl_ref.shape[1]
    kcat_ref[0:nc, :] = kc_ref[0]
    kcat_ref[nc:nc + n, :] = kl_ref[0]
    vcat_ref[0:nc, :] = vc_ref[0]
    vcat_ref[nc:nc + n, :] = vl_ref[0]
    lane = lax.broadcasted_iota(jnp.int32, (1, HEAD_W), 1)
    map_lanes = (lane < HEAD_DIM, lane >= HEAD_DIM)
    nt = (((1,), (1,)), ((), ()))
    bufs = (s_even_ref, s_odd_ref)
    lam = lam_ref[0]
    post_scale = sg_ref[...] * (1.0 - LAM_INIT)

    def scores(j):
        q = q_ref[0, j * tq:(j + 1) * tq, :]
        for mp in range(2):
            qm = jnp.where(map_lanes[mp], q, jnp.zeros_like(q))
            bufs[j % 2][mp] = lax.dot_general(qm, kcat_ref[...], nt, preferred_element_type=_F32)

    def finish(j):
        outs = []
        for mp in range(2):
            s = bufs[j % 2][mp]
            e = jnp.exp(s - jnp.max(s, axis=-1, keepdims=True))
            denom = jnp.sum(e, axis=-1, keepdims=True)
            o = jnp.dot(e.astype(_BF), vcat_ref[...], preferred_element_type=_F32)
            outs.append(o / denom)
        heads = outs[0] - lam * outs[1]
        o_ref[0, j * tq:(j + 1) * tq, :] = (_rms(heads) * post_scale).astype(_BF)

    n_sub = n // tq
    scores(0)
    for j in range(n_sub):
        if j + 1 < n_sub:
            scores(j + 1)
        finish(j)


def _attention(lam, p, kvc, subln_g, tq):
    b, n, _ = p.shape
    nc = kvc.shape[1]
    qb, kb, vb = P_Q // HEAD_W, P_K // HEAD_W, P_V // HEAD_W
    seq = lambda blk: pl.BlockSpec((1, n, HEAD_W), lambda bi, h: (bi, 0, blk + h))
    return pl.pallas_call(
        functools.partial(_attn_kernel, tq=tq),
        grid=(b, N_HEADS),
        in_specs=[pl.BlockSpec(memory_space=pltpu.SMEM),
                  seq(qb), seq(kb), seq(vb),
                  pl.BlockSpec((1, nc, HEAD_W), lambda bi, h: (bi, 0, h)),
                  pl.BlockSpec((1, nc, HEAD_W), lambda bi, h: (bi, 0, N_HEADS + h)),
                  pl.BlockSpec((1, HEAD_W), lambda bi, h: (0, 0))],
        out_specs=seq(0),
        out_shape=jax.ShapeDtypeStruct((b, n, N_HEADS * HEAD_W), _BF),
        scratch_shapes=[pltpu.VMEM((nc + n, HEAD_W), _BF), pltpu.VMEM((nc + n, HEAD_W), _BF),
                        pltpu.VMEM((2, tq, nc + n), _F32), pltpu.VMEM((2, tq, nc + n), _F32)],
        compiler_params=_cparams("parallel", "arbitrary"),
        name="diff_attention",
    )(lam, p, p, p, kvc, kvc, subln_g)


def _fnet_kernel(f_ref, cs_ch_ref, cs_seq_ref, o_ref, xcs_ref, *, n):
    for g in range(F_GROUPS):
        cols = slice(g * F_GROUP_DIM, (g + 1) * F_GROUP_DIM)
        t = jnp.dot(f_ref[0, :, cols], cs_ch_ref[...], preferred_element_type=_F32)
        xcs_ref[0:n, cols] = t[:, :F_GROUP_DIM].astype(_BF)
        xcs_ref[n:2 * n, cols] = t[:, F_GROUP_DIM:].astype(_BF)
    ortho = 1.0 / math.sqrt(n * F_GROUP_DIM)
    for r in range(n // FNET_TR):
        rows = slice(r * FNET_TR, (r + 1) * FNET_TR)
        y = jnp.dot(cs_seq_ref[rows, :], xcs_ref[...], preferred_element_type=_F32)
        o_ref[0, rows, :] = (y * ortho).astype(_BF)


def _fnet(p, cs_ch, cs_seq):
    b, n, _ = p.shape
    return pl.pallas_call(
        functools.partial(_fnet_kernel, n=n),
        grid=(b,),
        in_specs=[pl.BlockSpec((1, n, F_W), lambda bi: (bi, 0, P_F // F_W)),
                  _const_spec((F_GROUP_DIM, 2 * F_GROUP_DIM)),
                  _const_spec((n, 2 * n))],
        out_specs=pl.BlockSpec((1, n, F_W), lambda bi: (bi, 0, 0)),
        out_shape=jax.ShapeDtypeStruct((b, n, F_W), _BF),
        scratch_shapes=[pltpu.VMEM((2 * n, F_W), _BF)],
        compiler_params=_cparams("arbitrary"),
        name="fnet_dft",
    )(p, cs_ch, cs_seq)


def _pack_bf16_pair(hi, lo):
    hi_bits = pltpu.bitcast(hi.astype(_BF).astype(_F32), jnp.uint32)
    lo_bits = pltpu.bitcast(lo.astype(_BF).astype(_F32), jnp.uint32)
    return hi_bits | lax.shift_right_logical(lo_bits, jnp.uint32(16))


def _unpack_bf16_pair(packed):
    hi = pltpu.bitcast(packed & jnp.uint32(0xFFFF0000), _F32)
    lo = pltpu.bitcast(lax.shift_left(packed, jnp.uint32(16)), _F32)
    return hi, lo


def _route(logits, carry):
    rows = logits.shape[0]
    lane = lax.broadcasted_iota(jnp.int32, logits.shape, 1)
    neg = jnp.float32(-jnp.inf)
    big = jnp.int32(ROUTER_W)

    def first_argmax(v):
        m = jnp.max(v, axis=-1, keepdims=True)
        idx = jnp.min(jnp.where(v == m, lane, big), axis=-1, keepdims=True)
        return m, idx

    lg = jnp.where((lane >= GROUP_LANE0) & (lane < GROUP_LANE0 + N_GROUPS), logits, neg)
    mg, ig = first_argmax(lg)
    w_grp = 1.0 / jnp.sum(jnp.exp(lg - mg), axis=-1, keepdims=True)
    g_sel = ig - GROUP_LANE0
    le = jnp.where((lane < N_EXPERTS)
                   & (jnp.right_shift(lane, LOG2_EXPERTS_PER_GROUP) == g_sel), logits, neg)
    v1, i1 = first_argmax(le)
    le2 = jnp.where(lane == i1, neg, le)
    v2, i2 = first_argmax(le2)
    e2 = jnp.exp(v2 - v1)
    w1 = w_grp / (1.0 + e2)
    w2 = w_grp * e2 / (1.0 + e2)

    first_is_lower = i1 < i2
    gate_a = jnp.where(first_is_lower, w1, w2)
    gate_b = jnp.where(first_is_lower, w2, w1)
    la = jnp.minimum(i1, i2) - g_sel * EXPERTS_PER_GROUP
    lb = jnp.maximum(i1, i2) - g_sel * EXPERTS_PER_GROUP
    pair = jnp.right_shift(la * (2 * EXPERTS_PER_GROUP - 1 - la), 1) + lb - la - 1
    cls = g_sel * PAIRS_PER_GROUP + pair

    onehot = (lane == cls).astype(_F32)
    r_i = lax.broadcasted_iota(jnp.int32, (rows, rows), 0)
    c_i = lax.broadcasted_iota(jnp.int32, (rows, rows), 1)
    earlier = (c_i < r_i).astype(_BF)
    before = jnp.dot(earlier, onehot.astype(_BF), preferred_element_type=_F32) + carry
    rank = jnp.sum(onehot * before, axis=-1, keepdims=True)
    new_carry = carry + jnp.sum(onehot, axis=0, keepdims=True)

    route = (jnp.where(lane == 0, gate_a, 0.0) + jnp.where(lane == 1, gate_b, 0.0)
             + jnp.where(lane == 2, cls.astype(_F32), 0.0) + jnp.where(lane == 3, rank, 0.0))
    return route, new_carry


def _merge_kernel(hd_ref, fo_ref, ga_ref, gf_ref, x_ref, mod_ref, g2n_ref, wao_ref, wfo_ref,
                  wo_ref, wrh_ref, wrl_ref, br_ref, x1_ref, rows_ref, counts_ref, carry_ref):
    @pl.when((pl.program_id(0) == 0) & (pl.program_id(1) == 0))
    def _():
        carry_ref[...] = jnp.zeros_like(carry_ref)

    a = jnp.dot(hd_ref[0], wao_ref[...], preferred_element_type=_F32)
    ff = jnp.dot(fo_ref[0], wfo_ref[...], preferred_element_type=_F32)
    y = ga_ref[0].astype(_F32) * a + gf_ref[0].astype(_F32) * ff
    mix = jnp.dot(y.astype(_BF), wo_ref[...], preferred_element_type=_F32)
    x1 = x_ref[0] + mod_ref[0, 2:3, :] * mix
    x1_ref[0] = x1
    h2 = (_rms(x1) * g2n_ref[...]) * (1.0 + mod_ref[0, 4:5, :]) + mod_ref[0, 3:4, :]
    h2_hi = h2.astype(_BF)
    h2_lo = (h2 - h2_hi.astype(_F32)).astype(_BF)
    logits = (jnp.dot(h2_hi, wrh_ref[...], preferred_element_type=_F32)
              + jnp.dot(h2_lo, wrh_ref[...], preferred_element_type=_F32)
              + jnp.dot(h2_hi, wrl_ref[...], preferred_element_type=_F32)) + br_ref[...]
    route, new_carry = _route(logits, carry_ref[...])
    carry_ref[...] = new_carry
    counts_ref[...] = new_carry
    half = D_MODEL // 2
    tm = h2.shape[0]
    packed = _pack_bf16_pair(h2[:, :half], h2[:, half:])
    rows_ref[0, :, 0:H2_SLAB_ROWS, :] = packed.reshape(tm, H2_SLAB_ROWS, LANES)
    rows_ref[0, :, ROUTE_SLAB_ROW:ROUTE_SLAB_ROW + 1, :] = (
        pltpu.bitcast(route, jnp.uint32).reshape(tm, 1, LANES))
    rows_ref[0, :, ROUTE_SLAB_ROW + 1:SUBLANES, :] = jnp.zeros(
        (tm, SUBLANES - ROUTE_SLAB_ROW - 1, LANES), jnp.uint32)


def _merge(heads, four, p, x, mod3, norm2_g, w_ao, w_fo, w_o, wr_hi, wr_lo, b_r, tm):
    b, n, d = x.shape
    tok = lambda w: pl.BlockSpec((1, tm, w), lambda bi, i: (bi, i, 0))
    return pl.pallas_call(
        _merge_kernel,
        grid=(b, n // tm),
        in_specs=[tok(d), tok(F_W),
                  pl.BlockSpec((1, tm, d), lambda bi, i: (bi, i, P_GA // D_MODEL)),
                  pl.BlockSpec((1, tm, d), lambda bi, i: (bi, i, P_GF // D_MODEL)),
                  tok(d),
                  pl.BlockSpec((1, 6, d), lambda bi, i: (bi, 0, 0)),
                  pl.BlockSpec((1, d), lambda bi, i: (0, 0)),
                  _const_spec((d, d)), _const_spec((F_W, d)), _const_spec((d, d)),
                  _const_spec((d, ROUTER_W)), _const_spec((d, ROUTER_W)),
                  pl.BlockSpec((1, ROUTER_W), lambda bi, i: (0, 0))],
        out_specs=[tok(d),
                   pl.BlockSpec((1, tm, SUBLANES, LANES), lambda bi, i: (bi, i, 0, 0)),
                   pl.BlockSpec((1, ROUTER_W), lambda bi, i: (0, 0))],
        out_shape=[jax.ShapeDtypeStruct((b, n, d), _F32),
                   jax.ShapeDtypeStruct((b, n, SUBLANES, LANES), jnp.uint32),
                   jax.ShapeDtypeStruct((1, ROUTER_W), _F32)],
        scratch_shapes=[pltpu.VMEM((1, ROUTER_W), _F32)],
        compiler_params=_cparams("arbitrary", "arbitrary"),
        name="merge_router",
    )(heads, four, p, p, x, mod3, norm2_g, w_ao, w_fo, w_o, wr_hi, wr_lo, b_r)


def _issue_slab_copies(n, make_copy, slot_of):
    def group(g, c):
        r0 = g * DMA_ISSUE_GROUP
        slots = [slot_of(r0 + k) for k in range(DMA_ISSUE_GROUP)]
        for k in range(DMA_ISSUE_GROUP):
            make_copy(r0 + k, slots[k]).start()
        return c

    lax.fori_loop(0, n // DMA_ISSUE_GROUP, group, 0)


def _dispatch_kernel(pos_ref, src_ref, zeros_ref, dst_ref, sem, *, tm):
    del zeros_ref
    base = pl.program_id(0) * tm
    _issue_slab_copies(
        tm,
        lambda r, slot: pltpu.make_async_copy(src_ref.at[base + r], dst_ref.at[slot], sem),
        lambda r: pos_ref[base + r])
    pltpu.make_async_copy(src_ref.at[pl.ds(0, tm)], dst_ref.at[pl.ds(0, tm)], sem).wait()


def _dispatch(pos, slabs, n_sorted, tm):
    t = slabs.shape[0]
    zeros = jnp.zeros((n_sorted,) + slabs.shape[1:], slabs.dtype)
    return pl.pallas_call(
        functools.partial(_dispatch_kernel, tm=tm),
        grid_spec=pltpu.PrefetchScalarGridSpec(
            num_scalar_prefetch=1,
            grid=(t // tm,),
            in_specs=[pl.BlockSpec(memory_space=pl.ANY), pl.BlockSpec(memory_space=pl.ANY)],
            out_specs=pl.BlockSpec(memory_space=pl.ANY),
            scratch_shapes=[pltpu.SemaphoreType.DMA(())]),
        out_shape=jax.ShapeDtypeStruct(zeros.shape, slabs.dtype),
        input_output_aliases={2: 0},
        compiler_params=_cparams("arbitrary"),
        name="moe_dispatch",
    )(pos, slabs, zeros)


def _moe_kernel(ea_ref, eb_ref, valid_ref, slabs_ref, w1a_ref, w3a_ref, w2a_ref,
                w1b_ref, w3b_ref, w2b_ref, y_ref):
    del ea_ref, eb_ref
    i = pl.program_id(0)
    tm = slabs_ref.shape[0]

    @pl.when(valid_ref[i] != 0)
    def _():
        packed = slabs_ref[:, 0:H2_SLAB_ROWS, :].reshape(tm, H2_SLAB_ROWS * LANES)
        hi, lo = _unpack_bf16_pair(packed)
        t = jnp.concatenate([hi.astype(_BF), lo.astype(_BF)], axis=1)
        route = pltpu.bitcast(slabs_ref[:, ROUTE_SLAB_ROW, :], _F32)
        y = None
        for slot, (w1_ref, w3_ref, w2_ref) in enumerate(((w1a_ref, w3a_ref, w2a_ref),
                                                          (w1b_ref, w3b_ref, w2b_ref))):
            a = jnp.dot(t, w1_ref[0], preferred_element_type=_F32)
            u = jnp.dot(t, w3_ref[0], preferred_element_type=_F32)
            hid = ((a * _sigmoid(a)) * u * route[:, slot:slot + 1]).astype(_BF)
            part = jnp.dot(hid, w2_ref[0], preferred_element_type=_F32)
            y = part if y is None else y + part
        y_ref[...] = y.reshape(tm, SUBLANES, LANES)

    @pl.when(valid_ref[i] == 0)
    def _():
        y_ref[...] = jnp.zeros_like(y_ref)


def _moe(tile_ea, tile_eb, tile_valid, slabs_sorted, w1, w3, w2, tm):
    n_sorted = slabs_sorted.shape[0]
    d, hdn = w1.shape[1], w1.shape[2]
    assert d == SUBLANES * LANES
    pick = lambda which, a, b_: pl.BlockSpec(
        (1, a, b_), lambda i, ea, eb, valid: ((ea, eb)[which][i], 0, 0))
    slab_tile = pl.BlockSpec((tm, SUBLANES, LANES), lambda i, ea, eb, valid: (i, 0, 0))
    return pl.pallas_call(
        _moe_kernel,
        grid_spec=pltpu.PrefetchScalarGridSpec(
            num_scalar_prefetch=3,
            grid=(n_sorted // tm,),
            in_specs=[slab_tile,
                      pick(0, d, hdn), pick(0, d, hdn), pick(0, hdn, d),
                      pick(1, d, hdn), pick(1, d, hdn), pick(1, hdn, d)],
            out_specs=slab_tile),
        out_shape=jax.ShapeDtypeStruct((n_sorted, SUBLANES, LANES), _F32),
        compiler_params=_cparams("arbitrary"),
        name="moe_experts",
    )(tile_ea, tile_eb, tile_valid, slabs_sorted, w1, w3, w2, w1, w3, w2)


def _combine_kernel(pos_ref, y_hbm_ref, x1_ref, mod_ref, fg_ref, o_ref, ybuf_ref, sems, *, tm):
    i = pl.program_id(0)
    n_steps = pl.num_programs(0)

    def issue(tile, slot):
        _issue_slab_copies(
            tm,
            lambda r, src: pltpu.make_async_copy(y_hbm_ref.at[src], ybuf_ref.at[slot, r],
                                                 sems.at[slot]),
            lambda r: pos_ref[tile * tm + r])

    @pl.when(i == 0)
    def _():
        issue(0, 0)

    @pl.when(i + 1 < n_steps)
    def _():
        issue(i + 1, (i + 1) % 2)

    slot = i % 2
    pltpu.make_async_copy(y_hbm_ref.at[pl.ds(0, tm)], ybuf_ref.at[slot], sems.at[slot]).wait()
    y = ybuf_ref[slot].reshape(tm, SUBLANES * LANES)
    xo = x1_ref[...] + mod_ref[0, 5:6, :] * y
    o_ref[...] = _rms(xo) * fg_ref[...]


def _combine(pos, y_sorted, x1, mod3, final_g, seq, tm):
    t, d = x1.shape
    tiles_per_batch = seq // tm
    return pl.pallas_call(
        functools.partial(_combine_kernel, tm=tm),
        grid_spec=pltpu.PrefetchScalarGridSpec(
            num_scalar_prefetch=1,
            grid=(t // tm,),
            in_specs=[pl.BlockSpec(memory_space=pl.ANY),
                      pl.BlockSpec((tm, d), lambda i, pos: (i, 0)),
                      pl.BlockSpec((1, 6, d), lambda i, pos: (i // tiles_per_batch, 0, 0)),
                      pl.BlockSpec((1, d), lambda i, pos: (0, 0))],
            out_specs=pl.BlockSpec((tm, d), lambda i, pos: (i, 0)),
            scratch_shapes=[pltpu.VMEM((2, tm, SUBLANES, LANES), _F32),
                            pltpu.SemaphoreType.DMA((2,))]),
        out_shape=jax.ShapeDtypeStruct((t, d), _F32),
        compiler_params=_cparams("arbitrary"),
        name="moe_combine",
    )(pos, y_sorted, x1, mod3, final_g)


def _routing_tables(route_cls, route_rank, counts, n_tokens, tm):
    sizes = ((counts + (tm - 1)) // tm) * tm
    ends = jnp.cumsum(sizes)
    pos = (ends - sizes)[route_cls] + route_rank
    n_tiles = n_tokens // tm + N_CLASSES
    tile_start = jnp.arange(n_tiles, dtype=jnp.int32) * tm
    tile_cls = jnp.minimum(jnp.searchsorted(ends, tile_start, side="right"), N_CLASSES - 1)
    tile_valid = (tile_start < ends[-1]).astype(jnp.int32)
    pairs = [(a, b_) for a in range(EXPERTS_PER_GROUP) for b_ in range(a + 1, EXPERTS_PER_GROUP)]
    pair_a = jnp.asarray([a for a, _ in pairs], jnp.int32)
    pair_b = jnp.asarray([b_ for _, b_ in pairs], jnp.int32)
    grp, pair = tile_cls // PAIRS_PER_GROUP, tile_cls % PAIRS_PER_GROUP
    tile_ea = (grp * EXPERTS_PER_GROUP + pair_a[pair]).astype(jnp.int32)
    tile_eb = (grp * EXPERTS_PER_GROUP + pair_b[pair]).astype(jnp.int32)
    return pos.astype(jnp.int32), tile_ea, tile_eb, tile_valid, n_tiles


def _rope_tables(n):
    inv = (1.0 / (ROPE_BASE ** (np.arange(ROT_FREQS, dtype=np.float32) / ROT_FREQS))).astype(np.float32)
    pos = np.arange(n)
    row = (pos // GRID_W).astype(np.float32)[:, None] * inv[None, :]
    col = (pos % GRID_W).astype(np.float32)[:, None] * inv[None, :]
    cos64 = np.concatenate([np.cos(row), np.cos(row), np.cos(col), np.cos(col)], axis=1)
    sin64 = np.concatenate([-np.sin(row), np.sin(row), -np.sin(col), np.sin(col)], axis=1)
    tile = lambda a: np.tile(a.astype(np.float32), (1, LANES // HEAD_DIM))
    return jnp.asarray(tile(cos64)), jnp.asarray(tile(sin64))


def _dft_cos_sin(n):
    k = np.arange(n, dtype=np.int64)
    ang = (2.0 * np.pi / n) * ((k[:, None] * k[None, :]) % n).astype(np.float64)
    return np.cos(ang), np.sin(ang)


def _dft_tables(n):
    c_ch, s_ch = _dft_cos_sin(F_GROUP_DIM)
    c_seq, s_seq = _dft_cos_sin(n)
    cs_ch = np.concatenate([c_ch, s_ch], axis=1).astype(np.float32)
    cs_seq = np.concatenate([c_seq, -s_seq], axis=1).astype(np.float32)
    return jnp.asarray(cs_ch.astype(_BF)), jnp.asarray(cs_seq.astype(_BF))


def kernel(x, c, ctx, c_ctx, w_mod, b_mod, norm1_g, norm2_g, w_in, lam_q1, lam_k1, lam_q2, lam_k2,
           subln_g, w_attn_out, w_four_out, w_out, w_router_group, b_router_group, w_router_expert,
           b_router_expert, w_exp_gate, w_exp_up, w_exp_down, final_g):
    b, n, d = x.shape
    assert w_mod.shape[0] == 1, "depth-1 stack"
    assert b + 1 <= MOD_ROWS

    cc = jnp.concatenate([c, c_ctx[None, :], jnp.zeros((MOD_ROWS - b - 1, d), _F32)], axis=0)
    mod3 = _modulation(cc, w_mod[0], b_mod).reshape(MOD_ROWS, 6, d)

    lam = (jnp.exp(jnp.sum(lam_q1[0] * lam_k1[0])) - jnp.exp(jnp.sum(lam_q2[0] * lam_k2[0]))
           + LAM_INIT).reshape(1).astype(_F32)

    w = w_in[0]
    scale = HEAD_DIM ** -0.5
    w_lat = jnp.concatenate([w[:, REF_Q:REF_K] * scale, w[:, REF_K:REF_F],
                             w[:, REF_GA:REF_END], w[:, REF_F:REF_GA]], axis=1).astype(_BF)
    w_ctx = w[:, REF_K:REF_F].astype(_BF)
    lat_kinds = ("rope",) * ((P_V - P_Q) // PROJ_CW) + ("plain",) * ((P_GA - P_V) // PROJ_CW) \
        + ("sigmoid",) * ((P_F - P_GA) // PROJ_CW) + ("plain",) * ((P_W - P_F) // PROJ_CW)
    ctx_kinds = ("plain",) * (w_ctx.shape[1] // PROJ_CW)

    cos_t, sin_t = _rope_tables(n)
    p = _in_projection(x, mod3, lambda bi: bi, norm1_g, w_lat, cos_t, sin_t, lat_kinds, PROJ_TM)
    kvc = _in_projection(ctx, mod3, lambda bi: b, norm1_g, w_ctx, cos_t, sin_t, ctx_kinds,
                         ctx.shape[1])

    heads = _attention(lam, p, kvc, subln_g, ATTN_TQ)
    cs_ch, cs_seq = _dft_tables(n)
    four = _fnet(p, cs_ch, cs_seq)

    w_r = jnp.concatenate([w_router_expert[0], w_router_group[0],
                           jnp.zeros((d, ROUTER_W - N_EXPERTS - N_GROUPS), _F32)], axis=1)
    b_r = jnp.concatenate([b_router_expert[0], b_router_group[0],
                           jnp.zeros((ROUTER_W - N_EXPERTS - N_GROUPS,), _F32)])[None, :]
    wr_hi = w_r.astype(_BF)
    wr_lo = (w_r - wr_hi.astype(_F32)).astype(_BF)
    x1, rows, counts = _merge(heads, four, p, x, mod3, norm2_g, w_attn_out[0].astype(_BF),
                              w_four_out[0].astype(_BF), w_out[0].astype(_BF), wr_hi, wr_lo, b_r,
                              MERGE_TM)

    t = b * n
    slabs = rows.reshape(t, SUBLANES, LANES)
    record = lax.bitcast_convert_type(slabs[:, ROUTE_SLAB_ROW, 2:4], _F32).astype(jnp.int32)
    pos, tile_ea, tile_eb, tile_valid, n_tiles = _routing_tables(
        record[:, 0], record[:, 1], counts[0, :N_CLASSES].astype(jnp.int32), t, MOE_TM)

    slabs_sorted = _dispatch(pos, slabs, n_tiles * MOE_TM, DISPATCH_TM)
    y_sorted = _moe(tile_ea, tile_eb, tile_valid, slabs_sorted, w_exp_gate[0].astype(_BF),
                    w_exp_up[0].astype(_BF), w_exp_down[0].astype(_BF), MOE_TM)
    out = _combine(pos, y_sorted, x1.reshape(t, d), mod3, final_g[None, :], n, COMBINE_TM)
    return out.reshape(b, n, d)
```

```python
import functools
import math

import jax
import jax.numpy as jnp
import numpy as np
from jax import lax
from jax.experimental import pallas as pl
from jax.experimental.pallas import tpu as pltpu

D_MODEL = 1024
GRID_W = 64
EPS = 1e-6
N_HEADS = 8
HEAD_DIM = 64
HEAD_W = 2 * HEAD_DIM
ROT_FREQS = HEAD_DIM // 4
ROPE_BASE = 10000.0
F_GROUPS = 4
F_GROUP_DIM = 128
F_W = F_GROUPS * F_GROUP_DIM
N_GROUPS = 4
EXPERTS_PER_GROUP = 4
LOG2_EXPERTS_PER_GROUP = 2
N_EXPERTS = N_GROUPS * EXPERTS_PER_GROUP
PAIRS_PER_GROUP = EXPERTS_PER_GROUP * (EXPERTS_PER_GROUP - 1) // 2
N_CLASSES = N_GROUPS * PAIRS_PER_GROUP
EXPERT_HIDDEN = 512
LAM_INIT = 0.8 - 0.6 * math.exp(-0.3 * 0)

REF_Q, REF_K, REF_V, REF_F, REF_GA, REF_GF, REF_END = 0, 1024, 2048, 3072, 3584, 4608, 5632
P_Q, P_K, P_V, P_GA, P_GF, P_F, P_W = 0, 1024, 2048, 3072, 4096, 5120, 5632

LANES = 128
SUBLANES = 8
VMEM_LIMIT_BYTES = 56 * 1024 * 1024

PROJ_TM = 512
PROJ_CW = 512
ATTN_TQ = 256
FNET_TR = 512
MERGE_TM = 512
MOE_TM = 256
DISPATCH_TM = 512
COMBINE_TM = 512
DMA_ISSUE_GROUP = 8
MOD_ROWS = 40
ROUTER_W = LANES
GROUP_LANE0 = N_EXPERTS
H2_SLAB_ROWS = D_MODEL // 2 // LANES
ROUTE_SLAB_ROW = H2_SLAB_ROWS

_BF = jnp.bfloat16
_F32 = jnp.float32


def _cparams(*sem):
    return pltpu.CompilerParams(dimension_semantics=sem, vmem_limit_bytes=VMEM_LIMIT_BYTES)


def _const_spec(shape):
    return pl.BlockSpec(shape, lambda *_: (0,) * len(shape), pipeline_mode=pl.Buffered(1))


def _rms(x):
    return x * lax.rsqrt(jnp.mean(x * x, axis=-1, keepdims=True) + EPS)


def _sigmoid(x):
    return 1.0 / (1.0 + jnp.exp(-x))


def _mod_kernel(cc_ref, w_ref, b_ref, o_ref):
    cc = cc_ref[...]
    s = cc * _sigmoid(cc)
    o_ref[...] = jnp.dot(s, w_ref[...], preferred_element_type=_F32,
                         precision=lax.Precision.HIGHEST) + b_ref[...]


def _modulation(cc, w_mod, b_mod):
    n = w_mod.shape[1]
    bn = D_MODEL
    return pl.pallas_call(
        _mod_kernel,
        grid=(n // bn,),
        in_specs=[pl.BlockSpec((MOD_ROWS, D_MODEL), lambda j: (0, 0)),
                  pl.BlockSpec((D_MODEL, bn), lambda j: (0, j)),
                  pl.BlockSpec((1, bn), lambda j: (0, j))],
        out_specs=pl.BlockSpec((MOD_ROWS, bn), lambda j: (0, j)),
        out_shape=jax.ShapeDtypeStruct((MOD_ROWS, n), _F32),
        compiler_params=_cparams("arbitrary"),
        name="modulation",
    )(cc, w_mod, b_mod)


def _rope(acc, cos_ref, sin_ref):
    cos = cos_ref[...]
    sin = sin_ref[...]
    lane = lax.broadcasted_iota(jnp.int32, (1, LANES), 1)
    first_half = (lane % (2 * ROT_FREQS)) < ROT_FREQS
    outs = []
    for s in range(acc.shape[1] // LANES):
        xs = acc[:, s * LANES:(s + 1) * LANES]
        partner = jnp.where(first_half,
                            pltpu.roll(xs, LANES - ROT_FREQS, 1),
                            pltpu.roll(xs, ROT_FREQS, 1))
        outs.append(xs * cos + partner * sin)
    return jnp.concatenate(outs, axis=1)


def _inproj_kernel(x_ref, mod_ref, g_ref, w_ref, cos_ref, sin_ref, o_ref, *, chunk_kinds):
    x = x_ref[0]
    shift = mod_ref[0, 0:1, :]
    scale = mod_ref[0, 1:2, :]
    h = (_rms(x) * g_ref[...]) * (1.0 + scale) + shift
    hb = h.astype(_BF)
    for j, kind in enumerate(chunk_kinds):
        cols = slice(j * PROJ_CW, (j + 1) * PROJ_CW)
        acc = jnp.dot(hb, w_ref[:, cols], preferred_element_type=_F32)
        if kind == "rope":
            acc = _rope(acc, cos_ref, sin_ref)
        elif kind == "sigmoid":
            acc = _sigmoid(acc)
        o_ref[0, :, cols] = acc.astype(_BF)


def _in_projection(x, mod3, mod_row_of_batch, g, w, cos_t, sin_t, chunk_kinds, tm):
    b, n, d = x.shape
    width = w.shape[1]
    assert width == len(chunk_kinds) * PROJ_CW and n % tm == 0
    return pl.pallas_call(
        functools.partial(_inproj_kernel, chunk_kinds=chunk_kinds),
        grid=(b, n // tm),
        in_specs=[pl.BlockSpec((1, tm, d), lambda bi, i: (bi, i, 0)),
                  pl.BlockSpec((1, 6, d), lambda bi, i: (mod_row_of_batch(bi), 0, 0)),
                  pl.BlockSpec((1, d), lambda bi, i: (0, 0)),
                  _const_spec((d, width)),
                  pl.BlockSpec((tm, LANES), lambda bi, i: (i, 0)),
                  pl.BlockSpec((tm, LANES), lambda bi, i: (i, 0))],
        out_specs=pl.BlockSpec((1, tm, width), lambda bi, i: (bi, i, 0)),
        out_shape=jax.ShapeDtypeStruct((b, n, width), _BF),
        compiler_params=_cparams("parallel", "arbitrary"),
        name="in_projection",
    )(x, mod3, g, w, cos_t, sin_t)


def _attn_kernel(lam_ref, q_ref, kl_ref, vl_ref, kc_ref, vc_ref, sg_ref, o_ref,
                 kcat_ref, vcat_ref, s_even_ref, s_odd_ref, *, tq):
    nc, n = kc_ref.shape[1], kl_ref.shape[1]
    kcat_ref[0:nc, :] = kc_ref[0]
    kcat_ref[nc:nc + n, :] = kl_ref[0]
    vcat_ref[0:nc, :] = vc_ref[0]
    vcat_ref[nc:nc + n, :] = vl_ref[0]
    lane = lax.broadcasted_iota(jnp.int32, (1, HEAD_W), 1)
    map_lanes = (lane < HEAD_DIM, lane >= HEAD_DIM)
    nt = (((1,), (1,)), ((), ()))
    bufs = (s_even_ref, s_odd_ref)
    lam = lam_ref[0]
    post_scale = sg_ref[...] * (1.0 - LAM_INIT)

    def scores(j):
        q = q_ref[0, j * tq:(j + 1) * tq, :]
        for mp in range(2):
            qm = jnp.where(map_lanes[mp], q, jnp.zeros_like(q))
            bufs[j % 2][mp] = lax.dot_general(qm, kcat_ref[...], nt, preferred_element_type=_F32)

    def finish(j):
        outs = []
        for mp in range(2):
            s = bufs[j % 2][mp]
            e = jnp.exp(s - jnp.max(s, axis=-1, keepdims=True))
            denom = jnp.sum(e, axis=-1, keepdims=True)
            o = jnp.dot(e.astype(_BF), vcat_ref[...], preferred_element_type=_F32)
            outs.append(o / denom)
        heads = outs[0] - lam * outs[1]
        o_ref[0, j * tq:(j + 1) * tq, :] = (_rms(heads) * post_scale).astype(_BF)

    n_sub = n // tq
    scores(0)
    for j in range(n_sub):
        if j + 1 < n_sub:
            scores(j + 1)
        finish(j)


def _attention(lam, p, kvc, subln_g, tq):
    b, n, _ = p.shape
    nc = kvc.shape[1]
    qb, kb, vb = P_Q // HEAD_W, P_K // HEAD_W, P_V // HEAD_W
    seq = lambda blk: pl.BlockSpec((1, n, HEAD_W), lambda bi, h: (bi, 0, blk + h))
    return pl.pallas_call(
        functools.partial(_attn_kernel, tq=tq),
        grid=(b, N_HEADS),
        in_specs=[pl.BlockSpec(memory_space=pltpu.SMEM),
                  seq(qb), seq(kb), seq(vb),
                  pl.BlockSpec((1, nc, HEAD_W), lambda bi, h: (bi, 0, h)),
                  pl.BlockSpec((1, nc, HEAD_W), lambda bi, h: (bi, 0, N_HEADS + h)),
                  pl.BlockSpec((1, HEAD_W), lambda bi, h: (0, 0))],
        out_specs=seq(0),
        out_shape=jax.ShapeDtypeStruct((b, n, N_HEADS * HEAD_W), _BF),
        scratch_shapes=[pltpu.VMEM((nc + n, HEAD_W), _BF), pltpu.VMEM((nc + n, HEAD_W), _BF),
                        pltpu.VMEM((2, tq, nc + n), _F32), pltpu.VMEM((2, tq, nc + n), _F32)],
        compiler_params=_cparams("parallel", "arbitrary"),
        name="diff_attention",
    )(lam, p, p, p, kvc, kvc, subln_g)


def _fnet_kernel(f_ref, cs_ch_ref, cs_seq_ref, o_ref, xcs_ref, *, n):
    for g in range(F_GROUPS):
        cols = slice(g * F_GROUP_DIM, (g + 1) * F_GROUP_DIM)
        t = jnp.dot(f_ref[0, :, cols], cs_ch_ref[...], preferred_element_type=_F32)
        xcs_ref[0:n, cols] = t[:, :F_GROUP_DIM].astype(_BF)
        xcs_ref[n:2 * n, cols] = t[:, F_GROUP_DIM:].astype(_BF)
    ortho = 1.0 / math.sqrt(n * F_GROUP_DIM)
    for r in range(n // FNET_TR):
        rows = slice(r * FNET_TR, (r + 1) * FNET_TR)
        y = jnp.dot(cs_seq_ref[rows, :], xcs_ref[...], preferred_element_type=_F32)
        o_ref[0, rows, :] = (y * ortho).astype(_BF)


def _fnet(p, cs_ch, cs_seq):
    b, n, _ = p.shape
    return pl.pallas_call(
        functools.partial(_fnet_kernel, n=n),
        grid=(b,),
        in_specs=[pl.BlockSpec((1, n, F_W), lambda bi: (bi, 0, P_F // F_W)),
                  _const_spec((F_GROUP_DIM, 2 * F_GROUP_DIM)),
                  _const_spec((n, 2 * n))],
        out_specs=pl.BlockSpec((1, n, F_W), lambda bi: (bi, 0, 0)),
        out_shape=jax.ShapeDtypeStruct((b, n, F_W), _BF),
        scratch_shapes=[pltpu.VMEM((2 * n, F_W), _BF)],
        compiler_params=_cparams("arbitrary"),
        name="fnet_dft",
    )(p, cs_ch, cs_seq)


def _pack_bf16_pair(hi, lo):
    hi_bits = pltpu.bitcast(hi.astype(_BF).astype(_F32), jnp.uint32)
    lo_bits = pltpu.bitcast(lo.astype(_BF).astype(_F32), jnp.uint32)
    return hi_bits | lax.shift_right_logical(lo_bits, jnp.uint32(16))


def _unpack_bf16_pair(packed):
    hi = pltpu.bitcast(packed & jnp.uint32(0xFFFF0000), _F32)
    lo = pltpu.bitcast(lax.shift_left(packed, jnp.uint32(16)), _F32)
    return hi, lo


def _route(logits, carry):
    rows = logits.shape[0]
    lane = lax.broadcasted_iota(jnp.int32, logits.shape, 1)
    neg = jnp.float32(-jnp.inf)
    big = jnp.int32(ROUTER_W)

    def first_argmax(v):
        m = jnp.max(v, axis=-1, keepdims=True)
        idx = jnp.min(jnp.where(v == m, lane, big), axis=-1, keepdims=True)
        return m, idx

    lg = jnp.where((lane >= GROUP_LANE0) & (lane < GROUP_LANE0 + N_GROUPS), logits, neg)
    mg, ig = first_argmax(lg)
    w_grp = 1.0 / jnp.sum(jnp.exp(lg - mg), axis=-1, keepdims=True)
    g_sel = ig - GROUP_LANE0
    le = jnp.where((lane < N_EXPERTS)
                   & (jnp.right_shift(lane, LOG2_EXPERTS_PER_GROUP) == g_sel), logits, neg)
    v1, i1 = first_argmax(le)
    le2 = jnp.where(lane == i1, neg, le)
    v2, i2 = first_argmax(le2)
    e2 = jnp.exp(v2 - v1)
    w1 = w_grp / (1.0 + e2)
    w2 = w_grp * e2 / (1.0 + e2)

    first_is_lower = i1 < i2
    gate_a = jnp.where(first_is_lower, w1, w2)
    gate_b = jnp.where(first_is_lower, w2, w1)
    la = jnp.minimum(i1, i2) - g_sel * EXPERTS_PER_GROUP
    lb = jnp.maximum(i1, i2) - g_sel * EXPERTS_PER_GROUP
    pair = jnp.right_shift(la * (2 * EXPERTS_PER_GROUP - 1 - la), 1) + lb - la - 1
    cls = g_sel * PAIRS_PER_GROUP + pair

    onehot = (lane == cls).astype(_F32)
    r_i = lax.broadcasted_iota(jnp.int32, (rows, rows), 0)
    c_i = lax.broadcasted_iota(jnp.int32, (rows, rows), 1)
    earlier = (c_i < r_i).astype(_BF)
    before = jnp.dot(earlier, onehot.astype(_BF), preferred_element_type=_F32) + carry
    rank = jnp.sum(onehot * before, axis=-1, keepdims=True)
    new_carry = carry + jnp.sum(onehot, axis=0, keepdims=True)

    route = (jnp.where(lane == 0, gate_a, 0.0) + jnp.where(lane == 1, gate_b, 0.0)
             + jnp.where(lane == 2, cls.astype(_F32), 0.0) + jnp.where(lane == 3, rank, 0.0))
    return route, new_carry


def _merge_kernel(hd_ref, fo_ref, ga_ref, gf_ref, x_ref, mod_ref, g2n_ref, wao_ref, wfo_ref,
                  wo_ref, wrhl_ref, br_ref, x1_ref, rows_ref, rec_ref, counts_ref, carry_ref):
    @pl.when((pl.program_id(0) == 0) & (pl.program_id(1) == 0))
    def _():
        carry_ref[...] = jnp.zeros_like(carry_ref)

    a = jnp.dot(hd_ref[0], wao_ref[...], preferred_element_type=_F32)
    ff = jnp.dot(fo_ref[0], wfo_ref[...], preferred_element_type=_F32)
    y = ga_ref[0].astype(_F32) * a + gf_ref[0].astype(_F32) * ff
    mix = jnp.dot(y.astype(_BF), wo_ref[...], preferred_element_type=_F32)
    x1 = x_ref[0] + mod_ref[0, 2:3, :] * mix
    x1_ref[0] = x1
    h2 = (_rms(x1) * g2n_ref[...]) * (1.0 + mod_ref[0, 4:5, :]) + mod_ref[0, 3:4, :]
    h2_hi = h2.astype(_BF)
    h2_lo = (h2 - h2_hi.astype(_F32)).astype(_BF)
    hi_both = jnp.dot(h2_hi, wrhl_ref[...], preferred_element_type=_F32)
    logits = (hi_both[:, :ROUTER_W] + hi_both[:, ROUTER_W:]
              + jnp.dot(h2_lo, wrhl_ref[:, :ROUTER_W], preferred_element_type=_F32)) + br_ref[...]
    route, new_carry = _route(logits, carry_ref[...])
    carry_ref[...] = new_carry
    counts_ref[...] = new_carry
    rec_ref[0, 0] = route.T[0:SUBLANES, :]
    half = D_MODEL // 2
    tm = h2.shape[0]
    packed = _pack_bf16_pair(h2[:, :half], h2[:, half:])
    rows_ref[0, :, 0:H2_SLAB_ROWS, :] = packed.reshape(tm, H2_SLAB_ROWS, LANES)
    rows_ref[0, :, ROUTE_SLAB_ROW:ROUTE_SLAB_ROW + 1, :] = (
        pltpu.bitcast(route, jnp.uint32).reshape(tm, 1, LANES))
    rows_ref[0, :, ROUTE_SLAB_ROW + 1:SUBLANES, :] = jnp.zeros(
        (tm, SUBLANES - ROUTE_SLAB_ROW - 1, LANES), jnp.uint32)


def _merge(heads, four, p, x, mod3, norm2_g, w_ao, w_fo, w_o, wr_hilo, b_r, tm):
    b, n, d = x.shape
    tok = lambda w: pl.BlockSpec((1, tm, w), lambda bi, i: (bi, i, 0))
    return pl.pallas_call(
        _merge_kernel,
        grid=(b, n // tm),
        in_specs=[tok(d), tok(F_W),
                  pl.BlockSpec((1, tm, d), lambda bi, i: (bi, i, P_GA // D_MODEL)),
                  pl.BlockSpec((1, tm, d), lambda bi, i: (bi, i, P_GF // D_MODEL)),
                  tok(d),
                  pl.BlockSpec((1, 6, d), lambda bi, i: (bi, 0, 0)),
                  pl.BlockSpec((1, d), lambda bi, i: (0, 0)),
                  _const_spec((d, d)), _const_spec((F_W, d)), _const_spec((d, d)),
                  _const_spec((d, 2 * ROUTER_W)),
                  pl.BlockSpec((1, ROUTER_W), lambda bi, i: (0, 0))],
        out_specs=[tok(d),
                   pl.BlockSpec((1, tm, SUBLANES, LANES), lambda bi, i: (bi, i, 0, 0)),
                   pl.BlockSpec((1, 1, SUBLANES, tm), lambda bi, i: (bi, i, 0, 0)),
                   pl.BlockSpec((1, ROUTER_W), lambda bi, i: (0, 0))],
        out_shape=[jax.ShapeDtypeStruct((b, n, d), _F32),
                   jax.ShapeDtypeStruct((b, n, SUBLANES, LANES), jnp.uint32),
                   jax.ShapeDtypeStruct((b, n // tm, SUBLANES, tm), _F32),
                   jax.ShapeDtypeStruct((1, ROUTER_W), _F32)],
        scratch_shapes=[pltpu.VMEM((1, ROUTER_W), _F32)],
        compiler_params=_cparams("arbitrary", "arbitrary"),
        name="merge_router",
    )(heads, four, p, p, x, mod3, norm2_g, w_ao, w_fo, w_o, wr_hilo, b_r)


def _issue_slab_copies(n, make_copy, slot_of):
    def group(g, c):
        r0 = g * DMA_ISSUE_GROUP
        slots = [slot_of(r0 + k) for k in range(DMA_ISSUE_GROUP)]
        for k in range(DMA_ISSUE_GROUP):
            make_copy(r0 + k, slots[k]).start()
        return c

    lax.fori_loop(0, n // DMA_ISSUE_GROUP, group, 0)


def _dispatch_kernel(pos_ref, src_ref, zeros_ref, dst_ref, sem, *, tm):
    del zeros_ref
    base = pl.program_id(0) * tm
    _issue_slab_copies(
        tm,
        lambda r, slot: pltpu.make_async_copy(src_ref.at[r], dst_ref.at[slot], sem),
        lambda r: pos_ref[base + r])
    pltpu.make_async_copy(src_ref, dst_ref.at[pl.ds(0, tm)], sem).wait()


def _dispatch(pos, slabs, n_sorted, tm):
    t = slabs.shape[0]
    zeros = jnp.zeros((n_sorted,) + slabs.shape[1:], slabs.dtype)
    return pl.pallas_call(
        functools.partial(_dispatch_kernel, tm=tm),
        grid_spec=pltpu.PrefetchScalarGridSpec(
            num_scalar_prefetch=1,
            grid=(t // tm,),
            in_specs=[pl.BlockSpec((tm, SUBLANES, LANES), lambda i, pos: (i, 0, 0)),
                      pl.BlockSpec(memory_space=pl.ANY)],
            out_specs=pl.BlockSpec(memory_space=pl.ANY),
            scratch_shapes=[pltpu.SemaphoreType.DMA(())]),
        out_shape=jax.ShapeDtypeStruct(zeros.shape, slabs.dtype),
        input_output_aliases={2: 0},
        compiler_params=_cparams("arbitrary"),
        name="moe_dispatch",
    )(pos, slabs, zeros)


def _moe_kernel(ea_ref, eb_ref, valid_ref, slabs_ref, w1a_ref, w3a_ref, w2a_ref,
                w1b_ref, w3b_ref, w2b_ref, y_ref):
    del ea_ref, eb_ref
    i = pl.program_id(0)
    tm = slabs_ref.shape[0]

    @pl.when(valid_ref[i] != 0)
    def _():
        packed = slabs_ref[:, 0:H2_SLAB_ROWS, :].reshape(tm, H2_SLAB_ROWS * LANES)
        hi, lo = _unpack_bf16_pair(packed)
        t = jnp.concatenate([hi.astype(_BF), lo.astype(_BF)], axis=1)
        route = pltpu.bitcast(slabs_ref[:, ROUTE_SLAB_ROW, :], _F32)
        y = None
        for slot, (w1_ref, w3_ref, w2_ref) in enumerate(((w1a_ref, w3a_ref, w2a_ref),
                                                          (w1b_ref, w3b_ref, w2b_ref))):
            a = jnp.dot(t, w1_ref[0], preferred_element_type=_F32)
            u = jnp.dot(t, w3_ref[0], preferred_element_type=_F32)
            hid = ((a * _sigmoid(a)) * u * route[:, slot:slot + 1]).astype(_BF)
            part = jnp.dot(hid, w2_ref[0], preferred_element_type=_F32)
            y = part if y is None else y + part
        y_ref[...] = y.reshape(tm, SUBLANES, LANES)

    @pl.when(valid_ref[i] == 0)
    def _():
        y_ref[...] = jnp.zeros_like(y_ref)


def _moe(tile_ea, tile_eb, tile_valid, slabs_sorted, w1, w3, w2, tm):
    n_sorted = slabs_sorted.shape[0]
    d, hdn = w1.shape[1], w1.shape[2]
    assert d == SUBLANES * LANES
    pick = lambda which, a, b_: pl.BlockSpec(
        (1, a, b_), lambda i, ea, eb, valid: ((ea, eb)[which][i], 0, 0))
    slab_tile = pl.BlockSpec((tm, SUBLANES, LANES), lambda i, ea, eb, valid: (i, 0, 0))
    return pl.pallas_call(
        _moe_kernel,
        grid_spec=pltpu.PrefetchScalarGridSpec(
            num_scalar_prefetch=3,
            grid=(n_sorted // tm,),
            in_specs=[slab_tile,
                      pick(0, d, hdn), pick(0, d, hdn), pick(0, hdn, d),
                      pick(1, d, hdn), pick(1, d, hdn), pick(1, hdn, d)],
            out_specs=slab_tile),
        out_shape=jax.ShapeDtypeStruct((n_sorted, SUBLANES, LANES), _F32),
        compiler_params=_cparams("arbitrary"),
        name="moe_experts",
    )(tile_ea, tile_eb, tile_valid, slabs_sorted, w1, w3, w2, w1, w3, w2)


def _combine_kernel(pos_ref, y_hbm_ref, x1_ref, mod_ref, fg_ref, o_ref, ybuf_ref, sems, *, tm):
    i = pl.program_id(0)
    n_steps = pl.num_programs(0)

    def issue(tile, slot):
        _issue_slab_copies(
            tm,
            lambda r, src: pltpu.make_async_copy(y_hbm_ref.at[src], ybuf_ref.at[slot, r],
                                                 sems.at[slot]),
            lambda r: pos_ref[tile * tm + r])

    @pl.when(i == 0)
    def _():
        issue(0, 0)

    @pl.when(i + 1 < n_steps)
    def _():
        issue(i + 1, (i + 1) % 2)

    slot = i % 2
    pltpu.make_async_copy(y_hbm_ref.at[pl.ds(0, tm)], ybuf_ref.at[slot], sems.at[slot]).wait()
    y = ybuf_ref[slot].reshape(tm, SUBLANES * LANES)
    xo = x1_ref[...] + mod_ref[0, 5:6, :] * y
    o_ref[...] = _rms(xo) * fg_ref[...]


def _combine(pos, y_sorted, x1, mod3, final_g, seq, tm):
    t, d = x1.shape
    tiles_per_batch = seq // tm
    return pl.pallas_call(
        functools.partial(_combine_kernel, tm=tm),
        grid_spec=pltpu.PrefetchScalarGridSpec(
            num_scalar_prefetch=1,
            grid=(t // tm,),
            in_specs=[pl.BlockSpec(memory_space=pl.ANY),
                      pl.BlockSpec((tm, d), lambda i, pos: (i, 0)),
                      pl.BlockSpec((1, 6, d), lambda i, pos: (i // tiles_per_batch, 0, 0)),
                      pl.BlockSpec((1, d), lambda i, pos: (0, 0))],
            out_specs=pl.BlockSpec((tm, d), lambda i, pos: (i, 0)),
            scratch_shapes=[pltpu.VMEM((2, tm, SUBLANES, LANES), _F32),
                            pltpu.SemaphoreType.DMA((2,))]),
        out_shape=jax.ShapeDtypeStruct((t, d), _F32),
        compiler_params=_cparams("arbitrary"),
        name="moe_combine",
    )(pos, y_sorted, x1, mod3, final_g)


def _routing_tables(route_cls, route_rank, counts, n_tokens, tm):
    sizes = ((counts + (tm - 1)) // tm) * tm
    ends = jnp.cumsum(sizes)
    starts = ends - sizes
    n_tiles = n_tokens // tm + N_CLASSES
    tile_start = jnp.arange(n_tiles, dtype=jnp.int32) * tm
    pos = route_rank
    tile_cls = jnp.zeros((n_tiles,), jnp.int32)
    for c in range(N_CLASSES):
        pos = pos + jnp.where(route_cls == c, starts[c], 0)
        tile_cls = tile_cls + (tile_start >= ends[c]).astype(jnp.int32)
    tile_cls = jnp.minimum(tile_cls, N_CLASSES - 1)
    tile_valid = (tile_start < ends[-1]).astype(jnp.int32)
    pairs = [(a, b_) for a in range(EXPERTS_PER_GROUP) for b_ in range(a + 1, EXPERTS_PER_GROUP)]
    grp, pair = tile_cls // PAIRS_PER_GROUP, tile_cls % PAIRS_PER_GROUP
    tile_ea = grp * EXPERTS_PER_GROUP
    tile_eb = grp * EXPERTS_PER_GROUP
    for k, (a, b_) in enumerate(pairs):
        tile_ea = tile_ea + jnp.where(pair == k, a, 0)
        tile_eb = tile_eb + jnp.where(pair == k, b_, 0)
    return (pos.astype(jnp.int32), tile_ea.astype(jnp.int32), tile_eb.astype(jnp.int32),
            tile_valid, n_tiles)


def _rope_tables(n):
    inv = (1.0 / (ROPE_BASE ** (np.arange(ROT_FREQS, dtype=np.float32) / ROT_FREQS))).astype(np.float32)
    pos = np.arange(n)
    row = (pos // GRID_W).astype(np.float32)[:, None] * inv[None, :]
    col = (pos % GRID_W).astype(np.float32)[:, None] * inv[None, :]
    cos64 = np.concatenate([np.cos(row), np.cos(row), np.cos(col), np.cos(col)], axis=1)
    sin64 = np.concatenate([-np.sin(row), np.sin(row), -np.sin(col), np.sin(col)], axis=1)
    tile = lambda a: np.tile(a.astype(np.float32), (1, LANES // HEAD_DIM))
    return jnp.asarray(tile(cos64)), jnp.asarray(tile(sin64))


def _dft_cos_sin(n):
    k = np.arange(n, dtype=np.int64)
    ang = (2.0 * np.pi / n) * ((k[:, None] * k[None, :]) % n).astype(np.float64)
    return np.cos(ang), np.sin(ang)


def _dft_tables(n):
    c_ch, s_ch = _dft_cos_sin(F_GROUP_DIM)
    c_seq, s_seq = _dft_cos_sin(n)
    cs_ch = np.concatenate([c_ch, s_ch], axis=1).astype(np.float32)
    cs_seq = np.concatenate([c_seq, -s_seq], axis=1).astype(np.float32)
    return jnp.asarray(cs_ch.astype(_BF)), jnp.asarray(cs_seq.astype(_BF))


def kernel(x, c, ctx, c_ctx, w_mod, b_mod, norm1_g, norm2_g, w_in, lam_q1, lam_k1, lam_q2, lam_k2,
           subln_g, w_attn_out, w_four_out, w_out, w_router_group, b_router_group, w_router_expert,
           b_router_expert, w_exp_gate, w_exp_up, w_exp_down, final_g):
    b, n, d = x.shape
    assert w_mod.shape[0] == 1, "depth-1 stack"
    assert b + 1 <= MOD_ROWS

    cc = jnp.concatenate([c, c_ctx[None, :], jnp.zeros((MOD_ROWS - b - 1, d), _F32)], axis=0)
    mod3 = _modulation(cc, w_mod[0], b_mod).reshape(MOD_ROWS, 6, d)

    lam = (jnp.exp(jnp.sum(lam_q1[0] * lam_k1[0])) - jnp.exp(jnp.sum(lam_q2[0] * lam_k2[0]))
           + LAM_INIT).reshape(1).astype(_F32)

    w = w_in[0]
    scale = HEAD_DIM ** -0.5
    w_lat = jnp.concatenate([w[:, REF_Q:REF_K] * scale, w[:, REF_K:REF_F],
                             w[:, REF_GA:REF_END], w[:, REF_F:REF_GA]], axis=1).astype(_BF)
    w_ctx = w[:, REF_K:REF_F].astype(_BF)
    lat_kinds = ("rope",) * ((P_V - P_Q) // PROJ_CW) + ("plain",) * ((P_GA - P_V) // PROJ_CW) \
        + ("sigmoid",) * ((P_F - P_GA) // PROJ_CW) + ("plain",) * ((P_W - P_F) // PROJ_CW)
    ctx_kinds = ("plain",) * (w_ctx.shape[1] // PROJ_CW)

    cos_t, sin_t = _rope_tables(n)
    p = _in_projection(x, mod3, lambda bi: bi, norm1_g, w_lat, cos_t, sin_t, lat_kinds, PROJ_TM)
    kvc = _in_projection(ctx, mod3, lambda bi: b, norm1_g, w_ctx, cos_t, sin_t, ctx_kinds,
                         ctx.shape[1])

    heads = _attention(lam, p, kvc, subln_g, ATTN_TQ)
    cs_ch, cs_seq = _dft_tables(n)
    four = _fnet(p, cs_ch, cs_seq)

    w_r = jnp.concatenate([w_router_expert[0], w_router_group[0],
                           jnp.zeros((d, ROUTER_W - N_EXPERTS - N_GROUPS), _F32)], axis=1)
    b_r = jnp.concatenate([b_router_expert[0], b_router_group[0],
                           jnp.zeros((ROUTER_W - N_EXPERTS - N_GROUPS,), _F32)])[None, :]
    wr_hi = w_r.astype(_BF)
    wr_lo = (w_r - wr_hi.astype(_F32)).astype(_BF)
    x1, rows, rec, counts = _merge(heads, four, p, x, mod3, norm2_g, w_attn_out[0].astype(_BF),
                                   w_four_out[0].astype(_BF), w_out[0].astype(_BF),
                                   jnp.concatenate([wr_hi, wr_lo], axis=1), b_r, MERGE_TM)

    t = b * n
    slabs = rows.reshape(t, SUBLANES, LANES)
    rec = rec.reshape(t // MERGE_TM, SUBLANES, MERGE_TM)
    pos, tile_ea, tile_eb, tile_valid, n_tiles = _routing_tables(
        rec[:, 2, :].reshape(t).astype(jnp.int32), rec[:, 3, :].reshape(t).astype(jnp.int32),
        counts[0, :N_CLASSES].astype(jnp.int32), t, MOE_TM)

    slabs_sorted = _dispatch(pos, slabs, n_tiles * MOE_TM, DISPATCH_TM)
    y_sorted = _moe(tile_ea, tile_eb, tile_valid, slabs_sorted, w_exp_gate[0].astype(_BF),
                    w_exp_up[0].astype(_BF), w_exp_down[0].astype(_BF), MOE_TM)
    out = _combine(pos, y_sorted, x1.reshape(t, d), mod3, final_g[None, :], n, COMBINE_TM)
    return out.reshape(b, n, d)
```

```python
import functools
import math

import jax
import jax.numpy as jnp
import numpy as np
from jax import lax
from jax.experimental import pallas as pl
from jax.experimental.pallas import tpu as pltpu

D_MODEL = 1024
GRID_W = 64
EPS = 1e-6
N_HEADS = 8
HEAD_DIM = 64
HEAD_W = 2 * HEAD_DIM
ROT_FREQS = HEAD_DIM // 4
ROPE_BASE = 10000.0
F_GROUPS = 4
F_GROUP_DIM = 128
F_W = F_GROUPS * F_GROUP_DIM
N_GROUPS = 4
EXPERTS_PER_GROUP = 4
LOG2_EXPERTS_PER_GROUP = 2
N_EXPERTS = N_GROUPS * EXPERTS_PER_GROUP
PAIRS_PER_GROUP = EXPERTS_PER_GROUP * (EXPERTS_PER_GROUP - 1) // 2
N_CLASSES = N_GROUPS * PAIRS_PER_GROUP
EXPERT_HIDDEN = 512
LAM_INIT = 0.8 - 0.6 * math.exp(-0.3 * 0)

REF_Q, REF_K, REF_V, REF_F, REF_GA, REF_GF, REF_END = 0, 1024, 2048, 3072, 3584, 4608, 5632
P_Q, P_K, P_V, P_GA, P_GF, P_F, P_W = 0, 1024, 2048, 3072, 4096, 5120, 5632

LANES = 128
SUBLANES = 8
VMEM_LIMIT_BYTES = 56 * 1024 * 1024

PROJ_TM = 512
PROJ_CW = 512
ATTN_TQ = 256
FNET_TR = 512
MERGE_TM = 512
MOE_TM = 256
DISPATCH_TM = 512
COMBINE_TM = 512
DMA_ISSUE_GROUP = 8
MOD_ROWS = 40
ROUTER_W = LANES
GROUP_LANE0 = N_EXPERTS
H2_SLAB_ROWS = D_MODEL // 2 // LANES
ROUTE_SLAB_ROW = H2_SLAB_ROWS

_BF = jnp.bfloat16
_F32 = jnp.float32


def _cparams(*sem):
    return pltpu.CompilerParams(dimension_semantics=sem, vmem_limit_bytes=VMEM_LIMIT_BYTES)


def _const_spec(shape):
    return pl.BlockSpec(shape, lambda *_: (0,) * len(shape), pipeline_mode=pl.Buffered(1))


def _rms(x):
    return x * lax.rsqrt(jnp.mean(x * x, axis=-1, keepdims=True) + EPS)


def _sigmoid(x):
    return 1.0 / (1.0 + jnp.exp(-x))


def _mod_kernel(cc_ref, w_ref, b_ref, o_ref):
    cc = cc_ref[...]
    s = cc * _sigmoid(cc)
    o_ref[...] = jnp.dot(s, w_ref[...], preferred_element_type=_F32,
                         precision=lax.Precision.HIGHEST) + b_ref[...]


def _modulation(cc, w_mod, b_mod):
    n = w_mod.shape[1]
    bn = D_MODEL
    return pl.pallas_call(
        _mod_kernel,
        grid=(n // bn,),
        in_specs=[pl.BlockSpec((MOD_ROWS, D_MODEL), lambda j: (0, 0)),
                  pl.BlockSpec((D_MODEL, bn), lambda j: (0, j)),
                  pl.BlockSpec((1, bn), lambda j: (0, j))],
        out_specs=pl.BlockSpec((MOD_ROWS, bn), lambda j: (0, j)),
        out_shape=jax.ShapeDtypeStruct((MOD_ROWS, n), _F32),
        compiler_params=_cparams("arbitrary"),
        name="modulation",
    )(cc, w_mod, b_mod)


def _rope(acc, cos_ref, sin_ref):
    cos = cos_ref[...]
    sin = sin_ref[...]
    lane = lax.broadcasted_iota(jnp.int32, (1, LANES), 1)
    first_half = (lane % (2 * ROT_FREQS)) < ROT_FREQS
    outs = []
    for s in range(acc.shape[1] // LANES):
        xs = acc[:, s * LANES:(s + 1) * LANES]
        partner = jnp.where(first_half,
                            pltpu.roll(xs, LANES - ROT_FREQS, 1),
                            pltpu.roll(xs, ROT_FREQS, 1))
        outs.append(xs * cos + partner * sin)
    return jnp.concatenate(outs, axis=1)


def _inproj_kernel(x_ref, mod_ref, g_ref, w_ref, cos_ref, sin_ref, o_ref, *, chunk_kinds):
    x = x_ref[0]
    shift = mod_ref[0, 0:1, :]
    scale = mod_ref[0, 1:2, :]
    h = (_rms(x) * g_ref[...]) * (1.0 + scale) + shift
    hb = h.astype(_BF)
    for j, kind in enumerate(chunk_kinds):
        cols = slice(j * PROJ_CW, (j + 1) * PROJ_CW)
        acc = jnp.dot(hb, w_ref[:, cols], preferred_element_type=_F32)
        if kind == "rope":
            acc = _rope(acc, cos_ref, sin_ref)
        elif kind == "sigmoid":
            acc = _sigmoid(acc)
        o_ref[0, :, cols] = acc.astype(_BF)


def _in_projection(x, mod3, mod_row_of_batch, g, w, cos_t, sin_t, chunk_kinds, tm):
    b, n, d = x.shape
    width = w.shape[1]
    assert width == len(chunk_kinds) * PROJ_CW and n % tm == 0
    return pl.pallas_call(
        functools.partial(_inproj_kernel, chunk_kinds=chunk_kinds),
        grid=(b, n // tm),
        in_specs=[pl.BlockSpec((1, tm, d), lambda bi, i: (bi, i, 0)),
                  pl.BlockSpec((1, 6, d), lambda bi, i: (mod_row_of_batch(bi), 0, 0)),
                  pl.BlockSpec((1, d), lambda bi, i: (0, 0)),
                  _const_spec((d, width)),
                  pl.BlockSpec((tm, LANES), lambda bi, i: (i, 0)),
                  pl.BlockSpec((tm, LANES), lambda bi, i: (i, 0))],
        out_specs=pl.BlockSpec((1, tm, width), lambda bi, i: (bi, i, 0)),
        out_shape=jax.ShapeDtypeStruct((b, n, width), _BF),
        compiler_params=_cparams("parallel", "arbitrary"),
        name="in_projection",
    )(x, mod3, g, w, cos_t, sin_t)


def _attn_kernel(lam_ref, q_ref, kl_ref, vl_ref, kc_ref, vc_ref, sg_ref, o_ref,
                 kcat_ref, vext_ref, s_even_ref, s_odd_ref, *, tq):
    nc, n = kc_ref.shape[1], kl_ref.shape[1]
    kcat_ref[0:nc, :] = kc_ref[0]
    kcat_ref[nc:nc + n, :] = kl_ref[0]
    vext_ref[0:nc, 0:HEAD_W] = vc_ref[0]
    vext_ref[nc:nc + n, 0:HEAD_W] = vl_ref[0]
    vext_ref[:, HEAD_W:] = jnp.ones((nc + n, HEAD_W), _BF)
    lane = lax.broadcasted_iota(jnp.int32, (1, HEAD_W), 1)
    map_lanes = (lane < HEAD_DIM, lane >= HEAD_DIM)
    nt = (((1,), (1,)), ((), ()))
    bufs = (s_even_ref, s_odd_ref)
    lam = lam_ref[0]
    post_scale = sg_ref[...] * (1.0 - LAM_INIT)

    def scores(j):
        q = q_ref[0, j * tq:(j + 1) * tq, :]
        for mp in range(2):
            qm = jnp.where(map_lanes[mp], q, jnp.zeros_like(q))
            bufs[j % 2][mp] = lax.dot_general(qm, kcat_ref[...], nt, preferred_element_type=_F32)

    def finish(j):
        outs = []
        for mp in range(2):
            s = bufs[j % 2][mp]
            e = jnp.exp2(s - jnp.max(s, axis=-1, keepdims=True)).astype(_BF)
            r = jnp.dot(e, vext_ref[...], preferred_element_type=_F32)
            outs.append(r[:, :HEAD_W] / r[:, HEAD_W:])
        heads = outs[0] - lam * outs[1]
        o_ref[0, j * tq:(j + 1) * tq, :] = (_rms(heads) * post_scale).astype(_BF)

    n_sub = n // tq
    scores(0)
    for j in range(n_sub):
        if j + 1 < n_sub:
            scores(j + 1)
        finish(j)


def _attention(lam, p, kvc, subln_g, tq):
    b, n, _ = p.shape
    nc = kvc.shape[1]
    qb, kb, vb = P_Q // HEAD_W, P_K // HEAD_W, P_V // HEAD_W
    seq = lambda blk: pl.BlockSpec((1, n, HEAD_W), lambda bi, h: (bi, 0, blk + h))
    return pl.pallas_call(
        functools.partial(_attn_kernel, tq=tq),
        grid=(b, N_HEADS),
        in_specs=[pl.BlockSpec(memory_space=pltpu.SMEM),
                  seq(qb), seq(kb), seq(vb),
                  pl.BlockSpec((1, nc, HEAD_W), lambda bi, h: (bi, 0, h)),
                  pl.BlockSpec((1, nc, HEAD_W), lambda bi, h: (bi, 0, N_HEADS + h)),
                  pl.BlockSpec((1, HEAD_W), lambda bi, h: (0, 0))],
        out_specs=seq(0),
        out_shape=jax.ShapeDtypeStruct((b, n, N_HEADS * HEAD_W), _BF),
        scratch_shapes=[pltpu.VMEM((nc + n, HEAD_W), _BF), pltpu.VMEM((nc + n, 2 * HEAD_W), _BF),
                        pltpu.VMEM((2, tq, nc + n), _F32), pltpu.VMEM((2, tq, nc + n), _F32)],
        compiler_params=_cparams("parallel", "arbitrary"),
        name="diff_attention",
    )(lam, p, p, p, kvc, kvc, subln_g)


def _fnet_kernel(f_ref, cs_ch_ref, cs_seq_ref, o_ref, xcs_ref, *, n):
    for g in range(F_GROUPS):
        cols = slice(g * F_GROUP_DIM, (g + 1) * F_GROUP_DIM)
        t = jnp.dot(f_ref[0, :, cols], cs_ch_ref[...], preferred_element_type=_F32)
        xcs_ref[0:n, cols] = t[:, :F_GROUP_DIM].astype(_BF)
        xcs_ref[n:2 * n, cols] = t[:, F_GROUP_DIM:].astype(_BF)
    ortho = 1.0 / math.sqrt(n * F_GROUP_DIM)
    for r in range(n // FNET_TR):
        rows = slice(r * FNET_TR, (r + 1) * FNET_TR)
        y = jnp.dot(cs_seq_ref[rows, :], xcs_ref[...], preferred_element_type=_F32)
        o_ref[0, rows, :] = (y * ortho).astype(_BF)


def _fnet(p, cs_ch, cs_seq):
    b, n, _ = p.shape
    return pl.pallas_call(
        functools.partial(_fnet_kernel, n=n),
        grid=(b,),
        in_specs=[pl.BlockSpec((1, n, F_W), lambda bi: (bi, 0, P_F // F_W)),
                  _const_spec((F_GROUP_DIM, 2 * F_GROUP_DIM)),
                  _const_spec((n, 2 * n))],
        out_specs=pl.BlockSpec((1, n, F_W), lambda bi: (bi, 0, 0)),
        out_shape=jax.ShapeDtypeStruct((b, n, F_W), _BF),
        scratch_shapes=[pltpu.VMEM((2 * n, F_W), _BF)],
        compiler_params=_cparams("arbitrary"),
        name="fnet_dft",
    )(p, cs_ch, cs_seq)


def _pack_bf16_pair(hi, lo):
    hi_bits = pltpu.bitcast(hi.astype(_BF).astype(_F32), jnp.uint32)
    lo_bits = pltpu.bitcast(lo.astype(_BF).astype(_F32), jnp.uint32)
    return hi_bits | lax.shift_right_logical(lo_bits, jnp.uint32(16))


def _unpack_bf16_pair(packed):
    hi = pltpu.bitcast(packed & jnp.uint32(0xFFFF0000), _F32)
    lo = pltpu.bitcast(lax.shift_left(packed, jnp.uint32(16)), _F32)
    return hi, lo


def _route(logits, carry):
    rows = logits.shape[0]
    lane = lax.broadcasted_iota(jnp.int32, logits.shape, 1)
    neg = jnp.float32(-jnp.inf)
    big = jnp.int32(ROUTER_W)

    def first_argmax(v):
        m = jnp.max(v, axis=-1, keepdims=True)
        idx = jnp.min(jnp.where(v == m, lane, big), axis=-1, keepdims=True)
        return m, idx

    lg = jnp.where((lane >= GROUP_LANE0) & (lane < GROUP_LANE0 + N_GROUPS), logits, neg)
    mg, ig = first_argmax(lg)
    w_grp = 1.0 / jnp.sum(jnp.exp(lg - mg), axis=-1, keepdims=True)
    g_sel = ig - GROUP_LANE0
    le = jnp.where((lane < N_EXPERTS)
                   & (jnp.right_shift(lane, LOG2_EXPERTS_PER_GROUP) == g_sel), logits, neg)
    v1, i1 = first_argmax(le)
    le2 = jnp.where(lane == i1, neg, le)
    v2, i2 = first_argmax(le2)
    e2 = jnp.exp(v2 - v1)
    w1 = w_grp / (1.0 + e2)
    w2 = w_grp * e2 / (1.0 + e2)

    first_is_lower = i1 < i2
    gate_a = jnp.where(first_is_lower, w1, w2)
    gate_b = jnp.where(first_is_lower, w2, w1)
    la = jnp.minimum(i1, i2) - g_sel * EXPERTS_PER_GROUP
    lb = jnp.maximum(i1, i2) - g_sel * EXPERTS_PER_GROUP
    pair = jnp.right_shift(la * (2 * EXPERTS_PER_GROUP - 1 - la), 1) + lb - la - 1
    cls = g_sel * PAIRS_PER_GROUP + pair

    onehot = (lane == cls).astype(_F32)
    r_i = lax.broadcasted_iota(jnp.int32, (rows, rows), 0)
    c_i = lax.broadcasted_iota(jnp.int32, (rows, rows), 1)
    earlier = (c_i < r_i).astype(_BF)
    before = jnp.dot(earlier, onehot.astype(_BF), preferred_element_type=_F32) + carry
    rank = jnp.sum(onehot * before, axis=-1, keepdims=True)
    new_carry = carry + jnp.sum(onehot, axis=0, keepdims=True)

    route = (jnp.where(lane == 0, gate_a, 0.0) + jnp.where(lane == 1, gate_b, 0.0)
             + jnp.where(lane == 2, cls.astype(_F32), 0.0) + jnp.where(lane == 3, rank, 0.0))
    return route, new_carry


def _merge_kernel(hd_ref, fo_ref, ga_ref, gf_ref, x_ref, mod_ref, g2n_ref, wao_ref, wfo_ref,
                  wo_ref, wrhl_ref, br_ref, x1_ref, rows_ref, rec_ref, counts_ref, carry_ref):
    @pl.when((pl.program_id(0) == 0) & (pl.program_id(1) == 0))
    def _():
        carry_ref[...] = jnp.zeros_like(carry_ref)

    a = jnp.dot(hd_ref[0], wao_ref[...], preferred_element_type=_F32)
    ff = jnp.dot(fo_ref[0], wfo_ref[...], preferred_element_type=_F32)
    y = ga_ref[0].astype(_F32) * a + gf_ref[0].astype(_F32) * ff
    mix = jnp.dot(y.astype(_BF), wo_ref[...], preferred_element_type=_F32)
    x1 = x_ref[0] + mod_ref[0, 2:3, :] * mix
    x1_ref[0] = x1
    h2 = (_rms(x1) * g2n_ref[...]) * (1.0 + mod_ref[0, 4:5, :]) + mod_ref[0, 3:4, :]
    h2_hi = h2.astype(_BF)
    h2_lo = (h2 - h2_hi.astype(_F32)).astype(_BF)
    hi_both = jnp.dot(h2_hi, wrhl_ref[...], preferred_element_type=_F32)
    logits = (hi_both[:, :ROUTER_W] + hi_both[:, ROUTER_W:]
              + jnp.dot(h2_lo, wrhl_ref[:, :ROUTER_W], preferred_element_type=_F32)) + br_ref[...]
    route, new_carry = _route(logits, carry_ref[...])
    carry_ref[...] = new_carry
    counts_ref[...] = new_carry
    rec_ref[0, 0] = route.T[0:SUBLANES, :]
    half = D_MODEL // 2
    tm = h2.shape[0]
    packed = _pack_bf16_pair(h2[:, :half], h2[:, half:])
    rows_ref[0, :, 0:H2_SLAB_ROWS, :] = packed.reshape(tm, H2_SLAB_ROWS, LANES)
    rows_ref[0, :, ROUTE_SLAB_ROW:ROUTE_SLAB_ROW + 1, :] = (
        pltpu.bitcast(route, jnp.uint32).reshape(tm, 1, LANES))
    rows_ref[0, :, ROUTE_SLAB_ROW + 1:SUBLANES, :] = jnp.zeros(
        (tm, SUBLANES - ROUTE_SLAB_ROW - 1, LANES), jnp.uint32)


def _merge(heads, four, p, x, mod3, norm2_g, w_ao, w_fo, w_o, wr_hilo, b_r, tm):
    b, n, d = x.shape
    tok = lambda w: pl.BlockSpec((1, tm, w), lambda bi, i: (bi, i, 0))
    return pl.pallas_call(
        _merge_kernel,
        grid=(b, n // tm),
        in_specs=[tok(d), tok(F_W),
                  pl.BlockSpec((1, tm, d), lambda bi, i: (bi, i, P_GA // D_MODEL)),
                  pl.BlockSpec((1, tm, d), lambda bi, i: (bi, i, P_GF // D_MODEL)),
                  tok(d),
                  pl.BlockSpec((1, 6, d), lambda bi, i: (bi, 0, 0)),
                  pl.BlockSpec((1, d), lambda bi, i: (0, 0)),
                  _const_spec((d, d)), _const_spec((F_W, d)), _const_spec((d, d)),
                  _const_spec((d, 2 * ROUTER_W)),
                  pl.BlockSpec((1, ROUTER_W), lambda bi, i: (0, 0))],
        out_specs=[tok(d),
                   pl.BlockSpec((1, tm, SUBLANES, LANES), lambda bi, i: (bi, i, 0, 0)),
                   pl.BlockSpec((1, 1, SUBLANES, tm), lambda bi, i: (bi, i, 0, 0)),
                   pl.BlockSpec((1, ROUTER_W), lambda bi, i: (0, 0))],
        out_shape=[jax.ShapeDtypeStruct((b, n, d), _F32),
                   jax.ShapeDtypeStruct((b, n, SUBLANES, LANES), jnp.uint32),
                   jax.ShapeDtypeStruct((b, n // tm, SUBLANES, tm), _F32),
                   jax.ShapeDtypeStruct((1, ROUTER_W), _F32)],
        scratch_shapes=[pltpu.VMEM((1, ROUTER_W), _F32)],
        compiler_params=_cparams("arbitrary", "arbitrary"),
        name="merge_router",
    )(heads, four, p, p, x, mod3, norm2_g, w_ao, w_fo, w_o, wr_hilo, b_r)


def _issue_slab_copies(n, make_copy, slot_of):
    def group(g, c):
        r0 = g * DMA_ISSUE_GROUP
        slots = [slot_of(r0 + k) for k in range(DMA_ISSUE_GROUP)]
        for k in range(DMA_ISSUE_GROUP):
            make_copy(r0 + k, slots[k]).start()
        return c

    lax.fori_loop(0, n // DMA_ISSUE_GROUP, group, 0)


def _dispatch_kernel(pos_ref, src_ref, zeros_ref, dst_ref, sem, *, tm):
    del zeros_ref
    base = pl.program_id(0) * tm
    _issue_slab_copies(
        tm,
        lambda r, slot: pltpu.make_async_copy(src_ref.at[r], dst_ref.at[slot], sem),
        lambda r: pos_ref[base + r])
    pltpu.make_async_copy(src_ref, dst_ref.at[pl.ds(0, tm)], sem).wait()


def _dispatch(pos, slabs, n_sorted, tm):
    t = slabs.shape[0]
    zeros = jnp.zeros((n_sorted,) + slabs.shape[1:], slabs.dtype)
    return pl.pallas_call(
        functools.partial(_dispatch_kernel, tm=tm),
        grid_spec=pltpu.PrefetchScalarGridSpec(
            num_scalar_prefetch=1,
            grid=(t // tm,),
            in_specs=[pl.BlockSpec((tm, SUBLANES, LANES), lambda i, pos: (i, 0, 0)),
                      pl.BlockSpec(memory_space=pl.ANY)],
            out_specs=pl.BlockSpec(memory_space=pl.ANY),
            scratch_shapes=[pltpu.SemaphoreType.DMA(())]),
        out_shape=jax.ShapeDtypeStruct(zeros.shape, slabs.dtype),
        input_output_aliases={2: 0},
        compiler_params=_cparams("arbitrary"),
        name="moe_dispatch",
    )(pos, slabs, zeros)


def _moe_kernel(ea_ref, eb_ref, valid_ref, slabs_ref, w1a_ref, w3a_ref, w2a_ref,
                w1b_ref, w3b_ref, w2b_ref, y_ref):
    del ea_ref, eb_ref
    i = pl.program_id(0)
    tm = slabs_ref.shape[0]

    @pl.when(valid_ref[i] != 0)
    def _():
        packed = slabs_ref[:, 0:H2_SLAB_ROWS, :].reshape(tm, H2_SLAB_ROWS * LANES)
        hi, lo = _unpack_bf16_pair(packed)
        t = jnp.concatenate([hi.astype(_BF), lo.astype(_BF)], axis=1)
        route = pltpu.bitcast(slabs_ref[:, ROUTE_SLAB_ROW, :], _F32)
        y = None
        for slot, (w1_ref, w3_ref, w2_ref) in enumerate(((w1a_ref, w3a_ref, w2a_ref),
                                                          (w1b_ref, w3b_ref, w2b_ref))):
            a = jnp.dot(t, w1_ref[0], preferred_element_type=_F32)
            u = jnp.dot(t, w3_ref[0], preferred_element_type=_F32)
            hid = ((a * _sigmoid(a)) * u * route[:, slot:slot + 1]).astype(_BF)
            part = jnp.dot(hid, w2_ref[0], preferred_element_type=_F32)
            y = part if y is None else y + part
        y_ref[...] = y.reshape(tm, SUBLANES, LANES)

    @pl.when(valid_ref[i] == 0)
    def _():
        y_ref[...] = jnp.zeros_like(y_ref)


def _moe(tile_ea, tile_eb, tile_valid, slabs_sorted, w1, w3, w2, tm):
    n_sorted = slabs_sorted.shape[0]
    d, hdn = w1.shape[1], w1.shape[2]
    assert d == SUBLANES * LANES
    pick = lambda which, a, b_: pl.BlockSpec(
        (1, a, b_), lambda i, ea, eb, valid: ((ea, eb)[which][i], 0, 0))
    slab_tile = pl.BlockSpec((tm, SUBLANES, LANES), lambda i, ea, eb, valid: (i, 0, 0))
    return pl.pallas_call(
        _moe_kernel,
        grid_spec=pltpu.PrefetchScalarGridSpec(
            num_scalar_prefetch=3,
            grid=(n_sorted // tm,),
            in_specs=[slab_tile,
                      pick(0, d, hdn), pick(0, d, hdn), pick(0, hdn, d),
                      pick(1, d, hdn), pick(1, d, hdn), pick(1, hdn, d)],
            out_specs=slab_tile),
        out_shape=jax.ShapeDtypeStruct((n_sorted, SUBLANES, LANES), _F32),
        compiler_params=_cparams("arbitrary"),
        name="moe_experts",
    )(tile_ea, tile_eb, tile_valid, slabs_sorted, w1, w3, w2, w1, w3, w2)


def _combine_kernel(pos_ref, y_hbm_ref, x1_ref, mod_ref, fg_ref, o_ref, ybuf_ref, sems, *, tm):
    i = pl.program_id(0)
    n_steps = pl.num_programs(0)

    def issue(tile, slot):
        _issue_slab_copies(
            tm,
            lambda r, src: pltpu.make_async_copy(y_hbm_ref.at[src], ybuf_ref.at[slot, r],
                                                 sems.at[slot]),
            lambda r: pos_ref[tile * tm + r])

    @pl.when(i == 0)
    def _():
        issue(0, 0)

    @pl.when(i + 1 < n_steps)
    def _():
        issue(i + 1, (i + 1) % 2)

    slot = i % 2
    pltpu.make_async_copy(y_hbm_ref.at[pl.ds(0, tm)], ybuf_ref.at[slot], sems.at[slot]).wait()
    y = ybuf_ref[slot].reshape(tm, SUBLANES * LANES)
    xo = x1_ref[...] + mod_ref[0, 5:6, :] * y
    o_ref[...] = _rms(xo) * fg_ref[...]


def _combine(pos, y_sorted, x1, mod3, final_g, seq, tm):
    t, d = x1.shape
    tiles_per_batch = seq // tm
    return pl.pallas_call(
        functools.partial(_combine_kernel, tm=tm),
        grid_spec=pltpu.PrefetchScalarGridSpec(
            num_scalar_prefetch=1,
            grid=(t // tm,),
            in_specs=[pl.BlockSpec(memory_space=pl.ANY),
                      pl.BlockSpec((tm, d), lambda i, pos: (i, 0)),
                      pl.BlockSpec((1, 6, d), lambda i, pos: (i // tiles_per_batch, 0, 0)),
                      pl.BlockSpec((1, d), lambda i, pos: (0, 0))],
            out_specs=pl.BlockSpec((tm, d), lambda i, pos: (i, 0)),
            scratch_shapes=[pltpu.VMEM((2, tm, SUBLANES, LANES), _F32),
                            pltpu.SemaphoreType.DMA((2,))]),
        out_shape=jax.ShapeDtypeStruct((t, d), _F32),
        compiler_params=_cparams("arbitrary"),
        name="moe_combine",
    )(pos, y_sorted, x1, mod3, final_g)


def _routing_tables(route_cls, route_rank, counts, n_tokens, tm):
    sizes = ((counts + (tm - 1)) // tm) * tm
    ends = jnp.cumsum(sizes)
    starts = ends - sizes
    n_tiles = n_tokens // tm + N_CLASSES
    tile_start = jnp.arange(n_tiles, dtype=jnp.int32) * tm
    pos = route_rank
    tile_cls = jnp.zeros((n_tiles,), jnp.int32)
    for c in range(N_CLASSES):
        pos = pos + jnp.where(route_cls == c, starts[c], 0)
        tile_cls = tile_cls + (tile_start >= ends[c]).astype(jnp.int32)
    tile_cls = jnp.minimum(tile_cls, N_CLASSES - 1)
    tile_valid = (tile_start < ends[-1]).astype(jnp.int32)
    pairs = [(a, b_) for a in range(EXPERTS_PER_GROUP) for b_ in range(a + 1, EXPERTS_PER_GROUP)]
    grp, pair = tile_cls // PAIRS_PER_GROUP, tile_cls % PAIRS_PER_GROUP
    tile_ea = grp * EXPERTS_PER_GROUP
    tile_eb = grp * EXPERTS_PER_GROUP
    for k, (a, b_) in enumerate(pairs):
        tile_ea = tile_ea + jnp.where(pair == k, a, 0)
        tile_eb = tile_eb + jnp.where(pair == k, b_, 0)
    return (pos.astype(jnp.int32), tile_ea.astype(jnp.int32), tile_eb.astype(jnp.int32),
            tile_valid, n_tiles)


def _rope_tables(n):
    inv = (1.0 / (ROPE_BASE ** (np.arange(ROT_FREQS, dtype=np.float32) / ROT_FREQS))).astype(np.float32)
    pos = np.arange(n)
    row = (pos // GRID_W).astype(np.float32)[:, None] * inv[None, :]
    col = (pos % GRID_W).astype(np.float32)[:, None] * inv[None, :]
    cos64 = np.concatenate([np.cos(row), np.cos(row), np.cos(col), np.cos(col)], axis=1)
    sin64 = np.concatenate([-np.sin(row), np.sin(row), -np.sin(col), np.sin(col)], axis=1)
    tile = lambda a: np.tile(a.astype(np.float32), (1, LANES // HEAD_DIM))
    return jnp.asarray(tile(cos64)), jnp.asarray(tile(sin64))


def _dft_cos_sin(n):
    k = np.arange(n, dtype=np.int64)
    ang = (2.0 * np.pi / n) * ((k[:, None] * k[None, :]) % n).astype(np.float64)
    return np.cos(ang), np.sin(ang)


def _dft_tables(n):
    c_ch, s_ch = _dft_cos_sin(F_GROUP_DIM)
    c_seq, s_seq = _dft_cos_sin(n)
    cs_ch = np.concatenate([c_ch, s_ch], axis=1).astype(np.float32)
    cs_seq = np.concatenate([c_seq, -s_seq], axis=1).astype(np.float32)
    return jnp.asarray(cs_ch.astype(_BF)), jnp.asarray(cs_seq.astype(_BF))


def kernel(x, c, ctx, c_ctx, w_mod, b_mod, norm1_g, norm2_g, w_in, lam_q1, lam_k1, lam_q2, lam_k2,
           subln_g, w_attn_out, w_four_out, w_out, w_router_group, b_router_group, w_router_expert,
           b_router_expert, w_exp_gate, w_exp_up, w_exp_down, final_g):
    b, n, d = x.shape
    assert w_mod.shape[0] == 1, "depth-1 stack"
    assert b + 1 <= MOD_ROWS

    cc = jnp.concatenate([c, c_ctx[None, :], jnp.zeros((MOD_ROWS - b - 1, d), _F32)], axis=0)
    mod3 = _modulation(cc, w_mod[0], b_mod).reshape(MOD_ROWS, 6, d)

    lam = (jnp.exp(jnp.sum(lam_q1[0] * lam_k1[0])) - jnp.exp(jnp.sum(lam_q2[0] * lam_k2[0]))
           + LAM_INIT).reshape(1).astype(_F32)

    w = w_in[0]
    scale = HEAD_DIM ** -0.5 * math.log2(math.e)
    w_lat = jnp.concatenate([w[:, REF_Q:REF_K] * scale, w[:, REF_K:REF_F],
                             w[:, REF_GA:REF_END], w[:, REF_F:REF_GA]], axis=1).astype(_BF)
    w_ctx = w[:, REF_K:REF_F].astype(_BF)
    lat_kinds = ("rope",) * ((P_V - P_Q) // PROJ_CW) + ("plain",) * ((P_GA - P_V) // PROJ_CW) \
        + ("sigmoid",) * ((P_F - P_GA) // PROJ_CW) + ("plain",) * ((P_W - P_F) // PROJ_CW)
    ctx_kinds = ("plain",) * (w_ctx.shape[1] // PROJ_CW)

    cos_t, sin_t = _rope_tables(n)
    p = _in_projection(x, mod3, lambda bi: bi, norm1_g, w_lat, cos_t, sin_t, lat_kinds, PROJ_TM)
    kvc = _in_projection(ctx, mod3, lambda bi: b, norm1_g, w_ctx, cos_t, sin_t, ctx_kinds,
                         ctx.shape[1])

    heads = _attention(lam, p, kvc, subln_g, ATTN_TQ)
    cs_ch, cs_seq = _dft_tables(n)
    four = _fnet(p, cs_ch, cs_seq)

    w_r = jnp.concatenate([w_router_expert[0], w_router_group[0],
                           jnp.zeros((d, ROUTER_W - N_EXPERTS - N_GROUPS), _F32)], axis=1)
    b_r = jnp.concatenate([b_router_expert[0], b_router_group[0],
                           jnp.zeros((ROUTER_W - N_EXPERTS - N_GROUPS,), _F32)])[None, :]
    wr_hi = w_r.astype(_BF)
    wr_lo = (w_r - wr_hi.astype(_F32)).astype(_BF)
    x1, rows, rec, counts = _merge(heads, four, p, x, mod3, norm2_g, w_attn_out[0].astype(_BF),
                                   w_four_out[0].astype(_BF), w_out[0].astype(_BF),
                                   jnp.concatenate([wr_hi, wr_lo], axis=1), b_r, MERGE_TM)

    t = b * n
    slabs = rows.reshape(t, SUBLANES, LANES)
    rec = rec.reshape(t // MERGE_TM, SUBLANES, MERGE_TM)
    pos, tile_ea, tile_eb, tile_valid, n_tiles = _routing_tables(
        rec[:, 2, :].reshape(t).astype(jnp.int32), rec[:, 3, :].reshape(t).astype(jnp.int32),
        counts[0, :N_CLASSES].astype(jnp.int32), t, MOE_TM)

    slabs_sorted = _dispatch(pos, slabs, n_tiles * MOE_TM, DISPATCH_TM)
    y_sorted = _moe(tile_ea, tile_eb, tile_valid, slabs_sorted, w_exp_gate[0].astype(_BF),
                    w_exp_up[0].astype(_BF), w_exp_down[0].astype(_BF), MOE_TM)
    out = _combine(pos, y_sorted, x1.reshape(t, d), mod3, final_g[None, :], n, COMBINE_TM)
    return out.reshape(b, n, d)
```

```python
import functools
import math

import jax
import jax.numpy as jnp
import numpy as np
from jax import lax
from jax.experimental import pallas as pl
from jax.experimental.pallas import tpu as pltpu

D_MODEL = 1024
GRID_W = 64
EPS = 1e-6
N_HEADS = 8
HEAD_DIM = 64
HEAD_W = 2 * HEAD_DIM
ROT_FREQS = HEAD_DIM // 4
ROPE_BASE = 10000.0
F_GROUPS = 4
F_GROUP_DIM = 128
F_W = F_GROUPS * F_GROUP_DIM
N_GROUPS = 4
EXPERTS_PER_GROUP = 4
LOG2_EXPERTS_PER_GROUP = 2
N_EXPERTS = N_GROUPS * EXPERTS_PER_GROUP
PAIRS_PER_GROUP = EXPERTS_PER_GROUP * (EXPERTS_PER_GROUP - 1) // 2
N_CLASSES = N_GROUPS * PAIRS_PER_GROUP
EXPERT_HIDDEN = 512
LAM_INIT = 0.8 - 0.6 * math.exp(-0.3 * 0)

REF_Q, REF_K, REF_V, REF_F, REF_GA, REF_GF, REF_END = 0, 1024, 2048, 3072, 3584, 4608, 5632
P_Q, P_K, P_V, P_GA, P_GF, P_F, P_W = 0, 1024, 2048, 3072, 4096, 5120, 5632

LANES = 128
SUBLANES = 8
VMEM_LIMIT_BYTES = 56 * 1024 * 1024

PROJ_TM = 1024
PROJ_CW = 512
ATTN_TQ = 256
FNET_TR = 512
MERGE_TM = 512
MOE_TM = 256
DISPATCH_TM = 512
COMBINE_TM = 512
DMA_ISSUE_GROUP = 8
MOD_ROWS = 40
ROUTER_W = LANES
GROUP_LANE0 = N_EXPERTS
H2_SLAB_ROWS = D_MODEL // 2 // LANES
ROUTE_SLAB_ROW = H2_SLAB_ROWS

_BF = jnp.bfloat16
_F32 = jnp.float32


def _cparams(*sem):
    return pltpu.CompilerParams(dimension_semantics=sem, vmem_limit_bytes=VMEM_LIMIT_BYTES)


def _const_spec(shape):
    return pl.BlockSpec(shape, lambda *_: (0,) * len(shape), pipeline_mode=pl.Buffered(1))


def _rms(x):
    return x * lax.rsqrt(jnp.mean(x * x, axis=-1, keepdims=True) + EPS)


def _sigmoid(x):
    return 1.0 / (1.0 + jnp.exp(-x))


def _mod_kernel(cc_ref, w_ref, b_ref, o_ref):
    cc = cc_ref[...]
    s = cc * _sigmoid(cc)
    o_ref[...] = jnp.dot(s, w_ref[...], preferred_element_type=_F32,
                         precision=lax.Precision.HIGHEST) + b_ref[...]


def _modulation(cc, w_mod, b_mod):
    n = w_mod.shape[1]
    bn = D_MODEL
    return pl.pallas_call(
        _mod_kernel,
        grid=(n // bn,),
        in_specs=[pl.BlockSpec((MOD_ROWS, D_MODEL), lambda j: (0, 0)),
                  pl.BlockSpec((D_MODEL, bn), lambda j: (0, j)),
                  pl.BlockSpec((1, bn), lambda j: (0, j))],
        out_specs=pl.BlockSpec((MOD_ROWS, bn), lambda j: (0, j)),
        out_shape=jax.ShapeDtypeStruct((MOD_ROWS, n), _F32),
        compiler_params=_cparams("arbitrary"),
        name="modulation",
    )(cc, w_mod, b_mod)


def _rope(acc, cos_ref, sin_ref):
    cos = cos_ref[...]
    sin = sin_ref[...]
    lane = lax.broadcasted_iota(jnp.int32, (1, LANES), 1)
    first_half = (lane % (2 * ROT_FREQS)) < ROT_FREQS
    outs = []
    for s in range(acc.shape[1] // LANES):
        xs = acc[:, s * LANES:(s + 1) * LANES]
        partner = jnp.where(first_half,
                            pltpu.roll(xs, LANES - ROT_FREQS, 1),
                            pltpu.roll(xs, ROT_FREQS, 1))
        outs.append(xs * cos + partner * sin)
    return jnp.concatenate(outs, axis=1)


def _inproj_kernel(x_ref, mod_ref, g_ref, w_ref, cos_ref, sin_ref, o_ref, *, chunk_kinds):
    x = x_ref[0]
    shift = mod_ref[0, 0:1, :]
    scale = mod_ref[0, 1:2, :]
    h = (_rms(x) * g_ref[...]) * (1.0 + scale) + shift
    hb = h.astype(_BF)
    for j, kind in enumerate(chunk_kinds):
        cols = slice(j * PROJ_CW, (j + 1) * PROJ_CW)
        acc = jnp.dot(hb, w_ref[:, cols], preferred_element_type=_F32)
        if kind == "rope":
            acc = _rope(acc, cos_ref, sin_ref)
        elif kind == "sigmoid":
            acc = _sigmoid(acc)
        o_ref[0, :, cols] = acc.astype(_BF)


def _in_projection(x, mod3, mod_row_of_batch, g, w, cos_t, sin_t, chunk_kinds, tm):
    b, n, d = x.shape
    width = w.shape[1]
    assert width == len(chunk_kinds) * PROJ_CW and n % tm == 0
    return pl.pallas_call(
        functools.partial(_inproj_kernel, chunk_kinds=chunk_kinds),
        grid=(b, n // tm),
        in_specs=[pl.BlockSpec((1, tm, d), lambda bi, i: (bi, i, 0)),
                  pl.BlockSpec((1, 6, d), lambda bi, i: (mod_row_of_batch(bi), 0, 0)),
                  pl.BlockSpec((1, d), lambda bi, i: (0, 0)),
                  _const_spec((d, width)),
                  pl.BlockSpec((tm, LANES), lambda bi, i: (i, 0)),
                  pl.BlockSpec((tm, LANES), lambda bi, i: (i, 0))],
        out_specs=pl.BlockSpec((1, tm, width), lambda bi, i: (bi, i, 0)),
        out_shape=jax.ShapeDtypeStruct((b, n, width), _BF),
        compiler_params=_cparams("parallel", "arbitrary"),
        name="in_projection",
    )(x, mod3, g, w, cos_t, sin_t)


def _attn_kernel(lam_ref, q_ref, kl_ref, vl_ref, kc_ref, vc_ref, sg_ref, o_ref,
                 kcat_ref, vext_ref, s_even_ref, s_odd_ref, *, tq):
    nc, n = kc_ref.shape[1], kl_ref.shape[1]
    kcat_ref[0:nc, :] = kc_ref[0]
    kcat_ref[nc:nc + n, :] = kl_ref[0]
    vext_ref[0:nc, 0:HEAD_W] = vc_ref[0]
    vext_ref[nc:nc + n, 0:HEAD_W] = vl_ref[0]
    vext_ref[:, HEAD_W:] = jnp.ones((nc + n, HEAD_W), _BF)
    lane = lax.broadcasted_iota(jnp.int32, (1, HEAD_W), 1)
    map_lanes = (lane < HEAD_DIM, lane >= HEAD_DIM)
    nt = (((1,), (1,)), ((), ()))
    bufs = (s_even_ref, s_odd_ref)
    lam = lam_ref[0]
    post_scale = sg_ref[...] * (1.0 - LAM_INIT)

    def scores(j):
        q = q_ref[0, j * tq:(j + 1) * tq, :]
        for mp in range(2):
            qm = jnp.where(map_lanes[mp], q, jnp.zeros_like(q))
            bufs[j % 2][mp] = lax.dot_general(qm, kcat_ref[...], nt, preferred_element_type=_F32)

    def finish(j):
        outs = []
        for mp in range(2):
            s = bufs[j % 2][mp]
            e = jnp.exp2(s - jnp.max(s, axis=-1, keepdims=True)).astype(_BF)
            r = jnp.dot(e, vext_ref[...], preferred_element_type=_F32)
            outs.append(r[:, :HEAD_W] / r[:, HEAD_W:])
        heads = outs[0] - lam * outs[1]
        o_ref[0, j * tq:(j + 1) * tq, :] = (_rms(heads) * post_scale).astype(_BF)

    n_sub = n // tq
    scores(0)
    for j in range(n_sub):
        if j + 1 < n_sub:
            scores(j + 1)
        finish(j)


def _attention(lam, p, kvc, subln_g, tq):
    b, n, _ = p.shape
    nc = kvc.shape[1]
    qb, kb, vb = P_Q // HEAD_W, P_K // HEAD_W, P_V // HEAD_W
    seq = lambda blk: pl.BlockSpec((1, n, HEAD_W), lambda bi, h: (bi, 0, blk + h))
    return pl.pallas_call(
        functools.partial(_attn_kernel, tq=tq),
        grid=(b, N_HEADS),
        in_specs=[pl.BlockSpec(memory_space=pltpu.SMEM),
                  seq(qb), seq(kb), seq(vb),
                  pl.BlockSpec((1, nc, HEAD_W), lambda bi, h: (bi, 0, h)),
                  pl.BlockSpec((1, nc, HEAD_W), lambda bi, h: (bi, 0, N_HEADS + h)),
                  pl.BlockSpec((1, HEAD_W), lambda bi, h: (0, 0))],
        out_specs=seq(0),
        out_shape=jax.ShapeDtypeStruct((b, n, N_HEADS * HEAD_W), _BF),
        scratch_shapes=[pltpu.VMEM((nc + n, HEAD_W), _BF), pltpu.VMEM((nc + n, 2 * HEAD_W), _BF),
                        pltpu.VMEM((2, tq, nc + n), _F32), pltpu.VMEM((2, tq, nc + n), _F32)],
        compiler_params=_cparams("parallel", "arbitrary"),
        name="diff_attention",
    )(lam, p, p, p, kvc, kvc, subln_g)


def _fnet_kernel(f_ref, cs_ch_ref, cs_seq_ref, o_ref, xcs_ref, *, n):
    for g in range(F_GROUPS):
        cols = slice(g * F_GROUP_DIM, (g + 1) * F_GROUP_DIM)
        t = jnp.dot(f_ref[0, :, cols], cs_ch_ref[...], preferred_element_type=_F32)
        xcs_ref[0:n, cols] = t[:, :F_GROUP_DIM].astype(_BF)
        xcs_ref[n:2 * n, cols] = t[:, F_GROUP_DIM:].astype(_BF)
    ortho = 1.0 / math.sqrt(n * F_GROUP_DIM)
    for r in range(n // FNET_TR):
        rows = slice(r * FNET_TR, (r + 1) * FNET_TR)
        y = jnp.dot(cs_seq_ref[rows, :], xcs_ref[...], preferred_element_type=_F32)
        o_ref[0, rows, :] = (y * ortho).astype(_BF)


def _fnet(p, cs_ch, cs_seq):
    b, n, _ = p.shape
    return pl.pallas_call(
        functools.partial(_fnet_kernel, n=n),
        grid=(b,),
        in_specs=[pl.BlockSpec((1, n, F_W), lambda bi: (bi, 0, P_F // F_W)),
                  _const_spec((F_GROUP_DIM, 2 * F_GROUP_DIM)),
                  _const_spec((n, 2 * n))],
        out_specs=pl.BlockSpec((1, n, F_W), lambda bi: (bi, 0, 0)),
        out_shape=jax.ShapeDtypeStruct((b, n, F_W), _BF),
        scratch_shapes=[pltpu.VMEM((2 * n, F_W), _BF)],
        compiler_params=_cparams("arbitrary"),
        name="fnet_dft",
    )(p, cs_ch, cs_seq)


def _pack_bf16_pair(hi, lo):
    hi_bits = pltpu.bitcast(hi.astype(_BF).astype(_F32), jnp.uint32)
    lo_bits = pltpu.bitcast(lo.astype(_BF).astype(_F32), jnp.uint32)
    return hi_bits | lax.shift_right_logical(lo_bits, jnp.uint32(16))


def _unpack_bf16_pair(packed):
    hi = pltpu.bitcast(packed & jnp.uint32(0xFFFF0000), _F32)
    lo = pltpu.bitcast(lax.shift_left(packed, jnp.uint32(16)), _F32)
    return hi, lo


def _route(logits, carry):
    rows = logits.shape[0]
    lane = lax.broadcasted_iota(jnp.int32, logits.shape, 1)
    neg = jnp.float32(-jnp.inf)
    big = jnp.int32(ROUTER_W)

    def first_argmax(v):
        m = jnp.max(v, axis=-1, keepdims=True)
        idx = jnp.min(jnp.where(v == m, lane, big), axis=-1, keepdims=True)
        return m, idx

    lg = jnp.where((lane >= GROUP_LANE0) & (lane < GROUP_LANE0 + N_GROUPS), logits, neg)
    mg, ig = first_argmax(lg)
    w_grp = 1.0 / jnp.sum(jnp.exp(lg - mg), axis=-1, keepdims=True)
    g_sel = ig - GROUP_LANE0
    le = jnp.where((lane < N_EXPERTS)
                   & (jnp.right_shift(lane, LOG2_EXPERTS_PER_GROUP) == g_sel), logits, neg)
    v1, i1 = first_argmax(le)
    le2 = jnp.where(lane == i1, neg, le)
    v2, i2 = first_argmax(le2)
    e2 = jnp.exp(v2 - v1)
    w1 = w_grp / (1.0 + e2)
    w2 = w_grp * e2 / (1.0 + e2)

    first_is_lower = i1 < i2
    gate_a = jnp.where(first_is_lower, w1, w2)
    gate_b = jnp.where(first_is_lower, w2, w1)
    la = jnp.minimum(i1, i2) - g_sel * EXPERTS_PER_GROUP
    lb = jnp.maximum(i1, i2) - g_sel * EXPERTS_PER_GROUP
    pair = jnp.right_shift(la * (2 * EXPERTS_PER_GROUP - 1 - la), 1) + lb - la - 1
    cls = g_sel * PAIRS_PER_GROUP + pair

    onehot = (lane == cls).astype(_F32)
    r_i = lax.broadcasted_iota(jnp.int32, (rows, rows), 0)
    c_i = lax.broadcasted_iota(jnp.int32, (rows, rows), 1)
    earlier = (c_i < r_i).astype(_BF)
    before = jnp.dot(earlier, onehot.astype(_BF), preferred_element_type=_F32) + carry
    rank = jnp.sum(onehot * before, axis=-1, keepdims=True)
    new_carry = carry + jnp.sum(onehot, axis=0, keepdims=True)

    route = (jnp.where(lane == 0, gate_a, 0.0) + jnp.where(lane == 1, gate_b, 0.0)
             + jnp.where(lane == 2, cls.astype(_F32), 0.0) + jnp.where(lane == 3, rank, 0.0))
    return route, new_carry


def _merge_kernel(hd_ref, fo_ref, ga_ref, gf_ref, x_ref, mod_ref, g2n_ref, wao_ref, wfo_ref,
                  wo_ref, wrhl_ref, br_ref, x1_ref, rows_ref, rec_ref, counts_ref, carry_ref):
    @pl.when((pl.program_id(0) == 0) & (pl.program_id(1) == 0))
    def _():
        carry_ref[...] = jnp.zeros_like(carry_ref)

    a = jnp.dot(hd_ref[0], wao_ref[...], preferred_element_type=_F32)
    ff = jnp.dot(fo_ref[0], wfo_ref[...], preferred_element_type=_F32)
    y = ga_ref[0].astype(_F32) * a + gf_ref[0].astype(_F32) * ff
    mix = jnp.dot(y.astype(_BF), wo_ref[...], preferred_element_type=_F32)
    x1 = x_ref[0] + mod_ref[0, 2:3, :] * mix
    x1_ref[0] = x1
    h2 = (_rms(x1) * g2n_ref[...]) * (1.0 + mod_ref[0, 4:5, :]) + mod_ref[0, 3:4, :]
    h2_hi = h2.astype(_BF)
    h2_lo = (h2 - h2_hi.astype(_F32)).astype(_BF)
    hi_both = jnp.dot(h2_hi, wrhl_ref[...], preferred_element_type=_F32)
    logits = (hi_both[:, :ROUTER_W] + hi_both[:, ROUTER_W:]
              + jnp.dot(h2_lo, wrhl_ref[:, :ROUTER_W], preferred_element_type=_F32)) + br_ref[...]
    route, new_carry = _route(logits, carry_ref[...])
    carry_ref[...] = new_carry
    counts_ref[...] = new_carry
    rec_ref[0, 0] = route.T[0:SUBLANES, :]
    half = D_MODEL // 2
    tm = h2.shape[0]
    packed = _pack_bf16_pair(h2[:, :half], h2[:, half:])
    rows_ref[0, :, 0:H2_SLAB_ROWS, :] = packed.reshape(tm, H2_SLAB_ROWS, LANES)
    rows_ref[0, :, ROUTE_SLAB_ROW:ROUTE_SLAB_ROW + 1, :] = (
        pltpu.bitcast(route, jnp.uint32).reshape(tm, 1, LANES))
    rows_ref[0, :, ROUTE_SLAB_ROW + 1:SUBLANES, :] = jnp.zeros(
        (tm, SUBLANES - ROUTE_SLAB_ROW - 1, LANES), jnp.uint32)


def _merge(heads, four, p, x, mod3, norm2_g, w_ao, w_fo, w_o, wr_hilo, b_r, tm):
    b, n, d = x.shape
    tok = lambda w: pl.BlockSpec((1, tm, w), lambda bi, i: (bi, i, 0))
    return pl.pallas_call(
        _merge_kernel,
        grid=(b, n // tm),
        in_specs=[tok(d), tok(F_W),
                  pl.BlockSpec((1, tm, d), lambda bi, i: (bi, i, P_GA // D_MODEL)),
                  pl.BlockSpec((1, tm, d), lambda bi, i: (bi, i, P_GF // D_MODEL)),
                  tok(d),
                  pl.BlockSpec((1, 6, d), lambda bi, i: (bi, 0, 0)),
                  pl.BlockSpec((1, d), lambda bi, i: (0, 0)),
                  _const_spec((d, d)), _const_spec((F_W, d)), _const_spec((d, d)),
                  _const_spec((d, 2 * ROUTER_W)),
                  pl.BlockSpec((1, ROUTER_W), lambda bi, i: (0, 0))],
        out_specs=[tok(d),
                   pl.BlockSpec((1, tm, SUBLANES, LANES), lambda bi, i: (bi, i, 0, 0)),
                   pl.BlockSpec((1, 1, SUBLANES, tm), lambda bi, i: (bi, i, 0, 0)),
                   pl.BlockSpec((1, ROUTER_W), lambda bi, i: (0, 0))],
        out_shape=[jax.ShapeDtypeStruct((b, n, d), _F32),
                   jax.ShapeDtypeStruct((b, n, SUBLANES, LANES), jnp.uint32),
                   jax.ShapeDtypeStruct((b, n // tm, SUBLANES, tm), _F32),
                   jax.ShapeDtypeStruct((1, ROUTER_W), _F32)],
        scratch_shapes=[pltpu.VMEM((1, ROUTER_W), _F32)],
        compiler_params=_cparams("arbitrary", "arbitrary"),
        name="merge_router",
    )(heads, four, p, p, x, mod3, norm2_g, w_ao, w_fo, w_o, wr_hilo, b_r)


def _issue_slab_copies(n, make_copy, slot_of):
    def group(g, c):
        r0 = g * DMA_ISSUE_GROUP
        slots = [slot_of(r0 + k) for k in range(DMA_ISSUE_GROUP)]
        for k in range(DMA_ISSUE_GROUP):
            make_copy(r0 + k, slots[k]).start(priority=k % 2)
        return c

    lax.fori_loop(0, n // DMA_ISSUE_GROUP, group, 0)


def _dispatch_kernel(pos_ref, src_ref, zeros_ref, dst_ref, sem, *, tm):
    del zeros_ref
    base = pl.program_id(0) * tm
    _issue_slab_copies(
        tm,
        lambda r, slot: pltpu.make_async_copy(src_ref.at[r], dst_ref.at[slot], sem),
        lambda r: pos_ref[base + r])
    pltpu.make_async_copy(src_ref, dst_ref.at[pl.ds(0, tm)], sem).wait()


def _dispatch(pos, slabs, n_sorted, tm):
    t = slabs.shape[0]
    zeros = jnp.zeros((n_sorted,) + slabs.shape[1:], slabs.dtype)
    return pl.pallas_call(
        functools.partial(_dispatch_kernel, tm=tm),
        grid_spec=pltpu.PrefetchScalarGridSpec(
            num_scalar_prefetch=1,
            grid=(t // tm,),
            in_specs=[pl.BlockSpec((tm, SUBLANES, LANES), lambda i, pos: (i, 0, 0)),
                      pl.BlockSpec(memory_space=pl.ANY)],
            out_specs=pl.BlockSpec(memory_space=pl.ANY),
            scratch_shapes=[pltpu.SemaphoreType.DMA(())]),
        out_shape=jax.ShapeDtypeStruct(zeros.shape, slabs.dtype),
        input_output_aliases={2: 0},
        compiler_params=_cparams("arbitrary"),
        name="moe_dispatch",
    )(pos, slabs, zeros)


def _moe_kernel(ea_ref, eb_ref, valid_ref, slabs_ref, w1a_ref, w3a_ref, w2a_ref,
                w1b_ref, w3b_ref, w2b_ref, y_ref):
    del ea_ref, eb_ref
    i = pl.program_id(0)
    tm = slabs_ref.shape[0]

    @pl.when(valid_ref[i] != 0)
    def _():
        packed = slabs_ref[:, 0:H2_SLAB_ROWS, :].reshape(tm, H2_SLAB_ROWS * LANES)
        hi, lo = _unpack_bf16_pair(packed)
        t = jnp.concatenate([hi.astype(_BF), lo.astype(_BF)], axis=1)
        route = pltpu.bitcast(slabs_ref[:, ROUTE_SLAB_ROW, :], _F32)
        y = None
        for slot, (w1_ref, w3_ref, w2_ref) in enumerate(((w1a_ref, w3a_ref, w2a_ref),
                                                          (w1b_ref, w3b_ref, w2b_ref))):
            a = jnp.dot(t, w1_ref[0], preferred_element_type=_F32)
            u = jnp.dot(t, w3_ref[0], preferred_element_type=_F32)
            hid = ((a * _sigmoid(a)) * u * route[:, slot:slot + 1]).astype(_BF)
            part = jnp.dot(hid, w2_ref[0], preferred_element_type=_F32)
            y = part if y is None else y + part
        y_ref[...] = y.reshape(tm, SUBLANES, LANES)

    @pl.when(valid_ref[i] == 0)
    def _():
        y_ref[...] = jnp.zeros_like(y_ref)


def _moe(tile_ea, tile_eb, tile_valid, slabs_sorted, w1, w3, w2, tm):
    n_sorted = slabs_sorted.shape[0]
    d, hdn = w1.shape[1], w1.shape[2]
    assert d == SUBLANES * LANES
    pick = lambda which, a, b_: pl.BlockSpec(
        (1, a, b_), lambda i, ea, eb, valid: ((ea, eb)[which][i], 0, 0))
    slab_tile = pl.BlockSpec((tm, SUBLANES, LANES), lambda i, ea, eb, valid: (i, 0, 0))
    return pl.pallas_call(
        _moe_kernel,
        grid_spec=pltpu.PrefetchScalarGridSpec(
            num_scalar_prefetch=3,
            grid=(n_sorted // tm,),
            in_specs=[slab_tile,
                      pick(0, d, hdn), pick(0, d, hdn), pick(0, hdn, d),
                      pick(1, d, hdn), pick(1, d, hdn), pick(1, hdn, d)],
            out_specs=slab_tile),
        out_shape=jax.ShapeDtypeStruct((n_sorted, SUBLANES, LANES), _F32),
        compiler_params=_cparams("arbitrary"),
        name="moe_experts",
    )(tile_ea, tile_eb, tile_valid, slabs_sorted, w1, w3, w2, w1, w3, w2)


def _combine_kernel(pos_ref, y_hbm_ref, x1_ref, mod_ref, fg_ref, o_ref, ybuf_ref, sems, *, tm):
    i = pl.program_id(0)
    n_steps = pl.num_programs(0)

    def issue(tile, slot):
        _issue_slab_copies(
            tm,
            lambda r, src: pltpu.make_async_copy(y_hbm_ref.at[src], ybuf_ref.at[slot, r],
                                                 sems.at[slot]),
            lambda r: pos_ref[tile * tm + r])

    @pl.when(i == 0)
    def _():
        issue(0, 0)

    @pl.when(i + 1 < n_steps)
    def _():
        issue(i + 1, (i + 1) % 2)

    slot = i % 2
    pltpu.make_async_copy(y_hbm_ref.at[pl.ds(0, tm)], ybuf_ref.at[slot], sems.at[slot]).wait()
    y = ybuf_ref[slot].reshape(tm, SUBLANES * LANES)
    xo = x1_ref[...] + mod_ref[0, 5:6, :] * y
    o_ref[...] = _rms(xo) * fg_ref[...]


def _combine(pos, y_sorted, x1, mod3, final_g, seq, tm):
    t, d = x1.shape
    tiles_per_batch = seq // tm
    return pl.pallas_call(
        functools.partial(_combine_kernel, tm=tm),
        grid_spec=pltpu.PrefetchScalarGridSpec(
            num_scalar_prefetch=1,
            grid=(t // tm,),
            in_specs=[pl.BlockSpec(memory_space=pl.ANY),
                      pl.BlockSpec((tm, d), lambda i, pos: (i, 0)),
                      pl.BlockSpec((1, 6, d), lambda i, pos: (i // tiles_per_batch, 0, 0)),
                      pl.BlockSpec((1, d), lambda i, pos: (0, 0))],
            out_specs=pl.BlockSpec((tm, d), lambda i, pos: (i, 0)),
            scratch_shapes=[pltpu.VMEM((2, tm, SUBLANES, LANES), _F32),
                            pltpu.SemaphoreType.DMA((2,))]),
        out_shape=jax.ShapeDtypeStruct((t, d), _F32),
        compiler_params=_cparams("arbitrary"),
        name="moe_combine",
    )(pos, y_sorted, x1, mod3, final_g)


def _routing_tables(route_cls, route_rank, counts, n_tokens, tm):
    sizes = ((counts + (tm - 1)) // tm) * tm
    ends = jnp.cumsum(sizes)
    starts = ends - sizes
    n_tiles = n_tokens // tm + N_CLASSES
    tile_start = jnp.arange(n_tiles, dtype=jnp.int32) * tm
    pos = route_rank
    tile_cls = jnp.zeros((n_tiles,), jnp.int32)
    for c in range(N_CLASSES):
        pos = pos + jnp.where(route_cls == c, starts[c], 0)
        tile_cls = tile_cls + (tile_start >= ends[c]).astype(jnp.int32)
    tile_cls = jnp.minimum(tile_cls, N_CLASSES - 1)
    tile_valid = (tile_start < ends[-1]).astype(jnp.int32)
    pairs = [(a, b_) for a in range(EXPERTS_PER_GROUP) for b_ in range(a + 1, EXPERTS_PER_GROUP)]
    grp, pair = tile_cls // PAIRS_PER_GROUP, tile_cls % PAIRS_PER_GROUP
    tile_ea = grp * EXPERTS_PER_GROUP
    tile_eb = grp * EXPERTS_PER_GROUP
    for k, (a, b_) in enumerate(pairs):
        tile_ea = tile_ea + jnp.where(pair == k, a, 0)
        tile_eb = tile_eb + jnp.where(pair == k, b_, 0)
    return (pos.astype(jnp.int32), tile_ea.astype(jnp.int32), tile_eb.astype(jnp.int32),
            tile_valid, n_tiles)


def _rope_tables(n):
    inv = (1.0 / (ROPE_BASE ** (np.arange(ROT_FREQS, dtype=np.float32) / ROT_FREQS))).astype(np.float32)
    pos = np.arange(n)
    row = (pos // GRID_W).astype(np.float32)[:, None] * inv[None, :]
    col = (pos % GRID_W).astype(np.float32)[:, None] * inv[None, :]
    cos64 = np.concatenate([np.cos(row), np.cos(row), np.cos(col), np.cos(col)], axis=1)
    sin64 = np.concatenate([-np.sin(row), np.sin(row), -np.sin(col), np.sin(col)], axis=1)
    tile = lambda a: np.tile(a.astype(np.float32), (1, LANES // HEAD_DIM))
    return jnp.asarray(tile(cos64)), jnp.asarray(tile(sin64))


def _dft_cos_sin(n):
    k = np.arange(n, dtype=np.int64)
    ang = (2.0 * np.pi / n) * ((k[:, None] * k[None, :]) % n).astype(np.float64)
    return np.cos(ang), np.sin(ang)


def _dft_tables(n):
    c_ch, s_ch = _dft_cos_sin(F_GROUP_DIM)
    c_seq, s_seq = _dft_cos_sin(n)
    cs_ch = np.concatenate([c_ch, s_ch], axis=1).astype(np.float32)
    cs_seq = np.concatenate([c_seq, -s_seq], axis=1).astype(np.float32)
    return jnp.asarray(cs_ch.astype(_BF)), jnp.asarray(cs_seq.astype(_BF))


def kernel(x, c, ctx, c_ctx, w_mod, b_mod, norm1_g, norm2_g, w_in, lam_q1, lam_k1, lam_q2, lam_k2,
           subln_g, w_attn_out, w_four_out, w_out, w_router_group, b_router_group, w_router_expert,
           b_router_expert, w_exp_gate, w_exp_up, w_exp_down, final_g):
    b, n, d = x.shape
    assert w_mod.shape[0] == 1, "depth-1 stack"
    assert b + 1 <= MOD_ROWS

    cc = jnp.concatenate([c, c_ctx[None, :], jnp.zeros((MOD_ROWS - b - 1, d), _F32)], axis=0)
    mod3 = _modulation(cc, w_mod[0], b_mod).reshape(MOD_ROWS, 6, d)

    lam = (jnp.exp(jnp.sum(lam_q1[0] * lam_k1[0])) - jnp.exp(jnp.sum(lam_q2[0] * lam_k2[0]))
           + LAM_INIT).reshape(1).astype(_F32)

    w = w_in[0]
    scale = HEAD_DIM ** -0.5 * math.log2(math.e)
    w_lat = jnp.concatenate([w[:, REF_Q:REF_K] * scale, w[:, REF_K:REF_F],
                             w[:, REF_GA:REF_END], w[:, REF_F:REF_GA]], axis=1).astype(_BF)
    w_ctx = w[:, REF_K:REF_F].astype(_BF)
    lat_kinds = ("rope",) * ((P_V - P_Q) // PROJ_CW) + ("plain",) * ((P_GA - P_V) // PROJ_CW) \
        + ("sigmoid",) * ((P_F - P_GA) // PROJ_CW) + ("plain",) * ((P_W - P_F) // PROJ_CW)
    ctx_kinds = ("plain",) * (w_ctx.shape[1] // PROJ_CW)

    cos_t, sin_t = _rope_tables(n)
    p = _in_projection(x, mod3, lambda bi: bi, norm1_g, w_lat, cos_t, sin_t, lat_kinds, PROJ_TM)
    kvc = _in_projection(ctx, mod3, lambda bi: b, norm1_g, w_ctx, cos_t, sin_t, ctx_kinds,
                         ctx.shape[1])

    heads = _attention(lam, p, kvc, subln_g, ATTN_TQ)
    cs_ch, cs_seq = _dft_tables(n)
    four = _fnet(p, cs_ch, cs_seq)

    w_r = jnp.concatenate([w_router_expert[0], w_router_group[0],
                           jnp.zeros((d, ROUTER_W - N_EXPERTS - N_GROUPS), _F32)], axis=1)
    b_r = jnp.concatenate([b_router_expert[0], b_router_group[0],
                           jnp.zeros((ROUTER_W - N_EXPERTS - N_GROUPS,), _F32)])[None, :]
    wr_hi = w_r.astype(_BF)
    wr_lo = (w_r - wr_hi.astype(_F32)).astype(_BF)
    x1, rows, rec, counts = _merge(heads, four, p, x, mod3, norm2_g, w_attn_out[0].astype(_BF),
                                   w_four_out[0].astype(_BF), w_out[0].astype(_BF),
                                   jnp.concatenate([wr_hi, wr_lo], axis=1), b_r, MERGE_TM)

    t = b * n
    slabs = rows.reshape(t, SUBLANES, LANES)
    rec = rec.reshape(t // MERGE_TM, SUBLANES, MERGE_TM)
    pos, tile_ea, tile_eb, tile_valid, n_tiles = _routing_tables(
        rec[:, 2, :].reshape(t).astype(jnp.int32), rec[:, 3, :].reshape(t).astype(jnp.int32),
        counts[0, :N_CLASSES].astype(jnp.int32), t, MOE_TM)

    slabs_sorted = _dispatch(pos, slabs, n_tiles * MOE_TM, DISPATCH_TM)
    y_sorted = _moe(tile_ea, tile_eb, tile_valid, slabs_sorted, w_exp_gate[0].astype(_BF),
                    w_exp_up[0].astype(_BF), w_exp_down[0].astype(_BF), MOE_TM)
    out = _combine(pos, y_sorted, x1.reshape(t, d), mod3, final_g[None, :], n, COMBINE_TM)
    return out.reshape(b, n, d)
```

```python
import functools
import math

import jax
import jax.numpy as jnp
import numpy as np
from jax import lax
from jax.experimental import pallas as pl
from jax.experimental.pallas import tpu as pltpu

D_MODEL = 1024
GRID_W = 64
EPS = 1e-6
N_HEADS = 8
HEAD_DIM = 64
HEAD_W = 2 * HEAD_DIM
ROT_FREQS = HEAD_DIM // 4
ROPE_BASE = 10000.0
F_GROUPS = 4
F_GROUP_DIM = 128
F_W = F_GROUPS * F_GROUP_DIM
N_GROUPS = 4
EXPERTS_PER_GROUP = 4
LOG2_EXPERTS_PER_GROUP = 2
N_EXPERTS = N_GROUPS * EXPERTS_PER_GROUP
PAIRS_PER_GROUP = EXPERTS_PER_GROUP * (EXPERTS_PER_GROUP - 1) // 2
N_CLASSES = N_GROUPS * PAIRS_PER_GROUP
EXPERT_HIDDEN = 512
LAM_INIT = 0.8 - 0.6 * math.exp(-0.3 * 0)

REF_Q, REF_K, REF_V, REF_F, REF_GA, REF_GF, REF_END = 0, 1024, 2048, 3072, 3584, 4608, 5632
P_Q, P_K, P_V, P_GA, P_GF, P_F, P_W = 0, 1024, 2048, 3072, 4096, 5120, 5632

LANES = 128
SUBLANES = 8
VMEM_LIMIT_BYTES = 56 * 1024 * 1024

PROJ_TM = 1024
PROJ_CW = 512
ATTN_TQ = 256
FNET_TR = 512
MERGE_TM = 512
MOE_TM = 256
COMBINE_TM = 512
DMA_ISSUE_GROUP = 8
MOD_ROWS = 40
ROUTER_W = LANES
GROUP_LANE0 = N_EXPERTS
H2_SLAB_ROWS = D_MODEL // 2 // LANES
ROUTE_SLAB_ROW = H2_SLAB_ROWS

_BF = jnp.bfloat16
_F32 = jnp.float32


def _cparams(*sem):
    return pltpu.CompilerParams(dimension_semantics=sem, vmem_limit_bytes=VMEM_LIMIT_BYTES)


def _const_spec(shape):
    return pl.BlockSpec(shape, lambda *_: (0,) * len(shape), pipeline_mode=pl.Buffered(1))


def _rms(x):
    return x * lax.rsqrt(jnp.mean(x * x, axis=-1, keepdims=True) + EPS)


def _sigmoid(x):
    return 1.0 / (1.0 + jnp.exp(-x))


def _mod_kernel(cc_ref, w_ref, b_ref, o_ref):
    cc = cc_ref[...]
    s = cc * _sigmoid(cc)
    o_ref[...] = jnp.dot(s, w_ref[...], preferred_element_type=_F32,
                         precision=lax.Precision.HIGHEST) + b_ref[...]


def _modulation(cc, w_mod, b_mod):
    n = w_mod.shape[1]
    bn = D_MODEL
    return pl.pallas_call(
        _mod_kernel,
        grid=(n // bn,),
        in_specs=[pl.BlockSpec((MOD_ROWS, D_MODEL), lambda j: (0, 0)),
                  pl.BlockSpec((D_MODEL, bn), lambda j: (0, j)),
                  pl.BlockSpec((1, bn), lambda j: (0, j))],
        out_specs=pl.BlockSpec((MOD_ROWS, bn), lambda j: (0, j)),
        out_shape=jax.ShapeDtypeStruct((MOD_ROWS, n), _F32),
        compiler_params=_cparams("arbitrary"),
        name="modulation",
    )(cc, w_mod, b_mod)


def _rope(acc, cos_ref, sin_ref):
    cos = cos_ref[...]
    sin = sin_ref[...]
    lane = lax.broadcasted_iota(jnp.int32, (1, LANES), 1)
    first_half = (lane % (2 * ROT_FREQS)) < ROT_FREQS
    outs = []
    for s in range(acc.shape[1] // LANES):
        xs = acc[:, s * LANES:(s + 1) * LANES]
        partner = jnp.where(first_half,
                            pltpu.roll(xs, LANES - ROT_FREQS, 1),
                            pltpu.roll(xs, ROT_FREQS, 1))
        outs.append(xs * cos + partner * sin)
    return jnp.concatenate(outs, axis=1)


def _inproj_kernel(x_ref, mod_ref, g_ref, w_ref, cos_ref, sin_ref, o_ref, *, chunk_kinds):
    x = x_ref[0]
    shift = mod_ref[0, 0:1, :]
    scale = mod_ref[0, 1:2, :]
    h = (_rms(x) * g_ref[...]) * (1.0 + scale) + shift
    hb = h.astype(_BF)
    for j, kind in enumerate(chunk_kinds):
        cols = slice(j * PROJ_CW, (j + 1) * PROJ_CW)
        acc = jnp.dot(hb, w_ref[:, cols], preferred_element_type=_F32)
        if kind == "rope":
            acc = _rope(acc, cos_ref, sin_ref)
        elif kind == "sigmoid":
            acc = _sigmoid(acc)
        o_ref[0, :, cols] = acc.astype(_BF)


def _in_projection(x, mod3, mod_row_of_batch, g, w, cos_t, sin_t, chunk_kinds, tm):
    b, n, d = x.shape
    width = w.shape[1]
    assert width == len(chunk_kinds) * PROJ_CW and n % tm == 0
    return pl.pallas_call(
        functools.partial(_inproj_kernel, chunk_kinds=chunk_kinds),
        grid=(b, n // tm),
        in_specs=[pl.BlockSpec((1, tm, d), lambda bi, i: (bi, i, 0)),
                  pl.BlockSpec((1, 6, d), lambda bi, i: (mod_row_of_batch(bi), 0, 0)),
                  pl.BlockSpec((1, d), lambda bi, i: (0, 0)),
                  _const_spec((d, width)),
                  pl.BlockSpec((tm, LANES), lambda bi, i: (i, 0)),
                  pl.BlockSpec((tm, LANES), lambda bi, i: (i, 0))],
        out_specs=pl.BlockSpec((1, tm, width), lambda bi, i: (bi, i, 0)),
        out_shape=jax.ShapeDtypeStruct((b, n, width), _BF),
        compiler_params=_cparams("parallel", "arbitrary"),
        name="in_projection",
    )(x, mod3, g, w, cos_t, sin_t)


def _attn_kernel(lam_ref, q_ref, kl_ref, vl_ref, kc_ref, vc_ref, sg_ref, o_ref,
                 kcat_ref, vext_ref, s_even_ref, s_odd_ref, *, tq):
    nc, n = kc_ref.shape[1], kl_ref.shape[1]
    kcat_ref[0:nc, :] = kc_ref[0]
    kcat_ref[nc:nc + n, :] = kl_ref[0]
    vext_ref[0:nc, 0:HEAD_W] = vc_ref[0]
    vext_ref[nc:nc + n, 0:HEAD_W] = vl_ref[0]
    vext_ref[:, HEAD_W:] = jnp.ones((nc + n, HEAD_W), _BF)
    lane = lax.broadcasted_iota(jnp.int32, (1, HEAD_W), 1)
    map_lanes = (lane < HEAD_DIM, lane >= HEAD_DIM)
    nt = (((1,), (1,)), ((), ()))
    bufs = (s_even_ref, s_odd_ref)
    lam = lam_ref[0]
    post_scale = sg_ref[...] * (1.0 - LAM_INIT)

    def scores(j):
        q = q_ref[0, j * tq:(j + 1) * tq, :]
        for mp in range(2):
            qm = jnp.where(map_lanes[mp], q, jnp.zeros_like(q))
            bufs[j % 2][mp] = lax.dot_general(qm, kcat_ref[...], nt, preferred_element_type=_F32)

    def finish(j):
        outs = []
        for mp in range(2):
            s = bufs[j % 2][mp]
            e = jnp.exp2(s - jnp.max(s, axis=-1, keepdims=True)).astype(_BF)
            r = jnp.dot(e, vext_ref[...], preferred_element_type=_F32)
            outs.append(r[:, :HEAD_W] / r[:, HEAD_W:])
        heads = outs[0] - lam * outs[1]
        o_ref[0, j * tq:(j + 1) * tq, :] = (_rms(heads) * post_scale).astype(_BF)

    n_sub = n // tq
    scores(0)
    for j in range(n_sub):
        if j + 1 < n_sub:
            scores(j + 1)
        finish(j)


def _attention(lam, p, kvc, subln_g, tq):
    b, n, _ = p.shape
    nc = kvc.shape[1]
    qb, kb, vb = P_Q // HEAD_W, P_K // HEAD_W, P_V // HEAD_W
    seq = lambda blk: pl.BlockSpec((1, n, HEAD_W), lambda bi, h: (bi, 0, blk + h))
    return pl.pallas_call(
        functools.partial(_attn_kernel, tq=tq),
        grid=(b, N_HEADS),
        in_specs=[pl.BlockSpec(memory_space=pltpu.SMEM),
                  seq(qb), seq(kb), seq(vb),
                  pl.BlockSpec((1, nc, HEAD_W), lambda bi, h: (bi, 0, h)),
                  pl.BlockSpec((1, nc, HEAD_W), lambda bi, h: (bi, 0, N_HEADS + h)),
                  pl.BlockSpec((1, HEAD_W), lambda bi, h: (0, 0))],
        out_specs=seq(0),
        out_shape=jax.ShapeDtypeStruct((b, n, N_HEADS * HEAD_W), _BF),
        scratch_shapes=[pltpu.VMEM((nc + n, HEAD_W), _BF), pltpu.VMEM((nc + n, 2 * HEAD_W), _BF),
                        pltpu.VMEM((2, tq, nc + n), _F32), pltpu.VMEM((2, tq, nc + n), _F32)],
        compiler_params=_cparams("parallel", "arbitrary"),
        name="diff_attention",
    )(lam, p, p, p, kvc, kvc, subln_g)


def _fnet_kernel(f_ref, cs_ch_ref, cs_seq_ref, o_ref, xcs_ref, *, n):
    for g in range(F_GROUPS):
        cols = slice(g * F_GROUP_DIM, (g + 1) * F_GROUP_DIM)
        t = jnp.dot(f_ref[0, :, cols], cs_ch_ref[...], preferred_element_type=_F32)
        xcs_ref[0:n, cols] = t[:, :F_GROUP_DIM].astype(_BF)
        xcs_ref[n:2 * n, cols] = t[:, F_GROUP_DIM:].astype(_BF)
    ortho = 1.0 / math.sqrt(n * F_GROUP_DIM)
    for r in range(n // FNET_TR):
        rows = slice(r * FNET_TR, (r + 1) * FNET_TR)
        y = jnp.dot(cs_seq_ref[rows, :], xcs_ref[...], preferred_element_type=_F32)
        o_ref[0, rows, :] = (y * ortho).astype(_BF)


def _fnet(p, cs_ch, cs_seq):
    b, n, _ = p.shape
    return pl.pallas_call(
        functools.partial(_fnet_kernel, n=n),
        grid=(b,),
        in_specs=[pl.BlockSpec((1, n, F_W), lambda bi: (bi, 0, P_F // F_W)),
                  _const_spec((F_GROUP_DIM, 2 * F_GROUP_DIM)),
                  _const_spec((n, 2 * n))],
        out_specs=pl.BlockSpec((1, n, F_W), lambda bi: (bi, 0, 0)),
        out_shape=jax.ShapeDtypeStruct((b, n, F_W), _BF),
        scratch_shapes=[pltpu.VMEM((2 * n, F_W), _BF)],
        compiler_params=_cparams("arbitrary"),
        name="fnet_dft",
    )(p, cs_ch, cs_seq)


def _pack_bf16_pair(hi, lo):
    hi_bits = pltpu.bitcast(hi.astype(_BF).astype(_F32), jnp.uint32)
    lo_bits = pltpu.bitcast(lo.astype(_BF).astype(_F32), jnp.uint32)
    return hi_bits | lax.shift_right_logical(lo_bits, jnp.uint32(16))


def _unpack_bf16_pair(packed):
    hi = pltpu.bitcast(packed & jnp.uint32(0xFFFF0000), _F32)
    lo = pltpu.bitcast(lax.shift_left(packed, jnp.uint32(16)), _F32)
    return hi, lo


def _route(logits, carry):
    rows = logits.shape[0]
    lane = lax.broadcasted_iota(jnp.int32, logits.shape, 1)
    neg = jnp.float32(-jnp.inf)
    big = jnp.int32(ROUTER_W)

    def first_argmax(v):
        m = jnp.max(v, axis=-1, keepdims=True)
        idx = jnp.min(jnp.where(v == m, lane, big), axis=-1, keepdims=True)
        return m, idx

    lg = jnp.where((lane >= GROUP_LANE0) & (lane < GROUP_LANE0 + N_GROUPS), logits, neg)
    mg, ig = first_argmax(lg)
    w_grp = 1.0 / jnp.sum(jnp.exp(lg - mg), axis=-1, keepdims=True)
    g_sel = ig - GROUP_LANE0
    le = jnp.where((lane < N_EXPERTS)
                   & (jnp.right_shift(lane, LOG2_EXPERTS_PER_GROUP) == g_sel), logits, neg)
    v1, i1 = first_argmax(le)
    le2 = jnp.where(lane == i1, neg, le)
    v2, i2 = first_argmax(le2)
    e2 = jnp.exp(v2 - v1)
    w1 = w_grp / (1.0 + e2)
    w2 = w_grp * e2 / (1.0 + e2)

    first_is_lower = i1 < i2
    gate_a = jnp.where(first_is_lower, w1, w2)
    gate_b = jnp.where(first_is_lower, w2, w1)
    la = jnp.minimum(i1, i2) - g_sel * EXPERTS_PER_GROUP
    lb = jnp.maximum(i1, i2) - g_sel * EXPERTS_PER_GROUP
    pair = jnp.right_shift(la * (2 * EXPERTS_PER_GROUP - 1 - la), 1) + lb - la - 1
    cls = g_sel * PAIRS_PER_GROUP + pair

    onehot = (lane == cls).astype(_F32)
    r_i = lax.broadcasted_iota(jnp.int32, (rows, rows), 0)
    c_i = lax.broadcasted_iota(jnp.int32, (rows, rows), 1)
    earlier = (c_i < r_i).astype(_BF)
    before = jnp.dot(earlier, onehot.astype(_BF), preferred_element_type=_F32) + carry
    rank = jnp.sum(onehot * before, axis=-1, keepdims=True)
    new_carry = carry + jnp.sum(onehot, axis=0, keepdims=True)

    route = (jnp.where(lane == 0, gate_a, 0.0) + jnp.where(lane == 1, gate_b, 0.0)
             + jnp.where(lane == 2, cls.astype(_F32), 0.0) + jnp.where(lane == 3, rank, 0.0))
    return route, new_carry


def _merge_kernel(hd_ref, fo_ref, ga_ref, gf_ref, x_ref, mod_ref, g2n_ref, wao_ref, wfo_ref,
                  wo_ref, wrhl_ref, br_ref, x1_ref, rows_ref, rec_ref, counts_ref, carry_ref):
    @pl.when((pl.program_id(0) == 0) & (pl.program_id(1) == 0))
    def _():
        carry_ref[...] = jnp.zeros_like(carry_ref)

    a = jnp.dot(hd_ref[0], wao_ref[...], preferred_element_type=_F32)
    ff = jnp.dot(fo_ref[0], wfo_ref[...], preferred_element_type=_F32)
    y = ga_ref[0].astype(_F32) * a + gf_ref[0].astype(_F32) * ff
    mix = jnp.dot(y.astype(_BF), wo_ref[...], preferred_element_type=_F32)
    x1 = x_ref[0] + mod_ref[0, 2:3, :] * mix
    x1_ref[0] = x1
    h2 = (_rms(x1) * g2n_ref[...]) * (1.0 + mod_ref[0, 4:5, :]) + mod_ref[0, 3:4, :]
    h2_hi = h2.astype(_BF)
    h2_lo = (h2 - h2_hi.astype(_F32)).astype(_BF)
    hi_both = jnp.dot(h2_hi, wrhl_ref[...], preferred_element_type=_F32)
    logits = (hi_both[:, :ROUTER_W] + hi_both[:, ROUTER_W:]
              + jnp.dot(h2_lo, wrhl_ref[:, :ROUTER_W], preferred_element_type=_F32)) + br_ref[...]
    route, new_carry = _route(logits, carry_ref[...])
    carry_ref[...] = new_carry
    counts_ref[...] = new_carry
    rec_ref[0, 0] = route.T[0:SUBLANES, :]
    half = D_MODEL // 2
    tm = h2.shape[0]
    packed = _pack_bf16_pair(h2[:, :half], h2[:, half:])
    rows_ref[0, :, 0:H2_SLAB_ROWS, :] = packed.reshape(tm, H2_SLAB_ROWS, LANES)
    rows_ref[0, :, ROUTE_SLAB_ROW:ROUTE_SLAB_ROW + 1, :] = (
        pltpu.bitcast(route, jnp.uint32).reshape(tm, 1, LANES))
    rows_ref[0, :, ROUTE_SLAB_ROW + 1:SUBLANES, :] = jnp.zeros(
        (tm, SUBLANES - ROUTE_SLAB_ROW - 1, LANES), jnp.uint32)


def _merge(heads, four, p, x, mod3, norm2_g, w_ao, w_fo, w_o, wr_hilo, b_r, tm):
    b, n, d = x.shape
    tok = lambda w: pl.BlockSpec((1, tm, w), lambda bi, i: (bi, i, 0))
    return pl.pallas_call(
        _merge_kernel,
        grid=(b, n // tm),
        in_specs=[tok(d), tok(F_W),
                  pl.BlockSpec((1, tm, d), lambda bi, i: (bi, i, P_GA // D_MODEL)),
                  pl.BlockSpec((1, tm, d), lambda bi, i: (bi, i, P_GF // D_MODEL)),
                  tok(d),
                  pl.BlockSpec((1, 6, d), lambda bi, i: (bi, 0, 0)),
                  pl.BlockSpec((1, d), lambda bi, i: (0, 0)),
                  _const_spec((d, d)), _const_spec((F_W, d)), _const_spec((d, d)),
                  _const_spec((d, 2 * ROUTER_W)),
                  pl.BlockSpec((1, ROUTER_W), lambda bi, i: (0, 0))],
        out_specs=[tok(d),
                   pl.BlockSpec((1, tm, SUBLANES, LANES), lambda bi, i: (bi, i, 0, 0)),
                   pl.BlockSpec((1, 1, SUBLANES, tm), lambda bi, i: (bi, i, 0, 0)),
                   pl.BlockSpec((1, ROUTER_W), lambda bi, i: (0, 0))],
        out_shape=[jax.ShapeDtypeStruct((b, n, d), _F32),
                   jax.ShapeDtypeStruct((b, n, SUBLANES, LANES), jnp.uint32),
                   jax.ShapeDtypeStruct((b, n // tm, SUBLANES, tm), _F32),
                   jax.ShapeDtypeStruct((1, ROUTER_W), _F32)],
        scratch_shapes=[pltpu.VMEM((1, ROUTER_W), _F32)],
        compiler_params=_cparams("arbitrary", "arbitrary"),
        name="merge_router",
    )(heads, four, p, p, x, mod3, norm2_g, w_ao, w_fo, w_o, wr_hilo, b_r)


def _issue_slab_copies(n, make_copy, slot_of):
    def group(g, c):
        r0 = g * DMA_ISSUE_GROUP
        slots = [slot_of(r0 + k) for k in range(DMA_ISSUE_GROUP)]
        for k in range(DMA_ISSUE_GROUP):
            make_copy(r0 + k, slots[k]).start(priority=k % 2)
        return c

    lax.fori_loop(0, n // DMA_ISSUE_GROUP, group, 0)


def _moe_kernel(ea_ref, eb_ref, valid_ref, src_ref, slabs_hbm_ref, w1a_ref, w3a_ref, w2a_ref,
                w1b_ref, w3b_ref, w2b_ref, y_ref, gbuf_ref, sems):
    del ea_ref, eb_ref
    i = pl.program_id(0)
    n_steps = pl.num_programs(0)
    tm = y_ref.shape[0]

    def issue(tile, slot):
        _issue_slab_copies(
            tm,
            lambda r, src: pltpu.make_async_copy(slabs_hbm_ref.at[src], gbuf_ref.at[slot, r],
                                                 sems.at[slot]),
            lambda r: src_ref[tile * tm + r])

    def wait(slot):
        pltpu.make_async_copy(slabs_hbm_ref.at[pl.ds(0, tm)], gbuf_ref.at[slot],
                              sems.at[slot]).wait()

    @pl.when(i == 0)
    def _():
        issue(0, 0)

    @pl.when(i + 1 < n_steps)
    def _():
        issue(i + 1, (i + 1) % 2)

    slot = i % 2
    wait(slot)

    @pl.when(valid_ref[i] != 0)
    def _():
        slabs = gbuf_ref[slot]
        packed = slabs[:, 0:H2_SLAB_ROWS, :].reshape(tm, H2_SLAB_ROWS * LANES)
        hi, lo = _unpack_bf16_pair(packed)
        t = jnp.concatenate([hi.astype(_BF), lo.astype(_BF)], axis=1)
        route = pltpu.bitcast(slabs[:, ROUTE_SLAB_ROW, :], _F32)
        y = None
        for k, (w1_ref, w3_ref, w2_ref) in enumerate(((w1a_ref, w3a_ref, w2a_ref),
                                                       (w1b_ref, w3b_ref, w2b_ref))):
            a = jnp.dot(t, w1_ref[0], preferred_element_type=_F32)
            u = jnp.dot(t, w3_ref[0], preferred_element_type=_F32)
            hid = ((a * _sigmoid(a)) * u * route[:, k:k + 1]).astype(_BF)
            part = jnp.dot(hid, w2_ref[0], preferred_element_type=_F32)
            y = part if y is None else y + part
        y_ref[...] = y.reshape(tm, SUBLANES, LANES)

    @pl.when(valid_ref[i] == 0)
    def _():
        y_ref[...] = jnp.zeros_like(y_ref)


def _moe(tile_ea, tile_eb, tile_valid, src_of_slot, slabs, w1, w3, w2, tm):
    n_sorted = src_of_slot.shape[0]
    d, hdn = w1.shape[1], w1.shape[2]
    assert d == SUBLANES * LANES
    pick = lambda which, a, b_: pl.BlockSpec(
        (1, a, b_), lambda i, ea, eb, valid, src: ((ea, eb)[which][i], 0, 0))
    return pl.pallas_call(
        _moe_kernel,
        grid_spec=pltpu.PrefetchScalarGridSpec(
            num_scalar_prefetch=4,
            grid=(n_sorted // tm,),
            in_specs=[pl.BlockSpec(memory_space=pl.ANY),
                      pick(0, d, hdn), pick(0, d, hdn), pick(0, hdn, d),
                      pick(1, d, hdn), pick(1, d, hdn), pick(1, hdn, d)],
            out_specs=pl.BlockSpec((tm, SUBLANES, LANES), lambda i, ea, eb, valid, src: (i, 0, 0)),
            scratch_shapes=[pltpu.VMEM((2, tm, SUBLANES, LANES), jnp.uint32),
                            pltpu.SemaphoreType.DMA((2,))]),
        out_shape=jax.ShapeDtypeStruct((n_sorted, SUBLANES, LANES), _F32),
        compiler_params=_cparams("arbitrary"),
        name="moe_experts",
    )(tile_ea, tile_eb, tile_valid, src_of_slot, slabs, w1, w3, w2, w1, w3, w2)


def _combine_kernel(pos_ref, y_hbm_ref, x1_ref, mod_ref, fg_ref, o_ref, ybuf_ref, sems, *, tm):
    i = pl.program_id(0)
    n_steps = pl.num_programs(0)

    def issue(tile, slot):
        _issue_slab_copies(
            tm,
            lambda r, src: pltpu.make_async_copy(y_hbm_ref.at[src], ybuf_ref.at[slot, r],
                                                 sems.at[slot]),
            lambda r: pos_ref[tile * tm + r])

    @pl.when(i == 0)
    def _():
        issue(0, 0)

    @pl.when(i + 1 < n_steps)
    def _():
        issue(i + 1, (i + 1) % 2)

    slot = i % 2
    pltpu.make_async_copy(y_hbm_ref.at[pl.ds(0, tm)], ybuf_ref.at[slot], sems.at[slot]).wait()
    y = ybuf_ref[slot].reshape(tm, SUBLANES * LANES)
    xo = x1_ref[...] + mod_ref[0, 5:6, :] * y
    o_ref[...] = _rms(xo) * fg_ref[...]


def _combine(pos, y_sorted, x1, mod3, final_g, seq, tm):
    t, d = x1.shape
    tiles_per_batch = seq // tm
    return pl.pallas_call(
        functools.partial(_combine_kernel, tm=tm),
        grid_spec=pltpu.PrefetchScalarGridSpec(
            num_scalar_prefetch=1,
            grid=(t // tm,),
            in_specs=[pl.BlockSpec(memory_space=pl.ANY),
                      pl.BlockSpec((tm, d), lambda i, pos: (i, 0)),
                      pl.BlockSpec((1, 6, d), lambda i, pos: (i // tiles_per_batch, 0, 0)),
                      pl.BlockSpec((1, d), lambda i, pos: (0, 0))],
            out_specs=pl.BlockSpec((tm, d), lambda i, pos: (i, 0)),
            scratch_shapes=[pltpu.VMEM((2, tm, SUBLANES, LANES), _F32),
                            pltpu.SemaphoreType.DMA((2,))]),
        out_shape=jax.ShapeDtypeStruct((t, d), _F32),
        compiler_params=_cparams("arbitrary"),
        name="moe_combine",
    )(pos, y_sorted, x1, mod3, final_g)


def _routing_tables(route_cls, route_rank, counts, n_tokens, tm):
    sizes = ((counts + (tm - 1)) // tm) * tm
    ends = jnp.cumsum(sizes)
    starts = ends - sizes
    n_tiles = n_tokens // tm + N_CLASSES
    tile_start = jnp.arange(n_tiles, dtype=jnp.int32) * tm
    pos = route_rank
    tile_cls = jnp.zeros((n_tiles,), jnp.int32)
    for c in range(N_CLASSES):
        pos = pos + jnp.where(route_cls == c, starts[c], 0)
        tile_cls = tile_cls + (tile_start >= ends[c]).astype(jnp.int32)
    tile_cls = jnp.minimum(tile_cls, N_CLASSES - 1)
    tile_valid = (tile_start < ends[-1]).astype(jnp.int32)
    pairs = [(a, b_) for a in range(EXPERTS_PER_GROUP) for b_ in range(a + 1, EXPERTS_PER_GROUP)]
    grp, pair = tile_cls // PAIRS_PER_GROUP, tile_cls % PAIRS_PER_GROUP
    tile_ea = grp * EXPERTS_PER_GROUP
    tile_eb = grp * EXPERTS_PER_GROUP
    for k, (a, b_) in enumerate(pairs):
        tile_ea = tile_ea + jnp.where(pair == k, a, 0)
        tile_eb = tile_eb + jnp.where(pair == k, b_, 0)
    pos = pos.astype(jnp.int32)
    src_of_slot = jnp.zeros((n_tiles * tm,), jnp.int32).at[pos].set(
        jnp.arange(n_tokens, dtype=jnp.int32), unique_indices=True)
    return pos, src_of_slot, tile_ea.astype(jnp.int32), tile_eb.astype(jnp.int32), tile_valid


def _rope_tables(n):
    inv = (1.0 / (ROPE_BASE ** (np.arange(ROT_FREQS, dtype=np.float32) / ROT_FREQS))).astype(np.float32)
    pos = np.arange(n)
    row = (pos // GRID_W).astype(np.float32)[:, None] * inv[None, :]
    col = (pos % GRID_W).astype(np.float32)[:, None] * inv[None, :]
    cos64 = np.concatenate([np.cos(row), np.cos(row), np.cos(col), np.cos(col)], axis=1)
    sin64 = np.concatenate([-np.sin(row), np.sin(row), -np.sin(col), np.sin(col)], axis=1)
    tile = lambda a: np.tile(a.astype(np.float32), (1, LANES // HEAD_DIM))
    return jnp.asarray(tile(cos64)), jnp.asarray(tile(sin64))


def _dft_cos_sin(n):
    k = np.arange(n, dtype=np.int64)
    ang = (2.0 * np.pi / n) * ((k[:, None] * k[None, :]) % n).astype(np.float64)
    return np.cos(ang), np.sin(ang)


def _dft_tables(n):
    c_ch, s_ch = _dft_cos_sin(F_GROUP_DIM)
    c_seq, s_seq = _dft_cos_sin(n)
    cs_ch = np.concatenate([c_ch, s_ch], axis=1).astype(np.float32)
    cs_seq = np.concatenate([c_seq, -s_seq], axis=1).astype(np.float32)
    return jnp.asarray(cs_ch.astype(_BF)), jnp.asarray(cs_seq.astype(_BF))


def kernel(x, c, ctx, c_ctx, w_mod, b_mod, norm1_g, norm2_g, w_in, lam_q1, lam_k1, lam_q2, lam_k2,
           subln_g, w_attn_out, w_four_out, w_out, w_router_group, b_router_group, w_router_expert,
           b_router_expert, w_exp_gate, w_exp_up, w_exp_down, final_g):
    b, n, d = x.shape
    assert w_mod.shape[0] == 1, "depth-1 stack"
    assert b + 1 <= MOD_ROWS

    cc = jnp.concatenate([c, c_ctx[None, :], jnp.zeros((MOD_ROWS - b - 1, d), _F32)], axis=0)
    mod3 = _modulation(cc, w_mod[0], b_mod).reshape(MOD_ROWS, 6, d)

    lam = (jnp.exp(jnp.sum(lam_q1[0] * lam_k1[0])) - jnp.exp(jnp.sum(lam_q2[0] * lam_k2[0]))
           + LAM_INIT).reshape(1).astype(_F32)

    w = w_in[0]
    scale = HEAD_DIM ** -0.5 * math.log2(math.e)
    w_lat = jnp.concatenate([w[:, REF_Q:REF_K] * scale, w[:, REF_K:REF_F],
                             w[:, REF_GA:REF_END], w[:, REF_F:REF_GA]], axis=1).astype(_BF)
    w_ctx = w[:, REF_K:REF_F].astype(_BF)
    lat_kinds = ("rope",) * ((P_V - P_Q) // PROJ_CW) + ("plain",) * ((P_GA - P_V) // PROJ_CW) \
        + ("sigmoid",) * ((P_F - P_GA) // PROJ_CW) + ("plain",) * ((P_W - P_F) // PROJ_CW)
    ctx_kinds = ("plain",) * (w_ctx.shape[1] // PROJ_CW)

    cos_t, sin_t = _rope_tables(n)
    p = _in_projection(x, mod3, lambda bi: bi, norm1_g, w_lat, cos_t, sin_t, lat_kinds, PROJ_TM)
    kvc = _in_projection(ctx, mod3, lambda bi: b, norm1_g, w_ctx, cos_t, sin_t, ctx_kinds,
                         ctx.shape[1])

    heads = _attention(lam, p, kvc, subln_g, ATTN_TQ)
    cs_ch, cs_seq = _dft_tables(n)
    four = _fnet(p, cs_ch, cs_seq)

    w_r = jnp.concatenate([w_router_expert[0], w_router_group[0],
                           jnp.zeros((d, ROUTER_W - N_EXPERTS - N_GROUPS), _F32)], axis=1)
    b_r = jnp.concatenate([b_router_expert[0], b_router_group[0],
                           jnp.zeros((ROUTER_W - N_EXPERTS - N_GROUPS,), _F32)])[None, :]
    wr_hi = w_r.astype(_BF)
    wr_lo = (w_r - wr_hi.astype(_F32)).astype(_BF)
    x1, rows, rec, counts = _merge(heads, four, p, x, mod3, norm2_g, w_attn_out[0].astype(_BF),
                                   w_four_out[0].astype(_BF), w_out[0].astype(_BF),
                                   jnp.concatenate([wr_hi, wr_lo], axis=1), b_r, MERGE_TM)

    t = b * n
    slabs = rows.reshape(t, SUBLANES, LANES)
    rec = rec.reshape(t // MERGE_TM, SUBLANES, MERGE_TM)
    pos, src_of_slot, tile_ea, tile_eb, tile_valid = _routing_tables(
        rec[:, 2, :].reshape(t).astype(jnp.int32), rec[:, 3, :].reshape(t).astype(jnp.int32),
        counts[0, :N_CLASSES].astype(jnp.int32), t, MOE_TM)

    y_sorted = _moe(tile_ea, tile_eb, tile_valid, src_of_slot, slabs, w_exp_gate[0].astype(_BF),
                    w_exp_up[0].astype(_BF), w_exp_down[0].astype(_BF), MOE_TM)
    out = _combine(pos, y_sorted, x1.reshape(t, d), mod3, final_g[None, :], n, COMBINE_TM)
    return out.reshape(b, n, d)
```

```python
import functools
import math

import jax
import jax.numpy as jnp
import numpy as np
from jax import lax
from jax.experimental import pallas as pl
from jax.experimental.pallas import tpu as pltpu

D_MODEL = 1024
GRID_W = 64
EPS = 1e-6
N_HEADS = 8
HEAD_DIM = 64
HEAD_W = 2 * HEAD_DIM
ROT_FREQS = HEAD_DIM // 4
ROPE_BASE = 10000.0
F_GROUPS = 4
F_GROUP_DIM = 128
F_W = F_GROUPS * F_GROUP_DIM
N_GROUPS = 4
EXPERTS_PER_GROUP = 4
LOG2_EXPERTS_PER_GROUP = 2
N_EXPERTS = N_GROUPS * EXPERTS_PER_GROUP
PAIRS_PER_GROUP = EXPERTS_PER_GROUP * (EXPERTS_PER_GROUP - 1) // 2
N_CLASSES = N_GROUPS * PAIRS_PER_GROUP
EXPERT_HIDDEN = 512
LAM_INIT = 0.8 - 0.6 * math.exp(-0.3 * 0)

REF_Q, REF_K, REF_V, REF_F, REF_GA, REF_GF, REF_END = 0, 1024, 2048, 3072, 3584, 4608, 5632
P_Q, P_K, P_V, P_GA, P_GF, P_F, P_W = 0, 1024, 2048, 3072, 4096, 5120, 5632

LANES = 128
SUBLANES = 8
VMEM_LIMIT_BYTES = 56 * 1024 * 1024

PROJ_TM = 1024
PROJ_CW = 512
ATTN_TQ = 256
FNET_TR = 512
MERGE_TM = 512
MOE_TM = 256
COMBINE_TM = 512
DISPATCH_TM = 2048
DMA_ISSUE_GROUP = 8
MOD_ROWS = 40
ROUTER_W = LANES
GROUP_LANE0 = N_EXPERTS
H2_SLAB_ROWS = D_MODEL // 2 // LANES
ROUTE_SLAB_ROW = H2_SLAB_ROWS

_BF = jnp.bfloat16
_F32 = jnp.float32


def _cparams(*sem):
    return pltpu.CompilerParams(dimension_semantics=sem, vmem_limit_bytes=VMEM_LIMIT_BYTES)


def _const_spec(shape):
    return pl.BlockSpec(shape, lambda *_: (0,) * len(shape), pipeline_mode=pl.Buffered(1))


def _rms(x):
    return x * lax.rsqrt(jnp.mean(x * x, axis=-1, keepdims=True) + EPS)


def _sigmoid(x):
    return 1.0 / (1.0 + jnp.exp(-x))


def _mod_kernel(cc_ref, w_ref, b_ref, o_ref):
    cc = cc_ref[...]
    s = cc * _sigmoid(cc)
    o_ref[...] = jnp.dot(s, w_ref[...], preferred_element_type=_F32,
                         precision=lax.Precision.HIGHEST) + b_ref[...]


def _modulation(cc, w_mod, b_mod):
    n = w_mod.shape[1]
    bn = D_MODEL
    return pl.pallas_call(
        _mod_kernel,
        grid=(n // bn,),
        in_specs=[pl.BlockSpec((MOD_ROWS, D_MODEL), lambda j: (0, 0)),
                  pl.BlockSpec((D_MODEL, bn), lambda j: (0, j)),
                  pl.BlockSpec((1, bn), lambda j: (0, j))],
        out_specs=pl.BlockSpec((MOD_ROWS, bn), lambda j: (0, j)),
        out_shape=jax.ShapeDtypeStruct((MOD_ROWS, n), _F32),
        compiler_params=_cparams("arbitrary"),
        name="modulation",
    )(cc, w_mod, b_mod)


def _rope(acc, cos_ref, sin_ref):
    cos = cos_ref[...]
    sin = sin_ref[...]
    lane = lax.broadcasted_iota(jnp.int32, (1, LANES), 1)
    first_half = (lane % (2 * ROT_FREQS)) < ROT_FREQS
    outs = []
    for s in range(acc.shape[1] // LANES):
        xs = acc[:, s * LANES:(s + 1) * LANES]
        partner = jnp.where(first_half,
                            pltpu.roll(xs, LANES - ROT_FREQS, 1),
                            pltpu.roll(xs, ROT_FREQS, 1))
        outs.append(xs * cos + partner * sin)
    return jnp.concatenate(outs, axis=1)


def _inproj_kernel(x_ref, mod_ref, g_ref, w_ref, cos_ref, sin_ref, o_ref, *, chunk_kinds):
    x = x_ref[0]
    shift = mod_ref[0, 0:1, :]
    scale = mod_ref[0, 1:2, :]
    h = (_rms(x) * g_ref[...]) * (1.0 + scale) + shift
    hb = h.astype(_BF)
    for j, kind in enumerate(chunk_kinds):
        cols = slice(j * PROJ_CW, (j + 1) * PROJ_CW)
        acc = jnp.dot(hb, w_ref[:, cols], preferred_element_type=_F32)
        if kind == "rope":
            acc = _rope(acc, cos_ref, sin_ref)
        elif kind == "sigmoid":
            acc = _sigmoid(acc)
        o_ref[0, :, cols] = acc.astype(_BF)


def _in_projection(x, mod3, mod_row_of_batch, g, w, cos_t, sin_t, chunk_kinds, tm):
    b, n, d = x.shape
    width = w.shape[1]
    assert width == len(chunk_kinds) * PROJ_CW and n % tm == 0
    return pl.pallas_call(
        functools.partial(_inproj_kernel, chunk_kinds=chunk_kinds),
        grid=(b, n // tm),
        in_specs=[pl.BlockSpec((1, tm, d), lambda bi, i: (bi, i, 0)),
                  pl.BlockSpec((1, 6, d), lambda bi, i: (mod_row_of_batch(bi), 0, 0)),
                  pl.BlockSpec((1, d), lambda bi, i: (0, 0)),
                  _const_spec((d, width)),
                  pl.BlockSpec((tm, LANES), lambda bi, i: (i, 0)),
                  pl.BlockSpec((tm, LANES), lambda bi, i: (i, 0))],
        out_specs=pl.BlockSpec((1, tm, width), lambda bi, i: (bi, i, 0)),
        out_shape=jax.ShapeDtypeStruct((b, n, width), _BF),
        compiler_params=_cparams("parallel", "arbitrary"),
        name="in_projection",
    )(x, mod3, g, w, cos_t, sin_t)


def _attn_kernel(lam_ref, q_ref, kl_ref, vl_ref, kc_ref, vc_ref, sg_ref, o_ref,
                 kcat_ref, vext_ref, s_even_ref, s_odd_ref, *, tq):
    nc, n = kc_ref.shape[1], kl_ref.shape[1]
    kcat_ref[0:nc, :] = kc_ref[0]
    kcat_ref[nc:nc + n, :] = kl_ref[0]
    vext_ref[0:nc, 0:HEAD_W] = vc_ref[0]
    vext_ref[nc:nc + n, 0:HEAD_W] = vl_ref[0]
    vext_ref[:, HEAD_W:] = jnp.ones((nc + n, HEAD_W), _BF)
    lane = lax.broadcasted_iota(jnp.int32, (1, HEAD_W), 1)
    map_lanes = (lane < HEAD_DIM, lane >= HEAD_DIM)
    nt = (((1,), (1,)), ((), ()))
    bufs = (s_even_ref, s_odd_ref)
    lam = lam_ref[0]
    post_scale = sg_ref[...] * (1.0 - LAM_INIT)

    def scores(j):
        q = q_ref[0, j * tq:(j + 1) * tq, :]
        for mp in range(2):
            qm = jnp.where(map_lanes[mp], q, jnp.zeros_like(q))
            bufs[j % 2][mp] = lax.dot_general(qm, kcat_ref[...], nt, preferred_element_type=_F32)

    def finish(j):
        outs = []
        for mp in range(2):
            s = bufs[j % 2][mp]
            e = jnp.exp2(s - jnp.max(s, axis=-1, keepdims=True)).astype(_BF)
            r = jnp.dot(e, vext_ref[...], preferred_element_type=_F32)
            outs.append(r[:, :HEAD_W] / r[:, HEAD_W:])
        heads = outs[0] - lam * outs[1]
        o_ref[0, j * tq:(j + 1) * tq, :] = (_rms(heads) * post_scale).astype(_BF)

    n_sub = n // tq
    scores(0)
    for j in range(n_sub):
        if j + 1 < n_sub:
            scores(j + 1)
        finish(j)


def _attention(lam, p, kvc, subln_g, tq):
    b, n, _ = p.shape
    nc = kvc.shape[1]
    qb, kb, vb = P_Q // HEAD_W, P_K // HEAD_W, P_V // HEAD_W
    seq = lambda blk: pl.BlockSpec((1, n, HEAD_W), lambda bi, h: (bi, 0, blk + h))
    return pl.pallas_call(
        functools.partial(_attn_kernel, tq=tq),
        grid=(b, N_HEADS),
        in_specs=[pl.BlockSpec(memory_space=pltpu.SMEM),
                  seq(qb), seq(kb), seq(vb),
                  pl.BlockSpec((1, nc, HEAD_W), lambda bi, h: (bi, 0, h)),
                  pl.BlockSpec((1, nc, HEAD_W), lambda bi, h: (bi, 0, N_HEADS + h)),
                  pl.BlockSpec((1, HEAD_W), lambda bi, h: (0, 0))],
        out_specs=seq(0),
        out_shape=jax.ShapeDtypeStruct((b, n, N_HEADS * HEAD_W), _BF),
        scratch_shapes=[pltpu.VMEM((nc + n, HEAD_W), _BF), pltpu.VMEM((nc + n, 2 * HEAD_W), _BF),
                        pltpu.VMEM((2, tq, nc + n), _F32), pltpu.VMEM((2, tq, nc + n), _F32)],
        compiler_params=_cparams("parallel", "arbitrary"),
        name="diff_attention",
    )(lam, p, p, p, kvc, kvc, subln_g)


def _fnet_kernel(f_ref, cs_ch_ref, cs_seq_ref, o_ref, xcs_ref, *, n):
    for g in range(F_GROUPS):
        cols = slice(g * F_GROUP_DIM, (g + 1) * F_GROUP_DIM)
        t = jnp.dot(f_ref[0, :, cols], cs_ch_ref[...], preferred_element_type=_F32)
        xcs_ref[0:n, cols] = t[:, :F_GROUP_DIM].astype(_BF)
        xcs_ref[n:2 * n, cols] = t[:, F_GROUP_DIM:].astype(_BF)
    ortho = 1.0 / math.sqrt(n * F_GROUP_DIM)
    for r in range(n // FNET_TR):
        rows = slice(r * FNET_TR, (r + 1) * FNET_TR)
        y = jnp.dot(cs_seq_ref[rows, :], xcs_ref[...], preferred_element_type=_F32)
        o_ref[0, rows, :] = (y * ortho).astype(_BF)


def _fnet(p, cs_ch, cs_seq):
    b, n, _ = p.shape
    return pl.pallas_call(
        functools.partial(_fnet_kernel, n=n),
        grid=(b,),
        in_specs=[pl.BlockSpec((1, n, F_W), lambda bi: (bi, 0, P_F // F_W)),
                  _const_spec((F_GROUP_DIM, 2 * F_GROUP_DIM)),
                  _const_spec((n, 2 * n))],
        out_specs=pl.BlockSpec((1, n, F_W), lambda bi: (bi, 0, 0)),
        out_shape=jax.ShapeDtypeStruct((b, n, F_W), _BF),
        scratch_shapes=[pltpu.VMEM((2 * n, F_W), _BF)],
        compiler_params=_cparams("arbitrary"),
        name="fnet_dft",
    )(p, cs_ch, cs_seq)


def _pack_bf16_pair(hi, lo):
    hi_bits = pltpu.bitcast(hi.astype(_BF).astype(_F32), jnp.uint32)
    lo_bits = pltpu.bitcast(lo.astype(_BF).astype(_F32), jnp.uint32)
    return hi_bits | lax.shift_right_logical(lo_bits, jnp.uint32(16))


def _unpack_bf16_pair(packed):
    hi = pltpu.bitcast(packed & jnp.uint32(0xFFFF0000), _F32)
    lo = pltpu.bitcast(lax.shift_left(packed, jnp.uint32(16)), _F32)
    return hi, lo


def _route(logits, carry):
    rows = logits.shape[0]
    lane = lax.broadcasted_iota(jnp.int32, logits.shape, 1)
    neg = jnp.float32(-jnp.inf)
    big = jnp.int32(ROUTER_W)

    def first_argmax(v):
        m = jnp.max(v, axis=-1, keepdims=True)
        idx = jnp.min(jnp.where(v == m, lane, big), axis=-1, keepdims=True)
        return m, idx

    lg = jnp.where((lane >= GROUP_LANE0) & (lane < GROUP_LANE0 + N_GROUPS), logits, neg)
    mg, ig = first_argmax(lg)
    w_grp = 1.0 / jnp.sum(jnp.exp(lg - mg), axis=-1, keepdims=True)
    g_sel = ig - GROUP_LANE0
    le = jnp.where((lane < N_EXPERTS)
                   & (jnp.right_shift(lane, LOG2_EXPERTS_PER_GROUP) == g_sel), logits, neg)
    v1, i1 = first_argmax(le)
    le2 = jnp.where(lane == i1, neg, le)
    v2, i2 = first_argmax(le2)
    e2 = jnp.exp(v2 - v1)
    w1 = w_grp / (1.0 + e2)
    w2 = w_grp * e2 / (1.0 + e2)

    first_is_lower = i1 < i2
    gate_a = jnp.where(first_is_lower, w1, w2)
    gate_b = jnp.where(first_is_lower, w2, w1)
    la = jnp.minimum(i1, i2) - g_sel * EXPERTS_PER_GROUP
    lb = jnp.maximum(i1, i2) - g_sel * EXPERTS_PER_GROUP
    pair = jnp.right_shift(la * (2 * EXPERTS_PER_GROUP - 1 - la), 1) + lb - la - 1
    cls = g_sel * PAIRS_PER_GROUP + pair

    onehot = (lane == cls).astype(_F32)
    r_i = lax.broadcasted_iota(jnp.int32, (rows, rows), 0)
    c_i = lax.broadcasted_iota(jnp.int32, (rows, rows), 1)
    earlier = (c_i < r_i).astype(_BF)
    before = jnp.dot(earlier, onehot.astype(_BF), preferred_element_type=_F32) + carry
    rank = jnp.sum(onehot * before, axis=-1, keepdims=True)
    new_carry = carry + jnp.sum(onehot, axis=0, keepdims=True)

    route = (jnp.where(lane == 0, gate_a, 0.0) + jnp.where(lane == 1, gate_b, 0.0)
             + jnp.where(lane == 2, cls.astype(_F32), 0.0) + jnp.where(lane == 3, rank, 0.0))
    return route, new_carry


def _merge_kernel(hd_ref, fo_ref, ga_ref, gf_ref, x_ref, mod_ref, g2n_ref, wao_ref, wfo_ref,
                  wo_ref, wrhl_ref, br_ref, x1_ref, rows_ref, rec_ref, counts_ref, carry_ref):
    @pl.when((pl.program_id(0) == 0) & (pl.program_id(1) == 0))
    def _():
        carry_ref[...] = jnp.zeros_like(carry_ref)

    a = jnp.dot(hd_ref[0], wao_ref[...], preferred_element_type=_F32)
    ff = jnp.dot(fo_ref[0], wfo_ref[...], preferred_element_type=_F32)
    y = ga_ref[0].astype(_F32) * a + gf_ref[0].astype(_F32) * ff
    mix = jnp.dot(y.astype(_BF), wo_ref[...], preferred_element_type=_F32)
    x1 = x_ref[0] + mod_ref[0, 2:3, :] * mix
    x1_ref[0] = x1
    h2 = (_rms(x1) * g2n_ref[...]) * (1.0 + mod_ref[0, 4:5, :]) + mod_ref[0, 3:4, :]
    h2_hi = h2.astype(_BF)
    h2_lo = (h2 - h2_hi.astype(_F32)).astype(_BF)
    hi_both = jnp.dot(h2_hi, wrhl_ref[...], preferred_element_type=_F32)
    logits = (hi_both[:, :ROUTER_W] + hi_both[:, ROUTER_W:]
              + jnp.dot(h2_lo, wrhl_ref[:, :ROUTER_W], preferred_element_type=_F32)) + br_ref[...]
    route, new_carry = _route(logits, carry_ref[...])
    carry_ref[...] = new_carry
    counts_ref[...] = new_carry
    rec_ref[0, 0] = route.T[0:SUBLANES, :]
    half = D_MODEL // 2
    tm = h2.shape[0]
    packed = _pack_bf16_pair(h2[:, :half], h2[:, half:])
    rows_ref[0, :, 0:H2_SLAB_ROWS, :] = packed.reshape(tm, H2_SLAB_ROWS, LANES)
    rows_ref[0, :, ROUTE_SLAB_ROW:ROUTE_SLAB_ROW + 1, :] = (
        pltpu.bitcast(route, jnp.uint32).reshape(tm, 1, LANES))
    rows_ref[0, :, ROUTE_SLAB_ROW + 1:SUBLANES, :] = jnp.zeros(
        (tm, SUBLANES - ROUTE_SLAB_ROW - 1, LANES), jnp.uint32)


def _merge(heads, four, p, x, mod3, norm2_g, w_ao, w_fo, w_o, wr_hilo, b_r, tm):
    b, n, d = x.shape
    tok = lambda w: pl.BlockSpec((1, tm, w), lambda bi, i: (bi, i, 0))
    return pl.pallas_call(
        _merge_kernel,
        grid=(b, n // tm),
        in_specs=[tok(d), tok(F_W),
                  pl.BlockSpec((1, tm, d), lambda bi, i: (bi, i, P_GA // D_MODEL)),
                  pl.BlockSpec((1, tm, d), lambda bi, i: (bi, i, P_GF // D_MODEL)),
                  tok(d),
                  pl.BlockSpec((1, 6, d), lambda bi, i: (bi, 0, 0)),
                  pl.BlockSpec((1, d), lambda bi, i: (0, 0)),
                  _const_spec((d, d)), _const_spec((F_W, d)), _const_spec((d, d)),
                  _const_spec((d, 2 * ROUTER_W)),
                  pl.BlockSpec((1, ROUTER_W), lambda bi, i: (0, 0))],
        out_specs=[tok(d),
                   pl.BlockSpec((1, tm, SUBLANES, LANES), lambda bi, i: (bi, i, 0, 0)),
                   pl.BlockSpec((1, 1, SUBLANES, tm), lambda bi, i: (bi, i, 0, 0)),
                   pl.BlockSpec((1, ROUTER_W), lambda bi, i: (0, 0))],
        out_shape=[jax.ShapeDtypeStruct((b, n, d), _F32),
                   jax.ShapeDtypeStruct((b, n, SUBLANES, LANES), jnp.uint32),
                   jax.ShapeDtypeStruct((b, n // tm, SUBLANES, tm), _F32),
                   jax.ShapeDtypeStruct((1, ROUTER_W), _F32)],
        scratch_shapes=[pltpu.VMEM((1, ROUTER_W), _F32)],
        compiler_params=_cparams("arbitrary", "arbitrary"),
        name="merge_router",
    )(heads, four, p, p, x, mod3, norm2_g, w_ao, w_fo, w_o, wr_hilo, b_r)


def _issue_slab_copies(n, make_copy, slot_of):
    def group(g, c):
        r0 = g * DMA_ISSUE_GROUP
        slots = [slot_of(r0 + k) for k in range(DMA_ISSUE_GROUP)]
        for k in range(DMA_ISSUE_GROUP):
            make_copy(r0 + k, slots[k]).start(priority=k % 2)
        return c

    lax.fori_loop(0, n // DMA_ISSUE_GROUP, group, 0)


def _dispatch_kernel(pos_ref, pad_start_ref, pad_len_ref, src_ref, dst_ref, zero_ref, sem, pad_sem,
                     *, tm):
    base = pl.program_id(0) * tm

    @pl.when(pl.program_id(0) == 0)
    def _():
        zero_ref[...] = jnp.zeros_like(zero_ref)
        for c in range(N_CLASSES + 1):
            def start(r, carry, c=c):
                pltpu.make_async_copy(zero_ref, dst_ref.at[pad_start_ref[c] + r], pad_sem).start()
                return carry

            def drain(r, carry):
                pltpu.make_async_copy(zero_ref, dst_ref.at[0], pad_sem).wait()
                return carry

            lax.fori_loop(0, pad_len_ref[c], start, 0)
            lax.fori_loop(0, pad_len_ref[c], drain, 0)

    _issue_slab_copies(
        tm,
        lambda r, slot: pltpu.make_async_copy(src_ref.at[r], dst_ref.at[slot], sem),
        lambda r: pos_ref[base + r])
    pltpu.make_async_copy(src_ref, dst_ref.at[pl.ds(0, tm)], sem).wait()


def _dispatch(pos, pad_start, pad_len, slabs, n_sorted, tm):
    t = slabs.shape[0]
    return pl.pallas_call(
        functools.partial(_dispatch_kernel, tm=tm),
        grid_spec=pltpu.PrefetchScalarGridSpec(
            num_scalar_prefetch=3,
            grid=(t // tm,),
            in_specs=[pl.BlockSpec((tm, SUBLANES, LANES), lambda i, *_: (i, 0, 0))],
            out_specs=pl.BlockSpec(memory_space=pl.ANY),
            scratch_shapes=[pltpu.VMEM((SUBLANES, LANES), slabs.dtype),
                            pltpu.SemaphoreType.DMA(()), pltpu.SemaphoreType.DMA(())]),
        out_shape=jax.ShapeDtypeStruct((n_sorted,) + slabs.shape[1:], slabs.dtype),
        compiler_params=_cparams("arbitrary"),
        name="moe_dispatch",
    )(pos, pad_start, pad_len, slabs)


def _moe_kernel(ea_ref, eb_ref, valid_ref, slabs_ref, w1a_ref, w3a_ref, w2a_ref,
                w1b_ref, w3b_ref, w2b_ref, y_ref):
    del ea_ref, eb_ref
    i = pl.program_id(0)
    tm = slabs_ref.shape[0]

    @pl.when(valid_ref[i] != 0)
    def _():
        packed = slabs_ref[:, 0:H2_SLAB_ROWS, :].reshape(tm, H2_SLAB_ROWS * LANES)
        hi, lo = _unpack_bf16_pair(packed)
        t = jnp.concatenate([hi.astype(_BF), lo.astype(_BF)], axis=1)
        route = pltpu.bitcast(slabs_ref[:, ROUTE_SLAB_ROW, :], _F32)
        y = None
        for slot, (w1_ref, w3_ref, w2_ref) in enumerate(((w1a_ref, w3a_ref, w2a_ref),
                                                          (w1b_ref, w3b_ref, w2b_ref))):
            a = jnp.dot(t, w1_ref[0], preferred_element_type=_F32)
            u = jnp.dot(t, w3_ref[0], preferred_element_type=_F32)
            hid = ((a * _sigmoid(a)) * u * route[:, slot:slot + 1]).astype(_BF)
            part = jnp.dot(hid, w2_ref[0], preferred_element_type=_F32)
            y = part if y is None else y + part
        y_ref[...] = y.reshape(tm, SUBLANES, LANES)

    @pl.when(valid_ref[i] == 0)
    def _():
        y_ref[...] = jnp.zeros_like(y_ref)


def _moe(tile_ea, tile_eb, tile_valid, slabs_sorted, w1, w3, w2, tm):
    n_sorted = slabs_sorted.shape[0]
    d, hdn = w1.shape[1], w1.shape[2]
    assert d == SUBLANES * LANES
    pick = lambda which, a, b_: pl.BlockSpec(
        (1, a, b_), lambda i, ea, eb, valid: ((ea, eb)[which][i], 0, 0))
    slab_tile = pl.BlockSpec((tm, SUBLANES, LANES), lambda i, ea, eb, valid: (i, 0, 0))
    return pl.pallas_call(
        _moe_kernel,
        grid_spec=pltpu.PrefetchScalarGridSpec(
            num_scalar_prefetch=3,
            grid=(n_sorted // tm,),
            in_specs=[slab_tile,
                      pick(0, d, hdn), pick(0, d, hdn), pick(0, hdn, d),
                      pick(1, d, hdn), pick(1, d, hdn), pick(1, hdn, d)],
            out_specs=slab_tile),
        out_shape=jax.ShapeDtypeStruct((n_sorted, SUBLANES, LANES), _F32),
        compiler_params=_cparams("arbitrary"),
        name="moe_experts",
    )(tile_ea, tile_eb, tile_valid, slabs_sorted, w1, w3, w2, w1, w3, w2)


def _combine_kernel(pos_ref, y_hbm_ref, x1_ref, mod_ref, fg_ref, o_ref, ybuf_ref, sems, *, tm):
    i = pl.program_id(0)
    n_steps = pl.num_programs(0)

    def issue(tile, slot):
        _issue_slab_copies(
            tm,
            lambda r, src: pltpu.make_async_copy(y_hbm_ref.at[src], ybuf_ref.at[slot, r],
                                                 sems.at[slot]),
            lambda r: pos_ref[tile * tm + r])

    @pl.when(i == 0)
    def _():
        issue(0, 0)

    @pl.when(i + 1 < n_steps)
    def _():
        issue(i + 1, (i + 1) % 2)

    slot = i % 2
    pltpu.make_async_copy(y_hbm_ref.at[pl.ds(0, tm)], ybuf_ref.at[slot], sems.at[slot]).wait()
    y = ybuf_ref[slot].reshape(tm, SUBLANES * LANES)
    xo = x1_ref[...] + mod_ref[0, 5:6, :] * y
    o_ref[...] = _rms(xo) * fg_ref[...]


def _combine(pos, y_sorted, x1, mod3, final_g, seq, tm):
    t, d = x1.shape
    tiles_per_batch = seq // tm
    return pl.pallas_call(
        functools.partial(_combine_kernel, tm=tm),
        grid_spec=pltpu.PrefetchScalarGridSpec(
            num_scalar_prefetch=1,
            grid=(t // tm,),
            in_specs=[pl.BlockSpec(memory_space=pl.ANY),
                      pl.BlockSpec((tm, d), lambda i, pos: (i, 0)),
                      pl.BlockSpec((1, 6, d), lambda i, pos: (i // tiles_per_batch, 0, 0)),
                      pl.BlockSpec((1, d), lambda i, pos: (0, 0))],
            out_specs=pl.BlockSpec((tm, d), lambda i, pos: (i, 0)),
            scratch_shapes=[pltpu.VMEM((2, tm, SUBLANES, LANES), _F32),
                            pltpu.SemaphoreType.DMA((2,))]),
        out_shape=jax.ShapeDtypeStruct((t, d), _F32),
        compiler_params=_cparams("arbitrary"),
        name="moe_combine",
    )(pos, y_sorted, x1, mod3, final_g)


def _routing_tables(route_cls, route_rank, counts, n_tokens, tm):
    sizes = ((counts + (tm - 1)) // tm) * tm
    ends = jnp.cumsum(sizes)
    starts = ends - sizes
    n_tiles = n_tokens // tm + N_CLASSES
    tile_start = jnp.arange(n_tiles, dtype=jnp.int32) * tm
    pos = route_rank
    tile_cls = jnp.zeros((n_tiles,), jnp.int32)
    for c in range(N_CLASSES):
        pos = pos + jnp.where(route_cls == c, starts[c], 0)
        tile_cls = tile_cls + (tile_start >= ends[c]).astype(jnp.int32)
    tile_cls = jnp.minimum(tile_cls, N_CLASSES - 1)
    tile_valid = (tile_start < ends[-1]).astype(jnp.int32)
    pairs = [(a, b_) for a in range(EXPERTS_PER_GROUP) for b_ in range(a + 1, EXPERTS_PER_GROUP)]
    grp, pair = tile_cls // PAIRS_PER_GROUP, tile_cls % PAIRS_PER_GROUP
    tile_ea = grp * EXPERTS_PER_GROUP
    tile_eb = grp * EXPERTS_PER_GROUP
    for k, (a, b_) in enumerate(pairs):
        tile_ea = tile_ea + jnp.where(pair == k, a, 0)
        tile_eb = tile_eb + jnp.where(pair == k, b_, 0)
    pad_start = jnp.concatenate([starts + counts, ends[-1:]]).astype(jnp.int32)
    pad_len = jnp.concatenate([sizes - counts, n_tiles * tm - ends[-1:]]).astype(jnp.int32)
    return (pos.astype(jnp.int32), pad_start, pad_len, tile_ea.astype(jnp.int32),
            tile_eb.astype(jnp.int32), tile_valid, n_tiles)


def _rope_tables(n):
    inv = (1.0 / (ROPE_BASE ** (np.arange(ROT_FREQS, dtype=np.float32) / ROT_FREQS))).astype(np.float32)
    pos = np.arange(n)
    row = (pos // GRID_W).astype(np.float32)[:, None] * inv[None, :]
    col = (pos % GRID_W).astype(np.float32)[:, None] * inv[None, :]
    cos64 = np.concatenate([np.cos(row), np.cos(row), np.cos(col), np.cos(col)], axis=1)
    sin64 = np.concatenate([-np.sin(row), np.sin(row), -np.sin(col), np.sin(col)], axis=1)
    tile = lambda a: np.tile(a.astype(np.float32), (1, LANES // HEAD_DIM))
    return jnp.asarray(tile(cos64)), jnp.asarray(tile(sin64))


def _dft_cos_sin(n):
    k = np.arange(n, dtype=np.int64)
    ang = (2.0 * np.pi / n) * ((k[:, None] * k[None, :]) % n).astype(np.float64)
    return np.cos(ang), np.sin(ang)


def _dft_tables(n):
    c_ch, s_ch = _dft_cos_sin(F_GROUP_DIM)
    c_seq, s_seq = _dft_cos_sin(n)
    cs_ch = np.concatenate([c_ch, s_ch], axis=1).astype(np.float32)
    cs_seq = np.concatenate([c_seq, -s_seq], axis=1).astype(np.float32)
    return jnp.asarray(cs_ch.astype(_BF)), jnp.asarray(cs_seq.astype(_BF))


def kernel(x, c, ctx, c_ctx, w_mod, b_mod, norm1_g, norm2_g, w_in, lam_q1, lam_k1, lam_q2, lam_k2,
           subln_g, w_attn_out, w_four_out, w_out, w_router_group, b_router_group, w_router_expert,
           b_router_expert, w_exp_gate, w_exp_up, w_exp_down, final_g):
    b, n, d = x.shape
    assert w_mod.shape[0] == 1, "depth-1 stack"
    assert b + 1 <= MOD_ROWS

    cc = jnp.concatenate([c, c_ctx[None, :], jnp.zeros((MOD_ROWS - b - 1, d), _F32)], axis=0)
    mod3 = _modulation(cc, w_mod[0], b_mod).reshape(MOD_ROWS, 6, d)

    lam = (jnp.exp(jnp.sum(lam_q1[0] * lam_k1[0])) - jnp.exp(jnp.sum(lam_q2[0] * lam_k2[0]))
           + LAM_INIT).reshape(1).astype(_F32)

    w = w_in[0]
    scale = HEAD_DIM ** -0.5 * math.log2(math.e)
    w_lat = jnp.concatenate([w[:, REF_Q:REF_K] * scale, w[:, REF_K:REF_F],
                             w[:, REF_GA:REF_END], w[:, REF_F:REF_GA]], axis=1).astype(_BF)
    w_ctx = w[:, REF_K:REF_F].astype(_BF)
    lat_kinds = ("rope",) * ((P_V - P_Q) // PROJ_CW) + ("plain",) * ((P_GA - P_V) // PROJ_CW) \
        + ("sigmoid",) * ((P_F - P_GA) // PROJ_CW) + ("plain",) * ((P_W - P_F) // PROJ_CW)
    ctx_kinds = ("plain",) * (w_ctx.shape[1] // PROJ_CW)

    cos_t, sin_t = _rope_tables(n)
    p = _in_projection(x, mod3, lambda bi: bi, norm1_g, w_lat, cos_t, sin_t, lat_kinds, PROJ_TM)
    kvc = _in_projection(ctx, mod3, lambda bi: b, norm1_g, w_ctx, cos_t, sin_t, ctx_kinds,
                         ctx.shape[1])

    heads = _attention(lam, p, kvc, subln_g, ATTN_TQ)
    cs_ch, cs_seq = _dft_tables(n)
    four = _fnet(p, cs_ch, cs_seq)

    w_r = jnp.concatenate([w_router_expert[0], w_router_group[0],
                           jnp.zeros((d, ROUTER_W - N_EXPERTS - N_GROUPS), _F32)], axis=1)
    b_r = jnp.concatenate([b_router_expert[0], b_router_group[0],
                           jnp.zeros((ROUTER_W - N_EXPERTS - N_GROUPS,), _F32)])[None, :]
    wr_hi = w_r.astype(_BF)
    wr_lo = (w_r - wr_hi.astype(_F32)).astype(_BF)
    x1, rows, rec, counts = _merge(heads, four, p, x, mod3, norm2_g, w_attn_out[0].astype(_BF),
                                   w_four_out[0].astype(_BF), w_out[0].astype(_BF),
                                   jnp.concatenate([wr_hi, wr_lo], axis=1), b_r, MERGE_TM)

    t = b * n
    slabs = rows.reshape(t, SUBLANES, LANES)
    rec = rec.reshape(t // MERGE_TM, SUBLANES, MERGE_TM)
    pos, pad_start, pad_len, tile_ea, tile_eb, tile_valid, n_tiles = _routing_tables(
        rec[:, 2, :].reshape(t).astype(jnp.int32), rec[:, 3, :].reshape(t).astype(jnp.int32),
        counts[0, :N_CLASSES].astype(jnp.int32), t, MOE_TM)

    slabs_sorted = _dispatch(pos, pad_start, pad_len, slabs, n_tiles * MOE_TM, DISPATCH_TM)
    y_sorted = _moe(tile_ea, tile_eb, tile_valid, slabs_sorted, w_exp_gate[0].astype(_BF),
                    w_exp_up[0].astype(_BF), w_exp_down[0].astype(_BF), MOE_TM)
    out = _combine(pos, y_sorted, x1.reshape(t, d), mod3, final_g[None, :], n, COMBINE_TM)
    return out.reshape(b, n, d)
```

```python
import functools
import math

import jax
import jax.numpy as jnp
import numpy as np
from jax import lax
from jax.experimental import pallas as pl
from jax.experimental.pallas import tpu as pltpu

D_MODEL = 1024
GRID_W = 64
EPS = 1e-6
N_HEADS = 8
HEAD_DIM = 64
HEAD_W = 2 * HEAD_DIM
ROT_FREQS = HEAD_DIM // 4
ROPE_BASE = 10000.0
F_GROUPS = 4
F_GROUP_DIM = 128
F_W = F_GROUPS * F_GROUP_DIM
N_GROUPS = 4
EXPERTS_PER_GROUP = 4
LOG2_EXPERTS_PER_GROUP = 2
N_EXPERTS = N_GROUPS * EXPERTS_PER_GROUP
PAIRS_PER_GROUP = EXPERTS_PER_GROUP * (EXPERTS_PER_GROUP - 1) // 2
N_CLASSES = N_GROUPS * PAIRS_PER_GROUP
EXPERT_HIDDEN = 512
LAM_INIT = 0.8 - 0.6 * math.exp(-0.3 * 0)

REF_Q, REF_K, REF_V, REF_F, REF_GA, REF_GF, REF_END = 0, 1024, 2048, 3072, 3584, 4608, 5632
P_Q, P_K, P_V, P_GA, P_GF, P_F, P_W = 0, 1024, 2048, 3072, 4096, 5120, 5632

LANES = 128
SUBLANES = 8
VMEM_LIMIT_BYTES = 56 * 1024 * 1024

PROJ_TM = 1024
PROJ_CW = 512
ATTN_TQ = 256
ATTN_SUM_ROWS = 16
ATTN_SCORE_BUFS = 3
FNET_TR = 512
MERGE_TM = 512
MOE_TM = 256
COMBINE_TM = 512
DISPATCH_TM = 2048
DMA_ISSUE_GROUP = 8
MOD_ROWS = 40
ROUTER_W = LANES
GROUP_LANE0 = N_EXPERTS
H2_SLAB_ROWS = D_MODEL // 2 // LANES
ROUTE_SLAB_ROW = H2_SLAB_ROWS

_BF = jnp.bfloat16
_F32 = jnp.float32


def _cparams(*sem):
    return pltpu.CompilerParams(dimension_semantics=sem, vmem_limit_bytes=VMEM_LIMIT_BYTES)


def _const_spec(shape):
    return pl.BlockSpec(shape, lambda *_: (0,) * len(shape), pipeline_mode=pl.Buffered(1))


def _rms(x):
    return x * lax.rsqrt(jnp.mean(x * x, axis=-1, keepdims=True) + EPS)


def _sigmoid(x):
    return 1.0 / (1.0 + jnp.exp(-x))


def _mod_kernel(cc_ref, w_ref, b_ref, o_ref):
    cc = cc_ref[...]
    s = cc * _sigmoid(cc)
    o_ref[...] = jnp.dot(s, w_ref[...], preferred_element_type=_F32,
                         precision=lax.Precision.HIGHEST) + b_ref[...]


def _modulation(cc, w_mod, b_mod):
    n = w_mod.shape[1]
    bn = D_MODEL
    return pl.pallas_call(
        _mod_kernel,
        grid=(n // bn,),
        in_specs=[pl.BlockSpec((MOD_ROWS, D_MODEL), lambda j: (0, 0)),
                  pl.BlockSpec((D_MODEL, bn), lambda j: (0, j)),
                  pl.BlockSpec((1, bn), lambda j: (0, j))],
        out_specs=pl.BlockSpec((MOD_ROWS, bn), lambda j: (0, j)),
        out_shape=jax.ShapeDtypeStruct((MOD_ROWS, n), _F32),
        compiler_params=_cparams("arbitrary"),
        name="modulation",
    )(cc, w_mod, b_mod)


def _rope(acc, cos_ref, sin_ref):
    cos = cos_ref[...]
    sin = sin_ref[...]
    lane = lax.broadcasted_iota(jnp.int32, (1, LANES), 1)
    first_half = (lane % (2 * ROT_FREQS)) < ROT_FREQS
    outs = []
    for s in range(acc.shape[1] // LANES):
        xs = acc[:, s * LANES:(s + 1) * LANES]
        partner = jnp.where(first_half,
                            pltpu.roll(xs, LANES - ROT_FREQS, 1),
                            pltpu.roll(xs, ROT_FREQS, 1))
        outs.append(xs * cos + partner * sin)
    return jnp.concatenate(outs, axis=1)


def _inproj_kernel(x_ref, mod_ref, g_ref, w_ref, cos_ref, sin_ref, o_ref, *, chunk_kinds):
    x = x_ref[0]
    shift = mod_ref[0, 0:1, :]
    scale = mod_ref[0, 1:2, :]
    h = (_rms(x) * g_ref[...]) * (1.0 + scale) + shift
    hb = h.astype(_BF)
    for j, kind in enumerate(chunk_kinds):
        cols = slice(j * PROJ_CW, (j + 1) * PROJ_CW)
        acc = jnp.dot(hb, w_ref[:, cols], preferred_element_type=_F32)
        if kind == "rope":
            acc = _rope(acc, cos_ref, sin_ref)
        elif kind == "sigmoid":
            acc = _sigmoid(acc)
        o_ref[0, :, cols] = acc.astype(_BF)


def _in_projection(x, mod3, mod_row_of_batch, g, w, cos_t, sin_t, chunk_kinds, tm):
    b, n, d = x.shape
    width = w.shape[1]
    assert width == len(chunk_kinds) * PROJ_CW and n % tm == 0
    return pl.pallas_call(
        functools.partial(_inproj_kernel, chunk_kinds=chunk_kinds),
        grid=(b, n // tm),
        in_specs=[pl.BlockSpec((1, tm, d), lambda bi, i: (bi, i, 0)),
                  pl.BlockSpec((1, 6, d), lambda bi, i: (mod_row_of_batch(bi), 0, 0)),
                  pl.BlockSpec((1, d), lambda bi, i: (0, 0)),
                  _const_spec((d, width)),
                  pl.BlockSpec((tm, LANES), lambda bi, i: (i, 0)),
                  pl.BlockSpec((tm, LANES), lambda bi, i: (i, 0))],
        out_specs=pl.BlockSpec((1, tm, width), lambda bi, i: (bi, i, 0)),
        out_shape=jax.ShapeDtypeStruct((b, n, width), _BF),
        compiler_params=_cparams("parallel", "arbitrary"),
        name="in_projection",
    )(x, mod3, g, w, cos_t, sin_t)


def _attn_kernel(lam_ref, q_ref, kl_ref, vl_ref, kc_ref, vc_ref, sg_ref, o_ref,
                 kcat_ref, vt_ref, *bufs, tq):
    nc, n = kc_ref.shape[1], kl_ref.shape[1]
    kcat_ref[0:nc, :] = kc_ref[0]
    kcat_ref[nc:nc + n, :] = kl_ref[0]
    vt_ref[0:HEAD_W, 0:nc] = vc_ref[0].astype(_F32).T.astype(_BF)
    vt_ref[0:HEAD_W, nc:nc + n] = vl_ref[0].astype(_F32).T.astype(_BF)
    vt_ref[HEAD_W:, :] = jnp.ones((ATTN_SUM_ROWS, nc + n), _BF)
    lane = lax.broadcasted_iota(jnp.int32, (1, HEAD_W), 1)
    map_lanes = (lane < HEAD_DIM, lane >= HEAD_DIM)
    nt = (((1,), (1,)), ((), ()))
    lam = lam_ref[0]
    post_scale = sg_ref[...] * (1.0 - LAM_INIT)

    def scores(j):
        q = q_ref[0, j * tq:(j + 1) * tq, :]
        for mp in range(2):
            qm = jnp.where(map_lanes[mp], q, jnp.zeros_like(q))
            bufs[j % ATTN_SCORE_BUFS][mp] = lax.dot_general(kcat_ref[...], qm, nt,
                                                            preferred_element_type=_F32)

    def finish(j):
        outs = []
        for mp in range(2):
            s = bufs[j % ATTN_SCORE_BUFS][mp]
            e = jnp.exp2(s - jnp.max(s, axis=0, keepdims=True)).astype(_BF)
            r = jnp.dot(vt_ref[...], e, preferred_element_type=_F32)
            outs.append(r[:HEAD_W, :] / r[HEAD_W:HEAD_W + 1, :])
        heads = (outs[0] - lam * outs[1]).T
        o_ref[0, j * tq:(j + 1) * tq, :] = (_rms(heads) * post_scale).astype(_BF)

    n_sub = n // tq
    for j in range(ATTN_SCORE_BUFS - 1):
        scores(j)
    for j in range(n_sub):
        if j + ATTN_SCORE_BUFS - 1 < n_sub:
            scores(j + ATTN_SCORE_BUFS - 1)
        finish(j)


def _attention(lam, p, kvc, subln_g, tq):
    b, n, _ = p.shape
    nc = kvc.shape[1]
    qb, kb, vb = P_Q // HEAD_W, P_K // HEAD_W, P_V // HEAD_W
    seq = lambda blk: pl.BlockSpec((1, n, HEAD_W), lambda bi, h: (bi, 0, blk + h))
    return pl.pallas_call(
        functools.partial(_attn_kernel, tq=tq),
        grid=(b, N_HEADS),
        in_specs=[pl.BlockSpec(memory_space=pltpu.SMEM),
                  seq(qb), seq(kb), seq(vb),
                  pl.BlockSpec((1, nc, HEAD_W), lambda bi, h: (bi, 0, h)),
                  pl.BlockSpec((1, nc, HEAD_W), lambda bi, h: (bi, 0, N_HEADS + h)),
                  pl.BlockSpec((1, HEAD_W), lambda bi, h: (0, 0))],
        out_specs=seq(0),
        out_shape=jax.ShapeDtypeStruct((b, n, N_HEADS * HEAD_W), _BF),
        scratch_shapes=[pltpu.VMEM((nc + n, HEAD_W), _BF),
                        pltpu.VMEM((HEAD_W + ATTN_SUM_ROWS, nc + n), _BF),
                        ] + [pltpu.VMEM((2, nc + n, tq), _F32)] * ATTN_SCORE_BUFS,
        compiler_params=_cparams("parallel", "arbitrary"),
        name="diff_attention",
    )(lam, p, p, p, kvc, kvc, subln_g)


def _fnet_kernel(f_ref, cs_ch_ref, cs_seq_ref, o_ref, xcs_ref, *, n):
    for g in range(F_GROUPS):
        cols = slice(g * F_GROUP_DIM, (g + 1) * F_GROUP_DIM)
        t = jnp.dot(f_ref[0, :, cols], cs_ch_ref[...], preferred_element_type=_F32)
        xcs_ref[0:n, cols] = t[:, :F_GROUP_DIM].astype(_BF)
        xcs_ref[n:2 * n, cols] = t[:, F_GROUP_DIM:].astype(_BF)
    ortho = 1.0 / math.sqrt(n * F_GROUP_DIM)
    for r in range(n // FNET_TR):
        rows = slice(r * FNET_TR, (r + 1) * FNET_TR)
        y = jnp.dot(cs_seq_ref[rows, :], xcs_ref[...], preferred_element_type=_F32)
        o_ref[0, rows, :] = (y * ortho).astype(_BF)


def _fnet(p, cs_ch, cs_seq):
    b, n, _ = p.shape
    return pl.pallas_call(
        functools.partial(_fnet_kernel, n=n),
        grid=(b,),
        in_specs=[pl.BlockSpec((1, n, F_W), lambda bi: (bi, 0, P_F // F_W)),
                  _const_spec((F_GROUP_DIM, 2 * F_GROUP_DIM)),
                  _const_spec((n, 2 * n))],
        out_specs=pl.BlockSpec((1, n, F_W), lambda bi: (bi, 0, 0)),
        out_shape=jax.ShapeDtypeStruct((b, n, F_W), _BF),
        scratch_shapes=[pltpu.VMEM((2 * n, F_W), _BF)],
        compiler_params=_cparams("arbitrary"),
        name="fnet_dft",
    )(p, cs_ch, cs_seq)


def _pack_bf16_pair(hi, lo):
    hi_bits = pltpu.bitcast(hi.astype(_BF).astype(_F32), jnp.uint32)
    lo_bits = pltpu.bitcast(lo.astype(_BF).astype(_F32), jnp.uint32)
    return hi_bits | lax.shift_right_logical(lo_bits, jnp.uint32(16))


def _unpack_bf16_pair(packed):
    hi = pltpu.bitcast(packed & jnp.uint32(0xFFFF0000), _F32)
    lo = pltpu.bitcast(lax.shift_left(packed, jnp.uint32(16)), _F32)
    return hi, lo


def _route(logits, carry):
    rows = logits.shape[0]
    lane = lax.broadcasted_iota(jnp.int32, logits.shape, 1)
    neg = jnp.float32(-jnp.inf)
    big = jnp.int32(ROUTER_W)

    def first_argmax(v):
        m = jnp.max(v, axis=-1, keepdims=True)
        idx = jnp.min(jnp.where(v == m, lane, big), axis=-1, keepdims=True)
        return m, idx

    lg = jnp.where((lane >= GROUP_LANE0) & (lane < GROUP_LANE0 + N_GROUPS), logits, neg)
    mg, ig = first_argmax(lg)
    w_grp = 1.0 / jnp.sum(jnp.exp(lg - mg), axis=-1, keepdims=True)
    g_sel = ig - GROUP_LANE0
    le = jnp.where((lane < N_EXPERTS)
                   & (jnp.right_shift(lane, LOG2_EXPERTS_PER_GROUP) == g_sel), logits, neg)
    v1, i1 = first_argmax(le)
    le2 = jnp.where(lane == i1, neg, le)
    v2, i2 = first_argmax(le2)
    e2 = jnp.exp(v2 - v1)
    w1 = w_grp / (1.0 + e2)
    w2 = w_grp * e2 / (1.0 + e2)

    first_is_lower = i1 < i2
    gate_a = jnp.where(first_is_lower, w1, w2)
    gate_b = jnp.where(first_is_lower, w2, w1)
    la = jnp.minimum(i1, i2) - g_sel * EXPERTS_PER_GROUP
    lb = jnp.maximum(i1, i2) - g_sel * EXPERTS_PER_GROUP
    pair = jnp.right_shift(la * (2 * EXPERTS_PER_GROUP - 1 - la), 1) + lb - la - 1
    cls = g_sel * PAIRS_PER_GROUP + pair

    onehot = (lane == cls).astype(_F32)
    r_i = lax.broadcasted_iota(jnp.int32, (rows, rows), 0)
    c_i = lax.broadcasted_iota(jnp.int32, (rows, rows), 1)
    earlier = (c_i < r_i).astype(_BF)
    before = jnp.dot(earlier, onehot.astype(_BF), preferred_element_type=_F32) + carry
    rank = jnp.sum(onehot * before, axis=-1, keepdims=True)
    new_carry = carry + jnp.sum(onehot, axis=0, keepdims=True)

    route = (jnp.where(lane == 0, gate_a, 0.0) + jnp.where(lane == 1, gate_b, 0.0)
             + jnp.where(lane == 2, cls.astype(_F32), 0.0) + jnp.where(lane == 3, rank, 0.0))
    return route, new_carry


def _merge_kernel(hd_ref, fo_ref, ga_ref, gf_ref, x_ref, mod_ref, g2n_ref, wao_ref, wfo_ref,
                  wo_ref, wrhl_ref, br_ref, x1_ref, rows_ref, rec_ref, counts_ref, carry_ref):
    @pl.when((pl.program_id(0) == 0) & (pl.program_id(1) == 0))
    def _():
        carry_ref[...] = jnp.zeros_like(carry_ref)

    a = jnp.dot(hd_ref[0], wao_ref[...], preferred_element_type=_F32)
    ff = jnp.dot(fo_ref[0], wfo_ref[...], preferred_element_type=_F32)
    y = ga_ref[0].astype(_F32) * a + gf_ref[0].astype(_F32) * ff
    mix = jnp.dot(y.astype(_BF), wo_ref[...], preferred_element_type=_F32)
    x1 = x_ref[0] + mod_ref[0, 2:3, :] * mix
    x1_ref[0] = x1
    h2 = (_rms(x1) * g2n_ref[...]) * (1.0 + mod_ref[0, 4:5, :]) + mod_ref[0, 3:4, :]
    h2_hi = h2.astype(_BF)
    h2_lo = (h2 - h2_hi.astype(_F32)).astype(_BF)
    hi_both = jnp.dot(h2_hi, wrhl_ref[...], preferred_element_type=_F32)
    logits = (hi_both[:, :ROUTER_W] + hi_both[:, ROUTER_W:]
              + jnp.dot(h2_lo, wrhl_ref[:, :ROUTER_W], preferred_element_type=_F32)) + br_ref[...]
    route, new_carry = _route(logits, carry_ref[...])
    carry_ref[...] = new_carry
    counts_ref[...] = new_carry
    rec_ref[0, 0] = route.T[0:SUBLANES, :]
    half = D_MODEL // 2
    tm = h2.shape[0]
    packed = _pack_bf16_pair(h2[:, :half], h2[:, half:])
    rows_ref[0, :, 0:H2_SLAB_ROWS, :] = packed.reshape(tm, H2_SLAB_ROWS, LANES)
    rows_ref[0, :, ROUTE_SLAB_ROW:ROUTE_SLAB_ROW + 1, :] = (
        pltpu.bitcast(route, jnp.uint32).reshape(tm, 1, LANES))
    rows_ref[0, :, ROUTE_SLAB_ROW + 1:SUBLANES, :] = jnp.zeros(
        (tm, SUBLANES - ROUTE_SLAB_ROW - 1, LANES), jnp.uint32)


def _merge(heads, four, p, x, mod3, norm2_g, w_ao, w_fo, w_o, wr_hilo, b_r, tm):
    b, n, d = x.shape
    tok = lambda w: pl.BlockSpec((1, tm, w), lambda bi, i: (bi, i, 0))
    return pl.pallas_call(
        _merge_kernel,
        grid=(b, n // tm),
        in_specs=[tok(d), tok(F_W),
                  pl.BlockSpec((1, tm, d), lambda bi, i: (bi, i, P_GA // D_MODEL)),
                  pl.BlockSpec((1, tm, d), lambda bi, i: (bi, i, P_GF // D_MODEL)),
                  tok(d),
                  pl.BlockSpec((1, 6, d), lambda bi, i: (bi, 0, 0)),
                  pl.BlockSpec((1, d), lambda bi, i: (0, 0)),
                  _const_spec((d, d)), _const_spec((F_W, d)), _const_spec((d, d)),
                  _const_spec((d, 2 * ROUTER_W)),
                  pl.BlockSpec((1, ROUTER_W), lambda bi, i: (0, 0))],
        out_specs=[tok(d),
                   pl.BlockSpec((1, tm, SUBLANES, LANES), lambda bi, i: (bi, i, 0, 0)),
                   pl.BlockSpec((1, 1, SUBLANES, tm), lambda bi, i: (bi, i, 0, 0)),
                   pl.BlockSpec((1, ROUTER_W), lambda bi, i: (0, 0))],
        out_shape=[jax.ShapeDtypeStruct((b, n, d), _F32),
                   jax.ShapeDtypeStruct((b, n, SUBLANES, LANES), jnp.uint32),
                   jax.ShapeDtypeStruct((b, n // tm, SUBLANES, tm), _F32),
                   jax.ShapeDtypeStruct((1, ROUTER_W), _F32)],
        scratch_shapes=[pltpu.VMEM((1, ROUTER_W), _F32)],
        compiler_params=_cparams("arbitrary", "arbitrary"),
        name="merge_router",
    )(heads, four, p, p, x, mod3, norm2_g, w_ao, w_fo, w_o, wr_hilo, b_r)


def _issue_slab_copies(n, make_copy, slot_of):
    def group(g, c):
        r0 = g * DMA_ISSUE_GROUP
        slots = [slot_of(r0 + k) for k in range(DMA_ISSUE_GROUP)]
        for k in range(DMA_ISSUE_GROUP):
            make_copy(r0 + k, slots[k]).start(priority=k % 2)
        return c

    lax.fori_loop(0, n // DMA_ISSUE_GROUP, group, 0)


def _dispatch_kernel(pos_ref, pad_start_ref, pad_len_ref, src_ref, dst_ref, zero_ref, sem, pad_sem,
                     *, tm):
    base = pl.program_id(0) * tm

    @pl.when(pl.program_id(0) == 0)
    def _():
        zero_ref[...] = jnp.zeros_like(zero_ref)
        for c in range(N_CLASSES + 1):
            def start(r, carry, c=c):
                pltpu.make_async_copy(zero_ref, dst_ref.at[pad_start_ref[c] + r], pad_sem).start()
                return carry

            def drain(r, carry):
                pltpu.make_async_copy(zero_ref, dst_ref.at[0], pad_sem).wait()
                return carry

            lax.fori_loop(0, pad_len_ref[c], start, 0)
            lax.fori_loop(0, pad_len_ref[c], drain, 0)

    _issue_slab_copies(
        tm,
        lambda r, slot: pltpu.make_async_copy(src_ref.at[r], dst_ref.at[slot], sem),
        lambda r: pos_ref[base + r])
    pltpu.make_async_copy(src_ref, dst_ref.at[pl.ds(0, tm)], sem).wait()


def _dispatch(pos, pad_start, pad_len, slabs, n_sorted, tm):
    t = slabs.shape[0]
    return pl.pallas_call(
        functools.partial(_dispatch_kernel, tm=tm),
        grid_spec=pltpu.PrefetchScalarGridSpec(
            num_scalar_prefetch=3,
            grid=(t // tm,),
            in_specs=[pl.BlockSpec((tm, SUBLANES, LANES), lambda i, *_: (i, 0, 0))],
            out_specs=pl.BlockSpec(memory_space=pl.ANY),
            scratch_shapes=[pltpu.VMEM((SUBLANES, LANES), slabs.dtype),
                            pltpu.SemaphoreType.DMA(()), pltpu.SemaphoreType.DMA(())]),
        out_shape=jax.ShapeDtypeStruct((n_sorted,) + slabs.shape[1:], slabs.dtype),
        compiler_params=_cparams("arbitrary"),
        name="moe_dispatch",
    )(pos, pad_start, pad_len, slabs)


def _moe_kernel(ea_ref, eb_ref, valid_ref, slabs_ref, w1a_ref, w3a_ref, w2a_ref,
                w1b_ref, w3b_ref, w2b_ref, y_ref):
    del ea_ref, eb_ref
    i = pl.program_id(0)
    tm = slabs_ref.shape[0]

    @pl.when(valid_ref[i] != 0)
    def _():
        packed = slabs_ref[:, 0:H2_SLAB_ROWS, :].reshape(tm, H2_SLAB_ROWS * LANES)
        hi, lo = _unpack_bf16_pair(packed)
        t = jnp.concatenate([hi.astype(_BF), lo.astype(_BF)], axis=1)
        route = pltpu.bitcast(slabs_ref[:, ROUTE_SLAB_ROW, :], _F32)
        y = None
        for slot, (w1_ref, w3_ref, w2_ref) in enumerate(((w1a_ref, w3a_ref, w2a_ref),
                                                          (w1b_ref, w3b_ref, w2b_ref))):
            a = jnp.dot(t, w1_ref[0], preferred_element_type=_F32)
            u = jnp.dot(t, w3_ref[0], preferred_element_type=_F32)
            hid = ((a * _sigmoid(a)) * u * route[:, slot:slot + 1]).astype(_BF)
            part = jnp.dot(hid, w2_ref[0], preferred_element_type=_F32)
            y = part if y is None else y + part
        y_ref[...] = y.reshape(tm, SUBLANES, LANES)

    @pl.when(valid_ref[i] == 0)
    def _():
        y_ref[...] = jnp.zeros_like(y_ref)


def _moe(tile_ea, tile_eb, tile_valid, slabs_sorted, w1, w3, w2, tm):
    n_sorted = slabs_sorted.shape[0]
    d, hdn = w1.shape[1], w1.shape[2]
    assert d == SUBLANES * LANES
    pick = lambda which, a, b_: pl.BlockSpec(
        (1, a, b_), lambda i, ea, eb, valid: ((ea, eb)[which][i], 0, 0))
    slab_tile = pl.BlockSpec((tm, SUBLANES, LANES), lambda i, ea, eb, valid: (i, 0, 0))
    return pl.pallas_call(
        _moe_kernel,
        grid_spec=pltpu.PrefetchScalarGridSpec(
            num_scalar_prefetch=3,
            grid=(n_sorted // tm,),
            in_specs=[slab_tile,
                      pick(0, d, hdn), pick(0, d, hdn), pick(0, hdn, d),
                      pick(1, d, hdn), pick(1, d, hdn), pick(1, hdn, d)],
            out_specs=slab_tile),
        out_shape=jax.ShapeDtypeStruct((n_sorted, SUBLANES, LANES), _F32),
        compiler_params=_cparams("arbitrary"),
        name="moe_experts",
    )(tile_ea, tile_eb, tile_valid, slabs_sorted, w1, w3, w2, w1, w3, w2)


def _combine_kernel(pos_ref, y_hbm_ref, x1_ref, mod_ref, fg_ref, o_ref, ybuf_ref, sems, *, tm):
    i = pl.program_id(0)
    n_steps = pl.num_programs(0)

    def issue(tile, slot):
        _issue_slab_copies(
            tm,
            lambda r, src: pltpu.make_async_copy(y_hbm_ref.at[src], ybuf_ref.at[slot, r],
                                                 sems.at[slot]),
            lambda r: pos_ref[tile * tm + r])

    @pl.when(i == 0)
    def _():
        issue(0, 0)

    @pl.when(i + 1 < n_steps)
    def _():
        issue(i + 1, (i + 1) % 2)

    slot = i % 2
    pltpu.make_async_copy(y_hbm_ref.at[pl.ds(0, tm)], ybuf_ref.at[slot], sems.at[slot]).wait()
    y = ybuf_ref[slot].reshape(tm, SUBLANES * LANES)
    xo = x1_ref[...] + mod_ref[0, 5:6, :] * y
    o_ref[...] = _rms(xo) * fg_ref[...]


def _combine(pos, y_sorted, x1, mod3, final_g, seq, tm):
    t, d = x1.shape
    tiles_per_batch = seq // tm
    return pl.pallas_call(
        functools.partial(_combine_kernel, tm=tm),
        grid_spec=pltpu.PrefetchScalarGridSpec(
            num_scalar_prefetch=1,
            grid=(t // tm,),
            in_specs=[pl.BlockSpec(memory_space=pl.ANY),
                      pl.BlockSpec((tm, d), lambda i, pos: (i, 0)),
                      pl.BlockSpec((1, 6, d), lambda i, pos: (i // tiles_per_batch, 0, 0)),
                      pl.BlockSpec((1, d), lambda i, pos: (0, 0))],
            out_specs=pl.BlockSpec((tm, d), lambda i, pos: (i, 0)),
            scratch_shapes=[pltpu.VMEM((2, tm, SUBLANES, LANES), _F32),
                            pltpu.SemaphoreType.DMA((2,))]),
        out_shape=jax.ShapeDtypeStruct((t, d), _F32),
        compiler_params=_cparams("arbitrary"),
        name="moe_combine",
    )(pos, y_sorted, x1, mod3, final_g)


def _routing_tables(route_cls, route_rank, counts, n_tokens, tm):
    sizes = ((counts + (tm - 1)) // tm) * tm
    ends = jnp.cumsum(sizes)
    starts = ends - sizes
    n_tiles = n_tokens // tm + N_CLASSES
    tile_start = jnp.arange(n_tiles, dtype=jnp.int32) * tm
    pos = route_rank
    tile_cls = jnp.zeros((n_tiles,), jnp.int32)
    for c in range(N_CLASSES):
        pos = pos + jnp.where(route_cls == c, starts[c], 0)
        tile_cls = tile_cls + (tile_start >= ends[c]).astype(jnp.int32)
    tile_cls = jnp.minimum(tile_cls, N_CLASSES - 1)
    tile_valid = (tile_start < ends[-1]).astype(jnp.int32)
    pairs = [(a, b_) for a in range(EXPERTS_PER_GROUP) for b_ in range(a + 1, EXPERTS_PER_GROUP)]
    grp, pair = tile_cls // PAIRS_PER_GROUP, tile_cls % PAIRS_PER_GROUP
    tile_ea = grp * EXPERTS_PER_GROUP
    tile_eb = grp * EXPERTS_PER_GROUP
    for k, (a, b_) in enumerate(pairs):
        tile_ea = tile_ea + jnp.where(pair == k, a, 0)
        tile_eb = tile_eb + jnp.where(pair == k, b_, 0)
    pad_start = jnp.concatenate([starts + counts, ends[-1:]]).astype(jnp.int32)
    pad_len = jnp.concatenate([sizes - counts, n_tiles * tm - ends[-1:]]).astype(jnp.int32)
    return (pos.astype(jnp.int32), pad_start, pad_len, tile_ea.astype(jnp.int32),
            tile_eb.astype(jnp.int32), tile_valid, n_tiles)


def _rope_tables(n):
    inv = (1.0 / (ROPE_BASE ** (np.arange(ROT_FREQS, dtype=np.float32) / ROT_FREQS))).astype(np.float32)
    pos = np.arange(n)
    row = (pos // GRID_W).astype(np.float32)[:, None] * inv[None, :]
    col = (pos % GRID_W).astype(np.float32)[:, None] * inv[None, :]
    cos64 = np.concatenate([np.cos(row), np.cos(row), np.cos(col), np.cos(col)], axis=1)
    sin64 = np.concatenate([-np.sin(row), np.sin(row), -np.sin(col), np.sin(col)], axis=1)
    tile = lambda a: np.tile(a.astype(np.float32), (1, LANES // HEAD_DIM))
    return jnp.asarray(tile(cos64)), jnp.asarray(tile(sin64))


def _dft_cos_sin(n):
    k = np.arange(n, dtype=np.int64)
    ang = (2.0 * np.pi / n) * ((k[:, None] * k[None, :]) % n).astype(np.float64)
    return np.cos(ang), np.sin(ang)


def _dft_tables(n):
    c_ch, s_ch = _dft_cos_sin(F_GROUP_DIM)
    c_seq, s_seq = _dft_cos_sin(n)
    cs_ch = np.concatenate([c_ch, s_ch], axis=1).astype(np.float32)
    cs_seq = np.concatenate([c_seq, -s_seq], axis=1).astype(np.float32)
    return jnp.asarray(cs_ch.astype(_BF)), jnp.asarray(cs_seq.astype(_BF))


def kernel(x, c, ctx, c_ctx, w_mod, b_mod, norm1_g, norm2_g, w_in, lam_q1, lam_k1, lam_q2, lam_k2,
           subln_g, w_attn_out, w_four_out, w_out, w_router_group, b_router_group, w_router_expert,
           b_router_expert, w_exp_gate, w_exp_up, w_exp_down, final_g):
    b, n, d = x.shape
    assert w_mod.shape[0] == 1, "depth-1 stack"
    assert b + 1 <= MOD_ROWS

    cc = jnp.concatenate([c, c_ctx[None, :], jnp.zeros((MOD_ROWS - b - 1, d), _F32)], axis=0)
    mod3 = _modulation(cc, w_mod[0], b_mod).reshape(MOD_ROWS, 6, d)

    lam = (jnp.exp(jnp.sum(lam_q1[0] * lam_k1[0])) - jnp.exp(jnp.sum(lam_q2[0] * lam_k2[0]))
           + LAM_INIT).reshape(1).astype(_F32)

    w = w_in[0]
    scale = HEAD_DIM ** -0.5 * math.log2(math.e)
    w_lat = jnp.concatenate([w[:, REF_Q:REF_K] * scale, w[:, REF_K:REF_F],
                             w[:, REF_GA:REF_END], w[:, REF_F:REF_GA]], axis=1).astype(_BF)
    w_ctx = w[:, REF_K:REF_F].astype(_BF)
    lat_kinds = ("rope",) * ((P_V - P_Q) // PROJ_CW) + ("plain",) * ((P_GA - P_V) // PROJ_CW) \
        + ("sigmoid",) * ((P_F - P_GA) // PROJ_CW) + ("plain",) * ((P_W - P_F) // PROJ_CW)
    ctx_kinds = ("plain",) * (w_ctx.shape[1] // PROJ_CW)

    cos_t, sin_t = _rope_tables(n)
    p = _in_projection(x, mod3, lambda bi: bi, norm1_g, w_lat, cos_t, sin_t, lat_kinds, PROJ_TM)
    kvc = _in_projection(ctx, mod3, lambda bi: b, norm1_g, w_ctx, cos_t, sin_t, ctx_kinds,
                         ctx.shape[1])

    heads = _attention(lam, p, kvc, subln_g, ATTN_TQ)
    cs_ch, cs_seq = _dft_tables(n)
    four = _fnet(p, cs_ch, cs_seq)

    w_r = jnp.concatenate([w_router_expert[0], w_router_group[0],
                           jnp.zeros((d, ROUTER_W - N_EXPERTS - N_GROUPS), _F32)], axis=1)
    b_r = jnp.concatenate([b_router_expert[0], b_router_group[0],
                           jnp.zeros((ROUTER_W - N_EXPERTS - N_GROUPS,), _F32)])[None, :]
    wr_hi = w_r.astype(_BF)
    wr_lo = (w_r - wr_hi.astype(_F32)).astype(_BF)
    x1, rows, rec, counts = _merge(heads, four, p, x, mod3, norm2_g, w_attn_out[0].astype(_BF),
                                   w_four_out[0].astype(_BF), w_out[0].astype(_BF),
                                   jnp.concatenate([wr_hi, wr_lo], axis=1), b_r, MERGE_TM)

    t = b * n
    slabs = rows.reshape(t, SUBLANES, LANES)
    rec = rec.reshape(t // MERGE_TM, SUBLANES, MERGE_TM)
    pos, pad_start, pad_len, tile_ea, tile_eb, tile_valid, n_tiles = _routing_tables(
        rec[:, 2, :].reshape(t).astype(jnp.int32), rec[:, 3, :].reshape(t).astype(jnp.int32),
        counts[0, :N_CLASSES].astype(jnp.int32), t, MOE_TM)

    slabs_sorted = _dispatch(pos, pad_start, pad_len, slabs, n_tiles * MOE_TM, DISPATCH_TM)
    y_sorted = _moe(tile_ea, tile_eb, tile_valid, slabs_sorted, w_exp_gate[0].astype(_BF),
                    w_exp_up[0].astype(_BF), w_exp_down[0].astype(_BF), MOE_TM)
    out = _combine(pos, y_sorted, x1.reshape(t, d), mod3, final_g[None, :], n, COMBINE_TM)
    return out.reshape(b, n, d)
```

```python
import functools
import math

import jax
import jax.numpy as jnp
import numpy as np
from jax import lax
from jax.experimental import pallas as pl
from jax.experimental.pallas import tpu as pltpu

D_MODEL = 1024
GRID_W = 64
EPS = 1e-6
N_HEADS = 8
HEAD_DIM = 64
HEAD_W = 2 * HEAD_DIM
ROT_FREQS = HEAD_DIM // 4
ROPE_BASE = 10000.0
F_GROUPS = 4
F_GROUP_DIM = 128
F_W = F_GROUPS * F_GROUP_DIM
N_GROUPS = 4
EXPERTS_PER_GROUP = 4
LOG2_EXPERTS_PER_GROUP = 2
N_EXPERTS = N_GROUPS * EXPERTS_PER_GROUP
PAIRS_PER_GROUP = EXPERTS_PER_GROUP * (EXPERTS_PER_GROUP - 1) // 2
N_CLASSES = N_GROUPS * PAIRS_PER_GROUP
EXPERT_HIDDEN = 512
LAM_INIT = 0.8 - 0.6 * math.exp(-0.3 * 0)

REF_Q, REF_K, REF_V, REF_F, REF_GA, REF_GF, REF_END = 0, 1024, 2048, 3072, 3584, 4608, 5632
P_Q, P_K, P_V, P_GA, P_GF, P_F, P_W = 0, 1024, 2048, 3072, 4096, 5120, 5632

LANES = 128
SUBLANES = 8
VMEM_LIMIT_BYTES = 56 * 1024 * 1024

PROJ_TM = 1024
PROJ_CW = 512
ATTN_TQ = 256
ATTN_SUM_ROWS = 16
ATTN_SCORE_BUFS = 3
FNET_LEVELS = 2
FNET_CHUNK = 256
MERGE_TM = 512
MOE_TM = 256
COMBINE_TM = 512
DISPATCH_TM = 2048
DMA_ISSUE_GROUP = 8
MOD_ROWS = 40
ROUTER_W = LANES
GROUP_LANE0 = N_EXPERTS
H2_SLAB_ROWS = D_MODEL // 2 // LANES
ROUTE_SLAB_ROW = H2_SLAB_ROWS

_BF = jnp.bfloat16
_F32 = jnp.float32


def _cparams(*sem):
    return pltpu.CompilerParams(dimension_semantics=sem, vmem_limit_bytes=VMEM_LIMIT_BYTES)


def _const_spec(shape):
    return pl.BlockSpec(shape, lambda *_: (0,) * len(shape), pipeline_mode=pl.Buffered(1))


def _rms(x):
    return x * lax.rsqrt(jnp.mean(x * x, axis=-1, keepdims=True) + EPS)


def _sigmoid(x):
    return 1.0 / (1.0 + jnp.exp(-x))


def _mod_kernel(cc_ref, w_ref, b_ref, o_ref):
    cc = cc_ref[...]
    s = cc * _sigmoid(cc)
    o_ref[...] = jnp.dot(s, w_ref[...], preferred_element_type=_F32,
                         precision=lax.Precision.HIGHEST) + b_ref[...]


def _modulation(cc, w_mod, b_mod):
    n = w_mod.shape[1]
    bn = D_MODEL
    return pl.pallas_call(
        _mod_kernel,
        grid=(n // bn,),
        in_specs=[pl.BlockSpec((MOD_ROWS, D_MODEL), lambda j: (0, 0)),
                  pl.BlockSpec((D_MODEL, bn), lambda j: (0, j)),
                  pl.BlockSpec((1, bn), lambda j: (0, j))],
        out_specs=pl.BlockSpec((MOD_ROWS, bn), lambda j: (0, j)),
        out_shape=jax.ShapeDtypeStruct((MOD_ROWS, n), _F32),
        compiler_params=_cparams("arbitrary"),
        name="modulation",
    )(cc, w_mod, b_mod)


def _rope(acc, cos_ref, sin_ref):
    cos = cos_ref[...]
    sin = sin_ref[...]
    lane = lax.broadcasted_iota(jnp.int32, (1, LANES), 1)
    first_half = (lane % (2 * ROT_FREQS)) < ROT_FREQS
    outs = []
    for s in range(acc.shape[1] // LANES):
        xs = acc[:, s * LANES:(s + 1) * LANES]
        partner = jnp.where(first_half,
                            pltpu.roll(xs, LANES - ROT_FREQS, 1),
                            pltpu.roll(xs, ROT_FREQS, 1))
        outs.append(xs * cos + partner * sin)
    return jnp.concatenate(outs, axis=1)


def _inproj_kernel(x_ref, mod_ref, g_ref, w_ref, cos_ref, sin_ref, o_ref, *, chunk_kinds):
    x = x_ref[0]
    shift = mod_ref[0, 0:1, :]
    scale = mod_ref[0, 1:2, :]
    h = (_rms(x) * g_ref[...]) * (1.0 + scale) + shift
    hb = h.astype(_BF)
    for j, kind in enumerate(chunk_kinds):
        cols = slice(j * PROJ_CW, (j + 1) * PROJ_CW)
        acc = jnp.dot(hb, w_ref[:, cols], preferred_element_type=_F32)
        if kind == "rope":
            acc = _rope(acc, cos_ref, sin_ref)
        elif kind == "sigmoid":
            acc = _sigmoid(acc)
        o_ref[0, :, cols] = acc.astype(_BF)


def _in_projection(x, mod3, mod_row_of_batch, g, w, cos_t, sin_t, chunk_kinds, tm):
    b, n, d = x.shape
    width = w.shape[1]
    assert width == len(chunk_kinds) * PROJ_CW and n % tm == 0
    return pl.pallas_call(
        functools.partial(_inproj_kernel, chunk_kinds=chunk_kinds),
        grid=(b, n // tm),
        in_specs=[pl.BlockSpec((1, tm, d), lambda bi, i: (bi, i, 0)),
                  pl.BlockSpec((1, 6, d), lambda bi, i: (mod_row_of_batch(bi), 0, 0)),
                  pl.BlockSpec((1, d), lambda bi, i: (0, 0)),
                  _const_spec((d, width)),
                  pl.BlockSpec((tm, LANES), lambda bi, i: (i, 0)),
                  pl.BlockSpec((tm, LANES), lambda bi, i: (i, 0))],
        out_specs=pl.BlockSpec((1, tm, width), lambda bi, i: (bi, i, 0)),
        out_shape=jax.ShapeDtypeStruct((b, n, width), _BF),
        compiler_params=_cparams("parallel", "arbitrary"),
        name="in_projection",
    )(x, mod3, g, w, cos_t, sin_t)


def _attn_kernel(lam_ref, q_ref, kl_ref, vl_ref, kc_ref, vc_ref, sg_ref, o_ref,
                 kcat_ref, vt_ref, *bufs, tq):
    nc, n = kc_ref.shape[1], kl_ref.shape[1]
    kcat_ref[0:nc, :] = kc_ref[0]
    kcat_ref[nc:nc + n, :] = kl_ref[0]
    vt_ref[0:HEAD_W, 0:nc] = vc_ref[0].astype(_F32).T.astype(_BF)
    vt_ref[0:HEAD_W, nc:nc + n] = vl_ref[0].astype(_F32).T.astype(_BF)
    vt_ref[HEAD_W:, :] = jnp.ones((ATTN_SUM_ROWS, nc + n), _BF)
    lane = lax.broadcasted_iota(jnp.int32, (1, HEAD_W), 1)
    map_lanes = (lane < HEAD_DIM, lane >= HEAD_DIM)
    nt = (((1,), (1,)), ((), ()))
    lam = lam_ref[0]
    post_scale = sg_ref[...] * (1.0 - LAM_INIT)

    def scores(j):
        q = q_ref[0, j * tq:(j + 1) * tq, :]
        for mp in range(2):
            qm = jnp.where(map_lanes[mp], q, jnp.zeros_like(q))
            bufs[j % ATTN_SCORE_BUFS][mp] = lax.dot_general(kcat_ref[...], qm, nt,
                                                            preferred_element_type=_F32)

    def finish(j):
        outs = []
        for mp in range(2):
            s = bufs[j % ATTN_SCORE_BUFS][mp]
            e = jnp.exp2(s - jnp.max(s, axis=0, keepdims=True)).astype(_BF)
            r = jnp.dot(vt_ref[...], e, preferred_element_type=_F32)
            outs.append(r[:HEAD_W, :] / r[HEAD_W:HEAD_W + 1, :])
        heads = (outs[0] - lam * outs[1]).T
        o_ref[0, j * tq:(j + 1) * tq, :] = (_rms(heads) * post_scale).astype(_BF)

    n_sub = n // tq
    for j in range(ATTN_SCORE_BUFS - 1):
        scores(j)
    for j in range(n_sub):
        if j + ATTN_SCORE_BUFS - 1 < n_sub:
            scores(j + ATTN_SCORE_BUFS - 1)
        finish(j)


def _attention(lam, p, kvc, subln_g, tq):
    b, n, _ = p.shape
    nc = kvc.shape[1]
    qb, kb, vb = P_Q // HEAD_W, P_K // HEAD_W, P_V // HEAD_W
    seq = lambda blk: pl.BlockSpec((1, n, HEAD_W), lambda bi, h: (bi, 0, blk + h))
    return pl.pallas_call(
        functools.partial(_attn_kernel, tq=tq),
        grid=(b, N_HEADS),
        in_specs=[pl.BlockSpec(memory_space=pltpu.SMEM),
                  seq(qb), seq(kb), seq(vb),
                  pl.BlockSpec((1, nc, HEAD_W), lambda bi, h: (bi, 0, h)),
                  pl.BlockSpec((1, nc, HEAD_W), lambda bi, h: (bi, 0, N_HEADS + h)),
                  pl.BlockSpec((1, HEAD_W), lambda bi, h: (0, 0))],
        out_specs=seq(0),
        out_shape=jax.ShapeDtypeStruct((b, n, N_HEADS * HEAD_W), _BF),
        scratch_shapes=[pltpu.VMEM((nc + n, HEAD_W), _BF),
                        pltpu.VMEM((HEAD_W + ATTN_SUM_ROWS, nc + n), _BF),
                        ] + [pltpu.VMEM((2, nc + n, tq), _F32)] * ATTN_SCORE_BUFS,
        compiler_params=_cparams("parallel", "arbitrary"),
        name="diff_attention",
    )(lam, p, p, p, kvc, kvc, subln_g)


def _bit_reverse(j, bits):
    return int(format(j, "0%db" % bits)[::-1], 2) if bits else 0


def _fnet_kernel(f_ref, cs_ch_ref, cs_seq_ref, *rest, n):
    tw_refs = rest[:2 * FNET_LEVELS]
    o_ref, zr_ref, zi_ref, rhs_ref = rest[2 * FNET_LEVELS:]
    for g in range(F_GROUPS):
        cols = slice(g * F_GROUP_DIM, (g + 1) * F_GROUP_DIM)
        t = jnp.dot(f_ref[0, :, cols], cs_ch_ref[...], preferred_element_type=_F32)
        zr_ref[:, cols] = t[:, :F_GROUP_DIM]
        zi_ref[:, cols] = t[:, F_GROUP_DIM:]
    for level in range(FNET_LEVELS):
        m = n >> level
        half = m // 2
        cos_ref, sin_ref = tw_refs[2 * level], tw_refs[2 * level + 1]
        for seg in range(1 << level):
            for c0 in range(0, half, FNET_CHUNK):
                rows_t = slice(seg * m + c0, seg * m + c0 + FNET_CHUNK)
                rows_b = slice(seg * m + half + c0, seg * m + half + c0 + FNET_CHUNK)
                tr, ti = zr_ref[rows_t, :], zi_ref[rows_t, :]
                br, bi = zr_ref[rows_b, :], zi_ref[rows_b, :]
                cw, sw = cos_ref[c0:c0 + FNET_CHUNK, :], sin_ref[c0:c0 + FNET_CHUNK, :]
                dr, di = tr - br, ti - bi
                zr_ref[rows_t, :] = tr + br
                zi_ref[rows_t, :] = ti + bi
                zr_ref[rows_b, :] = dr * cw + di * sw
                zi_ref[rows_b, :] = di * cw - dr * sw
    m = n >> FNET_LEVELS
    for j in range(1 << FNET_LEVELS):
        cols = slice(_bit_reverse(j, FNET_LEVELS) * F_W, (_bit_reverse(j, FNET_LEVELS) + 1) * F_W)
        rhs_ref[0:m, cols] = zr_ref[j * m:(j + 1) * m, :].astype(_BF)
        rhs_ref[m:2 * m, cols] = zi_ref[j * m:(j + 1) * m, :].astype(_BF)
    ortho = 1.0 / math.sqrt(n * F_GROUP_DIM)
    y = jnp.dot(cs_seq_ref[...], rhs_ref[...], preferred_element_type=_F32)
    o_ref[0] = (y * ortho).astype(_BF)


def _fnet(p, cs_ch, cs_seq, twiddles):
    b, n, _ = p.shape
    m = n >> FNET_LEVELS
    wide = F_W << FNET_LEVELS
    out = pl.pallas_call(
        functools.partial(_fnet_kernel, n=n),
        grid=(b,),
        in_specs=[pl.BlockSpec((1, n, F_W), lambda bi: (bi, 0, P_F // F_W)),
                  _const_spec((F_GROUP_DIM, 2 * F_GROUP_DIM)),
                  _const_spec((m, 2 * m))] + [_const_spec(t.shape) for t in twiddles],
        out_specs=pl.BlockSpec((1, m, wide), lambda bi: (bi, 0, 0)),
        out_shape=jax.ShapeDtypeStruct((b, m, wide), _BF),
        scratch_shapes=[pltpu.VMEM((n, F_W), _F32), pltpu.VMEM((n, F_W), _F32),
                        pltpu.VMEM((2 * m, wide), _BF)],
        compiler_params=_cparams("arbitrary"),
        name="fnet_dft",
    )(p, cs_ch, cs_seq, *twiddles)
    return out.reshape(b, n, F_W)


def _pack_bf16_pair(hi, lo):
    hi_bits = pltpu.bitcast(hi.astype(_BF).astype(_F32), jnp.uint32)
    lo_bits = pltpu.bitcast(lo.astype(_BF).astype(_F32), jnp.uint32)
    return hi_bits | lax.shift_right_logical(lo_bits, jnp.uint32(16))


def _unpack_bf16_pair(packed):
    hi = pltpu.bitcast(packed & jnp.uint32(0xFFFF0000), _F32)
    lo = pltpu.bitcast(lax.shift_left(packed, jnp.uint32(16)), _F32)
    return hi, lo


def _route(logits, carry):
    rows = logits.shape[0]
    lane = lax.broadcasted_iota(jnp.int32, logits.shape, 1)
    neg = jnp.float32(-jnp.inf)
    big = jnp.int32(ROUTER_W)

    def first_argmax(v):
        m = jnp.max(v, axis=-1, keepdims=True)
        idx = jnp.min(jnp.where(v == m, lane, big), axis=-1, keepdims=True)
        return m, idx

    lg = jnp.where((lane >= GROUP_LANE0) & (lane < GROUP_LANE0 + N_GROUPS), logits, neg)
    mg, ig = first_argmax(lg)
    w_grp = 1.0 / jnp.sum(jnp.exp(lg - mg), axis=-1, keepdims=True)
    g_sel = ig - GROUP_LANE0
    le = jnp.where((lane < N_EXPERTS)
                   & (jnp.right_shift(lane, LOG2_EXPERTS_PER_GROUP) == g_sel), logits, neg)
    v1, i1 = first_argmax(le)
    le2 = jnp.where(lane == i1, neg, le)
    v2, i2 = first_argmax(le2)
    e2 = jnp.exp(v2 - v1)
    w1 = w_grp / (1.0 + e2)
    w2 = w_grp * e2 / (1.0 + e2)

    first_is_lower = i1 < i2
    gate_a = jnp.where(first_is_lower, w1, w2)
    gate_b = jnp.where(first_is_lower, w2, w1)
    la = jnp.minimum(i1, i2) - g_sel * EXPERTS_PER_GROUP
    lb = jnp.maximum(i1, i2) - g_sel * EXPERTS_PER_GROUP
    pair = jnp.right_shift(la * (2 * EXPERTS_PER_GROUP - 1 - la), 1) + lb - la - 1
    cls = g_sel * PAIRS_PER_GROUP + pair

    onehot = (lane == cls).astype(_F32)
    r_i = lax.broadcasted_iota(jnp.int32, (rows, rows), 0)
    c_i = lax.broadcasted_iota(jnp.int32, (rows, rows), 1)
    earlier = (c_i < r_i).astype(_BF)
    before = jnp.dot(earlier, onehot.astype(_BF), preferred_element_type=_F32) + carry
    rank = jnp.sum(onehot * before, axis=-1, keepdims=True)
    new_carry = carry + jnp.sum(onehot, axis=0, keepdims=True)

    route = (jnp.where(lane == 0, gate_a, 0.0) + jnp.where(lane == 1, gate_b, 0.0)
             + jnp.where(lane == 2, cls.astype(_F32), 0.0) + jnp.where(lane == 3, rank, 0.0))
    return route, new_carry


def _merge_kernel(hd_ref, fo_ref, ga_ref, gf_ref, x_ref, mod_ref, g2n_ref, wao_ref, wfo_ref,
                  wo_ref, wrhl_ref, br_ref, x1_ref, rows_ref, rec_ref, counts_ref, carry_ref):
    @pl.when((pl.program_id(0) == 0) & (pl.program_id(1) == 0))
    def _():
        carry_ref[...] = jnp.zeros_like(carry_ref)

    a = jnp.dot(hd_ref[0], wao_ref[...], preferred_element_type=_F32)
    ff = jnp.dot(fo_ref[0], wfo_ref[...], preferred_element_type=_F32)
    y = ga_ref[0].astype(_F32) * a + gf_ref[0].astype(_F32) * ff
    mix = jnp.dot(y.astype(_BF), wo_ref[...], preferred_element_type=_F32)
    x1 = x_ref[0] + mod_ref[0, 2:3, :] * mix
    x1_ref[0] = x1
    h2 = (_rms(x1) * g2n_ref[...]) * (1.0 + mod_ref[0, 4:5, :]) + mod_ref[0, 3:4, :]
    h2_hi = h2.astype(_BF)
    h2_lo = (h2 - h2_hi.astype(_F32)).astype(_BF)
    hi_both = jnp.dot(h2_hi, wrhl_ref[...], preferred_element_type=_F32)
    logits = (hi_both[:, :ROUTER_W] + hi_both[:, ROUTER_W:]
              + jnp.dot(h2_lo, wrhl_ref[:, :ROUTER_W], preferred_element_type=_F32)) + br_ref[...]
    route, new_carry = _route(logits, carry_ref[...])
    carry_ref[...] = new_carry
    counts_ref[...] = new_carry
    rec_ref[0, 0] = route.T[0:SUBLANES, :]
    half = D_MODEL // 2
    tm = h2.shape[0]
    packed = _pack_bf16_pair(h2[:, :half], h2[:, half:])
    rows_ref[0, :, 0:H2_SLAB_ROWS, :] = packed.reshape(tm, H2_SLAB_ROWS, LANES)
    rows_ref[0, :, ROUTE_SLAB_ROW:ROUTE_SLAB_ROW + 1, :] = (
        pltpu.bitcast(route, jnp.uint32).reshape(tm, 1, LANES))
    rows_ref[0, :, ROUTE_SLAB_ROW + 1:SUBLANES, :] = jnp.zeros(
        (tm, SUBLANES - ROUTE_SLAB_ROW - 1, LANES), jnp.uint32)


def _merge(heads, four, p, x, mod3, norm2_g, w_ao, w_fo, w_o, wr_hilo, b_r, tm):
    b, n, d = x.shape
    tok = lambda w: pl.BlockSpec((1, tm, w), lambda bi, i: (bi, i, 0))
    return pl.pallas_call(
        _merge_kernel,
        grid=(b, n // tm),
        in_specs=[tok(d), tok(F_W),
                  pl.BlockSpec((1, tm, d), lambda bi, i: (bi, i, P_GA // D_MODEL)),
                  pl.BlockSpec((1, tm, d), lambda bi, i: (bi, i, P_GF // D_MODEL)),
                  tok(d),
                  pl.BlockSpec((1, 6, d), lambda bi, i: (bi, 0, 0)),
                  pl.BlockSpec((1, d), lambda bi, i: (0, 0)),
                  _const_spec((d, d)), _const_spec((F_W, d)), _const_spec((d, d)),
                  _const_spec((d, 2 * ROUTER_W)),
                  pl.BlockSpec((1, ROUTER_W), lambda bi, i: (0, 0))],
        out_specs=[tok(d),
                   pl.BlockSpec((1, tm, SUBLANES, LANES), lambda bi, i: (bi, i, 0, 0)),
                   pl.BlockSpec((1, 1, SUBLANES, tm), lambda bi, i: (bi, i, 0, 0)),
                   pl.BlockSpec((1, ROUTER_W), lambda bi, i: (0, 0))],
        out_shape=[jax.ShapeDtypeStruct((b, n, d), _F32),
                   jax.ShapeDtypeStruct((b, n, SUBLANES, LANES), jnp.uint32),
                   jax.ShapeDtypeStruct((b, n // tm, SUBLANES, tm), _F32),
                   jax.ShapeDtypeStruct((1, ROUTER_W), _F32)],
        scratch_shapes=[pltpu.VMEM((1, ROUTER_W), _F32)],
        compiler_params=_cparams("arbitrary", "arbitrary"),
        name="merge_router",
    )(heads, four, p, p, x, mod3, norm2_g, w_ao, w_fo, w_o, wr_hilo, b_r)


def _issue_slab_copies(n, make_copy, slot_of):
    def group(g, c):
        r0 = g * DMA_ISSUE_GROUP
        slots = [slot_of(r0 + k) for k in range(DMA_ISSUE_GROUP)]
        for k in range(DMA_ISSUE_GROUP):
            make_copy(r0 + k, slots[k]).start(priority=k % 2)
        return c

    lax.fori_loop(0, n // DMA_ISSUE_GROUP, group, 0)


def _dispatch_kernel(pos_ref, pad_start_ref, pad_len_ref, src_ref, dst_ref, zero_ref, sem, pad_sem,
                     *, tm):
    base = pl.program_id(0) * tm

    @pl.when(pl.program_id(0) == 0)
    def _():
        zero_ref[...] = jnp.zeros_like(zero_ref)
        for c in range(N_CLASSES + 1):
            def start(r, carry, c=c):
                pltpu.make_async_copy(zero_ref, dst_ref.at[pad_start_ref[c] + r], pad_sem).start()
                return carry

            def drain(r, carry):
                pltpu.make_async_copy(zero_ref, dst_ref.at[0], pad_sem).wait()
                return carry

            lax.fori_loop(0, pad_len_ref[c], start, 0)
            lax.fori_loop(0, pad_len_ref[c], drain, 0)

    _issue_slab_copies(
        tm,
        lambda r, slot: pltpu.make_async_copy(src_ref.at[r], dst_ref.at[slot], sem),
        lambda r: pos_ref[base + r])
    pltpu.make_async_copy(src_ref, dst_ref.at[pl.ds(0, tm)], sem).wait()


def _dispatch(pos, pad_start, pad_len, slabs, n_sorted, tm):
    t = slabs.shape[0]
    return pl.pallas_call(
        functools.partial(_dispatch_kernel, tm=tm),
        grid_spec=pltpu.PrefetchScalarGridSpec(
            num_scalar_prefetch=3,
            grid=(t // tm,),
            in_specs=[pl.BlockSpec((tm, SUBLANES, LANES), lambda i, *_: (i, 0, 0))],
            out_specs=pl.BlockSpec(memory_space=pl.ANY),
            scratch_shapes=[pltpu.VMEM((SUBLANES, LANES), slabs.dtype),
                            pltpu.SemaphoreType.DMA(()), pltpu.SemaphoreType.DMA(())]),
        out_shape=jax.ShapeDtypeStruct((n_sorted,) + slabs.shape[1:], slabs.dtype),
        compiler_params=_cparams("arbitrary"),
        name="moe_dispatch",
    )(pos, pad_start, pad_len, slabs)


def _moe_kernel(ea_ref, eb_ref, valid_ref, slabs_ref, w1a_ref, w3a_ref, w2a_ref,
                w1b_ref, w3b_ref, w2b_ref, y_ref):
    del ea_ref, eb_ref
    i = pl.program_id(0)
    tm = slabs_ref.shape[0]

    @pl.when(valid_ref[i] != 0)
    def _():
        packed = slabs_ref[:, 0:H2_SLAB_ROWS, :].reshape(tm, H2_SLAB_ROWS * LANES)
        hi, lo = _unpack_bf16_pair(packed)
        t = jnp.concatenate([hi.astype(_BF), lo.astype(_BF)], axis=1)
        route = pltpu.bitcast(slabs_ref[:, ROUTE_SLAB_ROW, :], _F32)
        y = None
        for slot, (w1_ref, w3_ref, w2_ref) in enumerate(((w1a_ref, w3a_ref, w2a_ref),
                                                          (w1b_ref, w3b_ref, w2b_ref))):
            a = jnp.dot(t, w1_ref[0], preferred_element_type=_F32)
            u = jnp.dot(t, w3_ref[0], preferred_element_type=_F32)
            hid = ((a * _sigmoid(a)) * u * route[:, slot:slot + 1]).astype(_BF)
            part = jnp.dot(hid, w2_ref[0], preferred_element_type=_F32)
            y = part if y is None else y + part
        y_ref[...] = y.reshape(tm, SUBLANES, LANES)

    @pl.when(valid_ref[i] == 0)
    def _():
        y_ref[...] = jnp.zeros_like(y_ref)


def _moe(tile_ea, tile_eb, tile_valid, slabs_sorted, w1, w3, w2, tm):
    n_sorted = slabs_sorted.shape[0]
    d, hdn = w1.shape[1], w1.shape[2]
    assert d == SUBLANES * LANES
    pick = lambda which, a, b_: pl.BlockSpec(
        (1, a, b_), lambda i, ea, eb, valid: ((ea, eb)[which][i], 0, 0))
    slab_tile = pl.BlockSpec((tm, SUBLANES, LANES), lambda i, ea, eb, valid: (i, 0, 0))
    return pl.pallas_call(
        _moe_kernel,
        grid_spec=pltpu.PrefetchScalarGridSpec(
            num_scalar_prefetch=3,
            grid=(n_sorted // tm,),
            in_specs=[slab_tile,
                      pick(0, d, hdn), pick(0, d, hdn), pick(0, hdn, d),
                      pick(1, d, hdn), pick(1, d, hdn), pick(1, hdn, d)],
            out_specs=slab_tile),
        out_shape=jax.ShapeDtypeStruct((n_sorted, SUBLANES, LANES), _F32),
        compiler_params=_cparams("arbitrary"),
        name="moe_experts",
    )(tile_ea, tile_eb, tile_valid, slabs_sorted, w1, w3, w2, w1, w3, w2)


def _combine_kernel(pos_ref, y_hbm_ref, x1_ref, mod_ref, fg_ref, o_ref, ybuf_ref, sems, *, tm):
    i = pl.program_id(0)
    n_steps = pl.num_programs(0)

    def issue(tile, slot):
        _issue_slab_copies(
            tm,
            lambda r, src: pltpu.make_async_copy(y_hbm_ref.at[src], ybuf_ref.at[slot, r],
                                                 sems.at[slot]),
            lambda r: pos_ref[tile * tm + r])

    @pl.when(i == 0)
    def _():
        issue(0, 0)

    @pl.when(i + 1 < n_steps)
    def _():
        issue(i + 1, (i + 1) % 2)

    slot = i % 2
    pltpu.make_async_copy(y_hbm_ref.at[pl.ds(0, tm)], ybuf_ref.at[slot], sems.at[slot]).wait()
    y = ybuf_ref[slot].reshape(tm, SUBLANES * LANES)
    xo = x1_ref[...] + mod_ref[0, 5:6, :] * y
    o_ref[...] = _rms(xo) * fg_ref[...]


def _combine(pos, y_sorted, x1, mod3, final_g, seq, tm):
    t, d = x1.shape
    tiles_per_batch = seq // tm
    return pl.pallas_call(
        functools.partial(_combine_kernel, tm=tm),
        grid_spec=pltpu.PrefetchScalarGridSpec(
            num_scalar_prefetch=1,
            grid=(t // tm,),
            in_specs=[pl.BlockSpec(memory_space=pl.ANY),
                      pl.BlockSpec((tm, d), lambda i, pos: (i, 0)),
                      pl.BlockSpec((1, 6, d), lambda i, pos: (i // tiles_per_batch, 0, 0)),
                      pl.BlockSpec((1, d), lambda i, pos: (0, 0))],
            out_specs=pl.BlockSpec((tm, d), lambda i, pos: (i, 0)),
            scratch_shapes=[pltpu.VMEM((2, tm, SUBLANES, LANES), _F32),
                            pltpu.SemaphoreType.DMA((2,))]),
        out_shape=jax.ShapeDtypeStruct((t, d), _F32),
        compiler_params=_cparams("arbitrary"),
        name="moe_combine",
    )(pos, y_sorted, x1, mod3, final_g)


def _routing_tables(route_cls, route_rank, counts, n_tokens, tm):
    sizes = ((counts + (tm - 1)) // tm) * tm
    ends = jnp.cumsum(sizes)
    starts = ends - sizes
    n_tiles = n_tokens // tm + N_CLASSES
    tile_start = jnp.arange(n_tiles, dtype=jnp.int32) * tm
    pos = route_rank
    tile_cls = jnp.zeros((n_tiles,), jnp.int32)
    for c in range(N_CLASSES):
        pos = pos + jnp.where(route_cls == c, starts[c], 0)
        tile_cls = tile_cls + (tile_start >= ends[c]).astype(jnp.int32)
    tile_cls = jnp.minimum(tile_cls, N_CLASSES - 1)
    tile_valid = (tile_start < ends[-1]).astype(jnp.int32)
    pairs = [(a, b_) for a in range(EXPERTS_PER_GROUP) for b_ in range(a + 1, EXPERTS_PER_GROUP)]
    grp, pair = tile_cls // PAIRS_PER_GROUP, tile_cls % PAIRS_PER_GROUP
    tile_ea = grp * EXPERTS_PER_GROUP
    tile_eb = grp * EXPERTS_PER_GROUP
    for k, (a, b_) in enumerate(pairs):
        tile_ea = tile_ea + jnp.where(pair == k, a, 0)
        tile_eb = tile_eb + jnp.where(pair == k, b_, 0)
    pad_start = jnp.concatenate([starts + counts, ends[-1:]]).astype(jnp.int32)
    pad_len = jnp.concatenate([sizes - counts, n_tiles * tm - ends[-1:]]).astype(jnp.int32)
    return (pos.astype(jnp.int32), pad_start, pad_len, tile_ea.astype(jnp.int32),
            tile_eb.astype(jnp.int32), tile_valid, n_tiles)


def _rope_tables(n):
    inv = (1.0 / (ROPE_BASE ** (np.arange(ROT_FREQS, dtype=np.float32) / ROT_FREQS))).astype(np.float32)
    pos = np.arange(n)
    row = (pos // GRID_W).astype(np.float32)[:, None] * inv[None, :]
    col = (pos % GRID_W).astype(np.float32)[:, None] * inv[None, :]
    cos64 = np.concatenate([np.cos(row), np.cos(row), np.cos(col), np.cos(col)], axis=1)
    sin64 = np.concatenate([-np.sin(row), np.sin(row), -np.sin(col), np.sin(col)], axis=1)
    tile = lambda a: np.tile(a.astype(np.float32), (1, LANES // HEAD_DIM))
    return jnp.asarray(tile(cos64)), jnp.asarray(tile(sin64))


def _dft_cos_sin(n):
    k = np.arange(n, dtype=np.int64)
    ang = (2.0 * np.pi / n) * ((k[:, None] * k[None, :]) % n).astype(np.float64)
    return np.cos(ang), np.sin(ang)


def _dft_tables(n):
    c_ch, s_ch = _dft_cos_sin(F_GROUP_DIM)
    c_seq, s_seq = _dft_cos_sin(n >> FNET_LEVELS)
    cs_ch = np.concatenate([c_ch, -s_ch], axis=1).astype(np.float32)
    cs_seq = np.concatenate([c_seq, s_seq], axis=1).astype(np.float32)
    twiddles = []
    for level in range(FNET_LEVELS):
        m = n >> level
        ang = (2.0 * np.pi / m) * np.arange(m // 2, dtype=np.float64)[:, None]
        for tab in (np.cos(ang), np.sin(ang)):
            twiddles.append(jnp.asarray(np.broadcast_to(tab, (m // 2, F_W)).astype(np.float32)))
    return jnp.asarray(cs_ch.astype(_BF)), jnp.asarray(cs_seq.astype(_BF)), twiddles


def kernel(x, c, ctx, c_ctx, w_mod, b_mod, norm1_g, norm2_g, w_in, lam_q1, lam_k1, lam_q2, lam_k2,
           subln_g, w_attn_out, w_four_out, w_out, w_router_group, b_router_group, w_router_expert,
           b_router_expert, w_exp_gate, w_exp_up, w_exp_down, final_g):
    b, n, d = x.shape
    assert w_mod.shape[0] == 1, "depth-1 stack"
    assert b + 1 <= MOD_ROWS

    cc = jnp.concatenate([c, c_ctx[None, :], jnp.zeros((MOD_ROWS - b - 1, d), _F32)], axis=0)
    mod3 = _modulation(cc, w_mod[0], b_mod).reshape(MOD_ROWS, 6, d)

    lam = (jnp.exp(jnp.sum(lam_q1[0] * lam_k1[0])) - jnp.exp(jnp.sum(lam_q2[0] * lam_k2[0]))
           + LAM_INIT).reshape(1).astype(_F32)

    w = w_in[0]
    scale = HEAD_DIM ** -0.5 * math.log2(math.e)
    w_lat = jnp.concatenate([w[:, REF_Q:REF_K] * scale, w[:, REF_K:REF_F],
                             w[:, REF_GA:REF_END], w[:, REF_F:REF_GA]], axis=1).astype(_BF)
    w_ctx = w[:, REF_K:REF_F].astype(_BF)
    lat_kinds = ("rope",) * ((P_V - P_Q) // PROJ_CW) + ("plain",) * ((P_GA - P_V) // PROJ_CW) \
        + ("sigmoid",) * ((P_F - P_GA) // PROJ_CW) + ("plain",) * ((P_W - P_F) // PROJ_CW)
    ctx_kinds = ("plain",) * (w_ctx.shape[1] // PROJ_CW)

    cos_t, sin_t = _rope_tables(n)
    p = _in_projection(x, mod3, lambda bi: bi, norm1_g, w_lat, cos_t, sin_t, lat_kinds, PROJ_TM)
    kvc = _in_projection(ctx, mod3, lambda bi: b, norm1_g, w_ctx, cos_t, sin_t, ctx_kinds,
                         ctx.shape[1])

    heads = _attention(lam, p, kvc, subln_g, ATTN_TQ)
    cs_ch, cs_seq, twiddles = _dft_tables(n)
    four = _fnet(p, cs_ch, cs_seq, twiddles)

    w_r = jnp.concatenate([w_router_expert[0], w_router_group[0],
                           jnp.zeros((d, ROUTER_W - N_EXPERTS - N_GROUPS), _F32)], axis=1)
    b_r = jnp.concatenate([b_router_expert[0], b_router_group[0],
                           jnp.zeros((ROUTER_W - N_EXPERTS - N_GROUPS,), _F32)])[None, :]
    wr_hi = w_r.astype(_BF)
    wr_lo = (w_r - wr_hi.astype(_F32)).astype(_BF)
    x1, rows, rec, counts = _merge(heads, four, p, x, mod3, norm2_g, w_attn_out[0].astype(_BF),
                                   w_four_out[0].astype(_BF), w_out[0].astype(_BF),
                                   jnp.concatenate([wr_hi, wr_lo], axis=1), b_r, MERGE_TM)

    t = b * n
    slabs = rows.reshape(t, SUBLANES, LANES)
    rec = rec.reshape(t // MERGE_TM, SUBLANES, MERGE_TM)
    pos, pad_start, pad_len, tile_ea, tile_eb, tile_valid, n_tiles = _routing_tables(
        rec[:, 2, :].reshape(t).astype(jnp.int32), rec[:, 3, :].reshape(t).astype(jnp.int32),
        counts[0, :N_CLASSES].astype(jnp.int32), t, MOE_TM)

    slabs_sorted = _dispatch(pos, pad_start, pad_len, slabs, n_tiles * MOE_TM, DISPATCH_TM)
    y_sorted = _moe(tile_ea, tile_eb, tile_valid, slabs_sorted, w_exp_gate[0].astype(_BF),
                    w_exp_up[0].astype(_BF), w_exp_down[0].astype(_BF), MOE_TM)
    out = _combine(pos, y_sorted, x1.reshape(t, d), mod3, final_g[None, :], n, COMBINE_TM)
    return out.reshape(b, n, d)
```

```python
import functools
import math

import jax
import jax.numpy as jnp
import numpy as np
from jax import lax
from jax.experimental import pallas as pl
from jax.experimental.pallas import tpu as pltpu

D_MODEL = 1024
GRID_W = 64
EPS = 1e-6
N_HEADS = 8
HEAD_DIM = 64
HEAD_W = 2 * HEAD_DIM
ROT_FREQS = HEAD_DIM // 4
ROPE_BASE = 10000.0
F_GROUPS = 4
F_GROUP_DIM = 128
F_W = F_GROUPS * F_GROUP_DIM
N_GROUPS = 4
EXPERTS_PER_GROUP = 4
LOG2_EXPERTS_PER_GROUP = 2
N_EXPERTS = N_GROUPS * EXPERTS_PER_GROUP
PAIRS_PER_GROUP = EXPERTS_PER_GROUP * (EXPERTS_PER_GROUP - 1) // 2
N_CLASSES = N_GROUPS * PAIRS_PER_GROUP
EXPERT_HIDDEN = 512
LAM_INIT = 0.8 - 0.6 * math.exp(-0.3 * 0)

REF_Q, REF_K, REF_V, REF_F, REF_GA, REF_GF, REF_END = 0, 1024, 2048, 3072, 3584, 4608, 5632
P_Q, P_K, P_V, P_GA, P_GF, P_F, P_W = 0, 1024, 2048, 3072, 4096, 5120, 5632

LANES = 128
SUBLANES = 8
VMEM_LIMIT_BYTES = 56 * 1024 * 1024

PROJ_TM = 1024
PROJ_CW = 512
ATTN_TQ = 256
ATTN_SUM_ROWS = 16
ATTN_SCORE_BUFS = 3
FNET_LEVELS = 2
FNET_CHUNK = 256
MERGE_TM = 512
MOE_TM = 256
COMBINE_TM = 512
DISPATCH_TM = 2048
DMA_ISSUE_GROUP = 8
MOD_ROWS = 40
ROUTER_W = LANES
GROUP_LANE0 = N_EXPERTS
H2_SLAB_ROWS = D_MODEL // 2 // LANES
ROUTE_SLAB_ROW = H2_SLAB_ROWS

_BF = jnp.bfloat16
_F32 = jnp.float32


def _cparams(*sem):
    return pltpu.CompilerParams(dimension_semantics=sem, vmem_limit_bytes=VMEM_LIMIT_BYTES)


def _const_spec(shape):
    return pl.BlockSpec(shape, lambda *_: (0,) * len(shape), pipeline_mode=pl.Buffered(1))


def _rms(x):
    return x * lax.rsqrt(jnp.mean(x * x, axis=-1, keepdims=True) + EPS)


def _sigmoid(x):
    return 1.0 / (1.0 + jnp.exp(-x))


def _mod_kernel(cc_ref, w_ref, b_ref, o_ref):
    cc = cc_ref[...]
    s = cc * _sigmoid(cc)
    o_ref[...] = jnp.dot(s, w_ref[...], preferred_element_type=_F32,
                         precision=lax.Precision.HIGHEST) + b_ref[...]


def _modulation(cc, w_mod, b_mod):
    n = w_mod.shape[1]
    bn = D_MODEL
    return pl.pallas_call(
        _mod_kernel,
        grid=(n // bn,),
        in_specs=[pl.BlockSpec((MOD_ROWS, D_MODEL), lambda j: (0, 0)),
                  pl.BlockSpec((D_MODEL, bn), lambda j: (0, j)),
                  pl.BlockSpec((1, bn), lambda j: (0, j))],
        out_specs=pl.BlockSpec((MOD_ROWS, bn), lambda j: (0, j)),
        out_shape=jax.ShapeDtypeStruct((MOD_ROWS, n), _F32),
        compiler_params=_cparams("arbitrary"),
        name="modulation",
    )(cc, w_mod, b_mod)


def _rope(acc, cos_ref, sin_ref):
    cos = cos_ref[...]
    sin = sin_ref[...]
    lane = lax.broadcasted_iota(jnp.int32, (1, LANES), 1)
    first_half = (lane % (2 * ROT_FREQS)) < ROT_FREQS
    outs = []
    for s in range(acc.shape[1] // LANES):
        xs = acc[:, s * LANES:(s + 1) * LANES]
        partner = jnp.where(first_half,
                            pltpu.roll(xs, LANES - ROT_FREQS, 1),
                            pltpu.roll(xs, ROT_FREQS, 1))
        outs.append(xs * cos + partner * sin)
    return jnp.concatenate(outs, axis=1)


def _inproj_kernel(x_ref, mod_ref, g_ref, w_ref, cos_ref, sin_ref, o_ref, *, chunk_kinds):
    x = x_ref[0]
    shift = mod_ref[0, 0:1, :]
    scale = mod_ref[0, 1:2, :]
    h = (_rms(x) * g_ref[...]) * (1.0 + scale) + shift
    hb = h.astype(_BF)
    for j, kind in enumerate(chunk_kinds):
        cols = slice(j * PROJ_CW, (j + 1) * PROJ_CW)
        acc = jnp.dot(hb, w_ref[:, cols], preferred_element_type=_F32)
        if kind == "rope":
            acc = _rope(acc, cos_ref, sin_ref)
        elif kind == "sigmoid":
            acc = _sigmoid(acc)
        o_ref[0, :, cols] = acc.astype(_BF)


def _in_projection(x, mod3, mod_row_of_batch, g, w, cos_t, sin_t, chunk_kinds, tm):
    b, n, d = x.shape
    width = w.shape[1]
    assert width == len(chunk_kinds) * PROJ_CW and n % tm == 0
    return pl.pallas_call(
        functools.partial(_inproj_kernel, chunk_kinds=chunk_kinds),
        grid=(b, n // tm),
        in_specs=[pl.BlockSpec((1, tm, d), lambda bi, i: (bi, i, 0)),
                  pl.BlockSpec((1, 6, d), lambda bi, i: (mod_row_of_batch(bi), 0, 0)),
                  pl.BlockSpec((1, d), lambda bi, i: (0, 0)),
                  _const_spec((d, width)),
                  pl.BlockSpec((tm, LANES), lambda bi, i: (i, 0)),
                  pl.BlockSpec((tm, LANES), lambda bi, i: (i, 0))],
        out_specs=pl.BlockSpec((1, tm, width), lambda bi, i: (bi, i, 0)),
        out_shape=jax.ShapeDtypeStruct((b, n, width), _BF),
        compiler_params=_cparams("parallel", "arbitrary"),
        name="in_projection",
    )(x, mod3, g, w, cos_t, sin_t)


def _attn_kernel(lam_ref, q_ref, kl_ref, vl_ref, kc_ref, vc_ref, sg_ref, o_ref,
                 kcat_ref, vt_ref, *bufs, tq):
    nc, n = kc_ref.shape[1], kl_ref.shape[1]
    kcat_ref[0:nc, :] = kc_ref[0]
    kcat_ref[nc:nc + n, :] = kl_ref[0]
    vt_ref[0:HEAD_W, 0:nc] = vc_ref[0].astype(_F32).T.astype(_BF)
    vt_ref[0:HEAD_W, nc:nc + n] = vl_ref[0].astype(_F32).T.astype(_BF)
    vt_ref[HEAD_W:, :] = jnp.ones((ATTN_SUM_ROWS, nc + n), _BF)
    lane = lax.broadcasted_iota(jnp.int32, (1, HEAD_W), 1)
    map_lanes = (lane < HEAD_DIM, lane >= HEAD_DIM)
    nt = (((1,), (1,)), ((), ()))
    lam = lam_ref[0]
    post_scale = sg_ref[...] * (1.0 - LAM_INIT)

    def scores(j):
        q = q_ref[0, j * tq:(j + 1) * tq, :]
        for mp in range(2):
            qm = jnp.where(map_lanes[mp], q, jnp.zeros_like(q))
            bufs[j % ATTN_SCORE_BUFS][mp] = lax.dot_general(kcat_ref[...], qm, nt,
                                                            preferred_element_type=_F32)

    def finish(j):
        outs = []
        for mp in range(2):
            s = bufs[j % ATTN_SCORE_BUFS][mp]
            e = jnp.exp2(s - jnp.max(s, axis=0, keepdims=True)).astype(_BF)
            r = jnp.dot(vt_ref[...], e, preferred_element_type=_F32)
            outs.append(r[:HEAD_W, :] / r[HEAD_W:HEAD_W + 1, :])
        heads = (outs[0] - lam * outs[1]).T
        o_ref[0, j * tq:(j + 1) * tq, :] = (_rms(heads) * post_scale).astype(_BF)

    n_sub = n // tq
    for j in range(ATTN_SCORE_BUFS - 1):
        scores(j)
    for j in range(n_sub):
        if j + ATTN_SCORE_BUFS - 1 < n_sub:
            scores(j + ATTN_SCORE_BUFS - 1)
        finish(j)


def _attention(lam, p, kvc, subln_g, tq):
    b, n, _ = p.shape
    nc = kvc.shape[1]
    qb, kb, vb = P_Q // HEAD_W, P_K // HEAD_W, P_V // HEAD_W
    seq = lambda blk: pl.BlockSpec((1, n, HEAD_W), lambda bi, h: (bi, 0, blk + h))
    return pl.pallas_call(
        functools.partial(_attn_kernel, tq=tq),
        grid=(b, N_HEADS),
        in_specs=[pl.BlockSpec(memory_space=pltpu.SMEM),
                  seq(qb), seq(kb), seq(vb),
                  pl.BlockSpec((1, nc, HEAD_W), lambda bi, h: (bi, 0, h)),
                  pl.BlockSpec((1, nc, HEAD_W), lambda bi, h: (bi, 0, N_HEADS + h)),
                  pl.BlockSpec((1, HEAD_W), lambda bi, h: (0, 0))],
        out_specs=seq(0),
        out_shape=jax.ShapeDtypeStruct((b, n, N_HEADS * HEAD_W), _BF),
        scratch_shapes=[pltpu.VMEM((nc + n, HEAD_W), _BF),
                        pltpu.VMEM((HEAD_W + ATTN_SUM_ROWS, nc + n), _BF),
                        ] + [pltpu.VMEM((2, nc + n, tq), _F32)] * ATTN_SCORE_BUFS,
        compiler_params=_cparams("parallel", "arbitrary"),
        name="diff_attention",
    )(lam, p, p, p, kvc, kvc, subln_g)


def _bit_reverse(j, bits):
    return int(format(j, "0%db" % bits)[::-1], 2) if bits else 0


def _fnet_kernel(f_ref, cs_ch_ref, cs_seq_ref, *rest, n):
    tw_refs = rest[:2 * FNET_LEVELS]
    o_ref, zr_ref, zi_ref, rhs_ref = rest[2 * FNET_LEVELS:]
    for g in range(F_GROUPS):
        cols = slice(g * F_GROUP_DIM, (g + 1) * F_GROUP_DIM)
        t = jnp.dot(f_ref[0, :, cols], cs_ch_ref[...], preferred_element_type=_F32)
        zr_ref[:, cols] = t[:, :F_GROUP_DIM]
        zi_ref[:, cols] = t[:, F_GROUP_DIM:]
    for level in range(FNET_LEVELS):
        m = n >> level
        half = m // 2
        cos_ref, sin_ref = tw_refs[2 * level], tw_refs[2 * level + 1]
        for seg in range(1 << level):
            for c0 in range(0, half, FNET_CHUNK):
                rows_t = slice(seg * m + c0, seg * m + c0 + FNET_CHUNK)
                rows_b = slice(seg * m + half + c0, seg * m + half + c0 + FNET_CHUNK)
                tr, ti = zr_ref[rows_t, :], zi_ref[rows_t, :]
                br, bi = zr_ref[rows_b, :], zi_ref[rows_b, :]
                cw, sw = cos_ref[c0:c0 + FNET_CHUNK, :], sin_ref[c0:c0 + FNET_CHUNK, :]
                dr, di = tr - br, ti - bi
                zr_ref[rows_t, :] = tr + br
                zi_ref[rows_t, :] = ti + bi
                zr_ref[rows_b, :] = dr * cw + di * sw
                zi_ref[rows_b, :] = di * cw - dr * sw
    m = n >> FNET_LEVELS
    for j in range(1 << FNET_LEVELS):
        cols = slice(_bit_reverse(j, FNET_LEVELS) * F_W, (_bit_reverse(j, FNET_LEVELS) + 1) * F_W)
        rhs_ref[0:m, cols] = zr_ref[j * m:(j + 1) * m, :].astype(_BF)
        rhs_ref[m:2 * m, cols] = zi_ref[j * m:(j + 1) * m, :].astype(_BF)
    ortho = 1.0 / math.sqrt(n * F_GROUP_DIM)
    y = jnp.dot(cs_seq_ref[...], rhs_ref[...], preferred_element_type=_F32)
    o_ref[0] = (y * ortho).reshape(m, 1 << FNET_LEVELS, F_W).reshape(n, F_W).astype(_BF)


def _fnet(p, cs_ch, cs_seq, twiddles):
    b, n, _ = p.shape
    m = n >> FNET_LEVELS
    wide = F_W << FNET_LEVELS
    return pl.pallas_call(
        functools.partial(_fnet_kernel, n=n),
        grid=(b,),
        in_specs=[pl.BlockSpec((1, n, F_W), lambda bi: (bi, 0, P_F // F_W)),
                  _const_spec((F_GROUP_DIM, 2 * F_GROUP_DIM)),
                  _const_spec((m, 2 * m))] + [_const_spec(t.shape) for t in twiddles],
        out_specs=pl.BlockSpec((1, n, F_W), lambda bi: (bi, 0, 0)),
        out_shape=jax.ShapeDtypeStruct((b, n, F_W), _BF),
        scratch_shapes=[pltpu.VMEM((n, F_W), _F32), pltpu.VMEM((n, F_W), _F32),
                        pltpu.VMEM((2 * m, wide), _BF)],
        compiler_params=_cparams("arbitrary"),
        name="fnet_dft",
    )(p, cs_ch, cs_seq, *twiddles)


def _pack_bf16_pair(hi, lo):
    hi_bits = pltpu.bitcast(hi.astype(_BF).astype(_F32), jnp.uint32)
    lo_bits = pltpu.bitcast(lo.astype(_BF).astype(_F32), jnp.uint32)
    return hi_bits | lax.shift_right_logical(lo_bits, jnp.uint32(16))


def _unpack_bf16_pair(packed):
    hi = pltpu.bitcast(packed & jnp.uint32(0xFFFF0000), _F32)
    lo = pltpu.bitcast(lax.shift_left(packed, jnp.uint32(16)), _F32)
    return hi, lo


def _route(logits, carry):
    rows = logits.shape[0]
    lane = lax.broadcasted_iota(jnp.int32, logits.shape, 1)
    neg = jnp.float32(-jnp.inf)
    big = jnp.int32(ROUTER_W)

    def first_argmax(v):
        m = jnp.max(v, axis=-1, keepdims=True)
        idx = jnp.min(jnp.where(v == m, lane, big), axis=-1, keepdims=True)
        return m, idx

    lg = jnp.where((lane >= GROUP_LANE0) & (lane < GROUP_LANE0 + N_GROUPS), logits, neg)
    mg, ig = first_argmax(lg)
    w_grp = 1.0 / jnp.sum(jnp.exp(lg - mg), axis=-1, keepdims=True)
    g_sel = ig - GROUP_LANE0
    le = jnp.where((lane < N_EXPERTS)
                   & (jnp.right_shift(lane, LOG2_EXPERTS_PER_GROUP) == g_sel), logits, neg)
    v1, i1 = first_argmax(le)
    le2 = jnp.where(lane == i1, neg, le)
    v2, i2 = first_argmax(le2)
    e2 = jnp.exp(v2 - v1)
    w1 = w_grp / (1.0 + e2)
    w2 = w_grp * e2 / (1.0 + e2)

    first_is_lower = i1 < i2
    gate_a = jnp.where(first_is_lower, w1, w2)
    gate_b = jnp.where(first_is_lower, w2, w1)
    la = jnp.minimum(i1, i2) - g_sel * EXPERTS_PER_GROUP
    lb = jnp.maximum(i1, i2) - g_sel * EXPERTS_PER_GROUP
    pair = jnp.right_shift(la * (2 * EXPERTS_PER_GROUP - 1 - la), 1) + lb - la - 1
    cls = g_sel * PAIRS_PER_GROUP + pair

    onehot = (lane == cls).astype(_F32)
    r_i = lax.broadcasted_iota(jnp.int32, (rows, rows), 0)
    c_i = lax.broadcasted_iota(jnp.int32, (rows, rows), 1)
    earlier = (c_i < r_i).astype(_BF)
    before = jnp.dot(earlier, onehot.astype(_BF), preferred_element_type=_F32) + carry
    rank = jnp.sum(onehot * before, axis=-1, keepdims=True)
    new_carry = carry + jnp.sum(onehot, axis=0, keepdims=True)

    route = (jnp.where(lane == 0, gate_a, 0.0) + jnp.where(lane == 1, gate_b, 0.0)
             + jnp.where(lane == 2, cls.astype(_F32), 0.0) + jnp.where(lane == 3, rank, 0.0))
    return route, new_carry


def _merge_kernel(hd_ref, fo_ref, ga_ref, gf_ref, x_ref, mod_ref, g2n_ref, wao_ref, wfo_ref,
                  wo_ref, wrhl_ref, br_ref, x1_ref, rows_ref, rec_ref, counts_ref, carry_ref):
    @pl.when((pl.program_id(0) == 0) & (pl.program_id(1) == 0))
    def _():
        carry_ref[...] = jnp.zeros_like(carry_ref)

    a = jnp.dot(hd_ref[0], wao_ref[...], preferred_element_type=_F32)
    ff = jnp.dot(fo_ref[0], wfo_ref[...], preferred_element_type=_F32)
    y = ga_ref[0].astype(_F32) * a + gf_ref[0].astype(_F32) * ff
    mix = jnp.dot(y.astype(_BF), wo_ref[...], preferred_element_type=_F32)
    x1 = x_ref[0] + mod_ref[0, 2:3, :] * mix
    x1_ref[0] = x1
    h2 = (_rms(x1) * g2n_ref[...]) * (1.0 + mod_ref[0, 4:5, :]) + mod_ref[0, 3:4, :]
    h2_hi = h2.astype(_BF)
    h2_lo = (h2 - h2_hi.astype(_F32)).astype(_BF)
    hi_both = jnp.dot(h2_hi, wrhl_ref[...], preferred_element_type=_F32)
    logits = (hi_both[:, :ROUTER_W] + hi_both[:, ROUTER_W:]
              + jnp.dot(h2_lo, wrhl_ref[:, :ROUTER_W], preferred_element_type=_F32)) + br_ref[...]
    route, new_carry = _route(logits, carry_ref[...])
    carry_ref[...] = new_carry
    counts_ref[...] = new_carry
    rec_ref[0, 0] = route.T[0:SUBLANES, :]
    half = D_MODEL // 2
    tm = h2.shape[0]
    packed = _pack_bf16_pair(h2[:, :half], h2[:, half:])
    rows_ref[0, :, 0:H2_SLAB_ROWS, :] = packed.reshape(tm, H2_SLAB_ROWS, LANES)
    rows_ref[0, :, ROUTE_SLAB_ROW:ROUTE_SLAB_ROW + 1, :] = (
        pltpu.bitcast(route, jnp.uint32).reshape(tm, 1, LANES))
    rows_ref[0, :, ROUTE_SLAB_ROW + 1:SUBLANES, :] = jnp.zeros(
        (tm, SUBLANES - ROUTE_SLAB_ROW - 1, LANES), jnp.uint32)


def _merge(heads, four, p, x, mod3, norm2_g, w_ao, w_fo, w_o, wr_hilo, b_r, tm):
    b, n, d = x.shape
    tok = lambda w: pl.BlockSpec((1, tm, w), lambda bi, i: (bi, i, 0))
    return pl.pallas_call(
        _merge_kernel,
        grid=(b, n // tm),
        in_specs=[tok(d), tok(F_W),
                  pl.BlockSpec((1, tm, d), lambda bi, i: (bi, i, P_GA // D_MODEL)),
                  pl.BlockSpec((1, tm, d), lambda bi, i: (bi, i, P_GF // D_MODEL)),
                  tok(d),
                  pl.BlockSpec((1, 6, d), lambda bi, i: (bi, 0, 0)),
                  pl.BlockSpec((1, d), lambda bi, i: (0, 0)),
                  _const_spec((d, d)), _const_spec((F_W, d)), _const_spec((d, d)),
                  _const_spec((d, 2 * ROUTER_W)),
                  pl.BlockSpec((1, ROUTER_W), lambda bi, i: (0, 0))],
        out_specs=[tok(d),
                   pl.BlockSpec((1, tm, SUBLANES, LANES), lambda bi, i: (bi, i, 0, 0)),
                   pl.BlockSpec((1, 1, SUBLANES, tm), lambda bi, i: (bi, i, 0, 0)),
                   pl.BlockSpec((1, ROUTER_W), lambda bi, i: (0, 0))],
        out_shape=[jax.ShapeDtypeStruct((b, n, d), _F32),
                   jax.ShapeDtypeStruct((b, n, SUBLANES, LANES), jnp.uint32),
                   jax.ShapeDtypeStruct((b, n // tm, SUBLANES, tm), _F32),
                   jax.ShapeDtypeStruct((1, ROUTER_W), _F32)],
        scratch_shapes=[pltpu.VMEM((1, ROUTER_W), _F32)],
        compiler_params=_cparams("arbitrary", "arbitrary"),
        name="merge_router",
    )(heads, four, p, p, x, mod3, norm2_g, w_ao, w_fo, w_o, wr_hilo, b_r)


def _issue_slab_copies(n, make_copy, slot_of):
    def group(g, c):
        r0 = g * DMA_ISSUE_GROUP
        slots = [slot_of(r0 + k) for k in range(DMA_ISSUE_GROUP)]
        for k in range(DMA_ISSUE_GROUP):
            make_copy(r0 + k, slots[k]).start(priority=k % 2)
        return c

    lax.fori_loop(0, n // DMA_ISSUE_GROUP, group, 0)


def _dispatch_kernel(pos_ref, pad_start_ref, pad_len_ref, src_ref, dst_ref, zero_ref, sem, pad_sem,
                     *, tm):
    base = pl.program_id(0) * tm

    @pl.when(pl.program_id(0) == 0)
    def _():
        zero_ref[...] = jnp.zeros_like(zero_ref)
        for c in range(N_CLASSES + 1):
            def start(r, carry, c=c):
                pltpu.make_async_copy(zero_ref, dst_ref.at[pad_start_ref[c] + r], pad_sem).start()
                return carry

            def drain(r, carry):
                pltpu.make_async_copy(zero_ref, dst_ref.at[0], pad_sem).wait()
                return carry

            lax.fori_loop(0, pad_len_ref[c], start, 0)
            lax.fori_loop(0, pad_len_ref[c], drain, 0)

    _issue_slab_copies(
        tm,
        lambda r, slot: pltpu.make_async_copy(src_ref.at[r], dst_ref.at[slot], sem),
        lambda r: pos_ref[base + r])
    pltpu.make_async_copy(src_ref, dst_ref.at[pl.ds(0, tm)], sem).wait()


def _dispatch(pos, pad_start, pad_len, slabs, n_sorted, tm):
    t = slabs.shape[0]
    return pl.pallas_call(
        functools.partial(_dispatch_kernel, tm=tm),
        grid_spec=pltpu.PrefetchScalarGridSpec(
            num_scalar_prefetch=3,
            grid=(t // tm,),
            in_specs=[pl.BlockSpec((tm, SUBLANES, LANES), lambda i, *_: (i, 0, 0))],
            out_specs=pl.BlockSpec(memory_space=pl.ANY),
            scratch_shapes=[pltpu.VMEM((SUBLANES, LANES), slabs.dtype),
                            pltpu.SemaphoreType.DMA(()), pltpu.SemaphoreType.DMA(())]),
        out_shape=jax.ShapeDtypeStruct((n_sorted,) + slabs.shape[1:], slabs.dtype),
        compiler_params=_cparams("arbitrary"),
        name="moe_dispatch",
    )(pos, pad_start, pad_len, slabs)


def _moe_kernel(ea_ref, eb_ref, valid_ref, slabs_ref, w1a_ref, w3a_ref, w2a_ref,
                w1b_ref, w3b_ref, w2b_ref, y_ref):
    del ea_ref, eb_ref
    i = pl.program_id(0)
    tm = slabs_ref.shape[0]

    @pl.when(valid_ref[i] != 0)
    def _():
        packed = slabs_ref[:, 0:H2_SLAB_ROWS, :].reshape(tm, H2_SLAB_ROWS * LANES)
        hi, lo = _unpack_bf16_pair(packed)
        t = jnp.concatenate([hi.astype(_BF), lo.astype(_BF)], axis=1)
        route = pltpu.bitcast(slabs_ref[:, ROUTE_SLAB_ROW, :], _F32)
        y = None
        for slot, (w1_ref, w3_ref, w2_ref) in enumerate(((w1a_ref, w3a_ref, w2a_ref),
                                                          (w1b_ref, w3b_ref, w2b_ref))):
            a = jnp.dot(t, w1_ref[0], preferred_element_type=_F32)
            u = jnp.dot(t, w3_ref[0], preferred_element_type=_F32)
            hid = ((a * _sigmoid(a)) * u * route[:, slot:slot + 1]).astype(_BF)
            part = jnp.dot(hid, w2_ref[0], preferred_element_type=_F32)
            y = part if y is None else y + part
        y_ref[...] = y.reshape(tm, SUBLANES, LANES)

    @pl.when(valid_ref[i] == 0)
    def _():
        y_ref[...] = jnp.zeros_like(y_ref)


def _moe(tile_ea, tile_eb, tile_valid, slabs_sorted, w1, w3, w2, tm):
    n_sorted = slabs_sorted.shape[0]
    d, hdn = w1.shape[1], w1.shape[2]
    assert d == SUBLANES * LANES
    pick = lambda which, a, b_: pl.BlockSpec(
        (1, a, b_), lambda i, ea, eb, valid: ((ea, eb)[which][i], 0, 0))
    slab_tile = pl.BlockSpec((tm, SUBLANES, LANES), lambda i, ea, eb, valid: (i, 0, 0))
    return pl.pallas_call(
        _moe_kernel,
        grid_spec=pltpu.PrefetchScalarGridSpec(
            num_scalar_prefetch=3,
            grid=(n_sorted // tm,),
            in_specs=[slab_tile,
                      pick(0, d, hdn), pick(0, d, hdn), pick(0, hdn, d),
                      pick(1, d, hdn), pick(1, d, hdn), pick(1, hdn, d)],
            out_specs=slab_tile),
        out_shape=jax.ShapeDtypeStruct((n_sorted, SUBLANES, LANES), _F32),
        compiler_params=_cparams("arbitrary"),
        name="moe_experts",
    )(tile_ea, tile_eb, tile_valid, slabs_sorted, w1, w3, w2, w1, w3, w2)


def _combine_kernel(pos_ref, y_hbm_ref, x1_ref, mod_ref, fg_ref, o_ref, ybuf_ref, sems, *, tm):
    i = pl.program_id(0)
    n_steps = pl.num_programs(0)

    def issue(tile, slot):
        _issue_slab_copies(
            tm,
            lambda r, src: pltpu.make_async_copy(y_hbm_ref.at[src], ybuf_ref.at[slot, r],
                                                 sems.at[slot]),
            lambda r: pos_ref[tile * tm + r])

    @pl.when(i == 0)
    def _():
        issue(0, 0)

    @pl.when(i + 1 < n_steps)
    def _():
        issue(i + 1, (i + 1) % 2)

    slot = i % 2
    pltpu.make_async_copy(y_hbm_ref.at[pl.ds(0, tm)], ybuf_ref.at[slot], sems.at[slot]).wait()
    y = ybuf_ref[slot].reshape(tm, SUBLANES * LANES)
    xo = x1_ref[...] + mod_ref[0, 5:6, :] * y
    o_ref[...] = _rms(xo) * fg_ref[...]


def _combine(pos, y_sorted, x1, mod3, final_g, seq, tm):
    t, d = x1.shape
    tiles_per_batch = seq // tm
    return pl.pallas_call(
        functools.partial(_combine_kernel, tm=tm),
        grid_spec=pltpu.PrefetchScalarGridSpec(
            num_scalar_prefetch=1,
            grid=(t // tm,),
            in_specs=[pl.BlockSpec(memory_space=pl.ANY),
                      pl.BlockSpec((tm, d), lambda i, pos: (i, 0)),
                      pl.BlockSpec((1, 6, d), lambda i, pos: (i // tiles_per_batch, 0, 0)),
                      pl.BlockSpec((1, d), lambda i, pos: (0, 0))],
            out_specs=pl.BlockSpec((tm, d), lambda i, pos: (i, 0)),
            scratch_shapes=[pltpu.VMEM((2, tm, SUBLANES, LANES), _F32),
                            pltpu.SemaphoreType.DMA((2,))]),
        out_shape=jax.ShapeDtypeStruct((t, d), _F32),
        compiler_params=_cparams("arbitrary"),
        name="moe_combine",
    )(pos, y_sorted, x1, mod3, final_g)


def _routing_tables(route_cls, route_rank, counts, n_tokens, tm):
    sizes = ((counts + (tm - 1)) // tm) * tm
    ends = jnp.cumsum(sizes)
    starts = ends - sizes
    n_tiles = n_tokens // tm + N_CLASSES
    tile_start = jnp.arange(n_tiles, dtype=jnp.int32) * tm
    pos = route_rank
    tile_cls = jnp.zeros((n_tiles,), jnp.int32)
    for c in range(N_CLASSES):
        pos = pos + jnp.where(route_cls == c, starts[c], 0)
        tile_cls = tile_cls + (tile_start >= ends[c]).astype(jnp.int32)
    tile_cls = jnp.minimum(tile_cls, N_CLASSES - 1)
    tile_valid = (tile_start < ends[-1]).astype(jnp.int32)
    pairs = [(a, b_) for a in range(EXPERTS_PER_GROUP) for b_ in range(a + 1, EXPERTS_PER_GROUP)]
    grp, pair = tile_cls // PAIRS_PER_GROUP, tile_cls % PAIRS_PER_GROUP
    tile_ea = grp * EXPERTS_PER_GROUP
    tile_eb = grp * EXPERTS_PER_GROUP
    for k, (a, b_) in enumerate(pairs):
        tile_ea = tile_ea + jnp.where(pair == k, a, 0)
        tile_eb = tile_eb + jnp.where(pair == k, b_, 0)
    pad_start = jnp.concatenate([starts + counts, ends[-1:]]).astype(jnp.int32)
    pad_len = jnp.concatenate([sizes - counts, n_tiles * tm - ends[-1:]]).astype(jnp.int32)
    return (pos.astype(jnp.int32), pad_start, pad_len, tile_ea.astype(jnp.int32),
            tile_eb.astype(jnp.int32), tile_valid, n_tiles)


def _rope_tables(n):
    inv = (1.0 / (ROPE_BASE ** (np.arange(ROT_FREQS, dtype=np.float32) / ROT_FREQS))).astype(np.float32)
    pos = np.arange(n)
    row = (pos // GRID_W).astype(np.float32)[:, None] * inv[None, :]
    col = (pos % GRID_W).astype(np.float32)[:, None] * inv[None, :]
    cos64 = np.concatenate([np.cos(row), np.cos(row), np.cos(col), np.cos(col)], axis=1)
    sin64 = np.concatenate([-np.sin(row), np.sin(row), -np.sin(col), np.sin(col)], axis=1)
    tile = lambda a: np.tile(a.astype(np.float32), (1, LANES // HEAD_DIM))
    return jnp.asarray(tile(cos64)), jnp.asarray(tile(sin64))


def _dft_cos_sin(n):
    k = np.arange(n, dtype=np.int64)
    ang = (2.0 * np.pi / n) * ((k[:, None] * k[None, :]) % n).astype(np.float64)
    return np.cos(ang), np.sin(ang)


def _dft_tables(n):
    c_ch, s_ch = _dft_cos_sin(F_GROUP_DIM)
    c_seq, s_seq = _dft_cos_sin(n >> FNET_LEVELS)
    cs_ch = np.concatenate([c_ch, -s_ch], axis=1).astype(np.float32)
    cs_seq = np.concatenate([c_seq, s_seq], axis=1).astype(np.float32)
    twiddles = []
    for level in range(FNET_LEVELS):
        m = n >> level
        ang = (2.0 * np.pi / m) * np.arange(m // 2, dtype=np.float64)[:, None]
        for tab in (np.cos(ang), np.sin(ang)):
            twiddles.append(jnp.asarray(np.broadcast_to(tab, (m // 2, F_W)).astype(np.float32)))
    return jnp.asarray(cs_ch.astype(_BF)), jnp.asarray(cs_seq.astype(_BF)), twiddles


def kernel(x, c, ctx, c_ctx, w_mod, b_mod, norm1_g, norm2_g, w_in, lam_q1, lam_k1, lam_q2, lam_k2,
           subln_g, w_attn_out, w_four_out, w_out, w_router_group, b_router_group, w_router_expert,
           b_router_expert, w_exp_gate, w_exp_up, w_exp_down, final_g):
    b, n, d = x.shape
    assert w_mod.shape[0] == 1, "depth-1 stack"
    assert b + 1 <= MOD_ROWS

    cc = jnp.concatenate([c, c_ctx[None, :], jnp.zeros((MOD_ROWS - b - 1, d), _F32)], axis=0)
    mod3 = _modulation(cc, w_mod[0], b_mod).reshape(MOD_ROWS, 6, d)

    lam = (jnp.exp(jnp.sum(lam_q1[0] * lam_k1[0])) - jnp.exp(jnp.sum(lam_q2[0] * lam_k2[0]))
           + LAM_INIT).reshape(1).astype(_F32)

    w = w_in[0]
    scale = HEAD_DIM ** -0.5 * math.log2(math.e)
    w_lat = jnp.concatenate([w[:, REF_Q:REF_K] * scale, w[:, REF_K:REF_F],
                             w[:, REF_GA:REF_END], w[:, REF_F:REF_GA]], axis=1).astype(_BF)
    w_ctx = w[:, REF_K:REF_F].astype(_BF)
    lat_kinds = ("rope",) * ((P_V - P_Q) // PROJ_CW) + ("plain",) * ((P_GA - P_V) // PROJ_CW) \
        + ("sigmoid",) * ((P_F - P_GA) // PROJ_CW) + ("plain",) * ((P_W - P_F) // PROJ_CW)
    ctx_kinds = ("plain",) * (w_ctx.shape[1] // PROJ_CW)

    cos_t, sin_t = _rope_tables(n)
    p = _in_projection(x, mod3, lambda bi: bi, norm1_g, w_lat, cos_t, sin_t, lat_kinds, PROJ_TM)
    kvc = _in_projection(ctx, mod3, lambda bi: b, norm1_g, w_ctx, cos_t, sin_t, ctx_kinds,
                         ctx.shape[1])

    heads = _attention(lam, p, kvc, subln_g, ATTN_TQ)
    cs_ch, cs_seq, twiddles = _dft_tables(n)
    four = _fnet(p, cs_ch, cs_seq, twiddles)

    w_r = jnp.concatenate([w_router_expert[0], w_router_group[0],
                           jnp.zeros((d, ROUTER_W - N_EXPERTS - N_GROUPS), _F32)], axis=1)
    b_r = jnp.concatenate([b_router_expert[0], b_router_group[0],
                           jnp.zeros((ROUTER_W - N_EXPERTS - N_GROUPS,), _F32)])[None, :]
    wr_hi = w_r.astype(_BF)
    wr_lo = (w_r - wr_hi.astype(_F32)).astype(_BF)
    x1, rows, rec, counts = _merge(heads, four, p, x, mod3, norm2_g, w_attn_out[0].astype(_BF),
                                   w_four_out[0].astype(_BF), w_out[0].astype(_BF),
                                   jnp.concatenate([wr_hi, wr_lo], axis=1), b_r, MERGE_TM)

    t = b * n
    slabs = rows.reshape(t, SUBLANES, LANES)
    rec = rec.reshape(t // MERGE_TM, SUBLANES, MERGE_TM)
    pos, pad_start, pad_len, tile_ea, tile_eb, tile_valid, n_tiles = _routing_tables(
        rec[:, 2, :].reshape(t).astype(jnp.int32), rec[:, 3, :].reshape(t).astype(jnp.int32),
        counts[0, :N_CLASSES].astype(jnp.int32), t, MOE_TM)

    slabs_sorted = _dispatch(pos, pad_start, pad_len, slabs, n_tiles * MOE_TM, DISPATCH_TM)
    y_sorted = _moe(tile_ea, tile_eb, tile_valid, slabs_sorted, w_exp_gate[0].astype(_BF),
                    w_exp_up[0].astype(_BF), w_exp_down[0].astype(_BF), MOE_TM)
    out = _combine(pos, y_sorted, x1.reshape(t, d), mod3, final_g[None, :], n, COMBINE_TM)
    return out.reshape(b, n, d)
```

```python
import functools
import math

import jax
import jax.numpy as jnp
import numpy as np
from jax import lax
from jax.experimental import pallas as pl
from jax.experimental.pallas import tpu as pltpu

D_MODEL = 1024
GRID_W = 64
EPS = 1e-6
N_HEADS = 8
HEAD_DIM = 64
HEAD_W = 2 * HEAD_DIM
ROT_FREQS = HEAD_DIM // 4
ROPE_BASE = 10000.0
F_GROUPS = 4
F_GROUP_DIM = 128
F_W = F_GROUPS * F_GROUP_DIM
N_GROUPS = 4
EXPERTS_PER_GROUP = 4
LOG2_EXPERTS_PER_GROUP = 2
N_EXPERTS = N_GROUPS * EXPERTS_PER_GROUP
PAIRS_PER_GROUP = EXPERTS_PER_GROUP * (EXPERTS_PER_GROUP - 1) // 2
N_CLASSES = N_GROUPS * PAIRS_PER_GROUP
EXPERT_HIDDEN = 512
LAM_INIT = 0.8 - 0.6 * math.exp(-0.3 * 0)

REF_Q, REF_K, REF_V, REF_F, REF_GA, REF_GF, REF_END = 0, 1024, 2048, 3072, 3584, 4608, 5632
P_Q, P_K, P_V, P_GA, P_GF, P_F, P_W = 0, 1024, 2048, 3072, 4096, 5120, 5632

LANES = 128
SUBLANES = 8
VMEM_LIMIT_BYTES = 56 * 1024 * 1024

PROJ_TM = 1024
PROJ_CW = 512
ATTN_TQ = 256
ATTN_SUM_ROWS = 16
ATTN_SCORE_BUFS = 3
ATTN_HEADS_PER_STEP = 2
FNET_LEVELS = 2
FNET_CHUNK = 256
MERGE_TM = 512
MOE_TM = 256
COMBINE_TM = 512
DISPATCH_TM = 2048
DMA_ISSUE_GROUP = 8
MOD_ROWS = 40
ROUTER_W = LANES
GROUP_LANE0 = N_EXPERTS
H2_SLAB_ROWS = D_MODEL // 2 // LANES
ROUTE_SLAB_ROW = H2_SLAB_ROWS

_BF = jnp.bfloat16
_F32 = jnp.float32


def _cparams(*sem):
    return pltpu.CompilerParams(dimension_semantics=sem, vmem_limit_bytes=VMEM_LIMIT_BYTES)


def _const_spec(shape):
    return pl.BlockSpec(shape, lambda *_: (0,) * len(shape), pipeline_mode=pl.Buffered(1))


def _rms(x):
    return x * lax.rsqrt(jnp.mean(x * x, axis=-1, keepdims=True) + EPS)


def _sigmoid(x):
    return 1.0 / (1.0 + jnp.exp(-x))


def _mod_kernel(cc_ref, w_ref, b_ref, o_ref):
    cc = cc_ref[...]
    s = cc * _sigmoid(cc)
    o_ref[...] = jnp.dot(s, w_ref[...], preferred_element_type=_F32,
                         precision=lax.Precision.HIGHEST) + b_ref[...]


def _modulation(cc, w_mod, b_mod):
    n = w_mod.shape[1]
    bn = D_MODEL
    return pl.pallas_call(
        _mod_kernel,
        grid=(n // bn,),
        in_specs=[pl.BlockSpec((MOD_ROWS, D_MODEL), lambda j: (0, 0)),
                  pl.BlockSpec((D_MODEL, bn), lambda j: (0, j)),
                  pl.BlockSpec((1, bn), lambda j: (0, j))],
        out_specs=pl.BlockSpec((MOD_ROWS, bn), lambda j: (0, j)),
        out_shape=jax.ShapeDtypeStruct((MOD_ROWS, n), _F32),
        compiler_params=_cparams("arbitrary"),
        name="modulation",
    )(cc, w_mod, b_mod)


def _rope(acc, cos_ref, sin_ref):
    cos = cos_ref[...]
    sin = sin_ref[...]
    lane = lax.broadcasted_iota(jnp.int32, (1, LANES), 1)
    first_half = (lane % (2 * ROT_FREQS)) < ROT_FREQS
    outs = []
    for s in range(acc.shape[1] // LANES):
        xs = acc[:, s * LANES:(s + 1) * LANES]
        partner = jnp.where(first_half,
                            pltpu.roll(xs, LANES - ROT_FREQS, 1),
                            pltpu.roll(xs, ROT_FREQS, 1))
        outs.append(xs * cos + partner * sin)
    return jnp.concatenate(outs, axis=1)


def _inproj_kernel(x_ref, mod_ref, g_ref, w_ref, cos_ref, sin_ref, o_ref, *, chunk_kinds):
    x = x_ref[0]
    shift = mod_ref[0, 0:1, :]
    scale = mod_ref[0, 1:2, :]
    h = (_rms(x) * g_ref[...]) * (1.0 + scale) + shift
    hb = h.astype(_BF)
    for j, kind in enumerate(chunk_kinds):
        cols = slice(j * PROJ_CW, (j + 1) * PROJ_CW)
        acc = jnp.dot(hb, w_ref[:, cols], preferred_element_type=_F32)
        if kind == "rope":
            acc = _rope(acc, cos_ref, sin_ref)
        elif kind == "sigmoid":
            acc = _sigmoid(acc)
        o_ref[0, :, cols] = acc.astype(_BF)


def _in_projection(x, mod3, mod_row_of_batch, g, w, cos_t, sin_t, chunk_kinds, tm):
    b, n, d = x.shape
    width = w.shape[1]
    assert width == len(chunk_kinds) * PROJ_CW and n % tm == 0
    return pl.pallas_call(
        functools.partial(_inproj_kernel, chunk_kinds=chunk_kinds),
        grid=(b, n // tm),
        in_specs=[pl.BlockSpec((1, tm, d), lambda bi, i: (bi, i, 0)),
                  pl.BlockSpec((1, 6, d), lambda bi, i: (mod_row_of_batch(bi), 0, 0)),
                  pl.BlockSpec((1, d), lambda bi, i: (0, 0)),
                  _const_spec((d, width)),
                  pl.BlockSpec((tm, LANES), lambda bi, i: (i, 0)),
                  pl.BlockSpec((tm, LANES), lambda bi, i: (i, 0))],
        out_specs=pl.BlockSpec((1, tm, width), lambda bi, i: (bi, i, 0)),
        out_shape=jax.ShapeDtypeStruct((b, n, width), _BF),
        compiler_params=_cparams("parallel", "arbitrary"),
        name="in_projection",
    )(x, mod3, g, w, cos_t, sin_t)


def _attn_kernel(lam_ref, q_ref, kl_ref, vl_ref, kc_ref, vc_ref, sg_ref, o_ref,
                 kcat_ref, vt_ref, *bufs, tq):
    nc, n = kc_ref.shape[1], kl_ref.shape[1]
    for hh in range(ATTN_HEADS_PER_STEP):
        cols = slice(hh * HEAD_W, (hh + 1) * HEAD_W)
        kcat_ref[hh, 0:nc, :] = kc_ref[0, :, cols]
        kcat_ref[hh, nc:nc + n, :] = kl_ref[0, :, cols]
        vt_ref[hh, 0:HEAD_W, 0:nc] = vc_ref[0, :, cols].astype(_F32).T.astype(_BF)
        vt_ref[hh, 0:HEAD_W, nc:nc + n] = vl_ref[0, :, cols].astype(_F32).T.astype(_BF)
        vt_ref[hh, HEAD_W:, :] = jnp.ones((ATTN_SUM_ROWS, nc + n), _BF)
    lane = lax.broadcasted_iota(jnp.int32, (1, HEAD_W), 1)
    map_lanes = (lane < HEAD_DIM, lane >= HEAD_DIM)
    nt = (((1,), (1,)), ((), ()))
    lam = lam_ref[0]
    post_scale = sg_ref[...] * (1.0 - LAM_INIT)
    n_sub = n // tq

    def scores(g):
        hh, j = divmod(g, n_sub)
        q = q_ref[0, j * tq:(j + 1) * tq, hh * HEAD_W:(hh + 1) * HEAD_W]
        for mp in range(2):
            qm = jnp.where(map_lanes[mp], q, jnp.zeros_like(q))
            bufs[g % ATTN_SCORE_BUFS][mp] = lax.dot_general(kcat_ref[hh], qm, nt,
                                                            preferred_element_type=_F32)

    def finish(g):
        hh, j = divmod(g, n_sub)
        outs = []
        for mp in range(2):
            s = bufs[g % ATTN_SCORE_BUFS][mp]
            e = jnp.exp2(s - jnp.max(s, axis=0, keepdims=True)).astype(_BF)
            r = jnp.dot(vt_ref[hh], e, preferred_element_type=_F32)
            outs.append(r[:HEAD_W, :] / r[HEAD_W:HEAD_W + 1, :])
        heads = (outs[0] - lam * outs[1]).T
        o_ref[0, j * tq:(j + 1) * tq, hh * HEAD_W:(hh + 1) * HEAD_W] = (
            _rms(heads) * post_scale).astype(_BF)

    total = ATTN_HEADS_PER_STEP * n_sub
    for g in range(ATTN_SCORE_BUFS - 1):
        scores(g)
    for g in range(total):
        if g + ATTN_SCORE_BUFS - 1 < total:
            scores(g + ATTN_SCORE_BUFS - 1)
        finish(g)


def _attention(lam, p, kvc, subln_g, tq):
    b, n, _ = p.shape
    nc = kvc.shape[1]
    w = ATTN_HEADS_PER_STEP * HEAD_W
    qb, kb, vb = P_Q // w, P_K // w, P_V // w
    seq = lambda blk: pl.BlockSpec((1, n, w), lambda bi, h: (bi, 0, blk + h))
    return pl.pallas_call(
        functools.partial(_attn_kernel, tq=tq),
        grid=(b, N_HEADS // ATTN_HEADS_PER_STEP),
        in_specs=[pl.BlockSpec(memory_space=pltpu.SMEM),
                  seq(qb), seq(kb), seq(vb),
                  pl.BlockSpec((1, nc, w), lambda bi, h: (bi, 0, h)),
                  pl.BlockSpec((1, nc, w), lambda bi, h: (bi, 0, N_HEADS // ATTN_HEADS_PER_STEP + h)),
                  pl.BlockSpec((1, HEAD_W), lambda bi, h: (0, 0))],
        out_specs=seq(0),
        out_shape=jax.ShapeDtypeStruct((b, n, N_HEADS * HEAD_W), _BF),
        scratch_shapes=[pltpu.VMEM((ATTN_HEADS_PER_STEP, nc + n, HEAD_W), _BF),
                        pltpu.VMEM((ATTN_HEADS_PER_STEP, HEAD_W + ATTN_SUM_ROWS, nc + n), _BF),
                        ] + [pltpu.VMEM((2, nc + n, tq), _F32)] * ATTN_SCORE_BUFS,
        compiler_params=_cparams("parallel", "arbitrary"),
        name="diff_attention",
    )(lam, p, p, p, kvc, kvc, subln_g)


def _bit_reverse(j, bits):
    return int(format(j, "0%db" % bits)[::-1], 2) if bits else 0


def _fnet_kernel(f_ref, cs_ch_ref, cs_seq_ref, *rest, n):
    tw_refs = rest[:2 * FNET_LEVELS]
    o_ref, zr_ref, zi_ref, rhs_ref = rest[2 * FNET_LEVELS:]
    for g in range(F_GROUPS):
        cols = slice(g * F_GROUP_DIM, (g + 1) * F_GROUP_DIM)
        t = jnp.dot(f_ref[0, :, cols], cs_ch_ref[...], preferred_element_type=_F32)
        zr_ref[:, cols] = t[:, :F_GROUP_DIM]
        zi_ref[:, cols] = t[:, F_GROUP_DIM:]
    for level in range(FNET_LEVELS):
        m = n >> level
        half = m // 2
        cos_ref, sin_ref = tw_refs[2 * level], tw_refs[2 * level + 1]
        for seg in range(1 << level):
            for c0 in range(0, half, FNET_CHUNK):
                rows_t = slice(seg * m + c0, seg * m + c0 + FNET_CHUNK)
                rows_b = slice(seg * m + half + c0, seg * m + half + c0 + FNET_CHUNK)
                tr, ti = zr_ref[rows_t, :], zi_ref[rows_t, :]
                br, bi = zr_ref[rows_b, :], zi_ref[rows_b, :]
                cw, sw = cos_ref[c0:c0 + FNET_CHUNK, :], sin_ref[c0:c0 + FNET_CHUNK, :]
                dr, di = tr - br, ti - bi
                zr_ref[rows_t, :] = tr + br
                zi_ref[rows_t, :] = ti + bi
                zr_ref[rows_b, :] = dr * cw + di * sw
                zi_ref[rows_b, :] = di * cw - dr * sw
    m = n >> FNET_LEVELS
    for j in range(1 << FNET_LEVELS):
        cols = slice(_bit_reverse(j, FNET_LEVELS) * F_W, (_bit_reverse(j, FNET_LEVELS) + 1) * F_W)
        rhs_ref[0:m, cols] = zr_ref[j * m:(j + 1) * m, :].astype(_BF)
        rhs_ref[m:2 * m, cols] = zi_ref[j * m:(j + 1) * m, :].astype(_BF)
    ortho = 1.0 / math.sqrt(n * F_GROUP_DIM)
    y = jnp.dot(cs_seq_ref[...], rhs_ref[...], preferred_element_type=_F32)
    o_ref[0] = (y * ortho).reshape(m, 1 << FNET_LEVELS, F_W).reshape(n, F_W).astype(_BF)


def _fnet(p, cs_ch, cs_seq, twiddles):
    b, n, _ = p.shape
    m = n >> FNET_LEVELS
    wide = F_W << FNET_LEVELS
    return pl.pallas_call(
        functools.partial(_fnet_kernel, n=n),
        grid=(b,),
        in_specs=[pl.BlockSpec((1, n, F_W), lambda bi: (bi, 0, P_F // F_W)),
                  _const_spec((F_GROUP_DIM, 2 * F_GROUP_DIM)),
                  _const_spec((m, 2 * m))] + [_const_spec(t.shape) for t in twiddles],
        out_specs=pl.BlockSpec((1, n, F_W), lambda bi: (bi, 0, 0)),
        out_shape=jax.ShapeDtypeStruct((b, n, F_W), _BF),
        scratch_shapes=[pltpu.VMEM((n, F_W), _F32), pltpu.VMEM((n, F_W), _F32),
                        pltpu.VMEM((2 * m, wide), _BF)],
        compiler_params=_cparams("arbitrary"),
        name="fnet_dft",
    )(p, cs_ch, cs_seq, *twiddles)


def _pack_bf16_pair(hi, lo):
    hi_bits = pltpu.bitcast(hi.astype(_BF).astype(_F32), jnp.uint32)
    lo_bits = pltpu.bitcast(lo.astype(_BF).astype(_F32), jnp.uint32)
    return hi_bits | lax.shift_right_logical(lo_bits, jnp.uint32(16))


def _unpack_bf16_pair(packed):
    hi = pltpu.bitcast(packed & jnp.uint32(0xFFFF0000), _F32)
    lo = pltpu.bitcast(lax.shift_left(packed, jnp.uint32(16)), _F32)
    return hi, lo


def _route(logits, carry):
    rows = logits.shape[0]
    lane = lax.broadcasted_iota(jnp.int32, logits.shape, 1)
    neg = jnp.float32(-jnp.inf)
    big = jnp.int32(ROUTER_W)

    def first_argmax(v):
        m = jnp.max(v, axis=-1, keepdims=True)
        idx = jnp.min(jnp.where(v == m, lane, big), axis=-1, keepdims=True)
        return m, idx

    lg = jnp.where((lane >= GROUP_LANE0) & (lane < GROUP_LANE0 + N_GROUPS), logits, neg)
    mg, ig = first_argmax(lg)
    w_grp = 1.0 / jnp.sum(jnp.exp(lg - mg), axis=-1, keepdims=True)
    g_sel = ig - GROUP_LANE0
    le = jnp.where((lane < N_EXPERTS)
                   & (jnp.right_shift(lane, LOG2_EXPERTS_PER_GROUP) == g_sel), logits, neg)
    v1, i1 = first_argmax(le)
    le2 = jnp.where(lane == i1, neg, le)
    v2, i2 = first_argmax(le2)
    e2 = jnp.exp(v2 - v1)
    w1 = w_grp / (1.0 + e2)
    w2 = w_grp * e2 / (1.0 + e2)

    first_is_lower = i1 < i2
    gate_a = jnp.where(first_is_lower, w1, w2)
    gate_b = jnp.where(first_is_lower, w2, w1)
    la = jnp.minimum(i1, i2) - g_sel * EXPERTS_PER_GROUP
    lb = jnp.maximum(i1, i2) - g_sel * EXPERTS_PER_GROUP
    pair = jnp.right_shift(la * (2 * EXPERTS_PER_GROUP - 1 - la), 1) + lb - la - 1
    cls = g_sel * PAIRS_PER_GROUP + pair

    onehot = (lane == cls).astype(_F32)
    r_i = lax.broadcasted_iota(jnp.int32, (rows, rows), 0)
    c_i = lax.broadcasted_iota(jnp.int32, (rows, rows), 1)
    earlier = (c_i < r_i).astype(_BF)
    before = jnp.dot(earlier, onehot.astype(_BF), preferred_element_type=_F32) + carry
    rank = jnp.sum(onehot * before, axis=-1, keepdims=True)
    new_carry = carry + jnp.sum(onehot, axis=0, keepdims=True)

    route = (jnp.where(lane == 0, gate_a, 0.0) + jnp.where(lane == 1, gate_b, 0.0)
             + jnp.where(lane == 2, cls.astype(_F32), 0.0) + jnp.where(lane == 3, rank, 0.0))
    return route, new_carry


def _merge_kernel(hd_ref, fo_ref, ga_ref, gf_ref, x_ref, mod_ref, g2n_ref, wao_ref, wfo_ref,
                  wo_ref, wrhl_ref, br_ref, x1_ref, rows_ref, rec_ref, counts_ref, carry_ref):
    @pl.when((pl.program_id(0) == 0) & (pl.program_id(1) == 0))
    def _():
        carry_ref[...] = jnp.zeros_like(carry_ref)

    a = jnp.dot(hd_ref[0], wao_ref[...], preferred_element_type=_F32)
    ff = jnp.dot(fo_ref[0], wfo_ref[...], preferred_element_type=_F32)
    y = ga_ref[0].astype(_F32) * a + gf_ref[0].astype(_F32) * ff
    mix = jnp.dot(y.astype(_BF), wo_ref[...], preferred_element_type=_F32)
    x1 = x_ref[0] + mod_ref[0, 2:3, :] * mix
    x1_ref[0] = x1
    h2 = (_rms(x1) * g2n_ref[...]) * (1.0 + mod_ref[0, 4:5, :]) + mod_ref[0, 3:4, :]
    h2_hi = h2.astype(_BF)
    h2_lo = (h2 - h2_hi.astype(_F32)).astype(_BF)
    hi_both = jnp.dot(h2_hi, wrhl_ref[...], preferred_element_type=_F32)
    logits = (hi_both[:, :ROUTER_W] + hi_both[:, ROUTER_W:]
              + jnp.dot(h2_lo, wrhl_ref[:, :ROUTER_W], preferred_element_type=_F32)) + br_ref[...]
    route, new_carry = _route(logits, carry_ref[...])
    carry_ref[...] = new_carry
    counts_ref[...] = new_carry
    rec_ref[0, 0] = route.T[0:SUBLANES, :]
    half = D_MODEL // 2
    tm = h2.shape[0]
    packed = _pack_bf16_pair(h2[:, :half], h2[:, half:])
    rows_ref[0, :, 0:H2_SLAB_ROWS, :] = packed.reshape(tm, H2_SLAB_ROWS, LANES)
    rows_ref[0, :, ROUTE_SLAB_ROW:ROUTE_SLAB_ROW + 1, :] = (
        pltpu.bitcast(route, jnp.uint32).reshape(tm, 1, LANES))
    rows_ref[0, :, ROUTE_SLAB_ROW + 1:SUBLANES, :] = jnp.zeros(
        (tm, SUBLANES - ROUTE_SLAB_ROW - 1, LANES), jnp.uint32)


def _merge(heads, four, p, x, mod3, norm2_g, w_ao, w_fo, w_o, wr_hilo, b_r, tm):
    b, n, d = x.shape
    tok = lambda w: pl.BlockSpec((1, tm, w), lambda bi, i: (bi, i, 0))
    return pl.pallas_call(
        _merge_kernel,
        grid=(b, n // tm),
        in_specs=[tok(d), tok(F_W),
                  pl.BlockSpec((1, tm, d), lambda bi, i: (bi, i, P_GA // D_MODEL)),
                  pl.BlockSpec((1, tm, d), lambda bi, i: (bi, i, P_GF // D_MODEL)),
                  tok(d),
                  pl.BlockSpec((1, 6, d), lambda bi, i: (bi, 0, 0)),
                  pl.BlockSpec((1, d), lambda bi, i: (0, 0)),
                  _const_spec((d, d)), _const_spec((F_W, d)), _const_spec((d, d)),
                  _const_spec((d, 2 * ROUTER_W)),
                  pl.BlockSpec((1, ROUTER_W), lambda bi, i: (0, 0))],
        out_specs=[tok(d),
                   pl.BlockSpec((1, tm, SUBLANES, LANES), lambda bi, i: (bi, i, 0, 0)),
                   pl.BlockSpec((1, 1, SUBLANES, tm), lambda bi, i: (bi, i, 0, 0)),
                   pl.BlockSpec((1, ROUTER_W), lambda bi, i: (0, 0))],
        out_shape=[jax.ShapeDtypeStruct((b, n, d), _F32),
                   jax.ShapeDtypeStruct((b, n, SUBLANES, LANES), jnp.uint32),
                   jax.ShapeDtypeStruct((b, n // tm, SUBLANES, tm), _F32),
                   jax.ShapeDtypeStruct((1, ROUTER_W), _F32)],
        scratch_shapes=[pltpu.VMEM((1, ROUTER_W), _F32)],
        compiler_params=_cparams("arbitrary", "arbitrary"),
        name="merge_router",
    )(heads, four, p, p, x, mod3, norm2_g, w_ao, w_fo, w_o, wr_hilo, b_r)


def _issue_slab_copies(n, make_copy, slot_of):
    def group(g, c):
        r0 = g * DMA_ISSUE_GROUP
        slots = [slot_of(r0 + k) for k in range(DMA_ISSUE_GROUP)]
        for k in range(DMA_ISSUE_GROUP):
            make_copy(r0 + k, slots[k]).start(priority=k % 2)
        return c

    lax.fori_loop(0, n // DMA_ISSUE_GROUP, group, 0)


def _dispatch_kernel(pos_ref, pad_start_ref, pad_len_ref, src_ref, dst_ref, zero_ref, sem, pad_sem,
                     *, tm):
    base = pl.program_id(0) * tm

    @pl.when(pl.program_id(0) == 0)
    def _():
        zero_ref[...] = jnp.zeros_like(zero_ref)
        for c in range(N_CLASSES + 1):
            def start(r, carry, c=c):
                pltpu.make_async_copy(zero_ref, dst_ref.at[pad_start_ref[c] + r], pad_sem).start()
                return carry

            def drain(r, carry):
                pltpu.make_async_copy(zero_ref, dst_ref.at[0], pad_sem).wait()
                return carry

            lax.fori_loop(0, pad_len_ref[c], start, 0)
            lax.fori_loop(0, pad_len_ref[c], drain, 0)

    _issue_slab_copies(
        tm,
        lambda r, slot: pltpu.make_async_copy(src_ref.at[r], dst_ref.at[slot], sem),
        lambda r: pos_ref[base + r])
    pltpu.make_async_copy(src_ref, dst_ref.at[pl.ds(0, tm)], sem).wait()


def _dispatch(pos, pad_start, pad_len, slabs, n_sorted, tm):
    t = slabs.shape[0]
    return pl.pallas_call(
        functools.partial(_dispatch_kernel, tm=tm),
        grid_spec=pltpu.PrefetchScalarGridSpec(
            num_scalar_prefetch=3,
            grid=(t // tm,),
            in_specs=[pl.BlockSpec((tm, SUBLANES, LANES), lambda i, *_: (i, 0, 0))],
            out_specs=pl.BlockSpec(memory_space=pl.ANY),
            scratch_shapes=[pltpu.VMEM((SUBLANES, LANES), slabs.dtype),
                            pltpu.SemaphoreType.DMA(()), pltpu.SemaphoreType.DMA(())]),
        out_shape=jax.ShapeDtypeStruct((n_sorted,) + slabs.shape[1:], slabs.dtype),
        compiler_params=_cparams("arbitrary"),
        name="moe_dispatch",
    )(pos, pad_start, pad_len, slabs)


def _moe_kernel(ea_ref, eb_ref, valid_ref, slabs_ref, w1a_ref, w3a_ref, w2a_ref,
                w1b_ref, w3b_ref, w2b_ref, y_ref):
    del ea_ref, eb_ref
    i = pl.program_id(0)
    tm = slabs_ref.shape[0]

    @pl.when(valid_ref[i] != 0)
    def _():
        packed = slabs_ref[:, 0:H2_SLAB_ROWS, :].reshape(tm, H2_SLAB_ROWS * LANES)
        hi, lo = _unpack_bf16_pair(packed)
        t = jnp.concatenate([hi.astype(_BF), lo.astype(_BF)], axis=1)
        route = pltpu.bitcast(slabs_ref[:, ROUTE_SLAB_ROW, :], _F32)
        y = None
        for slot, (w1_ref, w3_ref, w2_ref) in enumerate(((w1a_ref, w3a_ref, w2a_ref),
                                                          (w1b_ref, w3b_ref, w2b_ref))):
            a = jnp.dot(t, w1_ref[0], preferred_element_type=_F32)
            u = jnp.dot(t, w3_ref[0], preferred_element_type=_F32)
            hid = ((a * _sigmoid(a)) * u * route[:, slot:slot + 1]).astype(_BF)
            part = jnp.dot(hid, w2_ref[0], preferred_element_type=_F32)
            y = part if y is None else y + part
        y_ref[...] = y.reshape(tm, SUBLANES, LANES)

    @pl.when(valid_ref[i] == 0)
    def _():
        y_ref[...] = jnp.zeros_like(y_ref)


def _moe(tile_ea, tile_eb, tile_valid, slabs_sorted, w1, w3, w2, tm):
    n_sorted = slabs_sorted.shape[0]
    d, hdn = w1.shape[1], w1.shape[2]
    assert d == SUBLANES * LANES
    pick = lambda which, a, b_: pl.BlockSpec(
        (1, a, b_), lambda i, ea, eb, valid: ((ea, eb)[which][i], 0, 0))
    slab_tile = pl.BlockSpec((tm, SUBLANES, LANES), lambda i, ea, eb, valid: (i, 0, 0))
    return pl.pallas_call(
        _moe_kernel,
        grid_spec=pltpu.PrefetchScalarGridSpec(
            num_scalar_prefetch=3,
            grid=(n_sorted // tm,),
            in_specs=[slab_tile,
                      pick(0, d, hdn), pick(0, d, hdn), pick(0, hdn, d),
                      pick(1, d, hdn), pick(1, d, hdn), pick(1, hdn, d)],
            out_specs=slab_tile),
        out_shape=jax.ShapeDtypeStruct((n_sorted, SUBLANES, LANES), _F32),
        compiler_params=_cparams("arbitrary"),
        name="moe_experts",
    )(tile_ea, tile_eb, tile_valid, slabs_sorted, w1, w3, w2, w1, w3, w2)


def _combine_kernel(pos_ref, y_hbm_ref, x1_ref, mod_ref, fg_ref, o_ref, ybuf_ref, sems, *, tm):
    i = pl.program_id(0)
    n_steps = pl.num_programs(0)

    def issue(tile, slot):
        _issue_slab_copies(
            tm,
            lambda r, src: pltpu.make_async_copy(y_hbm_ref.at[src], ybuf_ref.at[slot, r],
                                                 sems.at[slot]),
            lambda r: pos_ref[tile * tm + r])

    @pl.when(i == 0)
    def _():
        issue(0, 0)

    @pl.when(i + 1 < n_steps)
    def _():
        issue(i + 1, (i + 1) % 2)

    slot = i % 2
    pltpu.make_async_copy(y_hbm_ref.at[pl.ds(0, tm)], ybuf_ref.at[slot], sems.at[slot]).wait()
    y = ybuf_ref[slot].reshape(tm, SUBLANES * LANES)
    xo = x1_ref[...] + mod_ref[0, 5:6, :] * y
    o_ref[...] = _rms(xo) * fg_ref[...]


def _combine(pos, y_sorted, x1, mod3, final_g, seq, tm):
    t, d = x1.shape
    tiles_per_batch = seq // tm
    return pl.pallas_call(
        functools.partial(_combine_kernel, tm=tm),
        grid_spec=pltpu.PrefetchScalarGridSpec(
            num_scalar_prefetch=1,
            grid=(t // tm,),
            in_specs=[pl.BlockSpec(memory_space=pl.ANY),
                      pl.BlockSpec((tm, d), lambda i, pos: (i, 0)),
                      pl.BlockSpec((1, 6, d), lambda i, pos: (i // tiles_per_batch, 0, 0)),
                      pl.BlockSpec((1, d), lambda i, pos: (0, 0))],
            out_specs=pl.BlockSpec((tm, d), lambda i, pos: (i, 0)),
            scratch_shapes=[pltpu.VMEM((2, tm, SUBLANES, LANES), _F32),
                            pltpu.SemaphoreType.DMA((2,))]),
        out_shape=jax.ShapeDtypeStruct((t, d), _F32),
        compiler_params=_cparams("arbitrary"),
        name="moe_combine",
    )(pos, y_sorted, x1, mod3, final_g)


def _routing_tables(route_cls, route_rank, counts, n_tokens, tm):
    sizes = ((counts + (tm - 1)) // tm) * tm
    ends = jnp.cumsum(sizes)
    starts = ends - sizes
    n_tiles = n_tokens // tm + N_CLASSES
    tile_start = jnp.arange(n_tiles, dtype=jnp.int32) * tm
    pos = route_rank
    tile_cls = jnp.zeros((n_tiles,), jnp.int32)
    for c in range(N_CLASSES):
        pos = pos + jnp.where(route_cls == c, starts[c], 0)
        tile_cls = tile_cls + (tile_start >= ends[c]).astype(jnp.int32)
    tile_cls = jnp.minimum(tile_cls, N_CLASSES - 1)
    tile_valid = (tile_start < ends[-1]).astype(jnp.int32)
    pairs = [(a, b_) for a in range(EXPERTS_PER_GROUP) for b_ in range(a + 1, EXPERTS_PER_GROUP)]
    grp, pair = tile_cls // PAIRS_PER_GROUP, tile_cls % PAIRS_PER_GROUP
    tile_ea = grp * EXPERTS_PER_GROUP
    tile_eb = grp * EXPERTS_PER_GROUP
    for k, (a, b_) in enumerate(pairs):
        tile_ea = tile_ea + jnp.where(pair == k, a, 0)
        tile_eb = tile_eb + jnp.where(pair == k, b_, 0)
    pad_start = jnp.concatenate([starts + counts, ends[-1:]]).astype(jnp.int32)
    pad_len = jnp.concatenate([sizes - counts, n_tiles * tm - ends[-1:]]).astype(jnp.int32)
    return (pos.astype(jnp.int32), pad_start, pad_len, tile_ea.astype(jnp.int32),
            tile_eb.astype(jnp.int32), tile_valid, n_tiles)


def _rope_tables(n):
    inv = (1.0 / (ROPE_BASE ** (np.arange(ROT_FREQS, dtype=np.float32) / ROT_FREQS))).astype(np.float32)
    pos = np.arange(n)
    row = (pos // GRID_W).astype(np.float32)[:, None] * inv[None, :]
    col = (pos % GRID_W).astype(np.float32)[:, None] * inv[None, :]
    cos64 = np.concatenate([np.cos(row), np.cos(row), np.cos(col), np.cos(col)], axis=1)
    sin64 = np.concatenate([-np.sin(row), np.sin(row), -np.sin(col), np.sin(col)], axis=1)
    tile = lambda a: np.tile(a.astype(np.float32), (1, LANES // HEAD_DIM))
    return jnp.asarray(tile(cos64)), jnp.asarray(tile(sin64))


def _dft_cos_sin(n):
    k = np.arange(n, dtype=np.int64)
    ang = (2.0 * np.pi / n) * ((k[:, None] * k[None, :]) % n).astype(np.float64)
    return np.cos(ang), np.sin(ang)


def _dft_tables(n):
    c_ch, s_ch = _dft_cos_sin(F_GROUP_DIM)
    c_seq, s_seq = _dft_cos_sin(n >> FNET_LEVELS)
    cs_ch = np.concatenate([c_ch, -s_ch], axis=1).astype(np.float32)
    cs_seq = np.concatenate([c_seq, s_seq], axis=1).astype(np.float32)
    twiddles = []
    for level in range(FNET_LEVELS):
        m = n >> level
        ang = (2.0 * np.pi / m) * np.arange(m // 2, dtype=np.float64)[:, None]
        for tab in (np.cos(ang), np.sin(ang)):
            twiddles.append(jnp.asarray(np.broadcast_to(tab, (m // 2, F_W)).astype(np.float32)))
    return jnp.asarray(cs_ch.astype(_BF)), jnp.asarray(cs_seq.astype(_BF)), twiddles


def kernel(x, c, ctx, c_ctx, w_mod, b_mod, norm1_g, norm2_g, w_in, lam_q1, lam_k1, lam_q2, lam_k2,
           subln_g, w_attn_out, w_four_out, w_out, w_router_group, b_router_group, w_router_expert,
           b_router_expert, w_exp_gate, w_exp_up, w_exp_down, final_g):
    b, n, d = x.shape
    assert w_mod.shape[0] == 1, "depth-1 stack"
    assert b + 1 <= MOD_ROWS

    cc = jnp.concatenate([c, c_ctx[None, :], jnp.zeros((MOD_ROWS - b - 1, d), _F32)], axis=0)
    mod3 = _modulation(cc, w_mod[0], b_mod).reshape(MOD_ROWS, 6, d)

    lam = (jnp.exp(jnp.sum(lam_q1[0] * lam_k1[0])) - jnp.exp(jnp.sum(lam_q2[0] * lam_k2[0]))
           + LAM_INIT).reshape(1).astype(_F32)

    w = w_in[0]
    scale = HEAD_DIM ** -0.5 * math.log2(math.e)
    w_lat = jnp.concatenate([w[:, REF_Q:REF_K] * scale, w[:, REF_K:REF_F],
                             w[:, REF_GA:REF_END], w[:, REF_F:REF_GA]], axis=1).astype(_BF)
    w_ctx = w[:, REF_K:REF_F].astype(_BF)
    lat_kinds = ("rope",) * ((P_V - P_Q) // PROJ_CW) + ("plain",) * ((P_GA - P_V) // PROJ_CW) \
        + ("sigmoid",) * ((P_F - P_GA) // PROJ_CW) + ("plain",) * ((P_W - P_F) // PROJ_CW)
    ctx_kinds = ("plain",) * (w_ctx.shape[1] // PROJ_CW)

    cos_t, sin_t = _rope_tables(n)
    p = _in_projection(x, mod3, lambda bi: bi, norm1_g, w_lat, cos_t, sin_t, lat_kinds, PROJ_TM)
    kvc = _in_projection(ctx, mod3, lambda bi: b, norm1_g, w_ctx, cos_t, sin_t, ctx_kinds,
                         ctx.shape[1])

    heads = _attention(lam, p, kvc, subln_g, ATTN_TQ)
    cs_ch, cs_seq, twiddles = _dft_tables(n)
    four = _fnet(p, cs_ch, cs_seq, twiddles)

    w_r = jnp.concatenate([w_router_expert[0], w_router_group[0],
                           jnp.zeros((d, ROUTER_W - N_EXPERTS - N_GROUPS), _F32)], axis=1)
    b_r = jnp.concatenate([b_router_expert[0], b_router_group[0],
                           jnp.zeros((ROUTER_W - N_EXPERTS - N_GROUPS,), _F32)])[None, :]
    wr_hi = w_r.astype(_BF)
    wr_lo = (w_r - wr_hi.astype(_F32)).astype(_BF)
    x1, rows, rec, counts = _merge(heads, four, p, x, mod3, norm2_g, w_attn_out[0].astype(_BF),
                                   w_four_out[0].astype(_BF), w_out[0].astype(_BF),
                                   jnp.concatenate([wr_hi, wr_lo], axis=1), b_r, MERGE_TM)

    t = b * n
    slabs = rows.reshape(t, SUBLANES, LANES)
    rec = rec.reshape(t // MERGE_TM, SUBLANES, MERGE_TM)
    pos, pad_start, pad_len, tile_ea, tile_eb, tile_valid, n_tiles = _routing_tables(
        rec[:, 2, :].reshape(t).astype(jnp.int32), rec[:, 3, :].reshape(t).astype(jnp.int32),
        counts[0, :N_CLASSES].astype(jnp.int32), t, MOE_TM)

    slabs_sorted = _dispatch(pos, pad_start, pad_len, slabs, n_tiles * MOE_TM, DISPATCH_TM)
    y_sorted = _moe(tile_ea, tile_eb, tile_valid, slabs_sorted, w_exp_gate[0].astype(_BF),
                    w_exp_up[0].astype(_BF), w_exp_down[0].astype(_BF), MOE_TM)
    out = _combine(pos, y_sorted, x1.reshape(t, d), mod3, final_g[None, :], n, COMBINE_TM)
    return out.reshape(b, n, d)
```

```python
import functools
import math

import jax
import jax.numpy as jnp
import numpy as np
from jax import lax
from jax.experimental import pallas as pl
from jax.experimental.pallas import tpu as pltpu

D_MODEL = 1024
GRID_W = 64
EPS = 1e-6
N_HEADS = 8
HEAD_DIM = 64
HEAD_W = 2 * HEAD_DIM
ROT_FREQS = HEAD_DIM // 4
ROPE_BASE = 10000.0
F_GROUPS = 4
F_GROUP_DIM = 128
F_W = F_GROUPS * F_GROUP_DIM
N_GROUPS = 4
EXPERTS_PER_GROUP = 4
LOG2_EXPERTS_PER_GROUP = 2
N_EXPERTS = N_GROUPS * EXPERTS_PER_GROUP
PAIRS_PER_GROUP = EXPERTS_PER_GROUP * (EXPERTS_PER_GROUP - 1) // 2
N_CLASSES = N_GROUPS * PAIRS_PER_GROUP
EXPERT_HIDDEN = 512
LAM_INIT = 0.8 - 0.6 * math.exp(-0.3 * 0)

REF_Q, REF_K, REF_V, REF_F, REF_GA, REF_GF, REF_END = 0, 1024, 2048, 3072, 3584, 4608, 5632
P_Q, P_K, P_V, P_GA, P_GF, P_F, P_W = 0, 1024, 2048, 3072, 4096, 5120, 5632

LANES = 128
SUBLANES = 8
VMEM_LIMIT_BYTES = 56 * 1024 * 1024

PROJ_TM = 1024
PROJ_CW = 512
ATTN_TQ = 256
ATTN_SUM_ROWS = 16
ATTN_SCORE_BUFS = 3
ATTN_HEADS_PER_STEP = 4
FNET_LEVELS = 2
FNET_CHUNK = 256
MERGE_TM = 512
MOE_TM = 256
COMBINE_TM = 512
DISPATCH_TM = 2048
DMA_ISSUE_GROUP = 8
MOD_ROWS = 40
ROUTER_W = LANES
GROUP_LANE0 = N_EXPERTS
H2_SLAB_ROWS = D_MODEL // 2 // LANES
ROUTE_SLAB_ROW = H2_SLAB_ROWS

_BF = jnp.bfloat16
_F32 = jnp.float32


def _cparams(*sem):
    return pltpu.CompilerParams(dimension_semantics=sem, vmem_limit_bytes=VMEM_LIMIT_BYTES)


def _const_spec(shape):
    return pl.BlockSpec(shape, lambda *_: (0,) * len(shape), pipeline_mode=pl.Buffered(1))


def _rms(x):
    return x * lax.rsqrt(jnp.mean(x * x, axis=-1, keepdims=True) + EPS)


def _sigmoid(x):
    return 1.0 / (1.0 + jnp.exp(-x))


def _mod_kernel(cc_ref, w_ref, b_ref, o_ref):
    cc = cc_ref[...]
    s = cc * _sigmoid(cc)
    o_ref[...] = jnp.dot(s, w_ref[...], preferred_element_type=_F32,
                         precision=lax.Precision.HIGHEST) + b_ref[...]


def _modulation(cc, w_mod, b_mod):
    n = w_mod.shape[1]
    bn = D_MODEL
    return pl.pallas_call(
        _mod_kernel,
        grid=(n // bn,),
        in_specs=[pl.BlockSpec((MOD_ROWS, D_MODEL), lambda j: (0, 0)),
                  pl.BlockSpec((D_MODEL, bn), lambda j: (0, j)),
                  pl.BlockSpec((1, bn), lambda j: (0, j))],
        out_specs=pl.BlockSpec((MOD_ROWS, bn), lambda j: (0, j)),
        out_shape=jax.ShapeDtypeStruct((MOD_ROWS, n), _F32),
        compiler_params=_cparams("arbitrary"),
        name="modulation",
    )(cc, w_mod, b_mod)


def _rope(acc, cos_ref, sin_ref):
    cos = cos_ref[...]
    sin = sin_ref[...]
    lane = lax.broadcasted_iota(jnp.int32, (1, LANES), 1)
    first_half = (lane % (2 * ROT_FREQS)) < ROT_FREQS
    outs = []
    for s in range(acc.shape[1] // LANES):
        xs = acc[:, s * LANES:(s + 1) * LANES]
        partner = jnp.where(first_half,
                            pltpu.roll(xs, LANES - ROT_FREQS, 1),
                            pltpu.roll(xs, ROT_FREQS, 1))
        outs.append(xs * cos + partner * sin)
    return jnp.concatenate(outs, axis=1)


def _inproj_kernel(x_ref, mod_ref, g_ref, w_ref, cos_ref, sin_ref, o_ref, *, chunk_kinds):
    x = x_ref[0]
    shift = mod_ref[0, 0:1, :]
    scale = mod_ref[0, 1:2, :]
    h = (_rms(x) * g_ref[...]) * (1.0 + scale) + shift
    hb = h.astype(_BF)
    for j, kind in enumerate(chunk_kinds):
        cols = slice(j * PROJ_CW, (j + 1) * PROJ_CW)
        acc = jnp.dot(hb, w_ref[:, cols], preferred_element_type=_F32)
        if kind == "rope":
            acc = _rope(acc, cos_ref, sin_ref)
        elif kind == "sigmoid":
            acc = _sigmoid(acc)
        o_ref[0, :, cols] = acc.astype(_BF)


def _in_projection(x, mod3, mod_row_of_batch, g, w, cos_t, sin_t, chunk_kinds, tm):
    b, n, d = x.shape
    width = w.shape[1]
    assert width == len(chunk_kinds) * PROJ_CW and n % tm == 0
    return pl.pallas_call(
        functools.partial(_inproj_kernel, chunk_kinds=chunk_kinds),
        grid=(b, n // tm),
        in_specs=[pl.BlockSpec((1, tm, d), lambda bi, i: (bi, i, 0)),
                  pl.BlockSpec((1, 6, d), lambda bi, i: (mod_row_of_batch(bi), 0, 0)),
                  pl.BlockSpec((1, d), lambda bi, i: (0, 0)),
                  _const_spec((d, width)),
                  pl.BlockSpec((tm, LANES), lambda bi, i: (i, 0)),
                  pl.BlockSpec((tm, LANES), lambda bi, i: (i, 0))],
        out_specs=pl.BlockSpec((1, tm, width), lambda bi, i: (bi, i, 0)),
        out_shape=jax.ShapeDtypeStruct((b, n, width), _BF),
        compiler_params=_cparams("parallel", "arbitrary"),
        name="in_projection",
    )(x, mod3, g, w, cos_t, sin_t)


def _attn_kernel(lam_ref, q_ref, kl_ref, vl_ref, kc_ref, vc_ref, sg_ref, o_ref,
                 kcat_ref, vt_ref, *bufs, tq):
    nc, n = kc_ref.shape[1], kl_ref.shape[1]
    for hh in range(ATTN_HEADS_PER_STEP):
        cols = slice(hh * HEAD_W, (hh + 1) * HEAD_W)
        kcat_ref[hh, 0:nc, :] = kc_ref[0, :, cols]
        kcat_ref[hh, nc:nc + n, :] = kl_ref[0, :, cols]
        vt_ref[hh, 0:HEAD_W, 0:nc] = vc_ref[0, :, cols].astype(_F32).T.astype(_BF)
        vt_ref[hh, 0:HEAD_W, nc:nc + n] = vl_ref[0, :, cols].astype(_F32).T.astype(_BF)
        vt_ref[hh, HEAD_W:, :] = jnp.ones((ATTN_SUM_ROWS, nc + n), _BF)
    lane = lax.broadcasted_iota(jnp.int32, (1, HEAD_W), 1)
    map_lanes = (lane < HEAD_DIM, lane >= HEAD_DIM)
    nt = (((1,), (1,)), ((), ()))
    lam = lam_ref[0]
    post_scale = sg_ref[...] * (1.0 - LAM_INIT)
    n_sub = n // tq

    def scores(g):
        hh, j = divmod(g, n_sub)
        q = q_ref[0, j * tq:(j + 1) * tq, hh * HEAD_W:(hh + 1) * HEAD_W]
        for mp in range(2):
            qm = jnp.where(map_lanes[mp], q, jnp.zeros_like(q))
            bufs[g % ATTN_SCORE_BUFS][mp] = lax.dot_general(kcat_ref[hh], qm, nt,
                                                            preferred_element_type=_F32)

    def finish(g):
        hh, j = divmod(g, n_sub)
        outs = []
        for mp in range(2):
            s = bufs[g % ATTN_SCORE_BUFS][mp]
            e = jnp.exp2(s - jnp.max(s, axis=0, keepdims=True)).astype(_BF)
            r = jnp.dot(vt_ref[hh], e, preferred_element_type=_F32)
            outs.append(r[:HEAD_W, :] / r[HEAD_W:HEAD_W + 1, :])
        heads = (outs[0] - lam * outs[1]).T
        o_ref[0, j * tq:(j + 1) * tq, hh * HEAD_W:(hh + 1) * HEAD_W] = (
            _rms(heads) * post_scale).astype(_BF)

    total = ATTN_HEADS_PER_STEP * n_sub
    for g in range(ATTN_SCORE_BUFS - 1):
        scores(g)
    for g in range(total):
        if g + ATTN_SCORE_BUFS - 1 < total:
            scores(g + ATTN_SCORE_BUFS - 1)
        finish(g)


def _attention(lam, p, kvc, subln_g, tq):
    b, n, _ = p.shape
    nc = kvc.shape[1]
    w = ATTN_HEADS_PER_STEP * HEAD_W
    qb, kb, vb = P_Q // w, P_K // w, P_V // w
    seq = lambda blk: pl.BlockSpec((1, n, w), lambda bi, h: (bi, 0, blk + h))
    return pl.pallas_call(
        functools.partial(_attn_kernel, tq=tq),
        grid=(b, N_HEADS // ATTN_HEADS_PER_STEP),
        in_specs=[pl.BlockSpec(memory_space=pltpu.SMEM),
                  seq(qb), seq(kb), seq(vb),
                  pl.BlockSpec((1, nc, w), lambda bi, h: (bi, 0, h)),
                  pl.BlockSpec((1, nc, w), lambda bi, h: (bi, 0, N_HEADS // ATTN_HEADS_PER_STEP + h)),
                  pl.BlockSpec((1, HEAD_W), lambda bi, h: (0, 0))],
        out_specs=seq(0),
        out_shape=jax.ShapeDtypeStruct((b, n, N_HEADS * HEAD_W), _BF),
        scratch_shapes=[pltpu.VMEM((ATTN_HEADS_PER_STEP, nc + n, HEAD_W), _BF),
                        pltpu.VMEM((ATTN_HEADS_PER_STEP, HEAD_W + ATTN_SUM_ROWS, nc + n), _BF),
                        ] + [pltpu.VMEM((2, nc + n, tq), _F32)] * ATTN_SCORE_BUFS,
        compiler_params=_cparams("parallel", "arbitrary"),
        name="diff_attention",
    )(lam, p, p, p, kvc, kvc, subln_g)


def _bit_reverse(j, bits):
    return int(format(j, "0%db" % bits)[::-1], 2) if bits else 0


def _fnet_kernel(f_ref, cs_ch_ref, cs_seq_ref, *rest, n):
    tw_refs = rest[:2 * FNET_LEVELS]
    o_ref, zr_ref, zi_ref, rhs_ref = rest[2 * FNET_LEVELS:]
    for g in range(F_GROUPS):
        cols = slice(g * F_GROUP_DIM, (g + 1) * F_GROUP_DIM)
        t = jnp.dot(f_ref[0, :, cols], cs_ch_ref[...], preferred_element_type=_F32)
        zr_ref[:, cols] = t[:, :F_GROUP_DIM]
        zi_ref[:, cols] = t[:, F_GROUP_DIM:]
    for level in range(FNET_LEVELS):
        m = n >> level
        half = m // 2
        cos_ref, sin_ref = tw_refs[2 * level], tw_refs[2 * level + 1]
        for seg in range(1 << level):
            for c0 in range(0, half, FNET_CHUNK):
                rows_t = slice(seg * m + c0, seg * m + c0 + FNET_CHUNK)
                rows_b = slice(seg * m + half + c0, seg * m + half + c0 + FNET_CHUNK)
                tr, ti = zr_ref[rows_t, :], zi_ref[rows_t, :]
                br, bi = zr_ref[rows_b, :], zi_ref[rows_b, :]
                cw, sw = cos_ref[c0:c0 + FNET_CHUNK, :], sin_ref[c0:c0 + FNET_CHUNK, :]
                dr, di = tr - br, ti - bi
                zr_ref[rows_t, :] = tr + br
                zi_ref[rows_t, :] = ti + bi
                zr_ref[rows_b, :] = dr * cw + di * sw
                zi_ref[rows_b, :] = di * cw - dr * sw
    m = n >> FNET_LEVELS
    for j in range(1 << FNET_LEVELS):
        cols = slice(_bit_reverse(j, FNET_LEVELS) * F_W, (_bit_reverse(j, FNET_LEVELS) + 1) * F_W)
        rhs_ref[0:m, cols] = zr_ref[j * m:(j + 1) * m, :].astype(_BF)
        rhs_ref[m:2 * m, cols] = zi_ref[j * m:(j + 1) * m, :].astype(_BF)
    ortho = 1.0 / math.sqrt(n * F_GROUP_DIM)
    y = jnp.dot(cs_seq_ref[...], rhs_ref[...], preferred_element_type=_F32)
    o_ref[0] = (y * ortho).reshape(m, 1 << FNET_LEVELS, F_W).reshape(n, F_W).astype(_BF)


def _fnet(p, cs_ch, cs_seq, twiddles):
    b, n, _ = p.shape
    m = n >> FNET_LEVELS
    wide = F_W << FNET_LEVELS
    return pl.pallas_call(
        functools.partial(_fnet_kernel, n=n),
        grid=(b,),
        in_specs=[pl.BlockSpec((1, n, F_W), lambda bi: (bi, 0, P_F // F_W)),
                  _const_spec((F_GROUP_DIM, 2 * F_GROUP_DIM)),
                  _const_spec((m, 2 * m))] + [_const_spec(t.shape) for t in twiddles],
        out_specs=pl.BlockSpec((1, n, F_W), lambda bi: (bi, 0, 0)),
        out_shape=jax.ShapeDtypeStruct((b, n, F_W), _BF),
        scratch_shapes=[pltpu.VMEM((n, F_W), _F32), pltpu.VMEM((n, F_W), _F32),
                        pltpu.VMEM((2 * m, wide), _BF)],
        compiler_params=_cparams("arbitrary"),
        name="fnet_dft",
    )(p, cs_ch, cs_seq, *twiddles)


def _pack_bf16_pair(hi, lo):
    hi_bits = pltpu.bitcast(hi.astype(_BF).astype(_F32), jnp.uint32)
    lo_bits = pltpu.bitcast(lo.astype(_BF).astype(_F32), jnp.uint32)
    return hi_bits | lax.shift_right_logical(lo_bits, jnp.uint32(16))


def _unpack_bf16_pair(packed):
    hi = pltpu.bitcast(packed & jnp.uint32(0xFFFF0000), _F32)
    lo = pltpu.bitcast(lax.shift_left(packed, jnp.uint32(16)), _F32)
    return hi, lo


def _route(logits, carry):
    rows = logits.shape[0]
    lane = lax.broadcasted_iota(jnp.int32, logits.shape, 1)
    neg = jnp.float32(-jnp.inf)
    big = jnp.int32(ROUTER_W)

    def first_argmax(v):
        m = jnp.max(v, axis=-1, keepdims=True)
        idx = jnp.min(jnp.where(v == m, lane, big), axis=-1, keepdims=True)
        return m, idx

    lg = jnp.where((lane >= GROUP_LANE0) & (lane < GROUP_LANE0 + N_GROUPS), logits, neg)
    mg, ig = first_argmax(lg)
    w_grp = 1.0 / jnp.sum(jnp.exp(lg - mg), axis=-1, keepdims=True)
    g_sel = ig - GROUP_LANE0
    le = jnp.where((lane < N_EXPERTS)
                   & (jnp.right_shift(lane, LOG2_EXPERTS_PER_GROUP) == g_sel), logits, neg)
    v1, i1 = first_argmax(le)
    le2 = jnp.where(lane == i1, neg, le)
    v2, i2 = first_argmax(le2)
    e2 = jnp.exp(v2 - v1)
    w1 = w_grp / (1.0 + e2)
    w2 = w_grp * e2 / (1.0 + e2)

    first_is_lower = i1 < i2
    gate_a = jnp.where(first_is_lower, w1, w2)
    gate_b = jnp.where(first_is_lower, w2, w1)
    la = jnp.minimum(i1, i2) - g_sel * EXPERTS_PER_GROUP
    lb = jnp.maximum(i1, i2) - g_sel * EXPERTS_PER_GROUP
    pair = jnp.right_shift(la * (2 * EXPERTS_PER_GROUP - 1 - la), 1) + lb - la - 1
    cls = g_sel * PAIRS_PER_GROUP + pair

    onehot = (lane == cls).astype(_F32)
    r_i = lax.broadcasted_iota(jnp.int32, (rows, rows), 0)
    c_i = lax.broadcasted_iota(jnp.int32, (rows, rows), 1)
    earlier = (c_i < r_i).astype(_BF)
    before = jnp.dot(earlier, onehot.astype(_BF), preferred_element_type=_F32) + carry
    rank = jnp.sum(onehot * before, axis=-1, keepdims=True)
    new_carry = carry + jnp.sum(onehot, axis=0, keepdims=True)

    route = (jnp.where(lane == 0, gate_a, 0.0) + jnp.where(lane == 1, gate_b, 0.0)
             + jnp.where(lane == 2, cls.astype(_F32), 0.0) + jnp.where(lane == 3, rank, 0.0))
    return route, new_carry


def _merge_kernel(hd_ref, fo_ref, ga_ref, gf_ref, x_ref, mod_ref, g2n_ref, wao_ref, wfo_ref,
                  wo_ref, wrhl_ref, br_ref, x1_ref, rows_ref, rec_ref, counts_ref, carry_ref):
    @pl.when((pl.program_id(0) == 0) & (pl.program_id(1) == 0))
    def _():
        carry_ref[...] = jnp.zeros_like(carry_ref)

    a = jnp.dot(hd_ref[0], wao_ref[...], preferred_element_type=_F32)
    ff = jnp.dot(fo_ref[0], wfo_ref[...], preferred_element_type=_F32)
    y = ga_ref[0].astype(_F32) * a + gf_ref[0].astype(_F32) * ff
    mix = jnp.dot(y.astype(_BF), wo_ref[...], preferred_element_type=_F32)
    x1 = x_ref[0] + mod_ref[0, 2:3, :] * mix
    x1_ref[0] = x1
    h2 = (_rms(x1) * g2n_ref[...]) * (1.0 + mod_ref[0, 4:5, :]) + mod_ref[0, 3:4, :]
    h2_hi = h2.astype(_BF)
    h2_lo = (h2 - h2_hi.astype(_F32)).astype(_BF)
    hi_both = jnp.dot(h2_hi, wrhl_ref[...], preferred_element_type=_F32)
    logits = (hi_both[:, :ROUTER_W] + hi_both[:, ROUTER_W:]
              + jnp.dot(h2_lo, wrhl_ref[:, :ROUTER_W], preferred_element_type=_F32)) + br_ref[...]
    route, new_carry = _route(logits, carry_ref[...])
    carry_ref[...] = new_carry
    counts_ref[...] = new_carry
    rec_ref[0, 0] = route.T[0:SUBLANES, :]
    half = D_MODEL // 2
    tm = h2.shape[0]
    packed = _pack_bf16_pair(h2[:, :half], h2[:, half:])
    rows_ref[0, :, 0:H2_SLAB_ROWS, :] = packed.reshape(tm, H2_SLAB_ROWS, LANES)
    rows_ref[0, :, ROUTE_SLAB_ROW:ROUTE_SLAB_ROW + 1, :] = (
        pltpu.bitcast(route, jnp.uint32).reshape(tm, 1, LANES))
    rows_ref[0, :, ROUTE_SLAB_ROW + 1:SUBLANES, :] = jnp.zeros(
        (tm, SUBLANES - ROUTE_SLAB_ROW - 1, LANES), jnp.uint32)


def _merge(heads, four, p, x, mod3, norm2_g, w_ao, w_fo, w_o, wr_hilo, b_r, tm):
    b, n, d = x.shape
    tok = lambda w: pl.BlockSpec((1, tm, w), lambda bi, i: (bi, i, 0))
    return pl.pallas_call(
        _merge_kernel,
        grid=(b, n // tm),
        in_specs=[tok(d), tok(F_W),
                  pl.BlockSpec((1, tm, d), lambda bi, i: (bi, i, P_GA // D_MODEL)),
                  pl.BlockSpec((1, tm, d), lambda bi, i: (bi, i, P_GF // D_MODEL)),
                  tok(d),
                  pl.BlockSpec((1, 6, d), lambda bi, i: (bi, 0, 0)),
                  pl.BlockSpec((1, d), lambda bi, i: (0, 0)),
                  _const_spec((d, d)), _const_spec((F_W, d)), _const_spec((d, d)),
                  _const_spec((d, 2 * ROUTER_W)),
                  pl.BlockSpec((1, ROUTER_W), lambda bi, i: (0, 0))],
        out_specs=[tok(d),
                   pl.BlockSpec((1, tm, SUBLANES, LANES), lambda bi, i: (bi, i, 0, 0)),
                   pl.BlockSpec((1, 1, SUBLANES, tm), lambda bi, i: (bi, i, 0, 0)),
                   pl.BlockSpec((1, ROUTER_W), lambda bi, i: (0, 0))],
        out_shape=[jax.ShapeDtypeStruct((b, n, d), _F32),
                   jax.ShapeDtypeStruct((b, n, SUBLANES, LANES), jnp.uint32),
                   jax.ShapeDtypeStruct((b, n // tm, SUBLANES, tm), _F32),
                   jax.ShapeDtypeStruct((1, ROUTER_W), _F32)],
        scratch_shapes=[pltpu.VMEM((1, ROUTER_W), _F32)],
        compiler_params=_cparams("arbitrary", "arbitrary"),
        name="merge_router",
    )(heads, four, p, p, x, mod3, norm2_g, w_ao, w_fo, w_o, wr_hilo, b_r)


def _issue_slab_copies(n, make_copy, slot_of):
    def group(g, c):
        r0 = g * DMA_ISSUE_GROUP
        slots = [slot_of(r0 + k) for k in range(DMA_ISSUE_GROUP)]
        for k in range(DMA_ISSUE_GROUP):
            make_copy(r0 + k, slots[k]).start(priority=k % 2)
        return c

    lax.fori_loop(0, n // DMA_ISSUE_GROUP, group, 0)


def _dispatch_kernel(pos_ref, pad_start_ref, pad_len_ref, src_ref, dst_ref, zero_ref, sem, pad_sem,
                     *, tm):
    base = pl.program_id(0) * tm

    @pl.when(pl.program_id(0) == 0)
    def _():
        zero_ref[...] = jnp.zeros_like(zero_ref)
        for c in range(N_CLASSES + 1):
            def start(r, carry, c=c):
                pltpu.make_async_copy(zero_ref, dst_ref.at[pad_start_ref[c] + r], pad_sem).start()
                return carry

            def drain(r, carry):
                pltpu.make_async_copy(zero_ref, dst_ref.at[0], pad_sem).wait()
                return carry

            lax.fori_loop(0, pad_len_ref[c], start, 0)
            lax.fori_loop(0, pad_len_ref[c], drain, 0)

    _issue_slab_copies(
        tm,
        lambda r, slot: pltpu.make_async_copy(src_ref.at[r], dst_ref.at[slot], sem),
        lambda r: pos_ref[base + r])
    pltpu.make_async_copy(src_ref, dst_ref.at[pl.ds(0, tm)], sem).wait()


def _dispatch(pos, pad_start, pad_len, slabs, n_sorted, tm):
    t = slabs.shape[0]
    return pl.pallas_call(
        functools.partial(_dispatch_kernel, tm=tm),
        grid_spec=pltpu.PrefetchScalarGridSpec(
            num_scalar_prefetch=3,
            grid=(t // tm,),
            in_specs=[pl.BlockSpec((tm, SUBLANES, LANES), lambda i, *_: (i, 0, 0))],
            out_specs=pl.BlockSpec(memory_space=pl.ANY),
            scratch_shapes=[pltpu.VMEM((SUBLANES, LANES), slabs.dtype),
                            pltpu.SemaphoreType.DMA(()), pltpu.SemaphoreType.DMA(())]),
        out_shape=jax.ShapeDtypeStruct((n_sorted,) + slabs.shape[1:], slabs.dtype),
        compiler_params=_cparams("arbitrary"),
        name="moe_dispatch",
    )(pos, pad_start, pad_len, slabs)


def _moe_kernel(ea_ref, eb_ref, valid_ref, slabs_ref, w1a_ref, w3a_ref, w2a_ref,
                w1b_ref, w3b_ref, w2b_ref, y_ref):
    del ea_ref, eb_ref
    i = pl.program_id(0)
    tm = slabs_ref.shape[0]

    @pl.when(valid_ref[i] != 0)
    def _():
        packed = slabs_ref[:, 0:H2_SLAB_ROWS, :].reshape(tm, H2_SLAB_ROWS * LANES)
        hi, lo = _unpack_bf16_pair(packed)
        t = jnp.concatenate([hi.astype(_BF), lo.astype(_BF)], axis=1)
        route = pltpu.bitcast(slabs_ref[:, ROUTE_SLAB_ROW, :], _F32)
        y = None
        for slot, (w1_ref, w3_ref, w2_ref) in enumerate(((w1a_ref, w3a_ref, w2a_ref),
                                                          (w1b_ref, w3b_ref, w2b_ref))):
            a = jnp.dot(t, w1_ref[0], preferred_element_type=_F32)
            u = jnp.dot(t, w3_ref[0], preferred_element_type=_F32)
            hid = ((a * _sigmoid(a)) * u * route[:, slot:slot + 1]).astype(_BF)
            part = jnp.dot(hid, w2_ref[0], preferred_element_type=_F32)
            y = part if y is None else y + part
        y_ref[...] = y.reshape(tm, SUBLANES, LANES)

    @pl.when(valid_ref[i] == 0)
    def _():
        y_ref[...] = jnp.zeros_like(y_ref)


def _moe(tile_ea, tile_eb, tile_valid, slabs_sorted, w1, w3, w2, tm):
    n_sorted = slabs_sorted.shape[0]
    d, hdn = w1.shape[1], w1.shape[2]
    assert d == SUBLANES * LANES
    pick = lambda which, a, b_: pl.BlockSpec(
        (1, a, b_), lambda i, ea, eb, valid: ((ea, eb)[which][i], 0, 0))
    slab_tile = pl.BlockSpec((tm, SUBLANES, LANES), lambda i, ea, eb, valid: (i, 0, 0))
    return pl.pallas_call(
        _moe_kernel,
        grid_spec=pltpu.PrefetchScalarGridSpec(
            num_scalar_prefetch=3,
            grid=(n_sorted // tm,),
            in_specs=[slab_tile,
                      pick(0, d, hdn), pick(0, d, hdn), pick(0, hdn, d),
                      pick(1, d, hdn), pick(1, d, hdn), pick(1, hdn, d)],
            out_specs=slab_tile),
        out_shape=jax.ShapeDtypeStruct((n_sorted, SUBLANES, LANES), _F32),
        compiler_params=_cparams("arbitrary"),
        name="moe_experts",
    )(tile_ea, tile_eb, tile_valid, slabs_sorted, w1, w3, w2, w1, w3, w2)


def _combine_kernel(pos_ref, y_hbm_ref, x1_ref, mod_ref, fg_ref, o_ref, ybuf_ref, sems, *, tm):
    i = pl.program_id(0)
    n_steps = pl.num_programs(0)

    def issue(tile, slot):
        _issue_slab_copies(
            tm,
            lambda r, src: pltpu.make_async_copy(y_hbm_ref.at[src], ybuf_ref.at[slot, r],
                                                 sems.at[slot]),
            lambda r: pos_ref[tile * tm + r])

    @pl.when(i == 0)
    def _():
        issue(0, 0)

    @pl.when(i + 1 < n_steps)
    def _():
        issue(i + 1, (i + 1) % 2)

    slot = i % 2
    pltpu.make_async_copy(y_hbm_ref.at[pl.ds(0, tm)], ybuf_ref.at[slot], sems.at[slot]).wait()
    y = ybuf_ref[slot].reshape(tm, SUBLANES * LANES)
    xo = x1_ref[...] + mod_ref[0, 5:6, :] * y
    o_ref[...] = _rms(xo) * fg_ref[...]


def _combine(pos, y_sorted, x1, mod3, final_g, seq, tm):
    t, d = x1.shape
    tiles_per_batch = seq // tm
    return pl.pallas_call(
        functools.partial(_combine_kernel, tm=tm),
        grid_spec=pltpu.PrefetchScalarGridSpec(
            num_scalar_prefetch=1,
            grid=(t // tm,),
            in_specs=[pl.BlockSpec(memory_space=pl.ANY),
                      pl.BlockSpec((tm, d), lambda i, pos: (i, 0)),
                      pl.BlockSpec((1, 6, d), lambda i, pos: (i // tiles_per_batch, 0, 0)),
                      pl.BlockSpec((1, d), lambda i, pos: (0, 0))],
            out_specs=pl.BlockSpec((tm, d), lambda i, pos: (i, 0)),
            scratch_shapes=[pltpu.VMEM((2, tm, SUBLANES, LANES), _F32),
                            pltpu.SemaphoreType.DMA((2,))]),
        out_shape=jax.ShapeDtypeStruct((t, d), _F32),
        compiler_params=_cparams("arbitrary"),
        name="moe_combine",
    )(pos, y_sorted, x1, mod3, final_g)


def _routing_tables(route_cls, route_rank, counts, n_tokens, tm):
    sizes = ((counts + (tm - 1)) // tm) * tm
    ends = jnp.cumsum(sizes)
    starts = ends - sizes
    n_tiles = n_tokens // tm + N_CLASSES
    tile_start = jnp.arange(n_tiles, dtype=jnp.int32) * tm
    pos = route_rank
    tile_cls = jnp.zeros((n_tiles,), jnp.int32)
    for c in range(N_CLASSES):
        pos = pos + jnp.where(route_cls == c, starts[c], 0)
        tile_cls = tile_cls + (tile_start >= ends[c]).astype(jnp.int32)
    tile_cls = jnp.minimum(tile_cls, N_CLASSES - 1)
    tile_valid = (tile_start < ends[-1]).astype(jnp.int32)
    pairs = [(a, b_) for a in range(EXPERTS_PER_GROUP) for b_ in range(a + 1, EXPERTS_PER_GROUP)]
    grp, pair = tile_cls // PAIRS_PER_GROUP, tile_cls % PAIRS_PER_GROUP
    tile_ea = grp * EXPERTS_PER_GROUP
    tile_eb = grp * EXPERTS_PER_GROUP
    for k, (a, b_) in enumerate(pairs):
        tile_ea = tile_ea + jnp.where(pair == k, a, 0)
        tile_eb = tile_eb + jnp.where(pair == k, b_, 0)
    pad_start = jnp.concatenate([starts + counts, ends[-1:]]).astype(jnp.int32)
    pad_len = jnp.concatenate([sizes - counts, n_tiles * tm - ends[-1:]]).astype(jnp.int32)
    return (pos.astype(jnp.int32), pad_start, pad_len, tile_ea.astype(jnp.int32),
            tile_eb.astype(jnp.int32), tile_valid, n_tiles)


def _rope_tables(n):
    inv = (1.0 / (ROPE_BASE ** (np.arange(ROT_FREQS, dtype=np.float32) / ROT_FREQS))).astype(np.float32)
    pos = np.arange(n)
    row = (pos // GRID_W).astype(np.float32)[:, None] * inv[None, :]
    col = (pos % GRID_W).astype(np.float32)[:, None] * inv[None, :]
    cos64 = np.concatenate([np.cos(row), np.cos(row), np.cos(col), np.cos(col)], axis=1)
    sin64 = np.concatenate([-np.sin(row), np.sin(row), -np.sin(col), np.sin(col)], axis=1)
    tile = lambda a: np.tile(a.astype(np.float32), (1, LANES // HEAD_DIM))
    return jnp.asarray(tile(cos64)), jnp.asarray(tile(sin64))


def _dft_cos_sin(n):
    k = np.arange(n, dtype=np.int64)
    ang = (2.0 * np.pi / n) * ((k[:, None] * k[None, :]) % n).astype(np.float64)
    return np.cos(ang), np.sin(ang)


def _dft_tables(n):
    c_ch, s_ch = _dft_cos_sin(F_GROUP_DIM)
    c_seq, s_seq = _dft_cos_sin(n >> FNET_LEVELS)
    cs_ch = np.concatenate([c_ch, -s_ch], axis=1).astype(np.float32)
    cs_seq = np.concatenate([c_seq, s_seq], axis=1).astype(np.float32)
    twiddles = []
    for level in range(FNET_LEVELS):
        m = n >> level
        ang = (2.0 * np.pi / m) * np.arange(m // 2, dtype=np.float64)[:, None]
        for tab in (np.cos(ang), np.sin(ang)):
            twiddles.append(jnp.asarray(np.broadcast_to(tab, (m // 2, F_W)).astype(np.float32)))
    return jnp.asarray(cs_ch.astype(_BF)), jnp.asarray(cs_seq.astype(_BF)), twiddles


def kernel(x, c, ctx, c_ctx, w_mod, b_mod, norm1_g, norm2_g, w_in, lam_q1, lam_k1, lam_q2, lam_k2,
           subln_g, w_attn_out, w_four_out, w_out, w_router_group, b_router_group, w_router_expert,
           b_router_expert, w_exp_gate, w_exp_up, w_exp_down, final_g):
    b, n, d = x.shape
    assert w_mod.shape[0] == 1, "depth-1 stack"
    assert b + 1 <= MOD_ROWS

    cc = jnp.concatenate([c, c_ctx[None, :], jnp.zeros((MOD_ROWS - b - 1, d), _F32)], axis=0)
    mod3 = _modulation(cc, w_mod[0], b_mod).reshape(MOD_ROWS, 6, d)

    lam = (jnp.exp(jnp.sum(lam_q1[0] * lam_k1[0])) - jnp.exp(jnp.sum(lam_q2[0] * lam_k2[0]))
           + LAM_INIT).reshape(1).astype(_F32)

    w = w_in[0]
    scale = HEAD_DIM ** -0.5 * math.log2(math.e)
    w_lat = jnp.concatenate([w[:, REF_Q:REF_K] * scale, w[:, REF_K:REF_F],
                             w[:, REF_GA:REF_END], w[:, REF_F:REF_GA]], axis=1).astype(_BF)
    w_ctx = w[:, REF_K:REF_F].astype(_BF)
    lat_kinds = ("rope",) * ((P_V - P_Q) // PROJ_CW) + ("plain",) * ((P_GA - P_V) // PROJ_CW) \
        + ("sigmoid",) * ((P_F - P_GA) // PROJ_CW) + ("plain",) * ((P_W - P_F) // PROJ_CW)
    ctx_kinds = ("plain",) * (w_ctx.shape[1] // PROJ_CW)

    cos_t, sin_t = _rope_tables(n)
    p = _in_projection(x, mod3, lambda bi: bi, norm1_g, w_lat, cos_t, sin_t, lat_kinds, PROJ_TM)
    kvc = _in_projection(ctx, mod3, lambda bi: b, norm1_g, w_ctx, cos_t, sin_t, ctx_kinds,
                         ctx.shape[1])

    heads = _attention(lam, p, kvc, subln_g, ATTN_TQ)
    cs_ch, cs_seq, twiddles = _dft_tables(n)
    four = _fnet(p, cs_ch, cs_seq, twiddles)

    w_r = jnp.concatenate([w_router_expert[0], w_router_group[0],
                           jnp.zeros((d, ROUTER_W - N_EXPERTS - N_GROUPS), _F32)], axis=1)
    b_r = jnp.concatenate([b_router_expert[0], b_router_group[0],
                           jnp.zeros((ROUTER_W - N_EXPERTS - N_GROUPS,), _F32)])[None, :]
    wr_hi = w_r.astype(_BF)
    wr_lo = (w_r - wr_hi.astype(_F32)).astype(_BF)
    x1, rows, rec, counts = _merge(heads, four, p, x, mod3, norm2_g, w_attn_out[0].astype(_BF),
                                   w_four_out[0].astype(_BF), w_out[0].astype(_BF),
                                   jnp.concatenate([wr_hi, wr_lo], axis=1), b_r, MERGE_TM)

    t = b * n
    slabs = rows.reshape(t, SUBLANES, LANES)
    rec = rec.reshape(t // MERGE_TM, SUBLANES, MERGE_TM)
    pos, pad_start, pad_len, tile_ea, tile_eb, tile_valid, n_tiles = _routing_tables(
        rec[:, 2, :].reshape(t).astype(jnp.int32), rec[:, 3, :].reshape(t).astype(jnp.int32),
        counts[0, :N_CLASSES].astype(jnp.int32), t, MOE_TM)

    slabs_sorted = _dispatch(pos, pad_start, pad_len, slabs, n_tiles * MOE_TM, DISPATCH_TM)
    y_sorted = _moe(tile_ea, tile_eb, tile_valid, slabs_sorted, w_exp_gate[0].astype(_BF),
                    w_exp_up[0].astype(_BF), w_exp_down[0].astype(_BF), MOE_TM)
    out = _combine(pos, y_sorted, x1.reshape(t, d), mod3, final_g[None, :], n, COMBINE_TM)
    return out.reshape(b, n, d)
```

```python
import functools
import math

import jax
import jax.numpy as jnp
import numpy as np
from jax import lax
from jax.experimental import pallas as pl
from jax.experimental.pallas import tpu as pltpu

D_MODEL = 1024
GRID_W = 64
EPS = 1e-6
N_HEADS = 8
HEAD_DIM = 64
HEAD_W = 2 * HEAD_DIM
ROT_FREQS = HEAD_DIM // 4
ROPE_BASE = 10000.0
F_GROUPS = 4
F_GROUP_DIM = 128
F_W = F_GROUPS * F_GROUP_DIM
N_GROUPS = 4
EXPERTS_PER_GROUP = 4
LOG2_EXPERTS_PER_GROUP = 2
N_EXPERTS = N_GROUPS * EXPERTS_PER_GROUP
PAIRS_PER_GROUP = EXPERTS_PER_GROUP * (EXPERTS_PER_GROUP - 1) // 2
N_CLASSES = N_GROUPS * PAIRS_PER_GROUP
EXPERT_HIDDEN = 512
LAM_INIT = 0.8 - 0.6 * math.exp(-0.3 * 0)

REF_Q, REF_K, REF_V, REF_F, REF_GA, REF_GF, REF_END = 0, 1024, 2048, 3072, 3584, 4608, 5632
P_Q, P_K, P_GA, P_GF, P_F, P_W = 0, 1024, 2048, 3072, 4096, 4608

LANES = 128
SUBLANES = 8
VMEM_LIMIT_BYTES = 56 * 1024 * 1024

PROJ_TM = 1024
PROJ_CW = 512
ATTN_TQ = 256
ATTN_SUM_ROWS = 16
ATTN_SCORE_BUFS = 3
ATTN_HEADS_PER_STEP = 2
FNET_LEVELS = 2
FNET_CHUNK = 256
MERGE_TM = 512
MOE_TM = 256
COMBINE_TM = 512
DISPATCH_TM = 2048
DMA_ISSUE_GROUP = 8
MOD_ROWS = 40
ROUTER_W = LANES
GROUP_LANE0 = N_EXPERTS
H2_SLAB_ROWS = D_MODEL // 2 // LANES
ROUTE_SLAB_ROW = H2_SLAB_ROWS

_BF = jnp.bfloat16
_F32 = jnp.float32


def _cparams(*sem):
    return pltpu.CompilerParams(dimension_semantics=sem, vmem_limit_bytes=VMEM_LIMIT_BYTES)


def _const_spec(shape):
    return pl.BlockSpec(shape, lambda *_: (0,) * len(shape), pipeline_mode=pl.Buffered(1))


def _rms(x):
    return x * lax.rsqrt(jnp.mean(x * x, axis=-1, keepdims=True) + EPS)


def _sigmoid(x):
    return 1.0 / (1.0 + jnp.exp(-x))


def _mod_kernel(cc_ref, w_ref, b_ref, o_ref):
    cc = cc_ref[...]
    s = cc * _sigmoid(cc)
    o_ref[...] = jnp.dot(s, w_ref[...], preferred_element_type=_F32,
                         precision=lax.Precision.HIGHEST) + b_ref[...]


def _modulation(cc, w_mod, b_mod):
    n = w_mod.shape[1]
    bn = D_MODEL
    return pl.pallas_call(
        _mod_kernel,
        grid=(n // bn,),
        in_specs=[pl.BlockSpec((MOD_ROWS, D_MODEL), lambda j: (0, 0)),
                  pl.BlockSpec((D_MODEL, bn), lambda j: (0, j)),
                  pl.BlockSpec((1, bn), lambda j: (0, j))],
        out_specs=pl.BlockSpec((MOD_ROWS, bn), lambda j: (0, j)),
        out_shape=jax.ShapeDtypeStruct((MOD_ROWS, n), _F32),
        compiler_params=_cparams("arbitrary"),
        name="modulation",
    )(cc, w_mod, b_mod)


def _rope(acc, cos_ref, sin_ref):
    cos = cos_ref[...]
    sin = sin_ref[...]
    lane = lax.broadcasted_iota(jnp.int32, (1, LANES), 1)
    first_half = (lane % (2 * ROT_FREQS)) < ROT_FREQS
    outs = []
    for s in range(acc.shape[1] // LANES):
        xs = acc[:, s * LANES:(s + 1) * LANES]
        partner = jnp.where(first_half,
                            pltpu.roll(xs, LANES - ROT_FREQS, 1),
                            pltpu.roll(xs, ROT_FREQS, 1))
        outs.append(xs * cos + partner * sin)
    return jnp.concatenate(outs, axis=1)


def _inproj_kernel(x_ref, mod_ref, g_ref, w_ref, wvt_ref, cos_ref, sin_ref, o_ref, vt_ref, *,
                   chunk_kinds):
    x = x_ref[0]
    shift = mod_ref[0, 0:1, :]
    scale = mod_ref[0, 1:2, :]
    h = (_rms(x) * g_ref[...]) * (1.0 + scale) + shift
    hb = h.astype(_BF)
    for j, kind in enumerate(chunk_kinds):
        cols = slice(j * PROJ_CW, (j + 1) * PROJ_CW)
        acc = jnp.dot(hb, w_ref[:, cols], preferred_element_type=_F32)
        if kind == "rope":
            acc = _rope(acc, cos_ref, sin_ref)
        elif kind == "sigmoid":
            acc = _sigmoid(acc)
        o_ref[0, :, cols] = acc.astype(_BF)
    nt = (((1,), (1,)), ((), ()))
    vt_ref[0] = lax.dot_general(wvt_ref[...], hb, nt, preferred_element_type=_F32).astype(_BF)


def _in_projection(x, mod3, mod_row_of_batch, g, w, wvt, cos_t, sin_t, chunk_kinds, tm):
    b, n, d = x.shape
    width = w.shape[1]
    vw = wvt.shape[0]
    assert width == len(chunk_kinds) * PROJ_CW and n % tm == 0
    return pl.pallas_call(
        functools.partial(_inproj_kernel, chunk_kinds=chunk_kinds),
        grid=(b, n // tm),
        in_specs=[pl.BlockSpec((1, tm, d), lambda bi, i: (bi, i, 0)),
                  pl.BlockSpec((1, 6, d), lambda bi, i: (mod_row_of_batch(bi), 0, 0)),
                  pl.BlockSpec((1, d), lambda bi, i: (0, 0)),
                  _const_spec((d, width)), _const_spec((vw, d)),
                  pl.BlockSpec((tm, LANES), lambda bi, i: (i, 0)),
                  pl.BlockSpec((tm, LANES), lambda bi, i: (i, 0))],
        out_specs=[pl.BlockSpec((1, tm, width), lambda bi, i: (bi, i, 0)),
                   pl.BlockSpec((1, vw, tm), lambda bi, i: (bi, 0, i))],
        out_shape=[jax.ShapeDtypeStruct((b, n, width), _BF),
                   jax.ShapeDtypeStruct((b, vw, n), _BF)],
        compiler_params=_cparams("parallel", "arbitrary"),
        name="in_projection",
    )(x, mod3, g, w, wvt, cos_t, sin_t)


def _attn_kernel(lam_ref, q_ref, kl_ref, vl_ref, kc_ref, vc_ref, sg_ref, o_ref,
                 kcat_ref, vt_ref, *bufs, tq):
    nc, n = kc_ref.shape[1], kl_ref.shape[1]
    for hh in range(ATTN_HEADS_PER_STEP):
        cols = slice(hh * HEAD_W, (hh + 1) * HEAD_W)
        kcat_ref[hh, 0:nc, :] = kc_ref[0, :, cols]
        kcat_ref[hh, nc:nc + n, :] = kl_ref[0, :, cols]
        vt_ref[hh, 0:HEAD_W, 0:nc] = vc_ref[0, cols, :]
        vt_ref[hh, 0:HEAD_W, nc:nc + n] = vl_ref[0, cols, :]
        vt_ref[hh, HEAD_W:, :] = jnp.ones((ATTN_SUM_ROWS, nc + n), _BF)
    lane = lax.broadcasted_iota(jnp.int32, (1, HEAD_W), 1)
    map_lanes = (lane < HEAD_DIM, lane >= HEAD_DIM)
    nt = (((1,), (1,)), ((), ()))
    lam = lam_ref[0]
    post_scale = sg_ref[...] * (1.0 - LAM_INIT)
    n_sub = n // tq

    def scores(g):
        hh, j = divmod(g, n_sub)
        q = q_ref[0, j * tq:(j + 1) * tq, hh * HEAD_W:(hh + 1) * HEAD_W]
        for mp in range(2):
            qm = jnp.where(map_lanes[mp], q, jnp.zeros_like(q))
            bufs[g % ATTN_SCORE_BUFS][mp] = lax.dot_general(kcat_ref[hh], qm, nt,
                                                            preferred_element_type=_F32)

    def finish(g):
        hh, j = divmod(g, n_sub)
        outs = []
        for mp in range(2):
            s = bufs[g % ATTN_SCORE_BUFS][mp]
            e = jnp.exp2(s - jnp.max(s, axis=0, keepdims=True)).astype(_BF)
            r = jnp.dot(vt_ref[hh], e, preferred_element_type=_F32)
            outs.append(r[:HEAD_W, :] / r[HEAD_W:HEAD_W + 1, :])
        heads = (outs[0] - lam * outs[1]).T
        o_ref[0, j * tq:(j + 1) * tq, hh * HEAD_W:(hh + 1) * HEAD_W] = (
            _rms(heads) * post_scale).astype(_BF)

    total = ATTN_HEADS_PER_STEP * n_sub
    for g in range(ATTN_SCORE_BUFS - 1):
        scores(g)
    for g in range(total):
        if g + ATTN_SCORE_BUFS - 1 < total:
            scores(g + ATTN_SCORE_BUFS - 1)
        finish(g)


def _attention(lam, p, vt, kc, vct, subln_g, tq):
    b, n, _ = p.shape
    nc = kc.shape[1]
    w = ATTN_HEADS_PER_STEP * HEAD_W
    qb, kb = P_Q // w, P_K // w
    seq = lambda blk: pl.BlockSpec((1, n, w), lambda bi, h: (bi, 0, blk + h))
    return pl.pallas_call(
        functools.partial(_attn_kernel, tq=tq),
        grid=(b, N_HEADS // ATTN_HEADS_PER_STEP),
        in_specs=[pl.BlockSpec(memory_space=pltpu.SMEM),
                  seq(qb), seq(kb),
                  pl.BlockSpec((1, w, n), lambda bi, h: (bi, h, 0)),
                  pl.BlockSpec((1, nc, w), lambda bi, h: (bi, 0, h)),
                  pl.BlockSpec((1, w, nc), lambda bi, h: (bi, h, 0)),
                  pl.BlockSpec((1, HEAD_W), lambda bi, h: (0, 0))],
        out_specs=seq(0),
        out_shape=jax.ShapeDtypeStruct((b, n, N_HEADS * HEAD_W), _BF),
        scratch_shapes=[pltpu.VMEM((ATTN_HEADS_PER_STEP, nc + n, HEAD_W), _BF),
                        pltpu.VMEM((ATTN_HEADS_PER_STEP, HEAD_W + ATTN_SUM_ROWS, nc + n), _BF),
                        ] + [pltpu.VMEM((2, nc + n, tq), _F32)] * ATTN_SCORE_BUFS,
        compiler_params=_cparams("parallel", "arbitrary"),
        name="diff_attention",
    )(lam, p, p, vt, kc, vct, subln_g)


def _bit_reverse(j, bits):
    return int(format(j, "0%db" % bits)[::-1], 2) if bits else 0


def _fnet_kernel(f_ref, cs_ch_ref, cs_seq_ref, *rest, n):
    tw_refs = rest[:2 * FNET_LEVELS]
    o_ref, zr_ref, zi_ref, rhs_ref = rest[2 * FNET_LEVELS:]
    for g in range(F_GROUPS):
        cols = slice(g * F_GROUP_DIM, (g + 1) * F_GROUP_DIM)
        t = jnp.dot(f_ref[0, :, cols], cs_ch_ref[...], preferred_element_type=_F32)
        zr_ref[:, cols] = t[:, :F_GROUP_DIM]
        zi_ref[:, cols] = t[:, F_GROUP_DIM:]
    for level in range(FNET_LEVELS):
        m = n >> level
        half = m // 2
        cos_ref, sin_ref = tw_refs[2 * level], tw_refs[2 * level + 1]
        for seg in range(1 << level):
            for c0 in range(0, half, FNET_CHUNK):
                rows_t = slice(seg * m + c0, seg * m + c0 + FNET_CHUNK)
                rows_b = slice(seg * m + half + c0, seg * m + half + c0 + FNET_CHUNK)
                tr, ti = zr_ref[rows_t, :], zi_ref[rows_t, :]
                br, bi = zr_ref[rows_b, :], zi_ref[rows_b, :]
                cw, sw = cos_ref[c0:c0 + FNET_CHUNK, :], sin_ref[c0:c0 + FNET_CHUNK, :]
                dr, di = tr - br, ti - bi
                zr_ref[rows_t, :] = tr + br
                zi_ref[rows_t, :] = ti + bi
                zr_ref[rows_b, :] = dr * cw + di * sw
                zi_ref[rows_b, :] = di * cw - dr * sw
    m = n >> FNET_LEVELS
    for j in range(1 << FNET_LEVELS):
        cols = slice(_bit_reverse(j, FNET_LEVELS) * F_W, (_bit_reverse(j, FNET_LEVELS) + 1) * F_W)
        rhs_ref[0:m, cols] = zr_ref[j * m:(j + 1) * m, :].astype(_BF)
        rhs_ref[m:2 * m, cols] = zi_ref[j * m:(j + 1) * m, :].astype(_BF)
    ortho = 1.0 / math.sqrt(n * F_GROUP_DIM)
    y = jnp.dot(cs_seq_ref[...], rhs_ref[...], preferred_element_type=_F32)
    o_ref[0] = (y * ortho).reshape(m, 1 << FNET_LEVELS, F_W).reshape(n, F_W).astype(_BF)


def _fnet(p, cs_ch, cs_seq, twiddles):
    b, n, _ = p.shape
    m = n >> FNET_LEVELS
    wide = F_W << FNET_LEVELS
    return pl.pallas_call(
        functools.partial(_fnet_kernel, n=n),
        grid=(b,),
        in_specs=[pl.BlockSpec((1, n, F_W), lambda bi: (bi, 0, P_F // F_W)),
                  _const_spec((F_GROUP_DIM, 2 * F_GROUP_DIM)),
                  _const_spec((m, 2 * m))] + [_const_spec(t.shape) for t in twiddles],
        out_specs=pl.BlockSpec((1, n, F_W), lambda bi: (bi, 0, 0)),
        out_shape=jax.ShapeDtypeStruct((b, n, F_W), _BF),
        scratch_shapes=[pltpu.VMEM((n, F_W), _F32), pltpu.VMEM((n, F_W), _F32),
                        pltpu.VMEM((2 * m, wide), _BF)],
        compiler_params=_cparams("arbitrary"),
        name="fnet_dft",
    )(p, cs_ch, cs_seq, *twiddles)


def _pack_bf16_pair(hi, lo):
    hi_bits = pltpu.bitcast(hi.astype(_BF).astype(_F32), jnp.uint32)
    lo_bits = pltpu.bitcast(lo.astype(_BF).astype(_F32), jnp.uint32)
    return hi_bits | lax.shift_right_logical(lo_bits, jnp.uint32(16))


def _unpack_bf16_pair(packed):
    hi = pltpu.bitcast(packed & jnp.uint32(0xFFFF0000), _F32)
    lo = pltpu.bitcast(lax.shift_left(packed, jnp.uint32(16)), _F32)
    return hi, lo


def _route(logits, carry):
    rows = logits.shape[0]
    lane = lax.broadcasted_iota(jnp.int32, logits.shape, 1)
    neg = jnp.float32(-jnp.inf)
    big = jnp.int32(ROUTER_W)

    def first_argmax(v):
        m = jnp.max(v, axis=-1, keepdims=True)
        idx = jnp.min(jnp.where(v == m, lane, big), axis=-1, keepdims=True)
        return m, idx

    lg = jnp.where((lane >= GROUP_LANE0) & (lane < GROUP_LANE0 + N_GROUPS), logits, neg)
    mg, ig = first_argmax(lg)
    w_grp = 1.0 / jnp.sum(jnp.exp(lg - mg), axis=-1, keepdims=True)
    g_sel = ig - GROUP_LANE0
    le = jnp.where((lane < N_EXPERTS)
                   & (jnp.right_shift(lane, LOG2_EXPERTS_PER_GROUP) == g_sel), logits, neg)
    v1, i1 = first_argmax(le)
    le2 = jnp.where(lane == i1, neg, le)
    v2, i2 = first_argmax(le2)
    e2 = jnp.exp(v2 - v1)
    w1 = w_grp / (1.0 + e2)
    w2 = w_grp * e2 / (1.0 + e2)

    first_is_lower = i1 < i2
    gate_a = jnp.where(first_is_lower, w1, w2)
    gate_b = jnp.where(first_is_lower, w2, w1)
    la = jnp.minimum(i1, i2) - g_sel * EXPERTS_PER_GROUP
    lb = jnp.maximum(i1, i2) - g_sel * EXPERTS_PER_GROUP
    pair = jnp.right_shift(la * (2 * EXPERTS_PER_GROUP - 1 - la), 1) + lb - la - 1
    cls = g_sel * PAIRS_PER_GROUP + pair

    onehot = (lane == cls).astype(_F32)
    r_i = lax.broadcasted_iota(jnp.int32, (rows, rows), 0)
    c_i = lax.broadcasted_iota(jnp.int32, (rows, rows), 1)
    earlier = (c_i < r_i).astype(_BF)
    before = jnp.dot(earlier, onehot.astype(_BF), preferred_element_type=_F32) + carry
    rank = jnp.sum(onehot * before, axis=-1, keepdims=True)
    new_carry = carry + jnp.sum(onehot, axis=0, keepdims=True)

    route = (jnp.where(lane == 0, gate_a, 0.0) + jnp.where(lane == 1, gate_b, 0.0)
             + jnp.where(lane == 2, cls.astype(_F32), 0.0) + jnp.where(lane == 3, rank, 0.0))
    return route, new_carry


def _merge_kernel(hd_ref, fo_ref, ga_ref, gf_ref, x_ref, mod_ref, g2n_ref, wao_ref, wfo_ref,
                  wo_ref, wrhl_ref, br_ref, x1_ref, rows_ref, rec_ref, counts_ref, carry_ref):
    @pl.when((pl.program_id(0) == 0) & (pl.program_id(1) == 0))
    def _():
        carry_ref[...] = jnp.zeros_like(carry_ref)

    a = jnp.dot(hd_ref[0], wao_ref[...], preferred_element_type=_F32)
    ff = jnp.dot(fo_ref[0], wfo_ref[...], preferred_element_type=_F32)
    y = ga_ref[0].astype(_F32) * a + gf_ref[0].astype(_F32) * ff
    mix = jnp.dot(y.astype(_BF), wo_ref[...], preferred_element_type=_F32)
    x1 = x_ref[0] + mod_ref[0, 2:3, :] * mix
    x1_ref[0] = x1
    h2 = (_rms(x1) * g2n_ref[...]) * (1.0 + mod_ref[0, 4:5, :]) + mod_ref[0, 3:4, :]
    h2_hi = h2.astype(_BF)
    h2_lo = (h2 - h2_hi.astype(_F32)).astype(_BF)
    hi_both = jnp.dot(h2_hi, wrhl_ref[...], preferred_element_type=_F32)
    logits = (hi_both[:, :ROUTER_W] + hi_both[:, ROUTER_W:]
              + jnp.dot(h2_lo, wrhl_ref[:, :ROUTER_W], preferred_element_type=_F32)) + br_ref[...]
    route, new_carry = _route(logits, carry_ref[...])
    carry_ref[...] = new_carry
    counts_ref[...] = new_carry
    rec_ref[0, 0] = route.T[0:SUBLANES, :]
    half = D_MODEL // 2
    tm = h2.shape[0]
    packed = _pack_bf16_pair(h2[:, :half], h2[:, half:])
    rows_ref[0, :, 0:H2_SLAB_ROWS, :] = packed.reshape(tm, H2_SLAB_ROWS, LANES)
    rows_ref[0, :, ROUTE_SLAB_ROW:ROUTE_SLAB_ROW + 1, :] = (
        pltpu.bitcast(route, jnp.uint32).reshape(tm, 1, LANES))
    rows_ref[0, :, ROUTE_SLAB_ROW + 1:SUBLANES, :] = jnp.zeros(
        (tm, SUBLANES - ROUTE_SLAB_ROW - 1, LANES), jnp.uint32)


def _merge(heads, four, p, x, mod3, norm2_g, w_ao, w_fo, w_o, wr_hilo, b_r, tm):
    b, n, d = x.shape
    tok = lambda w: pl.BlockSpec((1, tm, w), lambda bi, i: (bi, i, 0))
    return pl.pallas_call(
        _merge_kernel,
        grid=(b, n // tm),
        in_specs=[tok(d), tok(F_W),
                  pl.BlockSpec((1, tm, d), lambda bi, i: (bi, i, P_GA // D_MODEL)),
                  pl.BlockSpec((1, tm, d), lambda bi, i: (bi, i, P_GF // D_MODEL)),
                  tok(d),
                  pl.BlockSpec((1, 6, d), lambda bi, i: (bi, 0, 0)),
                  pl.BlockSpec((1, d), lambda bi, i: (0, 0)),
                  _const_spec((d, d)), _const_spec((F_W, d)), _const_spec((d, d)),
                  _const_spec((d, 2 * ROUTER_W)),
                  pl.BlockSpec((1, ROUTER_W), lambda bi, i: (0, 0))],
        out_specs=[tok(d),
                   pl.BlockSpec((1, tm, SUBLANES, LANES), lambda bi, i: (bi, i, 0, 0)),
                   pl.BlockSpec((1, 1, SUBLANES, tm), lambda bi, i: (bi, i, 0, 0)),
                   pl.BlockSpec((1, ROUTER_W), lambda bi, i: (0, 0))],
        out_shape=[jax.ShapeDtypeStruct((b, n, d), _F32),
                   jax.ShapeDtypeStruct((b, n, SUBLANES, LANES), jnp.uint32),
                   jax.ShapeDtypeStruct((b, n // tm, SUBLANES, tm), _F32),
                   jax.ShapeDtypeStruct((1, ROUTER_W), _F32)],
        scratch_shapes=[pltpu.VMEM((1, ROUTER_W), _F32)],
        compiler_params=_cparams("arbitrary", "arbitrary"),
        name="merge_router",
    )(heads, four, p, p, x, mod3, norm2_g, w_ao, w_fo, w_o, wr_hilo, b_r)


def _issue_slab_copies(n, make_copy, slot_of):
    def group(g, c):
        r0 = g * DMA_ISSUE_GROUP
        slots = [slot_of(r0 + k) for k in range(DMA_ISSUE_GROUP)]
        for k in range(DMA_ISSUE_GROUP):
            make_copy(r0 + k, slots[k]).start(priority=k % 2)
        return c

    lax.fori_loop(0, n // DMA_ISSUE_GROUP, group, 0)


def _dispatch_kernel(pos_ref, pad_start_ref, pad_len_ref, src_ref, dst_ref, zero_ref, sem, pad_sem,
                     *, tm):
    base = pl.program_id(0) * tm

    @pl.when(pl.program_id(0) == 0)
    def _():
        zero_ref[...] = jnp.zeros_like(zero_ref)
        for c in range(N_CLASSES + 1):
            def start(r, carry, c=c):
                pltpu.make_async_copy(zero_ref, dst_ref.at[pad_start_ref[c] + r], pad_sem).start()
                return carry

            def drain(r, carry):
                pltpu.make_async_copy(zero_ref, dst_ref.at[0], pad_sem).wait()
                return carry

            lax.fori_loop(0, pad_len_ref[c], start, 0)
            lax.fori_loop(0, pad_len_ref[c], drain, 0)

    _issue_slab_copies(
        tm,
        lambda r, slot: pltpu.make_async_copy(src_ref.at[r], dst_ref.at[slot], sem),
        lambda r: pos_ref[base + r])
    pltpu.make_async_copy(src_ref, dst_ref.at[pl.ds(0, tm)], sem).wait()


def _dispatch(pos, pad_start, pad_len, slabs, n_sorted, tm):
    t = slabs.shape[0]
    return pl.pallas_call(
        functools.partial(_dispatch_kernel, tm=tm),
        grid_spec=pltpu.PrefetchScalarGridSpec(
            num_scalar_prefetch=3,
            grid=(t // tm,),
            in_specs=[pl.BlockSpec((tm, SUBLANES, LANES), lambda i, *_: (i, 0, 0))],
            out_specs=pl.BlockSpec(memory_space=pl.ANY),
            scratch_shapes=[pltpu.VMEM((SUBLANES, LANES), slabs.dtype),
                            pltpu.SemaphoreType.DMA(()), pltpu.SemaphoreType.DMA(())]),
        out_shape=jax.ShapeDtypeStruct((n_sorted,) + slabs.shape[1:], slabs.dtype),
        compiler_params=_cparams("arbitrary"),
        name="moe_dispatch",
    )(pos, pad_start, pad_len, slabs)


def _moe_kernel(ea_ref, eb_ref, valid_ref, slabs_ref, w1a_ref, w3a_ref, w2a_ref,
                w1b_ref, w3b_ref, w2b_ref, y_ref):
    del ea_ref, eb_ref
    i = pl.program_id(0)
    tm = slabs_ref.shape[0]

    @pl.when(valid_ref[i] != 0)
    def _():
        packed = slabs_ref[:, 0:H2_SLAB_ROWS, :].reshape(tm, H2_SLAB_ROWS * LANES)
        hi, lo = _unpack_bf16_pair(packed)
        t = jnp.concatenate([hi.astype(_BF), lo.astype(_BF)], axis=1)
        route = pltpu.bitcast(slabs_ref[:, ROUTE_SLAB_ROW, :], _F32)
        y = None
        for slot, (w1_ref, w3_ref, w2_ref) in enumerate(((w1a_ref, w3a_ref, w2a_ref),
                                                          (w1b_ref, w3b_ref, w2b_ref))):
            a = jnp.dot(t, w1_ref[0], preferred_element_type=_F32)
            u = jnp.dot(t, w3_ref[0], preferred_element_type=_F32)
            hid = ((a * _sigmoid(a)) * u * route[:, slot:slot + 1]).astype(_BF)
            part = jnp.dot(hid, w2_ref[0], preferred_element_type=_F32)
            y = part if y is None else y + part
        y_ref[...] = y.reshape(tm, SUBLANES, LANES)

    @pl.when(valid_ref[i] == 0)
    def _():
        y_ref[...] = jnp.zeros_like(y_ref)


def _moe(tile_ea, tile_eb, tile_valid, slabs_sorted, w1, w3, w2, tm):
    n_sorted = slabs_sorted.shape[0]
    d, hdn = w1.shape[1], w1.shape[2]
    assert d == SUBLANES * LANES
    pick = lambda which, a, b_: pl.BlockSpec(
        (1, a, b_), lambda i, ea, eb, valid: ((ea, eb)[which][i], 0, 0))
    slab_tile = pl.BlockSpec((tm, SUBLANES, LANES), lambda i, ea, eb, valid: (i, 0, 0))
    return pl.pallas_call(
        _moe_kernel,
        grid_spec=pltpu.PrefetchScalarGridSpec(
            num_scalar_prefetch=3,
            grid=(n_sorted // tm,),
            in_specs=[slab_tile,
                      pick(0, d, hdn), pick(0, d, hdn), pick(0, hdn, d),
                      pick(1, d, hdn), pick(1, d, hdn), pick(1, hdn, d)],
            out_specs=slab_tile),
        out_shape=jax.ShapeDtypeStruct((n_sorted, SUBLANES, LANES), _F32),
        compiler_params=_cparams("arbitrary"),
        name="moe_experts",
    )(tile_ea, tile_eb, tile_valid, slabs_sorted, w1, w3, w2, w1, w3, w2)


def _combine_kernel(pos_ref, y_hbm_ref, x1_ref, mod_ref, fg_ref, o_ref, ybuf_ref, sems, *, tm):
    i = pl.program_id(0)
    n_steps = pl.num_programs(0)

    def issue(tile, slot):
        _issue_slab_copies(
            tm,
            lambda r, src: pltpu.make_async_copy(y_hbm_ref.at[src], ybuf_ref.at[slot, r],
                                                 sems.at[slot]),
            lambda r: pos_ref[tile * tm + r])

    @pl.when(i == 0)
    def _():
        issue(0, 0)

    @pl.when(i + 1 < n_steps)
    def _():
        issue(i + 1, (i + 1) % 2)

    slot = i % 2
    pltpu.make_async_copy(y_hbm_ref.at[pl.ds(0, tm)], ybuf_ref.at[slot], sems.at[slot]).wait()
    y = ybuf_ref[slot].reshape(tm, SUBLANES * LANES)
    xo = x1_ref[...] + mod_ref[0, 5:6, :] * y
    o_ref[...] = _rms(xo) * fg_ref[...]


def _combine(pos, y_sorted, x1, mod3, final_g, seq, tm):
    t, d = x1.shape
    tiles_per_batch = seq // tm
    return pl.pallas_call(
        functools.partial(_combine_kernel, tm=tm),
        grid_spec=pltpu.PrefetchScalarGridSpec(
            num_scalar_prefetch=1,
            grid=(t // tm,),
            in_specs=[pl.BlockSpec(memory_space=pl.ANY),
                      pl.BlockSpec((tm, d), lambda i, pos: (i, 0)),
                      pl.BlockSpec((1, 6, d), lambda i, pos: (i // tiles_per_batch, 0, 0)),
                      pl.BlockSpec((1, d), lambda i, pos: (0, 0))],
            out_specs=pl.BlockSpec((tm, d), lambda i, pos: (i, 0)),
            scratch_shapes=[pltpu.VMEM((2, tm, SUBLANES, LANES), _F32),
                            pltpu.SemaphoreType.DMA((2,))]),
        out_shape=jax.ShapeDtypeStruct((t, d), _F32),
        compiler_params=_cparams("arbitrary"),
        name="moe_combine",
    )(pos, y_sorted, x1, mod3, final_g)


def _routing_tables(route_cls, route_rank, counts, n_tokens, tm):
    sizes = ((counts + (tm - 1)) // tm) * tm
    ends = jnp.cumsum(sizes)
    starts = ends - sizes
    n_tiles = n_tokens // tm + N_CLASSES
    tile_start = jnp.arange(n_tiles, dtype=jnp.int32) * tm
    pos = route_rank
    tile_cls = jnp.zeros((n_tiles,), jnp.int32)
    for c in range(N_CLASSES):
        pos = pos + jnp.where(route_cls == c, starts[c], 0)
        tile_cls = tile_cls + (tile_start >= ends[c]).astype(jnp.int32)
    tile_cls = jnp.minimum(tile_cls, N_CLASSES - 1)
    tile_valid = (tile_start < ends[-1]).astype(jnp.int32)
    pairs = [(a, b_) for a in range(EXPERTS_PER_GROUP) for b_ in range(a + 1, EXPERTS_PER_GROUP)]
    grp, pair = tile_cls // PAIRS_PER_GROUP, tile_cls % PAIRS_PER_GROUP
    tile_ea = grp * EXPERTS_PER_GROUP
    tile_eb = grp * EXPERTS_PER_GROUP
    for k, (a, b_) in enumerate(pairs):
        tile_ea = tile_ea + jnp.where(pair == k, a, 0)
        tile_eb = tile_eb + jnp.where(pair == k, b_, 0)
    pad_start = jnp.concatenate([starts + counts, ends[-1:]]).astype(jnp.int32)
    pad_len = jnp.concatenate([sizes - counts, n_tiles * tm - ends[-1:]]).astype(jnp.int32)
    return (pos.astype(jnp.int32), pad_start, pad_len, tile_ea.astype(jnp.int32),
            tile_eb.astype(jnp.int32), tile_valid, n_tiles)


def _rope_tables(n):
    inv = (1.0 / (ROPE_BASE ** (np.arange(ROT_FREQS, dtype=np.float32) / ROT_FREQS))).astype(np.float32)
    pos = np.arange(n)
    row = (pos // GRID_W).astype(np.float32)[:, None] * inv[None, :]
    col = (pos % GRID_W).astype(np.float32)[:, None] * inv[None, :]
    cos64 = np.concatenate([np.cos(row), np.cos(row), np.cos(col), np.cos(col)], axis=1)
    sin64 = np.concatenate([-np.sin(row), np.sin(row), -np.sin(col), np.sin(col)], axis=1)
    tile = lambda a: np.tile(a.astype(np.float32), (1, LANES // HEAD_DIM))
    return jnp.asarray(tile(cos64)), jnp.asarray(tile(sin64))


def _dft_cos_sin(n):
    k = np.arange(n, dtype=np.int64)
    ang = (2.0 * np.pi / n) * ((k[:, None] * k[None, :]) % n).astype(np.float64)
    return np.cos(ang), np.sin(ang)


def _dft_tables(n):
    c_ch, s_ch = _dft_cos_sin(F_GROUP_DIM)
    c_seq, s_seq = _dft_cos_sin(n >> FNET_LEVELS)
    cs_ch = np.concatenate([c_ch, -s_ch], axis=1).astype(np.float32)
    cs_seq = np.concatenate([c_seq, s_seq], axis=1).astype(np.float32)
    twiddles = []
    for level in range(FNET_LEVELS):
        m = n >> level
        ang = (2.0 * np.pi / m) * np.arange(m // 2, dtype=np.float64)[:, None]
        for tab in (np.cos(ang), np.sin(ang)):
            twiddles.append(jnp.asarray(np.broadcast_to(tab, (m // 2, F_W)).astype(np.float32)))
    return jnp.asarray(cs_ch.astype(_BF)), jnp.asarray(cs_seq.astype(_BF)), twiddles


def kernel(x, c, ctx, c_ctx, w_mod, b_mod, norm1_g, norm2_g, w_in, lam_q1, lam_k1, lam_q2, lam_k2,
           subln_g, w_attn_out, w_four_out, w_out, w_router_group, b_router_group, w_router_expert,
           b_router_expert, w_exp_gate, w_exp_up, w_exp_down, final_g):
    b, n, d = x.shape
    assert w_mod.shape[0] == 1, "depth-1 stack"
    assert b + 1 <= MOD_ROWS

    cc = jnp.concatenate([c, c_ctx[None, :], jnp.zeros((MOD_ROWS - b - 1, d), _F32)], axis=0)
    mod3 = _modulation(cc, w_mod[0], b_mod).reshape(MOD_ROWS, 6, d)

    lam = (jnp.exp(jnp.sum(lam_q1[0] * lam_k1[0])) - jnp.exp(jnp.sum(lam_q2[0] * lam_k2[0]))
           + LAM_INIT).reshape(1).astype(_F32)

    w = w_in[0]
    scale = HEAD_DIM ** -0.5 * math.log2(math.e)
    w_lat = jnp.concatenate([w[:, REF_Q:REF_K] * scale, w[:, REF_K:REF_V],
                             w[:, REF_GA:REF_END], w[:, REF_F:REF_GA]], axis=1).astype(_BF)
    w_ctx = w[:, REF_K:REF_V].astype(_BF)
    w_vt = w[:, REF_V:REF_F].T.astype(_BF)
    lat_kinds = ("rope",) * ((P_GA - P_Q) // PROJ_CW) + ("sigmoid",) * ((P_F - P_GA) // PROJ_CW) \
        + ("plain",) * ((P_W - P_F) // PROJ_CW)
    ctx_kinds = ("plain",) * (w_ctx.shape[1] // PROJ_CW)

    cos_t, sin_t = _rope_tables(n)
    p, vt = _in_projection(x, mod3, lambda bi: bi, norm1_g, w_lat, w_vt, cos_t, sin_t, lat_kinds,
                           PROJ_TM)
    kc, vct = _in_projection(ctx, mod3, lambda bi: b, norm1_g, w_ctx, w_vt, cos_t, sin_t,
                             ctx_kinds, ctx.shape[1])

    heads = _attention(lam, p, vt, kc, vct, subln_g, ATTN_TQ)
    cs_ch, cs_seq, twiddles = _dft_tables(n)
    four = _fnet(p, cs_ch, cs_seq, twiddles)

    w_r = jnp.concatenate([w_router_expert[0], w_router_group[0],
                           jnp.zeros((d, ROUTER_W - N_EXPERTS - N_GROUPS), _F32)], axis=1)
    b_r = jnp.concatenate([b_router_expert[0], b_router_group[0],
                           jnp.zeros((ROUTER_W - N_EXPERTS - N_GROUPS,), _F32)])[None, :]
    wr_hi = w_r.astype(_BF)
    wr_lo = (w_r - wr_hi.astype(_F32)).astype(_BF)
    x1, rows, rec, counts = _merge(heads, four, p, x, mod3, norm2_g, w_attn_out[0].astype(_BF),
                                   w_four_out[0].astype(_BF), w_out[0].astype(_BF),
                                   jnp.concatenate([wr_hi, wr_lo], axis=1), b_r, MERGE_TM)

    t = b * n
    slabs = rows.reshape(t, SUBLANES, LANES)
    rec = rec.reshape(t // MERGE_TM, SUBLANES, MERGE_TM)
    pos, pad_start, pad_len, tile_ea, tile_eb, tile_valid, n_tiles = _routing_tables(
        rec[:, 2, :].reshape(t).astype(jnp.int32), rec[:, 3, :].reshape(t).astype(jnp.int32),
        counts[0, :N_CLASSES].astype(jnp.int32), t, MOE_TM)

    slabs_sorted = _dispatch(pos, pad_start, pad_len, slabs, n_tiles * MOE_TM, DISPATCH_TM)
    y_sorted = _moe(tile_ea, tile_eb, tile_valid, slabs_sorted, w_exp_gate[0].astype(_BF),
                    w_exp_up[0].astype(_BF), w_exp_down[0].astype(_BF), MOE_TM)
    out = _combine(pos, y_sorted, x1.reshape(t, d), mod3, final_g[None, :], n, COMBINE_TM)
    return out.reshape(b, n, d)
```

```python
import functools
import math

import jax
import jax.numpy as jnp
import numpy as np
from jax import lax
from jax.experimental import pallas as pl
from jax.experimental.pallas import tpu as pltpu

D_MODEL = 1024
GRID_W = 64
EPS = 1e-6
N_HEADS = 8
HEAD_DIM = 64
HEAD_W = 2 * HEAD_DIM
ROT_FREQS = HEAD_DIM // 4
ROPE_BASE = 10000.0
F_GROUPS = 4
F_GROUP_DIM = 128
F_W = F_GROUPS * F_GROUP_DIM
N_GROUPS = 4
EXPERTS_PER_GROUP = 4
LOG2_EXPERTS_PER_GROUP = 2
N_EXPERTS = N_GROUPS * EXPERTS_PER_GROUP
PAIRS_PER_GROUP = EXPERTS_PER_GROUP * (EXPERTS_PER_GROUP - 1) // 2
N_CLASSES = N_GROUPS * PAIRS_PER_GROUP
EXPERT_HIDDEN = 512
LAM_INIT = 0.8 - 0.6 * math.exp(-0.3 * 0)

REF_Q, REF_K, REF_V, REF_F, REF_GA, REF_GF, REF_END = 0, 1024, 2048, 3072, 3584, 4608, 5632
P_Q, P_K, P_V, P_GA, P_GF, P_F, P_W = 0, 1024, 2048, 3072, 4096, 5120, 5632

LANES = 128
SUBLANES = 8
VMEM_LIMIT_BYTES = 56 * 1024 * 1024

PROJ_TM = 1024
PROJ_CW = 512
ATTN_TQ = 256
ATTN_SUM_ROWS = 16
ATTN_SCORE_BUFS = 3
ATTN_HEADS_PER_STEP = 2
FNET_LEVELS = 2
FNET_CHUNK = 256
MERGE_TM = 512
MOE_TM = 256
COMBINE_TM = 512
DISPATCH_TM = 2048
DMA_ISSUE_GROUP = 16
MOD_ROWS = 40
ROUTER_W = LANES
GROUP_LANE0 = N_EXPERTS
H2_SLAB_ROWS = D_MODEL // 2 // LANES
ROUTE_SLAB_ROW = H2_SLAB_ROWS

_BF = jnp.bfloat16
_F32 = jnp.float32


def _cparams(*sem):
    return pltpu.CompilerParams(dimension_semantics=sem, vmem_limit_bytes=VMEM_LIMIT_BYTES)


def _const_spec(shape):
    return pl.BlockSpec(shape, lambda *_: (0,) * len(shape), pipeline_mode=pl.Buffered(1))


def _rms(x):
    return x * lax.rsqrt(jnp.mean(x * x, axis=-1, keepdims=True) + EPS)


def _sigmoid(x):
    return 1.0 / (1.0 + jnp.exp(-x))


def _mod_kernel(cc_ref, w_ref, b_ref, o_ref):
    cc = cc_ref[...]
    s = cc * _sigmoid(cc)
    o_ref[...] = jnp.dot(s, w_ref[...], preferred_element_type=_F32,
                         precision=lax.Precision.HIGHEST) + b_ref[...]


def _modulation(cc, w_mod, b_mod):
    n = w_mod.shape[1]
    bn = D_MODEL
    return pl.pallas_call(
        _mod_kernel,
        grid=(n // bn,),
        in_specs=[pl.BlockSpec((MOD_ROWS, D_MODEL), lambda j: (0, 0)),
                  pl.BlockSpec((D_MODEL, bn), lambda j: (0, j)),
                  pl.BlockSpec((1, bn), lambda j: (0, j))],
        out_specs=pl.BlockSpec((MOD_ROWS, bn), lambda j: (0, j)),
        out_shape=jax.ShapeDtypeStruct((MOD_ROWS, n), _F32),
        compiler_params=_cparams("arbitrary"),
        name="modulation",
    )(cc, w_mod, b_mod)


def _rope(acc, cos_ref, sin_ref):
    cos = cos_ref[...]
    sin = sin_ref[...]
    lane = lax.broadcasted_iota(jnp.int32, (1, LANES), 1)
    first_half = (lane % (2 * ROT_FREQS)) < ROT_FREQS
    outs = []
    for s in range(acc.shape[1] // LANES):
        xs = acc[:, s * LANES:(s + 1) * LANES]
        partner = jnp.where(first_half,
                            pltpu.roll(xs, LANES - ROT_FREQS, 1),
                            pltpu.roll(xs, ROT_FREQS, 1))
        outs.append(xs * cos + partner * sin)
    return jnp.concatenate(outs, axis=1)


def _inproj_kernel(x_ref, mod_ref, g_ref, w_ref, cos_ref, sin_ref, o_ref, *, chunk_kinds):
    x = x_ref[0]
    shift = mod_ref[0, 0:1, :]
    scale = mod_ref[0, 1:2, :]
    h = (_rms(x) * g_ref[...]) * (1.0 + scale) + shift
    hb = h.astype(_BF)
    for j, kind in enumerate(chunk_kinds):
        cols = slice(j * PROJ_CW, (j + 1) * PROJ_CW)
        acc = jnp.dot(hb, w_ref[:, cols], preferred_element_type=_F32)
        if kind == "rope":
            acc = _rope(acc, cos_ref, sin_ref)
        elif kind == "sigmoid":
            acc = _sigmoid(acc)
        o_ref[0, :, cols] = acc.astype(_BF)


def _in_projection(x, mod3, mod_row_of_batch, g, w, cos_t, sin_t, chunk_kinds, tm):
    b, n, d = x.shape
    width = w.shape[1]
    assert width == len(chunk_kinds) * PROJ_CW and n % tm == 0
    return pl.pallas_call(
        functools.partial(_inproj_kernel, chunk_kinds=chunk_kinds),
        grid=(b, n // tm),
        in_specs=[pl.BlockSpec((1, tm, d), lambda bi, i: (bi, i, 0)),
                  pl.BlockSpec((1, 6, d), lambda bi, i: (mod_row_of_batch(bi), 0, 0)),
                  pl.BlockSpec((1, d), lambda bi, i: (0, 0)),
                  _const_spec((d, width)),
                  pl.BlockSpec((tm, LANES), lambda bi, i: (i, 0)),
                  pl.BlockSpec((tm, LANES), lambda bi, i: (i, 0))],
        out_specs=pl.BlockSpec((1, tm, width), lambda bi, i: (bi, i, 0)),
        out_shape=jax.ShapeDtypeStruct((b, n, width), _BF),
        compiler_params=_cparams("parallel", "arbitrary"),
        name="in_projection",
    )(x, mod3, g, w, cos_t, sin_t)


def _attn_kernel(lam_ref, q_ref, kl_ref, vl_ref, kc_ref, vc_ref, sg_ref, o_ref,
                 kcat_ref, vt_ref, *bufs, tq):
    nc, n = kc_ref.shape[1], kl_ref.shape[1]
    for hh in range(ATTN_HEADS_PER_STEP):
        cols = slice(hh * HEAD_W, (hh + 1) * HEAD_W)
        kcat_ref[hh, 0:nc, :] = kc_ref[0, :, cols]
        kcat_ref[hh, nc:nc + n, :] = kl_ref[0, :, cols]
        vt_ref[hh, 0:HEAD_W, 0:nc] = vc_ref[0, :, cols].astype(_F32).T.astype(_BF)
        vt_ref[hh, 0:HEAD_W, nc:nc + n] = vl_ref[0, :, cols].astype(_F32).T.astype(_BF)
        vt_ref[hh, HEAD_W:, :] = jnp.ones((ATTN_SUM_ROWS, nc + n), _BF)
    lane = lax.broadcasted_iota(jnp.int32, (1, HEAD_W), 1)
    map_lanes = (lane < HEAD_DIM, lane >= HEAD_DIM)
    nt = (((1,), (1,)), ((), ()))
    lam = lam_ref[0]
    post_scale = sg_ref[...] * (1.0 - LAM_INIT)
    n_sub = n // tq

    def scores(g):
        hh, j = divmod(g, n_sub)
        q = q_ref[0, j * tq:(j + 1) * tq, hh * HEAD_W:(hh + 1) * HEAD_W]
        for mp in range(2):
            qm = jnp.where(map_lanes[mp], q, jnp.zeros_like(q))
            bufs[g % ATTN_SCORE_BUFS][mp] = lax.dot_general(kcat_ref[hh], qm, nt,
                                                            preferred_element_type=_F32)

    def finish(g):
        hh, j = divmod(g, n_sub)
        outs = []
        for mp in range(2):
            s = bufs[g % ATTN_SCORE_BUFS][mp]
            e = jnp.exp2(s - jnp.max(s, axis=0, keepdims=True)).astype(_BF)
            r = jnp.dot(vt_ref[hh], e, preferred_element_type=_F32)
            outs.append(r[:HEAD_W, :] / r[HEAD_W:HEAD_W + 1, :])
        heads = (outs[0] - lam * outs[1]).T
        o_ref[0, j * tq:(j + 1) * tq, hh * HEAD_W:(hh + 1) * HEAD_W] = (
            _rms(heads) * post_scale).astype(_BF)

    total = ATTN_HEADS_PER_STEP * n_sub
    for g in range(ATTN_SCORE_BUFS - 1):
        scores(g)
    for g in range(total):
        if g + ATTN_SCORE_BUFS - 1 < total:
            scores(g + ATTN_SCORE_BUFS - 1)
        finish(g)


def _attention(lam, p, kvc, subln_g, tq):
    b, n, _ = p.shape
    nc = kvc.shape[1]
    w = ATTN_HEADS_PER_STEP * HEAD_W
    qb, kb, vb = P_Q // w, P_K // w, P_V // w
    seq = lambda blk: pl.BlockSpec((1, n, w), lambda bi, h: (bi, 0, blk + h))
    return pl.pallas_call(
        functools.partial(_attn_kernel, tq=tq),
        grid=(b, N_HEADS // ATTN_HEADS_PER_STEP),
        in_specs=[pl.BlockSpec(memory_space=pltpu.SMEM),
                  seq(qb), seq(kb), seq(vb),
                  pl.BlockSpec((1, nc, w), lambda bi, h: (bi, 0, h)),
                  pl.BlockSpec((1, nc, w), lambda bi, h: (bi, 0, N_HEADS // ATTN_HEADS_PER_STEP + h)),
                  pl.BlockSpec((1, HEAD_W), lambda bi, h: (0, 0))],
        out_specs=seq(0),
        out_shape=jax.ShapeDtypeStruct((b, n, N_HEADS * HEAD_W), _BF),
        scratch_shapes=[pltpu.VMEM((ATTN_HEADS_PER_STEP, nc + n, HEAD_W), _BF),
                        pltpu.VMEM((ATTN_HEADS_PER_STEP, HEAD_W + ATTN_SUM_ROWS, nc + n), _BF),
                        ] + [pltpu.VMEM((2, nc + n, tq), _F32)] * ATTN_SCORE_BUFS,
        compiler_params=_cparams("parallel", "arbitrary"),
        name="diff_attention",
    )(lam, p, p, p, kvc, kvc, subln_g)


def _bit_reverse(j, bits):
    return int(format(j, "0%db" % bits)[::-1], 2) if bits else 0


def _fnet_kernel(f_ref, cs_ch_ref, cs_seq_ref, *rest, n):
    tw_refs = rest[:2 * FNET_LEVELS]
    o_ref, zr_ref, zi_ref, rhs_ref = rest[2 * FNET_LEVELS:]
    for g in range(F_GROUPS):
        cols = slice(g * F_GROUP_DIM, (g + 1) * F_GROUP_DIM)
        t = jnp.dot(f_ref[0, :, cols], cs_ch_ref[...], preferred_element_type=_F32)
        zr_ref[:, cols] = t[:, :F_GROUP_DIM]
        zi_ref[:, cols] = t[:, F_GROUP_DIM:]
    for level in range(FNET_LEVELS):
        m = n >> level
        half = m // 2
        cos_ref, sin_ref = tw_refs[2 * level], tw_refs[2 * level + 1]
        for seg in range(1 << level):
            for c0 in range(0, half, FNET_CHUNK):
                rows_t = slice(seg * m + c0, seg * m + c0 + FNET_CHUNK)
                rows_b = slice(seg * m + half + c0, seg * m + half + c0 + FNET_CHUNK)
                tr, ti = zr_ref[rows_t, :], zi_ref[rows_t, :]
                br, bi = zr_ref[rows_b, :], zi_ref[rows_b, :]
                cw, sw = cos_ref[c0:c0 + FNET_CHUNK, :], sin_ref[c0:c0 + FNET_CHUNK, :]
                dr, di = tr - br, ti - bi
                zr_ref[rows_t, :] = tr + br
                zi_ref[rows_t, :] = ti + bi
                zr_ref[rows_b, :] = dr * cw + di * sw
                zi_ref[rows_b, :] = di * cw - dr * sw
    m = n >> FNET_LEVELS
    for j in range(1 << FNET_LEVELS):
        cols = slice(_bit_reverse(j, FNET_LEVELS) * F_W, (_bit_reverse(j, FNET_LEVELS) + 1) * F_W)
        rhs_ref[0:m, cols] = zr_ref[j * m:(j + 1) * m, :].astype(_BF)
        rhs_ref[m:2 * m, cols] = zi_ref[j * m:(j + 1) * m, :].astype(_BF)
    ortho = 1.0 / math.sqrt(n * F_GROUP_DIM)
    y = jnp.dot(cs_seq_ref[...], rhs_ref[...], preferred_element_type=_F32)
    o_ref[0] = (y * ortho).reshape(m, 1 << FNET_LEVELS, F_W).reshape(n, F_W).astype(_BF)


def _fnet(p, cs_ch, cs_seq, twiddles):
    b, n, _ = p.shape
    m = n >> FNET_LEVELS
    wide = F_W << FNET_LEVELS
    return pl.pallas_call(
        functools.partial(_fnet_kernel, n=n),
        grid=(b,),
        in_specs=[pl.BlockSpec((1, n, F_W), lambda bi: (bi, 0, P_F // F_W)),
                  _const_spec((F_GROUP_DIM, 2 * F_GROUP_DIM)),
                  _const_spec((m, 2 * m))] + [_const_spec(t.shape) for t in twiddles],
        out_specs=pl.BlockSpec((1, n, F_W), lambda bi: (bi, 0, 0)),
        out_shape=jax.ShapeDtypeStruct((b, n, F_W), _BF),
        scratch_shapes=[pltpu.VMEM((n, F_W), _F32), pltpu.VMEM((n, F_W), _F32),
                        pltpu.VMEM((2 * m, wide), _BF)],
        compiler_params=_cparams("arbitrary"),
        name="fnet_dft",
    )(p, cs_ch, cs_seq, *twiddles)


def _pack_bf16_pair(hi, lo):
    hi_bits = pltpu.bitcast(hi.astype(_BF).astype(_F32), jnp.uint32)
    lo_bits = pltpu.bitcast(lo.astype(_BF).astype(_F32), jnp.uint32)
    return hi_bits | lax.shift_right_logical(lo_bits, jnp.uint32(16))


def _unpack_bf16_pair(packed):
    hi = pltpu.bitcast(packed & jnp.uint32(0xFFFF0000), _F32)
    lo = pltpu.bitcast(lax.shift_left(packed, jnp.uint32(16)), _F32)
    return hi, lo


def _route(logits, carry):
    rows = logits.shape[0]
    lane = lax.broadcasted_iota(jnp.int32, logits.shape, 1)
    neg = jnp.float32(-jnp.inf)
    big = jnp.int32(ROUTER_W)

    def first_argmax(v):
        m = jnp.max(v, axis=-1, keepdims=True)
        idx = jnp.min(jnp.where(v == m, lane, big), axis=-1, keepdims=True)
        return m, idx

    lg = jnp.where((lane >= GROUP_LANE0) & (lane < GROUP_LANE0 + N_GROUPS), logits, neg)
    mg, ig = first_argmax(lg)
    w_grp = 1.0 / jnp.sum(jnp.exp(lg - mg), axis=-1, keepdims=True)
    g_sel = ig - GROUP_LANE0
    le = jnp.where((lane < N_EXPERTS)
                   & (jnp.right_shift(lane, LOG2_EXPERTS_PER_GROUP) == g_sel), logits, neg)
    v1, i1 = first_argmax(le)
    le2 = jnp.where(lane == i1, neg, le)
    v2, i2 = first_argmax(le2)
    e2 = jnp.exp(v2 - v1)
    w1 = w_grp / (1.0 + e2)
    w2 = w_grp * e2 / (1.0 + e2)

    first_is_lower = i1 < i2
    gate_a = jnp.where(first_is_lower, w1, w2)
    gate_b = jnp.where(first_is_lower, w2, w1)
    la = jnp.minimum(i1, i2) - g_sel * EXPERTS_PER_GROUP
    lb = jnp.maximum(i1, i2) - g_sel * EXPERTS_PER_GROUP
    pair = jnp.right_shift(la * (2 * EXPERTS_PER_GROUP - 1 - la), 1) + lb - la - 1
    cls = g_sel * PAIRS_PER_GROUP + pair

    onehot = (lane == cls).astype(_F32)
    r_i = lax.broadcasted_iota(jnp.int32, (rows, rows), 0)
    c_i = lax.broadcasted_iota(jnp.int32, (rows, rows), 1)
    earlier = (c_i < r_i).astype(_BF)
    before = jnp.dot(earlier, onehot.astype(_BF), preferred_element_type=_F32) + carry
    rank = jnp.sum(onehot * before, axis=-1, keepdims=True)
    new_carry = carry + jnp.sum(onehot, axis=0, keepdims=True)

    route = (jnp.where(lane == 0, gate_a, 0.0) + jnp.where(lane == 1, gate_b, 0.0)
             + jnp.where(lane == 2, cls.astype(_F32), 0.0) + jnp.where(lane == 3, rank, 0.0))
    return route, new_carry


def _merge_kernel(hd_ref, fo_ref, ga_ref, gf_ref, x_ref, mod_ref, g2n_ref, wao_ref, wfo_ref,
                  wo_ref, wrhl_ref, br_ref, x1_ref, rows_ref, rec_ref, counts_ref, carry_ref):
    @pl.when((pl.program_id(0) == 0) & (pl.program_id(1) == 0))
    def _():
        carry_ref[...] = jnp.zeros_like(carry_ref)

    a = jnp.dot(hd_ref[0], wao_ref[...], preferred_element_type=_F32)
    ff = jnp.dot(fo_ref[0], wfo_ref[...], preferred_element_type=_F32)
    y = ga_ref[0].astype(_F32) * a + gf_ref[0].astype(_F32) * ff
    mix = jnp.dot(y.astype(_BF), wo_ref[...], preferred_element_type=_F32)
    x1 = x_ref[0] + mod_ref[0, 2:3, :] * mix
    x1_ref[0] = x1
    h2 = (_rms(x1) * g2n_ref[...]) * (1.0 + mod_ref[0, 4:5, :]) + mod_ref[0, 3:4, :]
    h2_hi = h2.astype(_BF)
    h2_lo = (h2 - h2_hi.astype(_F32)).astype(_BF)
    hi_both = jnp.dot(h2_hi, wrhl_ref[...], preferred_element_type=_F32)
    logits = (hi_both[:, :ROUTER_W] + hi_both[:, ROUTER_W:]
              + jnp.dot(h2_lo, wrhl_ref[:, :ROUTER_W], preferred_element_type=_F32)) + br_ref[...]
    route, new_carry = _route(logits, carry_ref[...])
    carry_ref[...] = new_carry
    counts_ref[...] = new_carry
    rec_ref[0, 0] = route.T[0:SUBLANES, :]
    half = D_MODEL // 2
    tm = h2.shape[0]
    packed = _pack_bf16_pair(h2[:, :half], h2[:, half:])
    rows_ref[0, :, 0:H2_SLAB_ROWS, :] = packed.reshape(tm, H2_SLAB_ROWS, LANES)
    rows_ref[0, :, ROUTE_SLAB_ROW:ROUTE_SLAB_ROW + 1, :] = (
        pltpu.bitcast(route, jnp.uint32).reshape(tm, 1, LANES))
    rows_ref[0, :, ROUTE_SLAB_ROW + 1:SUBLANES, :] = jnp.zeros(
        (tm, SUBLANES - ROUTE_SLAB_ROW - 1, LANES), jnp.uint32)


def _merge(heads, four, p, x, mod3, norm2_g, w_ao, w_fo, w_o, wr_hilo, b_r, tm):
    b, n, d = x.shape
    tok = lambda w: pl.BlockSpec((1, tm, w), lambda bi, i: (bi, i, 0))
    return pl.pallas_call(
        _merge_kernel,
        grid=(b, n // tm),
        in_specs=[tok(d), tok(F_W),
                  pl.BlockSpec((1, tm, d), lambda bi, i: (bi, i, P_GA // D_MODEL)),
                  pl.BlockSpec((1, tm, d), lambda bi, i: (bi, i, P_GF // D_MODEL)),
                  tok(d),
                  pl.BlockSpec((1, 6, d), lambda bi, i: (bi, 0, 0)),
                  pl.BlockSpec((1, d), lambda bi, i: (0, 0)),
                  _const_spec((d, d)), _const_spec((F_W, d)), _const_spec((d, d)),
                  _const_spec((d, 2 * ROUTER_W)),
                  pl.BlockSpec((1, ROUTER_W), lambda bi, i: (0, 0))],
        out_specs=[tok(d),
                   pl.BlockSpec((1, tm, SUBLANES, LANES), lambda bi, i: (bi, i, 0, 0)),
                   pl.BlockSpec((1, 1, SUBLANES, tm), lambda bi, i: (bi, i, 0, 0)),
                   pl.BlockSpec((1, ROUTER_W), lambda bi, i: (0, 0))],
        out_shape=[jax.ShapeDtypeStruct((b, n, d), _F32),
                   jax.ShapeDtypeStruct((b, n, SUBLANES, LANES), jnp.uint32),
                   jax.ShapeDtypeStruct((b, n // tm, SUBLANES, tm), _F32),
                   jax.ShapeDtypeStruct((1, ROUTER_W), _F32)],
        scratch_shapes=[pltpu.VMEM((1, ROUTER_W), _F32)],
        compiler_params=_cparams("arbitrary", "arbitrary"),
        name="merge_router",
    )(heads, four, p, p, x, mod3, norm2_g, w_ao, w_fo, w_o, wr_hilo, b_r)


def _issue_slab_copies(n, make_copy, slot_of):
    def group(g, c):
        r0 = g * DMA_ISSUE_GROUP
        slots = [slot_of(r0 + k) for k in range(DMA_ISSUE_GROUP)]
        for k in range(DMA_ISSUE_GROUP):
            make_copy(r0 + k, slots[k]).start(priority=k % 2)
        return c

    lax.fori_loop(0, n // DMA_ISSUE_GROUP, group, 0)


def _dispatch_kernel(pos_ref, pad_start_ref, pad_len_ref, src_ref, dst_ref, zero_ref, sem, pad_sem,
                     *, tm):
    base = pl.program_id(0) * tm

    @pl.when(pl.program_id(0) == 0)
    def _():
        zero_ref[...] = jnp.zeros_like(zero_ref)
        for c in range(N_CLASSES + 1):
            def start(r, carry, c=c):
                pltpu.make_async_copy(zero_ref, dst_ref.at[pad_start_ref[c] + r], pad_sem).start()
                return carry

            def drain(r, carry):
                pltpu.make_async_copy(zero_ref, dst_ref.at[0], pad_sem).wait()
                return carry

            lax.fori_loop(0, pad_len_ref[c], start, 0)
            lax.fori_loop(0, pad_len_ref[c], drain, 0)

    _issue_slab_copies(
        tm,
        lambda r, slot: pltpu.make_async_copy(src_ref.at[r], dst_ref.at[slot], sem),
        lambda r: pos_ref[base + r])
    pltpu.make_async_copy(src_ref, dst_ref.at[pl.ds(0, tm)], sem).wait()


def _dispatch(pos, pad_start, pad_len, slabs, n_sorted, tm):
    t = slabs.shape[0]
    return pl.pallas_call(
        functools.partial(_dispatch_kernel, tm=tm),
        grid_spec=pltpu.PrefetchScalarGridSpec(
            num_scalar_prefetch=3,
            grid=(t // tm,),
            in_specs=[pl.BlockSpec((tm, SUBLANES, LANES), lambda i, *_: (i, 0, 0))],
            out_specs=pl.BlockSpec(memory_space=pl.ANY),
            scratch_shapes=[pltpu.VMEM((SUBLANES, LANES), slabs.dtype),
                            pltpu.SemaphoreType.DMA(()), pltpu.SemaphoreType.DMA(())]),
        out_shape=jax.ShapeDtypeStruct((n_sorted,) + slabs.shape[1:], slabs.dtype),
        compiler_params=_cparams("arbitrary"),
        name="moe_dispatch",
    )(pos, pad_start, pad_len, slabs)


def _moe_kernel(ea_ref, eb_ref, valid_ref, slabs_ref, w1a_ref, w3a_ref, w2a_ref,
                w1b_ref, w3b_ref, w2b_ref, y_ref):
    del ea_ref, eb_ref
    i = pl.program_id(0)
    tm = slabs_ref.shape[0]

    @pl.when(valid_ref[i] != 0)
    def _():
        packed = slabs_ref[:, 0:H2_SLAB_ROWS, :].reshape(tm, H2_SLAB_ROWS * LANES)
        hi, lo = _unpack_bf16_pair(packed)
        t = jnp.concatenate([hi.astype(_BF), lo.astype(_BF)], axis=1)
        route = pltpu.bitcast(slabs_ref[:, ROUTE_SLAB_ROW, :], _F32)
        y = None
        for slot, (w1_ref, w3_ref, w2_ref) in enumerate(((w1a_ref, w3a_ref, w2a_ref),
                                                          (w1b_ref, w3b_ref, w2b_ref))):
            a = jnp.dot(t, w1_ref[0], preferred_element_type=_F32)
            u = jnp.dot(t, w3_ref[0], preferred_element_type=_F32)
            hid = ((a * _sigmoid(a)) * u * route[:, slot:slot + 1]).astype(_BF)
            part = jnp.dot(hid, w2_ref[0], preferred_element_type=_F32)
            y = part if y is None else y + part
        y_ref[...] = y.reshape(tm, SUBLANES, LANES)

    @pl.when(valid_ref[i] == 0)
    def _():
        y_ref[...] = jnp.zeros_like(y_ref)


def _moe(tile_ea, tile_eb, tile_valid, slabs_sorted, w1, w3, w2, tm):
    n_sorted = slabs_sorted.shape[0]
    d, hdn = w1.shape[1], w1.shape[2]
    assert d == SUBLANES * LANES
    pick = lambda which, a, b_: pl.BlockSpec(
        (1, a, b_), lambda i, ea, eb, valid: ((ea, eb)[which][i], 0, 0))
    slab_tile = pl.BlockSpec((tm, SUBLANES, LANES), lambda i, ea, eb, valid: (i, 0, 0))
    return pl.pallas_call(
        _moe_kernel,
        grid_spec=pltpu.PrefetchScalarGridSpec(
            num_scalar_prefetch=3,
            grid=(n_sorted // tm,),
            in_specs=[slab_tile,
                      pick(0, d, hdn), pick(0, d, hdn), pick(0, hdn, d),
                      pick(1, d, hdn), pick(1, d, hdn), pick(1, hdn, d)],
            out_specs=slab_tile),
        out_shape=jax.ShapeDtypeStruct((n_sorted, SUBLANES, LANES), _F32),
        compiler_params=_cparams("arbitrary"),
        name="moe_experts",
    )(tile_ea, tile_eb, tile_valid, slabs_sorted, w1, w3, w2, w1, w3, w2)


def _combine_kernel(pos_ref, y_hbm_ref, x1_ref, mod_ref, fg_ref, o_ref, ybuf_ref, sems, *, tm):
    i = pl.program_id(0)
    n_steps = pl.num_programs(0)

    def issue(tile, slot):
        _issue_slab_copies(
            tm,
            lambda r, src: pltpu.make_async_copy(y_hbm_ref.at[src], ybuf_ref.at[slot, r],
                                                 sems.at[slot]),
            lambda r: pos_ref[tile * tm + r])

    @pl.when(i == 0)
    def _():
        issue(0, 0)

    @pl.when(i + 1 < n_steps)
    def _():
        issue(i + 1, (i + 1) % 2)

    slot = i % 2
    pltpu.make_async_copy(y_hbm_ref.at[pl.ds(0, tm)], ybuf_ref.at[slot], sems.at[slot]).wait()
    y = ybuf_ref[slot].reshape(tm, SUBLANES * LANES)
    xo = x1_ref[...] + mod_ref[0, 5:6, :] * y
    o_ref[...] = _rms(xo) * fg_ref[...]


def _combine(pos, y_sorted, x1, mod3, final_g, seq, tm):
    t, d = x1.shape
    tiles_per_batch = seq // tm
    return pl.pallas_call(
        functools.partial(_combine_kernel, tm=tm),
        grid_spec=pltpu.PrefetchScalarGridSpec(
            num_scalar_prefetch=1,
            grid=(t // tm,),
            in_specs=[pl.BlockSpec(memory_space=pl.ANY),
                      pl.BlockSpec((tm, d), lambda i, pos: (i, 0)),
                      pl.BlockSpec((1, 6, d), lambda i, pos: (i // tiles_per_batch, 0, 0)),
                      pl.BlockSpec((1, d), lambda i, pos: (0, 0))],
            out_specs=pl.BlockSpec((tm, d), lambda i, pos: (i, 0)),
            scratch_shapes=[pltpu.VMEM((2, tm, SUBLANES, LANES), _F32),
                            pltpu.SemaphoreType.DMA((2,))]),
        out_shape=jax.ShapeDtypeStruct((t, d), _F32),
        compiler_params=_cparams("arbitrary"),
        name="moe_combine",
    )(pos, y_sorted, x1, mod3, final_g)


def _routing_tables(route_cls, route_rank, counts, n_tokens, tm):
    sizes = ((counts + (tm - 1)) // tm) * tm
    ends = jnp.cumsum(sizes)
    starts = ends - sizes
    n_tiles = n_tokens // tm + N_CLASSES
    tile_start = jnp.arange(n_tiles, dtype=jnp.int32) * tm
    pos = route_rank
    tile_cls = jnp.zeros((n_tiles,), jnp.int32)
    for c in range(N_CLASSES):
        pos = pos + jnp.where(route_cls == c, starts[c], 0)
        tile_cls = tile_cls + (tile_start >= ends[c]).astype(jnp.int32)
    tile_cls = jnp.minimum(tile_cls, N_CLASSES - 1)
    tile_valid = (tile_start < ends[-1]).astype(jnp.int32)
    pairs = [(a, b_) for a in range(EXPERTS_PER_GROUP) for b_ in range(a + 1, EXPERTS_PER_GROUP)]
    grp, pair = tile_cls // PAIRS_PER_GROUP, tile_cls % PAIRS_PER_GROUP
    tile_ea = grp * EXPERTS_PER_GROUP
    tile_eb = grp * EXPERTS_PER_GROUP
    for k, (a, b_) in enumerate(pairs):
        tile_ea = tile_ea + jnp.where(pair == k, a, 0)
        tile_eb = tile_eb + jnp.where(pair == k, b_, 0)
    pad_start = jnp.concatenate([starts + counts, ends[-1:]]).astype(jnp.int32)
    pad_len = jnp.concatenate([sizes - counts, n_tiles * tm - ends[-1:]]).astype(jnp.int32)
    return (pos.astype(jnp.int32), pad_start, pad_len, tile_ea.astype(jnp.int32),
            tile_eb.astype(jnp.int32), tile_valid, n_tiles)


def _rope_tables(n):
    inv = (1.0 / (ROPE_BASE ** (np.arange(ROT_FREQS, dtype=np.float32) / ROT_FREQS))).astype(np.float32)
    pos = np.arange(n)
    row = (pos // GRID_W).astype(np.float32)[:, None] * inv[None, :]
    col = (pos % GRID_W).astype(np.float32)[:, None] * inv[None, :]
    cos64 = np.concatenate([np.cos(row), np.cos(row), np.cos(col), np.cos(col)], axis=1)
    sin64 = np.concatenate([-np.sin(row), np.sin(row), -np.sin(col), np.sin(col)], axis=1)
    tile = lambda a: np.tile(a.astype(np.float32), (1, LANES // HEAD_DIM))
    return jnp.asarray(tile(cos64)), jnp.asarray(tile(sin64))


def _dft_cos_sin(n):
    k = np.arange(n, dtype=np.int64)
    ang = (2.0 * np.pi / n) * ((k[:, None] * k[None, :]) % n).astype(np.float64)
    return np.cos(ang), np.sin(ang)


def _dft_tables(n):
    c_ch, s_ch = _dft_cos_sin(F_GROUP_DIM)
    c_seq, s_seq = _dft_cos_sin(n >> FNET_LEVELS)
    cs_ch = np.concatenate([c_ch, -s_ch], axis=1).astype(np.float32)
    cs_seq = np.concatenate([c_seq, s_seq], axis=1).astype(np.float32)
    twiddles = []
    for level in range(FNET_LEVELS):
        m = n >> level
        ang = (2.0 * np.pi / m) * np.arange(m // 2, dtype=np.float64)[:, None]
        for tab in (np.cos(ang), np.sin(ang)):
            twiddles.append(jnp.asarray(np.broadcast_to(tab, (m // 2, F_W)).astype(np.float32)))
    return jnp.asarray(cs_ch.astype(_BF)), jnp.asarray(cs_seq.astype(_BF)), twiddles


def kernel(x, c, ctx, c_ctx, w_mod, b_mod, norm1_g, norm2_g, w_in, lam_q1, lam_k1, lam_q2, lam_k2,
           subln_g, w_attn_out, w_four_out, w_out, w_router_group, b_router_group, w_router_expert,
           b_router_expert, w_exp_gate, w_exp_up, w_exp_down, final_g):
    b, n, d = x.shape
    assert w_mod.shape[0] == 1, "depth-1 stack"
    assert b + 1 <= MOD_ROWS

    cc = jnp.concatenate([c, c_ctx[None, :], jnp.zeros((MOD_ROWS - b - 1, d), _F32)], axis=0)
    mod3 = _modulation(cc, w_mod[0], b_mod).reshape(MOD_ROWS, 6, d)

    lam = (jnp.exp(jnp.sum(lam_q1[0] * lam_k1[0])) - jnp.exp(jnp.sum(lam_q2[0] * lam_k2[0]))
           + LAM_INIT).reshape(1).astype(_F32)

    w = w_in[0]
    scale = HEAD_DIM ** -0.5 * math.log2(math.e)
    w_lat = jnp.concatenate([w[:, REF_Q:REF_K] * scale, w[:, REF_K:REF_F],
                             w[:, REF_GA:REF_END], w[:, REF_F:REF_GA]], axis=1).astype(_BF)
    w_ctx = w[:, REF_K:REF_F].astype(_BF)
    lat_kinds = ("rope",) * ((P_V - P_Q) // PROJ_CW) + ("plain",) * ((P_GA - P_V) // PROJ_CW) \
        + ("sigmoid",) * ((P_F - P_GA) // PROJ_CW) + ("plain",) * ((P_W - P_F) // PROJ_CW)
    ctx_kinds = ("plain",) * (w_ctx.shape[1] // PROJ_CW)

    cos_t, sin_t = _rope_tables(n)
    p = _in_projection(x, mod3, lambda bi: bi, norm1_g, w_lat, cos_t, sin_t, lat_kinds, PROJ_TM)
    kvc = _in_projection(ctx, mod3, lambda bi: b, norm1_g, w_ctx, cos_t, sin_t, ctx_kinds,
                         ctx.shape[1])

    heads = _attention(lam, p, kvc, subln_g, ATTN_TQ)
    cs_ch, cs_seq, twiddles = _dft_tables(n)
    four = _fnet(p, cs_ch, cs_seq, twiddles)

    w_r = jnp.concatenate([w_router_expert[0], w_router_group[0],
                           jnp.zeros((d, ROUTER_W - N_EXPERTS - N_GROUPS), _F32)], axis=1)
    b_r = jnp.concatenate([b_router_expert[0], b_router_group[0],
                           jnp.zeros((ROUTER_W - N_EXPERTS - N_GROUPS,), _F32)])[None, :]
    wr_hi = w_r.astype(_BF)
    wr_lo = (w_r - wr_hi.astype(_F32)).astype(_BF)
    x1, rows, rec, counts = _merge(heads, four, p, x, mod3, norm2_g, w_attn_out[0].astype(_BF),
                                   w_four_out[0].astype(_BF), w_out[0].astype(_BF),
                                   jnp.concatenate([wr_hi, wr_lo], axis=1), b_r, MERGE_TM)

    t = b * n
    slabs = rows.reshape(t, SUBLANES, LANES)
    rec = rec.reshape(t // MERGE_TM, SUBLANES, MERGE_TM)
    pos, pad_start, pad_len, tile_ea, tile_eb, tile_valid, n_tiles = _routing_tables(
        rec[:, 2, :].reshape(t).astype(jnp.int32), rec[:, 3, :].reshape(t).astype(jnp.int32),
        counts[0, :N_CLASSES].astype(jnp.int32), t, MOE_TM)

    slabs_sorted = _dispatch(pos, pad_start, pad_len, slabs, n_tiles * MOE_TM, DISPATCH_TM)
    y_sorted = _moe(tile_ea, tile_eb, tile_valid, slabs_sorted, w_exp_gate[0].astype(_BF),
                    w_exp_up[0].astype(_BF), w_exp_down[0].astype(_BF), MOE_TM)
    out = _combine(pos, y_sorted, x1.reshape(t, d), mod3, final_g[None, :], n, COMBINE_TM)
    return out.reshape(b, n, d)
```

```python
import functools
import math

import jax
import jax.numpy as jnp
import numpy as np
from jax import lax
from jax.experimental import pallas as pl
from jax.experimental.pallas import tpu as pltpu

D_MODEL = 1024
GRID_W = 64
EPS = 1e-6
N_HEADS = 8
HEAD_DIM = 64
HEAD_W = 2 * HEAD_DIM
ROT_FREQS = HEAD_DIM // 4
ROPE_BASE = 10000.0
F_GROUPS = 4
F_GROUP_DIM = 128
F_W = F_GROUPS * F_GROUP_DIM
N_GROUPS = 4
EXPERTS_PER_GROUP = 4
LOG2_EXPERTS_PER_GROUP = 2
N_EXPERTS = N_GROUPS * EXPERTS_PER_GROUP
PAIRS_PER_GROUP = EXPERTS_PER_GROUP * (EXPERTS_PER_GROUP - 1) // 2
N_CLASSES = N_GROUPS * PAIRS_PER_GROUP
EXPERT_HIDDEN = 512
LAM_INIT = 0.8 - 0.6 * math.exp(-0.3 * 0)

REF_Q, REF_K, REF_V, REF_F, REF_GA, REF_GF, REF_END = 0, 1024, 2048, 3072, 3584, 4608, 5632
P_Q, P_K, P_V, P_GA, P_GF, P_F, P_W = 0, 1024, 2048, 3072, 4096, 5120, 5632

LANES = 128
SUBLANES = 8
VMEM_LIMIT_BYTES = 56 * 1024 * 1024

PROJ_TM = 1024
PROJ_CW = 512
ATTN_TQ = 256
ATTN_SUM_ROWS = 16
ATTN_SCORE_BUFS = 3
ATTN_HEADS_PER_STEP = 2
FNET_LEVELS = 2
FNET_CHUNK = 256
MERGE_TM = 512
MOE_TM = 512
COMBINE_TM = 512
DISPATCH_TM = 2048
DMA_ISSUE_GROUP = 8
MOD_ROWS = 40
ROUTER_W = LANES
GROUP_LANE0 = N_EXPERTS
H2_SLAB_ROWS = D_MODEL // 2 // LANES
ROUTE_SLAB_ROW = H2_SLAB_ROWS

_BF = jnp.bfloat16
_F32 = jnp.float32


def _cparams(*sem):
    return pltpu.CompilerParams(dimension_semantics=sem, vmem_limit_bytes=VMEM_LIMIT_BYTES)


def _const_spec(shape):
    return pl.BlockSpec(shape, lambda *_: (0,) * len(shape), pipeline_mode=pl.Buffered(1))


def _rms(x):
    return x * lax.rsqrt(jnp.mean(x * x, axis=-1, keepdims=True) + EPS)


def _sigmoid(x):
    return 1.0 / (1.0 + jnp.exp(-x))


def _mod_kernel(cc_ref, w_ref, b_ref, o_ref):
    cc = cc_ref[...]
    s = cc * _sigmoid(cc)
    o_ref[...] = jnp.dot(s, w_ref[...], preferred_element_type=_F32,
                         precision=lax.Precision.HIGHEST) + b_ref[...]


def _modulation(cc, w_mod, b_mod):
    n = w_mod.shape[1]
    bn = D_MODEL
    return pl.pallas_call(
        _mod_kernel,
        grid=(n // bn,),
        in_specs=[pl.BlockSpec((MOD_ROWS, D_MODEL), lambda j: (0, 0)),
                  pl.BlockSpec((D_MODEL, bn), lambda j: (0, j)),
                  pl.BlockSpec((1, bn), lambda j: (0, j))],
        out_specs=pl.BlockSpec((MOD_ROWS, bn), lambda j: (0, j)),
        out_shape=jax.ShapeDtypeStruct((MOD_ROWS, n), _F32),
        compiler_params=_cparams("arbitrary"),
        name="modulation",
    )(cc, w_mod, b_mod)


def _rope(acc, cos_ref, sin_ref):
    cos = cos_ref[...]
    sin = sin_ref[...]
    lane = lax.broadcasted_iota(jnp.int32, (1, LANES), 1)
    first_half = (lane % (2 * ROT_FREQS)) < ROT_FREQS
    outs = []
    for s in range(acc.shape[1] // LANES):
        xs = acc[:, s * LANES:(s + 1) * LANES]
        partner = jnp.where(first_half,
                            pltpu.roll(xs, LANES - ROT_FREQS, 1),
                            pltpu.roll(xs, ROT_FREQS, 1))
        outs.append(xs * cos + partner * sin)
    return jnp.concatenate(outs, axis=1)


def _inproj_kernel(x_ref, mod_ref, g_ref, w_ref, cos_ref, sin_ref, o_ref, *, chunk_kinds):
    x = x_ref[0]
    shift = mod_ref[0, 0:1, :]
    scale = mod_ref[0, 1:2, :]
    h = (_rms(x) * g_ref[...]) * (1.0 + scale) + shift
    hb = h.astype(_BF)
    for j, kind in enumerate(chunk_kinds):
        cols = slice(j * PROJ_CW, (j + 1) * PROJ_CW)
        acc = jnp.dot(hb, w_ref[:, cols], preferred_element_type=_F32)
        if kind == "rope":
            acc = _rope(acc, cos_ref, sin_ref)
        elif kind == "sigmoid":
            acc = _sigmoid(acc)
        o_ref[0, :, cols] = acc.astype(_BF)


def _in_projection(x, mod3, mod_row_of_batch, g, w, cos_t, sin_t, chunk_kinds, tm):
    b, n, d = x.shape
    width = w.shape[1]
    assert width == len(chunk_kinds) * PROJ_CW and n % tm == 0
    return pl.pallas_call(
        functools.partial(_inproj_kernel, chunk_kinds=chunk_kinds),
        grid=(b, n // tm),
        in_specs=[pl.BlockSpec((1, tm, d), lambda bi, i: (bi, i, 0)),
                  pl.BlockSpec((1, 6, d), lambda bi, i: (mod_row_of_batch(bi), 0, 0)),
                  pl.BlockSpec((1, d), lambda bi, i: (0, 0)),
                  _const_spec((d, width)),
                  pl.BlockSpec((tm, LANES), lambda bi, i: (i, 0)),
                  pl.BlockSpec((tm, LANES), lambda bi, i: (i, 0))],
        out_specs=pl.BlockSpec((1, tm, width), lambda bi, i: (bi, i, 0)),
        out_shape=jax.ShapeDtypeStruct((b, n, width), _BF),
        compiler_params=_cparams("parallel", "arbitrary"),
        name="in_projection",
    )(x, mod3, g, w, cos_t, sin_t)


def _attn_kernel(lam_ref, q_ref, kl_ref, vl_ref, kc_ref, vc_ref, sg_ref, o_ref,
                 kcat_ref, vt_ref, *bufs, tq):
    nc, n = kc_ref.shape[1], kl_ref.shape[1]
    for hh in range(ATTN_HEADS_PER_STEP):
        cols = slice(hh * HEAD_W, (hh + 1) * HEAD_W)
        kcat_ref[hh, 0:nc, :] = kc_ref[0, :, cols]
        kcat_ref[hh, nc:nc + n, :] = kl_ref[0, :, cols]
        vt_ref[hh, 0:HEAD_W, 0:nc] = vc_ref[0, :, cols].astype(_F32).T.astype(_BF)
        vt_ref[hh, 0:HEAD_W, nc:nc + n] = vl_ref[0, :, cols].astype(_F32).T.astype(_BF)
        vt_ref[hh, HEAD_W:, :] = jnp.ones((ATTN_SUM_ROWS, nc + n), _BF)
    lane = lax.broadcasted_iota(jnp.int32, (1, HEAD_W), 1)
    map_lanes = (lane < HEAD_DIM, lane >= HEAD_DIM)
    nt = (((1,), (1,)), ((), ()))
    lam = lam_ref[0]
    post_scale = sg_ref[...] * (1.0 - LAM_INIT)
    n_sub = n // tq

    def scores(g):
        hh, j = divmod(g, n_sub)
        q = q_ref[0, j * tq:(j + 1) * tq, hh * HEAD_W:(hh + 1) * HEAD_W]
        for mp in range(2):
            qm = jnp.where(map_lanes[mp], q, jnp.zeros_like(q))
            bufs[g % ATTN_SCORE_BUFS][mp] = lax.dot_general(kcat_ref[hh], qm, nt,
                                                            preferred_element_type=_F32)

    def finish(g):
        hh, j = divmod(g, n_sub)
        outs = []
        for mp in range(2):
            s = bufs[g % ATTN_SCORE_BUFS][mp]
            e = jnp.exp2(s - jnp.max(s, axis=0, keepdims=True)).astype(_BF)
            r = jnp.dot(vt_ref[hh], e, preferred_element_type=_F32)
            outs.append(r[:HEAD_W, :] / r[HEAD_W:HEAD_W + 1, :])
        heads = (outs[0] - lam * outs[1]).T
        o_ref[0, j * tq:(j + 1) * tq, hh * HEAD_W:(hh + 1) * HEAD_W] = (
            _rms(heads) * post_scale).astype(_BF)

    total = ATTN_HEADS_PER_STEP * n_sub
    for g in range(ATTN_SCORE_BUFS - 1):
        scores(g)
    for g in range(total):
        if g + ATTN_SCORE_BUFS - 1 < total:
            scores(g + ATTN_SCORE_BUFS - 1)
        finish(g)


def _attention(lam, p, kvc, subln_g, tq):
    b, n, _ = p.shape
    nc = kvc.shape[1]
    w = ATTN_HEADS_PER_STEP * HEAD_W
    qb, kb, vb = P_Q // w, P_K // w, P_V // w
    seq = lambda blk: pl.BlockSpec((1, n, w), lambda bi, h: (bi, 0, blk + h))
    return pl.pallas_call(
        functools.partial(_attn_kernel, tq=tq),
        grid=(b, N_HEADS // ATTN_HEADS_PER_STEP),
        in_specs=[pl.BlockSpec(memory_space=pltpu.SMEM),
                  seq(qb), seq(kb), seq(vb),
                  pl.BlockSpec((1, nc, w), lambda bi, h: (bi, 0, h)),
                  pl.BlockSpec((1, nc, w), lambda bi, h: (bi, 0, N_HEADS // ATTN_HEADS_PER_STEP + h)),
                  pl.BlockSpec((1, HEAD_W), lambda bi, h: (0, 0))],
        out_specs=seq(0),
        out_shape=jax.ShapeDtypeStruct((b, n, N_HEADS * HEAD_W), _BF),
        scratch_shapes=[pltpu.VMEM((ATTN_HEADS_PER_STEP, nc + n, HEAD_W), _BF),
                        pltpu.VMEM((ATTN_HEADS_PER_STEP, HEAD_W + ATTN_SUM_ROWS, nc + n), _BF),
                        ] + [pltpu.VMEM((2, nc + n, tq), _F32)] * ATTN_SCORE_BUFS,
        compiler_params=_cparams("parallel", "arbitrary"),
        name="diff_attention",
    )(lam, p, p, p, kvc, kvc, subln_g)


def _bit_reverse(j, bits):
    return int(format(j, "0%db" % bits)[::-1], 2) if bits else 0


def _fnet_kernel(f_ref, cs_ch_ref, cs_seq_ref, *rest, n):
    tw_refs = rest[:2 * FNET_LEVELS]
    o_ref, zr_ref, zi_ref, rhs_ref = rest[2 * FNET_LEVELS:]
    for g in range(F_GROUPS):
        cols = slice(g * F_GROUP_DIM, (g + 1) * F_GROUP_DIM)
        t = jnp.dot(f_ref[0, :, cols], cs_ch_ref[...], preferred_element_type=_F32)
        zr_ref[:, cols] = t[:, :F_GROUP_DIM]
        zi_ref[:, cols] = t[:, F_GROUP_DIM:]
    for level in range(FNET_LEVELS):
        m = n >> level
        half = m // 2
        cos_ref, sin_ref = tw_refs[2 * level], tw_refs[2 * level + 1]
        for seg in range(1 << level):
            for c0 in range(0, half, FNET_CHUNK):
                rows_t = slice(seg * m + c0, seg * m + c0 + FNET_CHUNK)
                rows_b = slice(seg * m + half + c0, seg * m + half + c0 + FNET_CHUNK)
                tr, ti = zr_ref[rows_t, :], zi_ref[rows_t, :]
                br, bi = zr_ref[rows_b, :], zi_ref[rows_b, :]
                cw, sw = cos_ref[c0:c0 + FNET_CHUNK, :], sin_ref[c0:c0 + FNET_CHUNK, :]
                dr, di = tr - br, ti - bi
                zr_ref[rows_t, :] = tr + br
                zi_ref[rows_t, :] = ti + bi
                zr_ref[rows_b, :] = dr * cw + di * sw
                zi_ref[rows_b, :] = di * cw - dr * sw
    m = n >> FNET_LEVELS
    for j in range(1 << FNET_LEVELS):
        cols = slice(_bit_reverse(j, FNET_LEVELS) * F_W, (_bit_reverse(j, FNET_LEVELS) + 1) * F_W)
        rhs_ref[0:m, cols] = zr_ref[j * m:(j + 1) * m, :].astype(_BF)
        rhs_ref[m:2 * m, cols] = zi_ref[j * m:(j + 1) * m, :].astype(_BF)
    ortho = 1.0 / math.sqrt(n * F_GROUP_DIM)
    y = jnp.dot(cs_seq_ref[...], rhs_ref[...], preferred_element_type=_F32)
    o_ref[0] = (y * ortho).reshape(m, 1 << FNET_LEVELS, F_W).reshape(n, F_W).astype(_BF)


def _fnet(p, cs_ch, cs_seq, twiddles):
    b, n, _ = p.shape
    m = n >> FNET_LEVELS
    wide = F_W << FNET_LEVELS
    return pl.pallas_call(
        functools.partial(_fnet_kernel, n=n),
        grid=(b,),
        in_specs=[pl.BlockSpec((1, n, F_W), lambda bi: (bi, 0, P_F // F_W)),
                  _const_spec((F_GROUP_DIM, 2 * F_GROUP_DIM)),
                  _const_spec((m, 2 * m))] + [_const_spec(t.shape) for t in twiddles],
        out_specs=pl.BlockSpec((1, n, F_W), lambda bi: (bi, 0, 0)),
        out_shape=jax.ShapeDtypeStruct((b, n, F_W), _BF),
        scratch_shapes=[pltpu.VMEM((n, F_W), _F32), pltpu.VMEM((n, F_W), _F32),
                        pltpu.VMEM((2 * m, wide), _BF)],
        compiler_params=_cparams("arbitrary"),
        name="fnet_dft",
    )(p, cs_ch, cs_seq, *twiddles)


def _pack_bf16_pair(hi, lo):
    hi_bits = pltpu.bitcast(hi.astype(_BF).astype(_F32), jnp.uint32)
    lo_bits = pltpu.bitcast(lo.astype(_BF).astype(_F32), jnp.uint32)
    return hi_bits | lax.shift_right_logical(lo_bits, jnp.uint32(16))


def _unpack_bf16_pair(packed):
    hi = pltpu.bitcast(packed & jnp.uint32(0xFFFF0000), _F32)
    lo = pltpu.bitcast(lax.shift_left(packed, jnp.uint32(16)), _F32)
    return hi, lo


def _route(logits, carry):
    rows = logits.shape[0]
    lane = lax.broadcasted_iota(jnp.int32, logits.shape, 1)
    neg = jnp.float32(-jnp.inf)
    big = jnp.int32(ROUTER_W)

    def first_argmax(v):
        m = jnp.max(v, axis=-1, keepdims=True)
        idx = jnp.min(jnp.where(v == m, lane, big), axis=-1, keepdims=True)
        return m, idx

    lg = jnp.where((lane >= GROUP_LANE0) & (lane < GROUP_LANE0 + N_GROUPS), logits, neg)
    mg, ig = first_argmax(lg)
    w_grp = 1.0 / jnp.sum(jnp.exp(lg - mg), axis=-1, keepdims=True)
    g_sel = ig - GROUP_LANE0
    le = jnp.where((lane < N_EXPERTS)
                   & (jnp.right_shift(lane, LOG2_EXPERTS_PER_GROUP) == g_sel), logits, neg)
    v1, i1 = first_argmax(le)
    le2 = jnp.where(lane == i1, neg, le)
    v2, i2 = first_argmax(le2)
    e2 = jnp.exp(v2 - v1)
    w1 = w_grp / (1.0 + e2)
    w2 = w_grp * e2 / (1.0 + e2)

    first_is_lower = i1 < i2
    gate_a = jnp.where(first_is_lower, w1, w2)
    gate_b = jnp.where(first_is_lower, w2, w1)
    la = jnp.minimum(i1, i2) - g_sel * EXPERTS_PER_GROUP
    lb = jnp.maximum(i1, i2) - g_sel * EXPERTS_PER_GROUP
    pair = jnp.right_shift(la * (2 * EXPERTS_PER_GROUP - 1 - la), 1) + lb - la - 1
    cls = g_sel * PAIRS_PER_GROUP + pair

    onehot = (lane == cls).astype(_F32)
    r_i = lax.broadcasted_iota(jnp.int32, (rows, rows), 0)
    c_i = lax.broadcasted_iota(jnp.int32, (rows, rows), 1)
    earlier = (c_i < r_i).astype(_BF)
    before = jnp.dot(earlier, onehot.astype(_BF), preferred_element_type=_F32) + carry
    rank = jnp.sum(onehot * before, axis=-1, keepdims=True)
    new_carry = carry + jnp.sum(onehot, axis=0, keepdims=True)

    route = (jnp.where(lane == 0, gate_a, 0.0) + jnp.where(lane == 1, gate_b, 0.0)
             + jnp.where(lane == 2, cls.astype(_F32), 0.0) + jnp.where(lane == 3, rank, 0.0))
    return route, new_carry


def _merge_kernel(hd_ref, fo_ref, ga_ref, gf_ref, x_ref, mod_ref, g2n_ref, wao_ref, wfo_ref,
                  wo_ref, wrhl_ref, br_ref, x1_ref, rows_ref, rec_ref, counts_ref, carry_ref):
    @pl.when((pl.program_id(0) == 0) & (pl.program_id(1) == 0))
    def _():
        carry_ref[...] = jnp.zeros_like(carry_ref)

    a = jnp.dot(hd_ref[0], wao_ref[...], preferred_element_type=_F32)
    ff = jnp.dot(fo_ref[0], wfo_ref[...], preferred_element_type=_F32)
    y = ga_ref[0].astype(_F32) * a + gf_ref[0].astype(_F32) * ff
    mix = jnp.dot(y.astype(_BF), wo_ref[...], preferred_element_type=_F32)
    x1 = x_ref[0] + mod_ref[0, 2:3, :] * mix
    x1_ref[0] = x1
    h2 = (_rms(x1) * g2n_ref[...]) * (1.0 + mod_ref[0, 4:5, :]) + mod_ref[0, 3:4, :]
    h2_hi = h2.astype(_BF)
    h2_lo = (h2 - h2_hi.astype(_F32)).astype(_BF)
    hi_both = jnp.dot(h2_hi, wrhl_ref[...], preferred_element_type=_F32)
    logits = (hi_both[:, :ROUTER_W] + hi_both[:, ROUTER_W:]
              + jnp.dot(h2_lo, wrhl_ref[:, :ROUTER_W], preferred_element_type=_F32)) + br_ref[...]
    route, new_carry = _route(logits, carry_ref[...])
    carry_ref[...] = new_carry
    counts_ref[...] = new_carry
    rec_ref[0, 0] = route.T[0:SUBLANES, :]
    half = D_MODEL // 2
    tm = h2.shape[0]
    packed = _pack_bf16_pair(h2[:, :half], h2[:, half:])
    rows_ref[0, :, 0:H2_SLAB_ROWS, :] = packed.reshape(tm, H2_SLAB_ROWS, LANES)
    rows_ref[0, :, ROUTE_SLAB_ROW:ROUTE_SLAB_ROW + 1, :] = (
        pltpu.bitcast(route, jnp.uint32).reshape(tm, 1, LANES))
    rows_ref[0, :, ROUTE_SLAB_ROW + 1:SUBLANES, :] = jnp.zeros(
        (tm, SUBLANES - ROUTE_SLAB_ROW - 1, LANES), jnp.uint32)


def _merge(heads, four, p, x, mod3, norm2_g, w_ao, w_fo, w_o, wr_hilo, b_r, tm):
    b, n, d = x.shape
    tok = lambda w: pl.BlockSpec((1, tm, w), lambda bi, i: (bi, i, 0))
    return pl.pallas_call(
        _merge_kernel,
        grid=(b, n // tm),
        in_specs=[tok(d), tok(F_W),
                  pl.BlockSpec((1, tm, d), lambda bi, i: (bi, i, P_GA // D_MODEL)),
                  pl.BlockSpec((1, tm, d), lambda bi, i: (bi, i, P_GF // D_MODEL)),
                  tok(d),
                  pl.BlockSpec((1, 6, d), lambda bi, i: (bi, 0, 0)),
                  pl.BlockSpec((1, d), lambda bi, i: (0, 0)),
                  _const_spec((d, d)), _const_spec((F_W, d)), _const_spec((d, d)),
                  _const_spec((d, 2 * ROUTER_W)),
                  pl.BlockSpec((1, ROUTER_W), lambda bi, i: (0, 0))],
        out_specs=[tok(d),
                   pl.BlockSpec((1, tm, SUBLANES, LANES), lambda bi, i: (bi, i, 0, 0)),
                   pl.BlockSpec((1, 1, SUBLANES, tm), lambda bi, i: (bi, i, 0, 0)),
                   pl.BlockSpec((1, ROUTER_W), lambda bi, i: (0, 0))],
        out_shape=[jax.ShapeDtypeStruct((b, n, d), _F32),
                   jax.ShapeDtypeStruct((b, n, SUBLANES, LANES), jnp.uint32),
                   jax.ShapeDtypeStruct((b, n // tm, SUBLANES, tm), _F32),
                   jax.ShapeDtypeStruct((1, ROUTER_W), _F32)],
        scratch_shapes=[pltpu.VMEM((1, ROUTER_W), _F32)],
        compiler_params=_cparams("arbitrary", "arbitrary"),
        name="merge_router",
    )(heads, four, p, p, x, mod3, norm2_g, w_ao, w_fo, w_o, wr_hilo, b_r)


def _issue_slab_copies(n, make_copy, slot_of):
    def group(g, c):
        r0 = g * DMA_ISSUE_GROUP
        slots = [slot_of(r0 + k) for k in range(DMA_ISSUE_GROUP)]
        for k in range(DMA_ISSUE_GROUP):
            make_copy(r0 + k, slots[k]).start(priority=k % 2)
        return c

    lax.fori_loop(0, n // DMA_ISSUE_GROUP, group, 0)


def _dispatch_kernel(pos_ref, pad_start_ref, pad_len_ref, src_ref, dst_ref, zero_ref, sem, pad_sem,
                     *, tm):
    base = pl.program_id(0) * tm

    @pl.when(pl.program_id(0) == 0)
    def _():
        zero_ref[...] = jnp.zeros_like(zero_ref)
        for c in range(N_CLASSES + 1):
            def start(r, carry, c=c):
                pltpu.make_async_copy(zero_ref, dst_ref.at[pad_start_ref[c] + r], pad_sem).start()
                return carry

            def drain(r, carry):
                pltpu.make_async_copy(zero_ref, dst_ref.at[0], pad_sem).wait()
                return carry

            lax.fori_loop(0, pad_len_ref[c], start, 0)
            lax.fori_loop(0, pad_len_ref[c], drain, 0)

    _issue_slab_copies(
        tm,
        lambda r, slot: pltpu.make_async_copy(src_ref.at[r], dst_ref.at[slot], sem),
        lambda r: pos_ref[base + r])
    pltpu.make_async_copy(src_ref, dst_ref.at[pl.ds(0, tm)], sem).wait()


def _dispatch(pos, pad_start, pad_len, slabs, n_sorted, tm):
    t = slabs.shape[0]
    return pl.pallas_call(
        functools.partial(_dispatch_kernel, tm=tm),
        grid_spec=pltpu.PrefetchScalarGridSpec(
            num_scalar_prefetch=3,
            grid=(t // tm,),
            in_specs=[pl.BlockSpec((tm, SUBLANES, LANES), lambda i, *_: (i, 0, 0))],
            out_specs=pl.BlockSpec(memory_space=pl.ANY),
            scratch_shapes=[pltpu.VMEM((SUBLANES, LANES), slabs.dtype),
                            pltpu.SemaphoreType.DMA(()), pltpu.SemaphoreType.DMA(())]),
        out_shape=jax.ShapeDtypeStruct((n_sorted,) + slabs.shape[1:], slabs.dtype),
        compiler_params=_cparams("arbitrary"),
        name="moe_dispatch",
    )(pos, pad_start, pad_len, slabs)


def _moe_kernel(ea_ref, eb_ref, valid_ref, slabs_ref, w1a_ref, w3a_ref, w2a_ref,
                w1b_ref, w3b_ref, w2b_ref, y_ref):
    del ea_ref, eb_ref
    i = pl.program_id(0)
    tm = slabs_ref.shape[0]

    @pl.when(valid_ref[i] != 0)
    def _():
        packed = slabs_ref[:, 0:H2_SLAB_ROWS, :].reshape(tm, H2_SLAB_ROWS * LANES)
        hi, lo = _unpack_bf16_pair(packed)
        t = jnp.concatenate([hi.astype(_BF), lo.astype(_BF)], axis=1)
        route = pltpu.bitcast(slabs_ref[:, ROUTE_SLAB_ROW, :], _F32)
        y = None
        for slot, (w1_ref, w3_ref, w2_ref) in enumerate(((w1a_ref, w3a_ref, w2a_ref),
                                                          (w1b_ref, w3b_ref, w2b_ref))):
            a = jnp.dot(t, w1_ref[0], preferred_element_type=_F32)
            u = jnp.dot(t, w3_ref[0], preferred_element_type=_F32)
            hid = ((a * _sigmoid(a)) * u * route[:, slot:slot + 1]).astype(_BF)
            part = jnp.dot(hid, w2_ref[0], preferred_element_type=_F32)
            y = part if y is None else y + part
        y_ref[...] = y.reshape(tm, SUBLANES, LANES)

    @pl.when(valid_ref[i] == 0)
    def _():
        y_ref[...] = jnp.zeros_like(y_ref)


def _moe(tile_ea, tile_eb, tile_valid, slabs_sorted, w1, w3, w2, tm):
    n_sorted = slabs_sorted.shape[0]
    d, hdn = w1.shape[1], w1.shape[2]
    assert d == SUBLANES * LANES
    pick = lambda which, a, b_: pl.BlockSpec(
        (1, a, b_), lambda i, ea, eb, valid: ((ea, eb)[which][i], 0, 0))
    slab_tile = pl.BlockSpec((tm, SUBLANES, LANES), lambda i, ea, eb, valid: (i, 0, 0))
    return pl.pallas_call(
        _moe_kernel,
        grid_spec=pltpu.PrefetchScalarGridSpec(
            num_scalar_prefetch=3,
            grid=(n_sorted // tm,),
            in_specs=[slab_tile,
                      pick(0, d, hdn), pick(0, d, hdn), pick(0, hdn, d),
                      pick(1, d, hdn), pick(1, d, hdn), pick(1, hdn, d)],
            out_specs=slab_tile),
        out_shape=jax.ShapeDtypeStruct((n_sorted, SUBLANES, LANES), _F32),
        compiler_params=_cparams("arbitrary"),
        name="moe_experts",
    )(tile_ea, tile_eb, tile_valid, slabs_sorted, w1, w3, w2, w1, w3, w2)


def _combine_kernel(pos_ref, y_hbm_ref, x1_ref, mod_ref, fg_ref, o_ref, ybuf_ref, sems, *, tm):
    i = pl.program_id(0)
    n_steps = pl.num_programs(0)

    def issue(tile, slot):
        _issue_slab_copies(
            tm,
            lambda r, src: pltpu.make_async_copy(y_hbm_ref.at[src], ybuf_ref.at[slot, r],
                                                 sems.at[slot]),
            lambda r: pos_ref[tile * tm + r])

    @pl.when(i == 0)
    def _():
        issue(0, 0)

    @pl.when(i + 1 < n_steps)
    def _():
        issue(i + 1, (i + 1) % 2)

    slot = i % 2
    pltpu.make_async_copy(y_hbm_ref.at[pl.ds(0, tm)], ybuf_ref.at[slot], sems.at[slot]).wait()
    y = ybuf_ref[slot].reshape(tm, SUBLANES * LANES)
    xo = x1_ref[...] + mod_ref[0, 5:6, :] * y
    o_ref[...] = _rms(xo) * fg_ref[...]


def _combine(pos, y_sorted, x1, mod3, final_g, seq, tm):
    t, d = x1.shape
    tiles_per_batch = seq // tm
    return pl.pallas_call(
        functools.partial(_combine_kernel, tm=tm),
        grid_spec=pltpu.PrefetchScalarGridSpec(
            num_scalar_prefetch=1,
            grid=(t // tm,),
            in_specs=[pl.BlockSpec(memory_space=pl.ANY),
                      pl.BlockSpec((tm, d), lambda i, pos: (i, 0)),
                      pl.BlockSpec((1, 6, d), lambda i, pos: (i // tiles_per_batch, 0, 0)),
                      pl.BlockSpec((1, d), lambda i, pos: (0, 0))],
            out_specs=pl.BlockSpec((tm, d), lambda i, pos: (i, 0)),
            scratch_shapes=[pltpu.VMEM((2, tm, SUBLANES, LANES), _F32),
                            pltpu.SemaphoreType.DMA((2,))]),
        out_shape=jax.ShapeDtypeStruct((t, d), _F32),
        compiler_params=_cparams("arbitrary"),
        name="moe_combine",
    )(pos, y_sorted, x1, mod3, final_g)


def _routing_tables(route_cls, route_rank, counts, n_tokens, tm):
    sizes = ((counts + (tm - 1)) // tm) * tm
    ends = jnp.cumsum(sizes)
    starts = ends - sizes
    n_tiles = n_tokens // tm + N_CLASSES
    tile_start = jnp.arange(n_tiles, dtype=jnp.int32) * tm
    pos = route_rank
    tile_cls = jnp.zeros((n_tiles,), jnp.int32)
    for c in range(N_CLASSES):
        pos = pos + jnp.where(route_cls == c, starts[c], 0)
        tile_cls = tile_cls + (tile_start >= ends[c]).astype(jnp.int32)
    tile_cls = jnp.minimum(tile_cls, N_CLASSES - 1)
    tile_valid = (tile_start < ends[-1]).astype(jnp.int32)
    pairs = [(a, b_) for a in range(EXPERTS_PER_GROUP) for b_ in range(a + 1, EXPERTS_PER_GROUP)]
    grp, pair = tile_cls // PAIRS_PER_GROUP, tile_cls % PAIRS_PER_GROUP
    tile_ea = grp * EXPERTS_PER_GROUP
    tile_eb = grp * EXPERTS_PER_GROUP
    for k, (a, b_) in enumerate(pairs):
        tile_ea = tile_ea + jnp.where(pair == k, a, 0)
        tile_eb = tile_eb + jnp.where(pair == k, b_, 0)
    pad_start = jnp.concatenate([starts + counts, ends[-1:]]).astype(jnp.int32)
    pad_len = jnp.concatenate([sizes - counts, n_tiles * tm - ends[-1:]]).astype(jnp.int32)
    return (pos.astype(jnp.int32), pad_start, pad_len, tile_ea.astype(jnp.int32),
            tile_eb.astype(jnp.int32), tile_valid, n_tiles)


def _rope_tables(n):
    inv = (1.0 / (ROPE_BASE ** (np.arange(ROT_FREQS, dtype=np.float32) / ROT_FREQS))).astype(np.float32)
    pos = np.arange(n)
    row = (pos // GRID_W).astype(np.float32)[:, None] * inv[None, :]
    col = (pos % GRID_W).astype(np.float32)[:, None] * inv[None, :]
    cos64 = np.concatenate([np.cos(row), np.cos(row), np.cos(col), np.cos(col)], axis=1)
    sin64 = np.concatenate([-np.sin(row), np.sin(row), -np.sin(col), np.sin(col)], axis=1)
    tile = lambda a: np.tile(a.astype(np.float32), (1, LANES // HEAD_DIM))
    return jnp.asarray(tile(cos64)), jnp.asarray(tile(sin64))


def _dft_cos_sin(n):
    k = np.arange(n, dtype=np.int64)
    ang = (2.0 * np.pi / n) * ((k[:, None] * k[None, :]) % n).astype(np.float64)
    return np.cos(ang), np.sin(ang)


def _dft_tables(n):
    c_ch, s_ch = _dft_cos_sin(F_GROUP_DIM)
    c_seq, s_seq = _dft_cos_sin(n >> FNET_LEVELS)
    cs_ch = np.concatenate([c_ch, -s_ch], axis=1).astype(np.float32)
    cs_seq = np.concatenate([c_seq, s_seq], axis=1).astype(np.float32)
    twiddles = []
    for level in range(FNET_LEVELS):
        m = n >> level
        ang = (2.0 * np.pi / m) * np.arange(m // 2, dtype=np.float64)[:, None]
        for tab in (np.cos(ang), np.sin(ang)):
            twiddles.append(jnp.asarray(np.broadcast_to(tab, (m // 2, F_W)).astype(np.float32)))
    return jnp.asarray(cs_ch.astype(_BF)), jnp.asarray(cs_seq.astype(_BF)), twiddles


def kernel(x, c, ctx, c_ctx, w_mod, b_mod, norm1_g, norm2_g, w_in, lam_q1, lam_k1, lam_q2, lam_k2,
           subln_g, w_attn_out, w_four_out, w_out, w_router_group, b_router_group, w_router_expert,
           b_router_expert, w_exp_gate, w_exp_up, w_exp_down, final_g):
    b, n, d = x.shape
    assert w_mod.shape[0] == 1, "depth-1 stack"
    assert b + 1 <= MOD_ROWS

    cc = jnp.concatenate([c, c_ctx[None, :], jnp.zeros((MOD_ROWS - b - 1, d), _F32)], axis=0)
    mod3 = _modulation(cc, w_mod[0], b_mod).reshape(MOD_ROWS, 6, d)

    lam = (jnp.exp(jnp.sum(lam_q1[0] * lam_k1[0])) - jnp.exp(jnp.sum(lam_q2[0] * lam_k2[0]))
           + LAM_INIT).reshape(1).astype(_F32)

    w = w_in[0]
    scale = HEAD_DIM ** -0.5 * math.log2(math.e)
    w_lat = jnp.concatenate([w[:, REF_Q:REF_K] * scale, w[:, REF_K:REF_F],
                             w[:, REF_GA:REF_END], w[:, REF_F:REF_GA]], axis=1).astype(_BF)
    w_ctx = w[:, REF_K:REF_F].astype(_BF)
    lat_kinds = ("rope",) * ((P_V - P_Q) // PROJ_CW) + ("plain",) * ((P_GA - P_V) // PROJ_CW) \
        + ("sigmoid",) * ((P_F - P_GA) // PROJ_CW) + ("plain",) * ((P_W - P_F) // PROJ_CW)
    ctx_kinds = ("plain",) * (w_ctx.shape[1] // PROJ_CW)

    cos_t, sin_t = _rope_tables(n)
    p = _in_projection(x, mod3, lambda bi: bi, norm1_g, w_lat, cos_t, sin_t, lat_kinds, PROJ_TM)
    kvc = _in_projection(ctx, mod3, lambda bi: b, norm1_g, w_ctx, cos_t, sin_t, ctx_kinds,
                         ctx.shape[1])

    heads = _attention(lam, p, kvc, subln_g, ATTN_TQ)
    cs_ch, cs_seq, twiddles = _dft_tables(n)
    four = _fnet(p, cs_ch, cs_seq, twiddles)

    w_r = jnp.concatenate([w_router_expert[0], w_router_group[0],
                           jnp.zeros((d, ROUTER_W - N_EXPERTS - N_GROUPS), _F32)], axis=1)
    b_r = jnp.concatenate([b_router_expert[0], b_router_group[0],
                           jnp.zeros((ROUTER_W - N_EXPERTS - N_GROUPS,), _F32)])[None, :]
    wr_hi = w_r.astype(_BF)
    wr_lo = (w_r - wr_hi.astype(_F32)).astype(_BF)
    x1, rows, rec, counts = _merge(heads, four, p, x, mod3, norm2_g, w_attn_out[0].astype(_BF),
                                   w_four_out[0].astype(_BF), w_out[0].astype(_BF),
                                   jnp.concatenate([wr_hi, wr_lo], axis=1), b_r, MERGE_TM)

    t = b * n
    slabs = rows.reshape(t, SUBLANES, LANES)
    rec = rec.reshape(t // MERGE_TM, SUBLANES, MERGE_TM)
    pos, pad_start, pad_len, tile_ea, tile_eb, tile_valid, n_tiles = _routing_tables(
        rec[:, 2, :].reshape(t).astype(jnp.int32), rec[:, 3, :].reshape(t).astype(jnp.int32),
        counts[0, :N_CLASSES].astype(jnp.int32), t, MOE_TM)

    slabs_sorted = _dispatch(pos, pad_start, pad_len, slabs, n_tiles * MOE_TM, DISPATCH_TM)
    y_sorted = _moe(tile_ea, tile_eb, tile_valid, slabs_sorted, w_exp_gate[0].astype(_BF),
                    w_exp_up[0].astype(_BF), w_exp_down[0].astype(_BF), MOE_TM)
    out = _combine(pos, y_sorted, x1.reshape(t, d), mod3, final_g[None, :], n, COMBINE_TM)
    return out.reshape(b, n, d)
```

```python
import functools
import math

import jax
import jax.numpy as jnp
import numpy as np
from jax import lax
from jax.experimental import pallas as pl
from jax.experimental.pallas import tpu as pltpu

D_MODEL = 1024
GRID_W = 64
EPS = 1e-6
N_HEADS = 8
HEAD_DIM = 64
HEAD_W = 2 * HEAD_DIM
ROT_FREQS = HEAD_DIM // 4
ROPE_BASE = 10000.0
F_GROUPS = 4
F_GROUP_DIM = 128
F_W = F_GROUPS * F_GROUP_DIM
N_GROUPS = 4
EXPERTS_PER_GROUP = 4
LOG2_EXPERTS_PER_GROUP = 2
N_EXPERTS = N_GROUPS * EXPERTS_PER_GROUP
PAIRS_PER_GROUP = EXPERTS_PER_GROUP * (EXPERTS_PER_GROUP - 1) // 2
N_CLASSES = N_GROUPS * PAIRS_PER_GROUP
EXPERT_HIDDEN = 512
LAM_INIT = 0.8 - 0.6 * math.exp(-0.3 * 0)

REF_Q, REF_K, REF_V, REF_F, REF_GA, REF_GF, REF_END = 0, 1024, 2048, 3072, 3584, 4608, 5632
P_Q, P_K, P_V, P_GA, P_GF, P_F, P_W = 0, 1024, 2048, 3072, 4096, 5120, 5632

LANES = 128
SUBLANES = 8
VMEM_LIMIT_BYTES = 56 * 1024 * 1024

PROJ_TM = 1024
PROJ_CW = 512
ATTN_TQ = 256
ATTN_SUM_ROWS = 16
ATTN_SCORE_BUFS = 3
ATTN_HEADS_PER_STEP = 2
FNET_LEVELS = 2
FNET_CHUNK = 256
MERGE_TM = 512
MOE_TM = 512
COMBINE_TM = 512
DISPATCH_TM = 2048
DMA_ISSUE_GROUP = 8
MOD_ROWS = 40
ROUTER_W = LANES
GROUP_LANE0 = N_EXPERTS
H2_SLAB_ROWS = D_MODEL // 2 // LANES
ROUTE_SLAB_ROW = H2_SLAB_ROWS

_BF = jnp.bfloat16
_F32 = jnp.float32


def _cparams(*sem):
    return pltpu.CompilerParams(dimension_semantics=sem, vmem_limit_bytes=VMEM_LIMIT_BYTES)


def _const_spec(shape):
    return pl.BlockSpec(shape, lambda *_: (0,) * len(shape), pipeline_mode=pl.Buffered(1))


def _rms(x):
    return x * lax.rsqrt(jnp.mean(x * x, axis=-1, keepdims=True) + EPS)


def _sigmoid(x):
    return 1.0 / (1.0 + jnp.exp(-x))


def _mod_kernel(cc_ref, w_ref, b_ref, o_ref):
    cc = cc_ref[...]
    s = cc * _sigmoid(cc)
    o_ref[...] = jnp.dot(s, w_ref[...], preferred_element_type=_F32,
                         precision=lax.Precision.HIGHEST) + b_ref[...]


def _modulation(cc, w_mod, b_mod):
    n = w_mod.shape[1]
    bn = D_MODEL
    return pl.pallas_call(
        _mod_kernel,
        grid=(n // bn,),
        in_specs=[pl.BlockSpec((MOD_ROWS, D_MODEL), lambda j: (0, 0)),
                  pl.BlockSpec((D_MODEL, bn), lambda j: (0, j)),
                  pl.BlockSpec((1, bn), lambda j: (0, j))],
        out_specs=pl.BlockSpec((MOD_ROWS, bn), lambda j: (0, j)),
        out_shape=jax.ShapeDtypeStruct((MOD_ROWS, n), _F32),
        compiler_params=_cparams("arbitrary"),
        name="modulation",
    )(cc, w_mod, b_mod)


def _rope(acc, cos_ref, sin_ref):
    cos = cos_ref[...]
    sin = sin_ref[...]
    lane = lax.broadcasted_iota(jnp.int32, (1, LANES), 1)
    first_half = (lane % (2 * ROT_FREQS)) < ROT_FREQS
    outs = []
    for s in range(acc.shape[1] // LANES):
        xs = acc[:, s * LANES:(s + 1) * LANES]
        partner = jnp.where(first_half,
                            pltpu.roll(xs, LANES - ROT_FREQS, 1),
                            pltpu.roll(xs, ROT_FREQS, 1))
        outs.append(xs * cos + partner * sin)
    return jnp.concatenate(outs, axis=1)


def _inproj_kernel(x_ref, mod_ref, g_ref, w_ref, cos_ref, sin_ref, o_ref, *, chunk_kinds):
    x = x_ref[0]
    shift = mod_ref[0, 0:1, :]
    scale = mod_ref[0, 1:2, :]
    h = (_rms(x) * g_ref[...]) * (1.0 + scale) + shift
    hb = h.astype(_BF)
    for j, kind in enumerate(chunk_kinds):
        cols = slice(j * PROJ_CW, (j + 1) * PROJ_CW)
        acc = jnp.dot(hb, w_ref[:, cols], preferred_element_type=_F32)
        if kind == "rope":
            acc = _rope(acc, cos_ref, sin_ref)
        elif kind == "sigmoid":
            acc = _sigmoid(acc)
        o_ref[0, :, cols] = acc.astype(_BF)


def _in_projection(x, mod3, mod_row_of_batch, g, w, cos_t, sin_t, chunk_kinds, tm):
    b, n, d = x.shape
    width = w.shape[1]
    assert width == len(chunk_kinds) * PROJ_CW and n % tm == 0
    return pl.pallas_call(
        functools.partial(_inproj_kernel, chunk_kinds=chunk_kinds),
        grid=(b, n // tm),
        in_specs=[pl.BlockSpec((1, tm, d), lambda bi, i: (bi, i, 0)),
                  pl.BlockSpec((1, 6, d), lambda bi, i: (mod_row_of_batch(bi), 0, 0)),
                  pl.BlockSpec((1, d), lambda bi, i: (0, 0)),
                  _const_spec((d, width)),
                  pl.BlockSpec((tm, LANES), lambda bi, i: (i, 0)),
                  pl.BlockSpec((tm, LANES), lambda bi, i: (i, 0))],
        out_specs=pl.BlockSpec((1, tm, width), lambda bi, i: (bi, i, 0)),
        out_shape=jax.ShapeDtypeStruct((b, n, width), _BF),
        compiler_params=_cparams("parallel", "arbitrary"),
        name="in_projection",
    )(x, mod3, g, w, cos_t, sin_t)


def _attn_kernel(lam_ref, q_ref, kl_ref, vl_ref, kc_ref, vc_ref, sg_ref, o_ref,
                 kcat_ref, vt_ref, *bufs, tq):
    nc, n = kc_ref.shape[1], kl_ref.shape[1]
    for hh in range(ATTN_HEADS_PER_STEP):
        cols = slice(hh * HEAD_W, (hh + 1) * HEAD_W)
        kcat_ref[hh, 0:nc, :] = kc_ref[0, :, cols]
        kcat_ref[hh, nc:nc + n, :] = kl_ref[0, :, cols]
        vt_ref[hh, 0:HEAD_W, 0:nc] = vc_ref[0, :, cols].astype(_F32).T.astype(_BF)
        vt_ref[hh, 0:HEAD_W, nc:nc + n] = vl_ref[0, :, cols].astype(_F32).T.astype(_BF)
        vt_ref[hh, HEAD_W:, :] = jnp.ones((ATTN_SUM_ROWS, nc + n), _BF)
    lane = lax.broadcasted_iota(jnp.int32, (1, HEAD_W), 1)
    map_lanes = (lane < HEAD_DIM, lane >= HEAD_DIM)
    nt = (((1,), (1,)), ((), ()))
    lam = lam_ref[0]
    post_scale = sg_ref[...] * (1.0 - LAM_INIT)
    n_sub = n // tq

    def scores(g):
        hh, j = divmod(g, n_sub)
        q = q_ref[0, j * tq:(j + 1) * tq, hh * HEAD_W:(hh + 1) * HEAD_W]
        for mp in range(2):
            qm = jnp.where(map_lanes[mp], q, jnp.zeros_like(q))
            bufs[g % ATTN_SCORE_BUFS][mp] = lax.dot_general(kcat_ref[hh], qm, nt,
                                                            preferred_element_type=_F32)

    def finish(g):
        hh, j = divmod(g, n_sub)
        outs = []
        for mp in range(2):
            s = bufs[g % ATTN_SCORE_BUFS][mp]
            e = jnp.exp2(s - jnp.max(s, axis=0, keepdims=True)).astype(_BF)
            r = jnp.dot(vt_ref[hh], e, preferred_element_type=_F32)
            outs.append(r[:HEAD_W, :] / r[HEAD_W:HEAD_W + 1, :])
        heads = (outs[0] - lam * outs[1]).T
        o_ref[0, j * tq:(j + 1) * tq, hh * HEAD_W:(hh + 1) * HEAD_W] = (
            _rms(heads) * post_scale).astype(_BF)

    total = ATTN_HEADS_PER_STEP * n_sub
    for g in range(ATTN_SCORE_BUFS - 1):
        scores(g)
    for g in range(total):
        if g + ATTN_SCORE_BUFS - 1 < total:
            scores(g + ATTN_SCORE_BUFS - 1)
        finish(g)


def _attention(lam, p, kvc, subln_g, tq):
    b, n, _ = p.shape
    nc = kvc.shape[1]
    w = ATTN_HEADS_PER_STEP * HEAD_W
    qb, kb, vb = P_Q // w, P_K // w, P_V // w
    seq = lambda blk: pl.BlockSpec((1, n, w), lambda bi, h: (bi, 0, blk + h))
    return pl.pallas_call(
        functools.partial(_attn_kernel, tq=tq),
        grid=(b, N_HEADS // ATTN_HEADS_PER_STEP),
        in_specs=[pl.BlockSpec(memory_space=pltpu.SMEM),
                  seq(qb), seq(kb), seq(vb),
                  pl.BlockSpec((1, nc, w), lambda bi, h: (bi, 0, h)),
                  pl.BlockSpec((1, nc, w), lambda bi, h: (bi, 0, N_HEADS // ATTN_HEADS_PER_STEP + h)),
                  pl.BlockSpec((1, HEAD_W), lambda bi, h: (0, 0))],
        out_specs=seq(0),
        out_shape=jax.ShapeDtypeStruct((b, n, N_HEADS * HEAD_W), _BF),
        scratch_shapes=[pltpu.VMEM((ATTN_HEADS_PER_STEP, nc + n, HEAD_W), _BF),
                        pltpu.VMEM((ATTN_HEADS_PER_STEP, HEAD_W + ATTN_SUM_ROWS, nc + n), _BF),
                        ] + [pltpu.VMEM((2, nc + n, tq), _F32)] * ATTN_SCORE_BUFS,
        compiler_params=_cparams("parallel", "arbitrary"),
        name="diff_attention",
    )(lam, p, p, p, kvc, kvc, subln_g)


def _bit_reverse(j, bits):
    return int(format(j, "0%db" % bits)[::-1], 2) if bits else 0


def _fnet_kernel(f_ref, cs_ch_ref, cs_seq_ref, *rest, n):
    tw_refs = rest[:2 * FNET_LEVELS]
    o_ref, zr_ref, zi_ref, rhs_ref = rest[2 * FNET_LEVELS:]
    for g in range(F_GROUPS):
        cols = slice(g * F_GROUP_DIM, (g + 1) * F_GROUP_DIM)
        t = jnp.dot(f_ref[0, :, cols], cs_ch_ref[...], preferred_element_type=_F32)
        zr_ref[:, cols] = t[:, :F_GROUP_DIM]
        zi_ref[:, cols] = t[:, F_GROUP_DIM:]
    for level in range(FNET_LEVELS):
        m = n >> level
        half = m // 2
        cos_ref, sin_ref = tw_refs[2 * level], tw_refs[2 * level + 1]
        for seg in range(1 << level):
            for c0 in range(0, half, FNET_CHUNK):
                rows_t = slice(seg * m + c0, seg * m + c0 + FNET_CHUNK)
                rows_b = slice(seg * m + half + c0, seg * m + half + c0 + FNET_CHUNK)
                tr, ti = zr_ref[rows_t, :], zi_ref[rows_t, :]
                br, bi = zr_ref[rows_b, :], zi_ref[rows_b, :]
                cw, sw = cos_ref[c0:c0 + FNET_CHUNK, :], sin_ref[c0:c0 + FNET_CHUNK, :]
                dr, di = tr - br, ti - bi
                zr_ref[rows_t, :] = tr + br
                zi_ref[rows_t, :] = ti + bi
                zr_ref[rows_b, :] = dr * cw + di * sw
                zi_ref[rows_b, :] = di * cw - dr * sw
    m = n >> FNET_LEVELS
    for j in range(1 << FNET_LEVELS):
        cols = slice(_bit_reverse(j, FNET_LEVELS) * F_W, (_bit_reverse(j, FNET_LEVELS) + 1) * F_W)
        rhs_ref[0:m, cols] = zr_ref[j * m:(j + 1) * m, :].astype(_BF)
        rhs_ref[m:2 * m, cols] = zi_ref[j * m:(j + 1) * m, :].astype(_BF)
    ortho = 1.0 / math.sqrt(n * F_GROUP_DIM)
    y = jnp.dot(cs_seq_ref[...], rhs_ref[...], preferred_element_type=_F32)
    o_ref[0] = (y * ortho).reshape(m, 1 << FNET_LEVELS, F_W).reshape(n, F_W).astype(_BF)


def _fnet(p, cs_ch, cs_seq, twiddles):
    b, n, _ = p.shape
    m = n >> FNET_LEVELS
    wide = F_W << FNET_LEVELS
    return pl.pallas_call(
        functools.partial(_fnet_kernel, n=n),
        grid=(b,),
        in_specs=[pl.BlockSpec((1, n, F_W), lambda bi: (bi, 0, P_F // F_W)),
                  _const_spec((F_GROUP_DIM, 2 * F_GROUP_DIM)),
                  _const_spec((m, 2 * m))] + [_const_spec(t.shape) for t in twiddles],
        out_specs=pl.BlockSpec((1, n, F_W), lambda bi: (bi, 0, 0)),
        out_shape=jax.ShapeDtypeStruct((b, n, F_W), _BF),
        scratch_shapes=[pltpu.VMEM((n, F_W), _F32), pltpu.VMEM((n, F_W), _F32),
                        pltpu.VMEM((2 * m, wide), _BF)],
        compiler_params=_cparams("arbitrary"),
        name="fnet_dft",
    )(p, cs_ch, cs_seq, *twiddles)


def _pack_bf16_pair(hi, lo):
    hi_bits = pltpu.bitcast(hi.astype(_BF).astype(_F32), jnp.uint32)
    lo_bits = pltpu.bitcast(lo.astype(_BF).astype(_F32), jnp.uint32)
    return hi_bits | lax.shift_right_logical(lo_bits, jnp.uint32(16))


def _unpack_bf16_pair(packed):
    hi = pltpu.bitcast(packed & jnp.uint32(0xFFFF0000), _F32)
    lo = pltpu.bitcast(lax.shift_left(packed, jnp.uint32(16)), _F32)
    return hi, lo


def _route(logits, carry):
    rows = logits.shape[0]
    lane = lax.broadcasted_iota(jnp.int32, logits.shape, 1)
    neg = jnp.float32(-jnp.inf)
    big = jnp.int32(ROUTER_W)

    def first_argmax(v):
        m = jnp.max(v, axis=-1, keepdims=True)
        idx = jnp.min(jnp.where(v == m, lane, big), axis=-1, keepdims=True)
        return m, idx

    lg = jnp.where((lane >= GROUP_LANE0) & (lane < GROUP_LANE0 + N_GROUPS), logits, neg)
    mg, ig = first_argmax(lg)
    w_grp = 1.0 / jnp.sum(jnp.exp(lg - mg), axis=-1, keepdims=True)
    g_sel = ig - GROUP_LANE0
    le = jnp.where((lane < N_EXPERTS)
                   & (jnp.right_shift(lane, LOG2_EXPERTS_PER_GROUP) == g_sel), logits, neg)
    v1, i1 = first_argmax(le)
    le2 = jnp.where(lane == i1, neg, le)
    v2, i2 = first_argmax(le2)
    e2 = jnp.exp(v2 - v1)
    w1 = w_grp / (1.0 + e2)
    w2 = w_grp * e2 / (1.0 + e2)

    first_is_lower = i1 < i2
    gate_a = jnp.where(first_is_lower, w1, w2)
    gate_b = jnp.where(first_is_lower, w2, w1)
    la = jnp.minimum(i1, i2) - g_sel * EXPERTS_PER_GROUP
    lb = jnp.maximum(i1, i2) - g_sel * EXPERTS_PER_GROUP
    pair = jnp.right_shift(la * (2 * EXPERTS_PER_GROUP - 1 - la), 1) + lb - la - 1
    cls = g_sel * PAIRS_PER_GROUP + pair

    onehot = (lane == cls).astype(_F32)
    r_i = lax.broadcasted_iota(jnp.int32, (rows, rows), 0)
    c_i = lax.broadcasted_iota(jnp.int32, (rows, rows), 1)
    earlier = (c_i < r_i).astype(_BF)
    before = jnp.dot(earlier, onehot.astype(_BF), preferred_element_type=_F32) + carry
    rank = jnp.sum(onehot * before, axis=-1, keepdims=True)
    new_carry = carry + jnp.sum(onehot, axis=0, keepdims=True)

    route = (jnp.where(lane == 0, gate_a, 0.0) + jnp.where(lane == 1, gate_b, 0.0)
             + jnp.where(lane == 2, cls.astype(_F32), 0.0) + jnp.where(lane == 3, rank, 0.0))
    return route, new_carry


def _merge_kernel(hd_ref, fo_ref, ga_ref, gf_ref, x_ref, mod_ref, g2n_ref, wao_ref, wfo_ref,
                  wo_ref, wrhl_ref, br_ref, x1_ref, rows_ref, rec_ref, counts_ref, carry_ref):
    @pl.when((pl.program_id(0) == 0) & (pl.program_id(1) == 0))
    def _():
        carry_ref[...] = jnp.zeros_like(carry_ref)

    a = jnp.dot(hd_ref[0], wao_ref[...], preferred_element_type=_F32)
    ff = jnp.dot(fo_ref[0], wfo_ref[...], preferred_element_type=_F32)
    y = ga_ref[0].astype(_F32) * a + gf_ref[0].astype(_F32) * ff
    mix = jnp.dot(y.astype(_BF), wo_ref[...], preferred_element_type=_F32)
    x1 = x_ref[0] + mod_ref[0, 2:3, :] * mix
    x1_ref[0] = x1
    h2 = (_rms(x1) * g2n_ref[...]) * (1.0 + mod_ref[0, 4:5, :]) + mod_ref[0, 3:4, :]
    h2_hi = h2.astype(_BF)
    h2_lo = (h2 - h2_hi.astype(_F32)).astype(_BF)
    hi_both = jnp.dot(h2_hi, wrhl_ref[...], preferred_element_type=_F32)
    logits = (hi_both[:, :ROUTER_W] + hi_both[:, ROUTER_W:]
              + jnp.dot(h2_lo, wrhl_ref[:, :ROUTER_W], preferred_element_type=_F32)) + br_ref[...]
    route, new_carry = _route(logits, carry_ref[...])
    carry_ref[...] = new_carry
    counts_ref[...] = new_carry
    rec_ref[0, 0] = route.T[0:SUBLANES, :]
    half = D_MODEL // 2
    tm = h2.shape[0]
    packed = _pack_bf16_pair(h2[:, :half], h2[:, half:])
    rows_ref[0, :, 0:H2_SLAB_ROWS, :] = packed.reshape(tm, H2_SLAB_ROWS, LANES)
    rows_ref[0, :, ROUTE_SLAB_ROW:ROUTE_SLAB_ROW + 1, :] = (
        pltpu.bitcast(route, jnp.uint32).reshape(tm, 1, LANES))
    rows_ref[0, :, ROUTE_SLAB_ROW + 1:SUBLANES, :] = jnp.zeros(
        (tm, SUBLANES - ROUTE_SLAB_ROW - 1, LANES), jnp.uint32)


def _merge(heads, four, p, x, mod3, norm2_g, w_ao, w_fo, w_o, wr_hilo, b_r, tm):
    b, n, d = x.shape
    tok = lambda w: pl.BlockSpec((1, tm, w), lambda bi, i: (bi, i, 0))
    return pl.pallas_call(
        _merge_kernel,
        grid=(b, n // tm),
        in_specs=[tok(d), tok(F_W),
                  pl.BlockSpec((1, tm, d), lambda bi, i: (bi, i, P_GA // D_MODEL)),
                  pl.BlockSpec((1, tm, d), lambda bi, i: (bi, i, P_GF // D_MODEL)),
                  tok(d),
                  pl.BlockSpec((1, 6, d), lambda bi, i: (bi, 0, 0)),
                  pl.BlockSpec((1, d), lambda bi, i: (0, 0)),
                  _const_spec((d, d)), _const_spec((F_W, d)), _const_spec((d, d)),
                  _const_spec((d, 2 * ROUTER_W)),
                  pl.BlockSpec((1, ROUTER_W), lambda bi, i: (0, 0))],
        out_specs=[tok(d),
                   pl.BlockSpec((1, tm, SUBLANES, LANES), lambda bi, i: (bi, i, 0, 0)),
                   pl.BlockSpec((1, 1, SUBLANES, tm), lambda bi, i: (bi, i, 0, 0)),
                   pl.BlockSpec((1, ROUTER_W), lambda bi, i: (0, 0))],
        out_shape=[jax.ShapeDtypeStruct((b, n, d), _F32),
                   jax.ShapeDtypeStruct((b, n, SUBLANES, LANES), jnp.uint32),
                   jax.ShapeDtypeStruct((b, n // tm, SUBLANES, tm), _F32),
                   jax.ShapeDtypeStruct((1, ROUTER_W), _F32)],
        scratch_shapes=[pltpu.VMEM((1, ROUTER_W), _F32)],
        compiler_params=_cparams("arbitrary", "arbitrary"),
        name="merge_router",
    )(heads, four, p, p, x, mod3, norm2_g, w_ao, w_fo, w_o, wr_hilo, b_r)


def _issue_slab_copies(n, make_copy, slot_of):
    def group(g, c):
        r0 = g * DMA_ISSUE_GROUP
        slots = [slot_of(r0 + k) for k in range(DMA_ISSUE_GROUP)]
        for k in range(DMA_ISSUE_GROUP):
            make_copy(r0 + k, slots[k]).start(priority=k % 2)
        return c

    lax.fori_loop(0, n // DMA_ISSUE_GROUP, group, 0)


def _dispatch_kernel(pos_ref, zero_tile_ref, src_ref, dst_ref, zero_ref, sem, pad_sem, *, tm):
    base = pl.program_id(0) * tm
    tile_rows = zero_ref.shape[0]

    @pl.when(pl.program_id(0) == 0)
    def _():
        zero_ref[...] = jnp.zeros_like(zero_ref)

        def fill(k):
            start = pl.multiple_of(zero_tile_ref[k] * tile_rows, tile_rows)
            return pltpu.make_async_copy(zero_ref, dst_ref.at[pl.ds(start, tile_rows)], pad_sem)

        for k in range(zero_tile_ref.shape[0]):
            @pl.when(zero_tile_ref[k] >= 0)
            def _(k=k):
                fill(k).start()
        for k in range(zero_tile_ref.shape[0]):
            @pl.when(zero_tile_ref[k] >= 0)
            def _(k=k):
                fill(k).wait()

    _issue_slab_copies(
        tm,
        lambda r, slot: pltpu.make_async_copy(src_ref.at[r], dst_ref.at[slot], sem),
        lambda r: pos_ref[base + r])
    pltpu.make_async_copy(src_ref, dst_ref.at[pl.ds(0, tm)], sem).wait()


def _dispatch(pos, zero_tiles, slabs, n_sorted, tm, tile_rows):
    t = slabs.shape[0]
    return pl.pallas_call(
        functools.partial(_dispatch_kernel, tm=tm),
        grid_spec=pltpu.PrefetchScalarGridSpec(
            num_scalar_prefetch=2,
            grid=(t // tm,),
            in_specs=[pl.BlockSpec((tm, SUBLANES, LANES), lambda i, *_: (i, 0, 0))],
            out_specs=pl.BlockSpec(memory_space=pl.ANY),
            scratch_shapes=[pltpu.VMEM((tile_rows, SUBLANES, LANES), slabs.dtype),
                            pltpu.SemaphoreType.DMA(()), pltpu.SemaphoreType.DMA(())]),
        out_shape=jax.ShapeDtypeStruct((n_sorted,) + slabs.shape[1:], slabs.dtype),
        compiler_params=_cparams("arbitrary"),
        name="moe_dispatch",
    )(pos, zero_tiles, slabs)


def _moe_kernel(ea_ref, eb_ref, valid_ref, slabs_ref, w1a_ref, w3a_ref, w2a_ref,
                w1b_ref, w3b_ref, w2b_ref, y_ref):
    del ea_ref, eb_ref
    i = pl.program_id(0)
    tm = slabs_ref.shape[0]

    @pl.when(valid_ref[i] != 0)
    def _():
        packed = slabs_ref[:, 0:H2_SLAB_ROWS, :].reshape(tm, H2_SLAB_ROWS * LANES)
        hi, lo = _unpack_bf16_pair(packed)
        t = jnp.concatenate([hi.astype(_BF), lo.astype(_BF)], axis=1)
        route = pltpu.bitcast(slabs_ref[:, ROUTE_SLAB_ROW, :], _F32)
        y = None
        for slot, (w1_ref, w3_ref, w2_ref) in enumerate(((w1a_ref, w3a_ref, w2a_ref),
                                                          (w1b_ref, w3b_ref, w2b_ref))):
            a = jnp.dot(t, w1_ref[0], preferred_element_type=_F32)
            u = jnp.dot(t, w3_ref[0], preferred_element_type=_F32)
            hid = ((a * _sigmoid(a)) * u * route[:, slot:slot + 1]).astype(_BF)
            part = jnp.dot(hid, w2_ref[0], preferred_element_type=_F32)
            y = part if y is None else y + part
        y_ref[...] = y.reshape(tm, SUBLANES, LANES)

    @pl.when(valid_ref[i] == 0)
    def _():
        y_ref[...] = jnp.zeros_like(y_ref)


def _moe(tile_ea, tile_eb, tile_valid, slabs_sorted, w1, w3, w2, tm):
    n_sorted = slabs_sorted.shape[0]
    d, hdn = w1.shape[1], w1.shape[2]
    assert d == SUBLANES * LANES
    pick = lambda which, a, b_: pl.BlockSpec(
        (1, a, b_), lambda i, ea, eb, valid: ((ea, eb)[which][i], 0, 0))
    slab_tile = pl.BlockSpec((tm, SUBLANES, LANES), lambda i, ea, eb, valid: (i, 0, 0))
    return pl.pallas_call(
        _moe_kernel,
        grid_spec=pltpu.PrefetchScalarGridSpec(
            num_scalar_prefetch=3,
            grid=(n_sorted // tm,),
            in_specs=[slab_tile,
                      pick(0, d, hdn), pick(0, d, hdn), pick(0, hdn, d),
                      pick(1, d, hdn), pick(1, d, hdn), pick(1, hdn, d)],
            out_specs=slab_tile),
        out_shape=jax.ShapeDtypeStruct((n_sorted, SUBLANES, LANES), _F32),
        compiler_params=_cparams("arbitrary"),
        name="moe_experts",
    )(tile_ea, tile_eb, tile_valid, slabs_sorted, w1, w3, w2, w1, w3, w2)


def _combine_kernel(pos_ref, y_hbm_ref, x1_ref, mod_ref, fg_ref, o_ref, ybuf_ref, sems, *, tm):
    i = pl.program_id(0)
    n_steps = pl.num_programs(0)

    def issue(tile, slot):
        _issue_slab_copies(
            tm,
            lambda r, src: pltpu.make_async_copy(y_hbm_ref.at[src], ybuf_ref.at[slot, r],
                                                 sems.at[slot]),
            lambda r: pos_ref[tile * tm + r])

    @pl.when(i == 0)
    def _():
        issue(0, 0)

    @pl.when(i + 1 < n_steps)
    def _():
        issue(i + 1, (i + 1) % 2)

    slot = i % 2
    pltpu.make_async_copy(y_hbm_ref.at[pl.ds(0, tm)], ybuf_ref.at[slot], sems.at[slot]).wait()
    y = ybuf_ref[slot].reshape(tm, SUBLANES * LANES)
    xo = x1_ref[...] + mod_ref[0, 5:6, :] * y
    o_ref[...] = _rms(xo) * fg_ref[...]


def _combine(pos, y_sorted, x1, mod3, final_g, seq, tm):
    t, d = x1.shape
    tiles_per_batch = seq // tm
    return pl.pallas_call(
        functools.partial(_combine_kernel, tm=tm),
        grid_spec=pltpu.PrefetchScalarGridSpec(
            num_scalar_prefetch=1,
            grid=(t // tm,),
            in_specs=[pl.BlockSpec(memory_space=pl.ANY),
                      pl.BlockSpec((tm, d), lambda i, pos: (i, 0)),
                      pl.BlockSpec((1, 6, d), lambda i, pos: (i // tiles_per_batch, 0, 0)),
                      pl.BlockSpec((1, d), lambda i, pos: (0, 0))],
            out_specs=pl.BlockSpec((tm, d), lambda i, pos: (i, 0)),
            scratch_shapes=[pltpu.VMEM((2, tm, SUBLANES, LANES), _F32),
                            pltpu.SemaphoreType.DMA((2,))]),
        out_shape=jax.ShapeDtypeStruct((t, d), _F32),
        compiler_params=_cparams("arbitrary"),
        name="moe_combine",
    )(pos, y_sorted, x1, mod3, final_g)


def _routing_tables(route_cls, route_rank, counts, n_tokens, tm):
    sizes = ((counts + (tm - 1)) // tm) * tm
    ends = jnp.cumsum(sizes)
    starts = ends - sizes
    n_tiles = n_tokens // tm + N_CLASSES
    tile_start = jnp.arange(n_tiles, dtype=jnp.int32) * tm
    pos = route_rank
    tile_cls = jnp.zeros((n_tiles,), jnp.int32)
    for c in range(N_CLASSES):
        pos = pos + jnp.where(route_cls == c, starts[c], 0)
        tile_cls = tile_cls + (tile_start >= ends[c]).astype(jnp.int32)
    tile_cls = jnp.minimum(tile_cls, N_CLASSES - 1)
    tile_valid = (tile_start < ends[-1]).astype(jnp.int32)
    pairs = [(a, b_) for a in range(EXPERTS_PER_GROUP) for b_ in range(a + 1, EXPERTS_PER_GROUP)]
    grp, pair = tile_cls // PAIRS_PER_GROUP, tile_cls % PAIRS_PER_GROUP
    tile_ea = grp * EXPERTS_PER_GROUP
    tile_eb = grp * EXPERTS_PER_GROUP
    for k, (a, b_) in enumerate(pairs):
        tile_ea = tile_ea + jnp.where(pair == k, a, 0)
        tile_eb = tile_eb + jnp.where(pair == k, b_, 0)
    last_tile = jnp.where(sizes > 0, ends // tm - 1, -1)
    tail = ends[-1] // tm + jnp.arange(N_CLASSES, dtype=jnp.int32)
    zero_tiles = jnp.concatenate([last_tile, jnp.where(tail < n_tiles, tail, -1)]).astype(jnp.int32)
    return (pos.astype(jnp.int32), zero_tiles, tile_ea.astype(jnp.int32),
            tile_eb.astype(jnp.int32), tile_valid, n_tiles)


def _rope_tables(n):
    inv = (1.0 / (ROPE_BASE ** (np.arange(ROT_FREQS, dtype=np.float32) / ROT_FREQS))).astype(np.float32)
    pos = np.arange(n)
    row = (pos // GRID_W).astype(np.float32)[:, None] * inv[None, :]
    col = (pos % GRID_W).astype(np.float32)[:, None] * inv[None, :]
    cos64 = np.concatenate([np.cos(row), np.cos(row), np.cos(col), np.cos(col)], axis=1)
    sin64 = np.concatenate([-np.sin(row), np.sin(row), -np.sin(col), np.sin(col)], axis=1)
    tile = lambda a: np.tile(a.astype(np.float32), (1, LANES // HEAD_DIM))
    return jnp.asarray(tile(cos64)), jnp.asarray(tile(sin64))


def _dft_cos_sin(n):
    k = np.arange(n, dtype=np.int64)
    ang = (2.0 * np.pi / n) * ((k[:, None] * k[None, :]) % n).astype(np.float64)
    return np.cos(ang), np.sin(ang)


def _dft_tables(n):
    c_ch, s_ch = _dft_cos_sin(F_GROUP_DIM)
    c_seq, s_seq = _dft_cos_sin(n >> FNET_LEVELS)
    cs_ch = np.concatenate([c_ch, -s_ch], axis=1).astype(np.float32)
    cs_seq = np.concatenate([c_seq, s_seq], axis=1).astype(np.float32)
    twiddles = []
    for level in range(FNET_LEVELS):
        m = n >> level
        ang = (2.0 * np.pi / m) * np.arange(m // 2, dtype=np.float64)[:, None]
        for tab in (np.cos(ang), np.sin(ang)):
            twiddles.append(jnp.asarray(np.broadcast_to(tab, (m // 2, F_W)).astype(np.float32)))
    return jnp.asarray(cs_ch.astype(_BF)), jnp.asarray(cs_seq.astype(_BF)), twiddles


def kernel(x, c, ctx, c_ctx, w_mod, b_mod, norm1_g, norm2_g, w_in, lam_q1, lam_k1, lam_q2, lam_k2,
           subln_g, w_attn_out, w_four_out, w_out, w_router_group, b_router_group, w_router_expert,
           b_router_expert, w_exp_gate, w_exp_up, w_exp_down, final_g):
    b, n, d = x.shape
    assert w_mod.shape[0] == 1, "depth-1 stack"
    assert b + 1 <= MOD_ROWS

    cc = jnp.concatenate([c, c_ctx[None, :], jnp.zeros((MOD_ROWS - b - 1, d), _F32)], axis=0)
    mod3 = _modulation(cc, w_mod[0], b_mod).reshape(MOD_ROWS, 6, d)

    lam = (jnp.exp(jnp.sum(lam_q1[0] * lam_k1[0])) - jnp.exp(jnp.sum(lam_q2[0] * lam_k2[0]))
           + LAM_INIT).reshape(1).astype(_F32)

    w = w_in[0]
    scale = HEAD_DIM ** -0.5 * math.log2(math.e)
    w_lat = jnp.concatenate([w[:, REF_Q:REF_K] * scale, w[:, REF_K:REF_F],
                             w[:, REF_GA:REF_END], w[:, REF_F:REF_GA]], axis=1).astype(_BF)
    w_ctx = w[:, REF_K:REF_F].astype(_BF)
    lat_kinds = ("rope",) * ((P_V - P_Q) // PROJ_CW) + ("plain",) * ((P_GA - P_V) // PROJ_CW) \
        + ("sigmoid",) * ((P_F - P_GA) // PROJ_CW) + ("plain",) * ((P_W - P_F) // PROJ_CW)
    ctx_kinds = ("plain",) * (w_ctx.shape[1] // PROJ_CW)

    cos_t, sin_t = _rope_tables(n)
    p = _in_projection(x, mod3, lambda bi: bi, norm1_g, w_lat, cos_t, sin_t, lat_kinds, PROJ_TM)
    kvc = _in_projection(ctx, mod3, lambda bi: b, norm1_g, w_ctx, cos_t, sin_t, ctx_kinds,
                         ctx.shape[1])

    heads = _attention(lam, p, kvc, subln_g, ATTN_TQ)
    cs_ch, cs_seq, twiddles = _dft_tables(n)
    four = _fnet(p, cs_ch, cs_seq, twiddles)

    w_r = jnp.concatenate([w_router_expert[0], w_router_group[0],
                           jnp.zeros((d, ROUTER_W - N_EXPERTS - N_GROUPS), _F32)], axis=1)
    b_r = jnp.concatenate([b_router_expert[0], b_router_group[0],
                           jnp.zeros((ROUTER_W - N_EXPERTS - N_GROUPS,), _F32)])[None, :]
    wr_hi = w_r.astype(_BF)
    wr_lo = (w_r - wr_hi.astype(_F32)).astype(_BF)
    x1, rows, rec, counts = _merge(heads, four, p, x, mod3, norm2_g, w_attn_out[0].astype(_BF),
                                   w_four_out[0].astype(_BF), w_out[0].astype(_BF),
                                   jnp.concatenate([wr_hi, wr_lo], axis=1), b_r, MERGE_TM)

    t = b * n
    slabs = rows.reshape(t, SUBLANES, LANES)
    rec = rec.reshape(t // MERGE_TM, SUBLANES, MERGE_TM)
    pos, zero_tiles, tile_ea, tile_eb, tile_valid, n_tiles = _routing_tables(
        rec[:, 2, :].reshape(t).astype(jnp.int32), rec[:, 3, :].reshape(t).astype(jnp.int32),
        counts[0, :N_CLASSES].astype(jnp.int32), t, MOE_TM)

    slabs_sorted = _dispatch(pos, zero_tiles, slabs, n_tiles * MOE_TM, DISPATCH_TM, MOE_TM)
    y_sorted = _moe(tile_ea, tile_eb, tile_valid, slabs_sorted, w_exp_gate[0].astype(_BF),
                    w_exp_up[0].astype(_BF), w_exp_down[0].astype(_BF), MOE_TM)
    out = _combine(pos, y_sorted, x1.reshape(t, d), mod3, final_g[None, :], n, COMBINE_TM)
    return out.reshape(b, n, d)
```

```python
import functools
import math

import jax
import jax.numpy as jnp
import numpy as np
from jax import lax
from jax.experimental import pallas as pl
from jax.experimental.pallas import tpu as pltpu

D_MODEL = 1024
GRID_W = 64
EPS = 1e-6
N_HEADS = 8
HEAD_DIM = 64
HEAD_W = 2 * HEAD_DIM
ROT_FREQS = HEAD_DIM // 4
ROPE_BASE = 10000.0
F_GROUPS = 4
F_GROUP_DIM = 128
F_W = F_GROUPS * F_GROUP_DIM
N_GROUPS = 4
EXPERTS_PER_GROUP = 4
LOG2_EXPERTS_PER_GROUP = 2
N_EXPERTS = N_GROUPS * EXPERTS_PER_GROUP
PAIRS_PER_GROUP = EXPERTS_PER_GROUP * (EXPERTS_PER_GROUP - 1) // 2
N_CLASSES = N_GROUPS * PAIRS_PER_GROUP
EXPERT_HIDDEN = 512
LAM_INIT = 0.8 - 0.6 * math.exp(-0.3 * 0)

REF_Q, REF_K, REF_V, REF_F, REF_GA, REF_GF, REF_END = 0, 1024, 2048, 3072, 3584, 4608, 5632
P_Q, P_K, P_V, P_GA, P_GF, P_F, P_W = 0, 1024, 2048, 3072, 4096, 5120, 5632

LANES = 128
SUBLANES = 8
VMEM_LIMIT_BYTES = 56 * 1024 * 1024

PROJ_TM = 1024
PROJ_CW = 512
ATTN_TQ = 256
ATTN_SUM_ROWS = 16
ATTN_SCORE_BUFS = 3
ATTN_HEADS_PER_STEP = 2
FNET_LEVELS = 2
FNET_CHUNK = 256
MERGE_TM = 512
MOE_TM = 512
COMBINE_TM = 512
DISPATCH_TM = 4096
DMA_ISSUE_GROUP = 16
MOD_ROWS = 40
ROUTER_W = LANES
GROUP_LANE0 = N_EXPERTS
H2_SLAB_ROWS = D_MODEL // 2 // LANES
ROUTE_SLAB_ROW = H2_SLAB_ROWS

_BF = jnp.bfloat16
_F32 = jnp.float32


def _cparams(*sem):
    return pltpu.CompilerParams(dimension_semantics=sem, vmem_limit_bytes=VMEM_LIMIT_BYTES)


def _const_spec(shape):
    return pl.BlockSpec(shape, lambda *_: (0,) * len(shape), pipeline_mode=pl.Buffered(1))


def _rms(x):
    return x * lax.rsqrt(jnp.mean(x * x, axis=-1, keepdims=True) + EPS)


def _sigmoid(x):
    return 1.0 / (1.0 + jnp.exp(-x))


def _mod_kernel(cc_ref, w_ref, b_ref, o_ref):
    cc = cc_ref[...]
    s = cc * _sigmoid(cc)
    o_ref[...] = jnp.dot(s, w_ref[...], preferred_element_type=_F32,
                         precision=lax.Precision.HIGHEST) + b_ref[...]


def _modulation(cc, w_mod, b_mod):
    n = w_mod.shape[1]
    bn = D_MODEL
    return pl.pallas_call(
        _mod_kernel,
        grid=(n // bn,),
        in_specs=[pl.BlockSpec((MOD_ROWS, D_MODEL), lambda j: (0, 0)),
                  pl.BlockSpec((D_MODEL, bn), lambda j: (0, j)),
                  pl.BlockSpec((1, bn), lambda j: (0, j))],
        out_specs=pl.BlockSpec((MOD_ROWS, bn), lambda j: (0, j)),
        out_shape=jax.ShapeDtypeStruct((MOD_ROWS, n), _F32),
        compiler_params=_cparams("arbitrary"),
        name="modulation",
    )(cc, w_mod, b_mod)


def _rope(acc, cos_ref, sin_ref):
    cos = cos_ref[...]
    sin = sin_ref[...]
    lane = lax.broadcasted_iota(jnp.int32, (1, LANES), 1)
    first_half = (lane % (2 * ROT_FREQS)) < ROT_FREQS
    outs = []
    for s in range(acc.shape[1] // LANES):
        xs = acc[:, s * LANES:(s + 1) * LANES]
        partner = jnp.where(first_half,
                            pltpu.roll(xs, LANES - ROT_FREQS, 1),
                            pltpu.roll(xs, ROT_FREQS, 1))
        outs.append(xs * cos + partner * sin)
    return jnp.concatenate(outs, axis=1)


def _inproj_kernel(x_ref, mod_ref, g_ref, w_ref, cos_ref, sin_ref, o_ref, *, chunk_kinds):
    x = x_ref[0]
    shift = mod_ref[0, 0:1, :]
    scale = mod_ref[0, 1:2, :]
    h = (_rms(x) * g_ref[...]) * (1.0 + scale) + shift
    hb = h.astype(_BF)
    for j, kind in enumerate(chunk_kinds):
        cols = slice(j * PROJ_CW, (j + 1) * PROJ_CW)
        acc = jnp.dot(hb, w_ref[:, cols], preferred_element_type=_F32)
        if kind == "rope":
            acc = _rope(acc, cos_ref, sin_ref)
        elif kind == "sigmoid":
            acc = _sigmoid(acc)
        o_ref[0, :, cols] = acc.astype(_BF)


def _in_projection(x, mod3, mod_row_of_batch, g, w, cos_t, sin_t, chunk_kinds, tm):
    b, n, d = x.shape
    width = w.shape[1]
    assert width == len(chunk_kinds) * PROJ_CW and n % tm == 0
    return pl.pallas_call(
        functools.partial(_inproj_kernel, chunk_kinds=chunk_kinds),
        grid=(b, n // tm),
        in_specs=[pl.BlockSpec((1, tm, d), lambda bi, i: (bi, i, 0)),
                  pl.BlockSpec((1, 6, d), lambda bi, i: (mod_row_of_batch(bi), 0, 0)),
                  pl.BlockSpec((1, d), lambda bi, i: (0, 0)),
                  _const_spec((d, width)),
                  pl.BlockSpec((tm, LANES), lambda bi, i: (i, 0)),
                  pl.BlockSpec((tm, LANES), lambda bi, i: (i, 0))],
        out_specs=pl.BlockSpec((1, tm, width), lambda bi, i: (bi, i, 0)),
        out_shape=jax.ShapeDtypeStruct((b, n, width), _BF),
        compiler_params=_cparams("parallel", "arbitrary"),
        name="in_projection",
    )(x, mod3, g, w, cos_t, sin_t)


def _attn_kernel(lam_ref, q_ref, kl_ref, vl_ref, kc_ref, vc_ref, sg_ref, o_ref,
                 kcat_ref, vt_ref, *bufs, tq):
    nc, n = kc_ref.shape[1], kl_ref.shape[1]
    for hh in range(ATTN_HEADS_PER_STEP):
        cols = slice(hh * HEAD_W, (hh + 1) * HEAD_W)
        kcat_ref[hh, 0:nc, :] = kc_ref[0, :, cols]
        kcat_ref[hh, nc:nc + n, :] = kl_ref[0, :, cols]
        vt_ref[hh, 0:HEAD_W, 0:nc] = vc_ref[0, :, cols].astype(_F32).T.astype(_BF)
        vt_ref[hh, 0:HEAD_W, nc:nc + n] = vl_ref[0, :, cols].astype(_F32).T.astype(_BF)
        vt_ref[hh, HEAD_W:, :] = jnp.ones((ATTN_SUM_ROWS, nc + n), _BF)
    lane = lax.broadcasted_iota(jnp.int32, (1, HEAD_W), 1)
    map_lanes = (lane < HEAD_DIM, lane >= HEAD_DIM)
    nt = (((1,), (1,)), ((), ()))
    lam = lam_ref[0]
    post_scale = sg_ref[...] * (1.0 - LAM_INIT)
    n_sub = n // tq

    def scores(g):
        hh, j = divmod(g, n_sub)
        q = q_ref[0, j * tq:(j + 1) * tq, hh * HEAD_W:(hh + 1) * HEAD_W]
        for mp in range(2):
            qm = jnp.where(map_lanes[mp], q, jnp.zeros_like(q))
            bufs[g % ATTN_SCORE_BUFS][mp] = lax.dot_general(kcat_ref[hh], qm, nt,
                                                            preferred_element_type=_F32)

    def finish(g):
        hh, j = divmod(g, n_sub)
        outs = []
        for mp in range(2):
            s = bufs[g % ATTN_SCORE_BUFS][mp]
            e = jnp.exp2(s - jnp.max(s, axis=0, keepdims=True)).astype(_BF)
            r = jnp.dot(vt_ref[hh], e, preferred_element_type=_F32)
            outs.append(r[:HEAD_W, :] / r[HEAD_W:HEAD_W + 1, :])
        heads = (outs[0] - lam * outs[1]).T
        o_ref[0, j * tq:(j + 1) * tq, hh * HEAD_W:(hh + 1) * HEAD_W] = (
            _rms(heads) * post_scale).astype(_BF)

    total = ATTN_HEADS_PER_STEP * n_sub
    for g in range(ATTN_SCORE_BUFS - 1):
        scores(g)
    for g in range(total):
        if g + ATTN_SCORE_BUFS - 1 < total:
            scores(g + ATTN_SCORE_BUFS - 1)
        finish(g)


def _attention(lam, p, kvc, subln_g, tq):
    b, n, _ = p.shape
    nc = kvc.shape[1]
    w = ATTN_HEADS_PER_STEP * HEAD_W
    qb, kb, vb = P_Q // w, P_K // w, P_V // w
    seq = lambda blk: pl.BlockSpec((1, n, w), lambda bi, h: (bi, 0, blk + h))
    return pl.pallas_call(
        functools.partial(_attn_kernel, tq=tq),
        grid=(b, N_HEADS // ATTN_HEADS_PER_STEP),
        in_specs=[pl.BlockSpec(memory_space=pltpu.SMEM),
                  seq(qb), seq(kb), seq(vb),
                  pl.BlockSpec((1, nc, w), lambda bi, h: (bi, 0, h)),
                  pl.BlockSpec((1, nc, w), lambda bi, h: (bi, 0, N_HEADS // ATTN_HEADS_PER_STEP + h)),
                  pl.BlockSpec((1, HEAD_W), lambda bi, h: (0, 0))],
        out_specs=seq(0),
        out_shape=jax.ShapeDtypeStruct((b, n, N_HEADS * HEAD_W), _BF),
        scratch_shapes=[pltpu.VMEM((ATTN_HEADS_PER_STEP, nc + n, HEAD_W), _BF),
                        pltpu.VMEM((ATTN_HEADS_PER_STEP, HEAD_W + ATTN_SUM_ROWS, nc + n), _BF),
                        ] + [pltpu.VMEM((2, nc + n, tq), _F32)] * ATTN_SCORE_BUFS,
        compiler_params=_cparams("parallel", "arbitrary"),
        name="diff_attention",
    )(lam, p, p, p, kvc, kvc, subln_g)


def _bit_reverse(j, bits):
    return int(format(j, "0%db" % bits)[::-1], 2) if bits else 0


def _fnet_kernel(f_ref, cs_ch_ref, cs_seq_ref, *rest, n):
    tw_refs = rest[:2 * FNET_LEVELS]
    o_ref, zr_ref, zi_ref, rhs_ref = rest[2 * FNET_LEVELS:]
    for g in range(F_GROUPS):
        cols = slice(g * F_GROUP_DIM, (g + 1) * F_GROUP_DIM)
        t = jnp.dot(f_ref[0, :, cols], cs_ch_ref[...], preferred_element_type=_F32)
        zr_ref[:, cols] = t[:, :F_GROUP_DIM]
        zi_ref[:, cols] = t[:, F_GROUP_DIM:]
    for level in range(FNET_LEVELS):
        m = n >> level
        half = m // 2
        cos_ref, sin_ref = tw_refs[2 * level], tw_refs[2 * level + 1]
        for seg in range(1 << level):
            for c0 in range(0, half, FNET_CHUNK):
                rows_t = slice(seg * m + c0, seg * m + c0 + FNET_CHUNK)
                rows_b = slice(seg * m + half + c0, seg * m + half + c0 + FNET_CHUNK)
                tr, ti = zr_ref[rows_t, :], zi_ref[rows_t, :]
                br, bi = zr_ref[rows_b, :], zi_ref[rows_b, :]
                cw, sw = cos_ref[c0:c0 + FNET_CHUNK, :], sin_ref[c0:c0 + FNET_CHUNK, :]
                dr, di = tr - br, ti - bi
                zr_ref[rows_t, :] = tr + br
                zi_ref[rows_t, :] = ti + bi
                zr_ref[rows_b, :] = dr * cw + di * sw
                zi_ref[rows_b, :] = di * cw - dr * sw
    m = n >> FNET_LEVELS
    for j in range(1 << FNET_LEVELS):
        cols = slice(_bit_reverse(j, FNET_LEVELS) * F_W, (_bit_reverse(j, FNET_LEVELS) + 1) * F_W)
        rhs_ref[0:m, cols] = zr_ref[j * m:(j + 1) * m, :].astype(_BF)
        rhs_ref[m:2 * m, cols] = zi_ref[j * m:(j + 1) * m, :].astype(_BF)
    ortho = 1.0 / math.sqrt(n * F_GROUP_DIM)
    y = jnp.dot(cs_seq_ref[...], rhs_ref[...], preferred_element_type=_F32)
    o_ref[0] = (y * ortho).reshape(m, 1 << FNET_LEVELS, F_W).reshape(n, F_W).astype(_BF)


def _fnet(p, cs_ch, cs_seq, twiddles):
    b, n, _ = p.shape
    m = n >> FNET_LEVELS
    wide = F_W << FNET_LEVELS
    return pl.pallas_call(
        functools.partial(_fnet_kernel, n=n),
        grid=(b,),
        in_specs=[pl.BlockSpec((1, n, F_W), lambda bi: (bi, 0, P_F // F_W)),
                  _const_spec((F_GROUP_DIM, 2 * F_GROUP_DIM)),
                  _const_spec((m, 2 * m))] + [_const_spec(t.shape) for t in twiddles],
        out_specs=pl.BlockSpec((1, n, F_W), lambda bi: (bi, 0, 0)),
        out_shape=jax.ShapeDtypeStruct((b, n, F_W), _BF),
        scratch_shapes=[pltpu.VMEM((n, F_W), _F32), pltpu.VMEM((n, F_W), _F32),
                        pltpu.VMEM((2 * m, wide), _BF)],
        compiler_params=_cparams("arbitrary"),
        name="fnet_dft",
    )(p, cs_ch, cs_seq, *twiddles)


def _pack_bf16_pair(hi, lo):
    hi_bits = pltpu.bitcast(hi.astype(_BF).astype(_F32), jnp.uint32)
    lo_bits = pltpu.bitcast(lo.astype(_BF).astype(_F32), jnp.uint32)
    return hi_bits | lax.shift_right_logical(lo_bits, jnp.uint32(16))


def _unpack_bf16_pair(packed):
    hi = pltpu.bitcast(packed & jnp.uint32(0xFFFF0000), _F32)
    lo = pltpu.bitcast(lax.shift_left(packed, jnp.uint32(16)), _F32)
    return hi, lo


def _route(logits, carry):
    rows = logits.shape[0]
    lane = lax.broadcasted_iota(jnp.int32, logits.shape, 1)
    neg = jnp.float32(-jnp.inf)
    big = jnp.int32(ROUTER_W)

    def first_argmax(v):
        m = jnp.max(v, axis=-1, keepdims=True)
        idx = jnp.min(jnp.where(v == m, lane, big), axis=-1, keepdims=True)
        return m, idx

    lg = jnp.where((lane >= GROUP_LANE0) & (lane < GROUP_LANE0 + N_GROUPS), logits, neg)
    mg, ig = first_argmax(lg)
    w_grp = 1.0 / jnp.sum(jnp.exp(lg - mg), axis=-1, keepdims=True)
    g_sel = ig - GROUP_LANE0
    le = jnp.where((lane < N_EXPERTS)
                   & (jnp.right_shift(lane, LOG2_EXPERTS_PER_GROUP) == g_sel), logits, neg)
    v1, i1 = first_argmax(le)
    le2 = jnp.where(lane == i1, neg, le)
    v2, i2 = first_argmax(le2)
    e2 = jnp.exp(v2 - v1)
    w1 = w_grp / (1.0 + e2)
    w2 = w_grp * e2 / (1.0 + e2)

    first_is_lower = i1 < i2
    gate_a = jnp.where(first_is_lower, w1, w2)
    gate_b = jnp.where(first_is_lower, w2, w1)
    la = jnp.minimum(i1, i2) - g_sel * EXPERTS_PER_GROUP
    lb = jnp.maximum(i1, i2) - g_sel * EXPERTS_PER_GROUP
    pair = jnp.right_shift(la * (2 * EXPERTS_PER_GROUP - 1 - la), 1) + lb - la - 1
    cls = g_sel * PAIRS_PER_GROUP + pair

    onehot = (lane == cls).astype(_F32)
    r_i = lax.broadcasted_iota(jnp.int32, (rows, rows), 0)
    c_i = lax.broadcasted_iota(jnp.int32, (rows, rows), 1)
    earlier = (c_i < r_i).astype(_BF)
    before = jnp.dot(earlier, onehot.astype(_BF), preferred_element_type=_F32) + carry
    rank = jnp.sum(onehot * before, axis=-1, keepdims=True)
    new_carry = carry + jnp.sum(onehot, axis=0, keepdims=True)

    route = (jnp.where(lane == 0, gate_a, 0.0) + jnp.where(lane == 1, gate_b, 0.0)
             + jnp.where(lane == 2, cls.astype(_F32), 0.0) + jnp.where(lane == 3, rank, 0.0))
    return route, new_carry


def _merge_kernel(hd_ref, fo_ref, ga_ref, gf_ref, x_ref, mod_ref, g2n_ref, wao_ref, wfo_ref,
                  wo_ref, wrhl_ref, br_ref, x1_ref, rows_ref, rec_ref, counts_ref, carry_ref):
    @pl.when((pl.program_id(0) == 0) & (pl.program_id(1) == 0))
    def _():
        carry_ref[...] = jnp.zeros_like(carry_ref)

    a = jnp.dot(hd_ref[0], wao_ref[...], preferred_element_type=_F32)
    ff = jnp.dot(fo_ref[0], wfo_ref[...], preferred_element_type=_F32)
    y = ga_ref[0].astype(_F32) * a + gf_ref[0].astype(_F32) * ff
    mix = jnp.dot(y.astype(_BF), wo_ref[...], preferred_element_type=_F32)
    x1 = x_ref[0] + mod_ref[0, 2:3, :] * mix
    x1_ref[0] = x1
    h2 = (_rms(x1) * g2n_ref[...]) * (1.0 + mod_ref[0, 4:5, :]) + mod_ref[0, 3:4, :]
    h2_hi = h2.astype(_BF)
    h2_lo = (h2 - h2_hi.astype(_F32)).astype(_BF)
    hi_both = jnp.dot(h2_hi, wrhl_ref[...], preferred_element_type=_F32)
    logits = (hi_both[:, :ROUTER_W] + hi_both[:, ROUTER_W:]
              + jnp.dot(h2_lo, wrhl_ref[:, :ROUTER_W], preferred_element_type=_F32)) + br_ref[...]
    route, new_carry = _route(logits, carry_ref[...])
    carry_ref[...] = new_carry
    counts_ref[...] = new_carry
    rec_ref[0, 0] = route.T[0:SUBLANES, :]
    half = D_MODEL // 2
    tm = h2.shape[0]
    packed = _pack_bf16_pair(h2[:, :half], h2[:, half:])
    rows_ref[0, :, 0:H2_SLAB_ROWS, :] = packed.reshape(tm, H2_SLAB_ROWS, LANES)
    rows_ref[0, :, ROUTE_SLAB_ROW:ROUTE_SLAB_ROW + 1, :] = (
        pltpu.bitcast(route, jnp.uint32).reshape(tm, 1, LANES))
    rows_ref[0, :, ROUTE_SLAB_ROW + 1:SUBLANES, :] = jnp.zeros(
        (tm, SUBLANES - ROUTE_SLAB_ROW - 1, LANES), jnp.uint32)


def _merge(heads, four, p, x, mod3, norm2_g, w_ao, w_fo, w_o, wr_hilo, b_r, tm):
    b, n, d = x.shape
    tok = lambda w: pl.BlockSpec((1, tm, w), lambda bi, i: (bi, i, 0))
    return pl.pallas_call(
        _merge_kernel,
        grid=(b, n // tm),
        in_specs=[tok(d), tok(F_W),
                  pl.BlockSpec((1, tm, d), lambda bi, i: (bi, i, P_GA // D_MODEL)),
                  pl.BlockSpec((1, tm, d), lambda bi, i: (bi, i, P_GF // D_MODEL)),
                  tok(d),
                  pl.BlockSpec((1, 6, d), lambda bi, i: (bi, 0, 0)),
                  pl.BlockSpec((1, d), lambda bi, i: (0, 0)),
                  _const_spec((d, d)), _const_spec((F_W, d)), _const_spec((d, d)),
                  _const_spec((d, 2 * ROUTER_W)),
                  pl.BlockSpec((1, ROUTER_W), lambda bi, i: (0, 0))],
        out_specs=[tok(d),
                   pl.BlockSpec((1, tm, SUBLANES, LANES), lambda bi, i: (bi, i, 0, 0)),
                   pl.BlockSpec((1, 1, SUBLANES, tm), lambda bi, i: (bi, i, 0, 0)),
                   pl.BlockSpec((1, ROUTER_W), lambda bi, i: (0, 0))],
        out_shape=[jax.ShapeDtypeStruct((b, n, d), _F32),
                   jax.ShapeDtypeStruct((b, n, SUBLANES, LANES), jnp.uint32),
                   jax.ShapeDtypeStruct((b, n // tm, SUBLANES, tm), _F32),
                   jax.ShapeDtypeStruct((1, ROUTER_W), _F32)],
        scratch_shapes=[pltpu.VMEM((1, ROUTER_W), _F32)],
        compiler_params=_cparams("arbitrary", "arbitrary"),
        name="merge_router",
    )(heads, four, p, p, x, mod3, norm2_g, w_ao, w_fo, w_o, wr_hilo, b_r)


def _issue_slab_copies(n, make_copy, slot_of):
    def group(g, c):
        r0 = g * DMA_ISSUE_GROUP
        slots = [slot_of(r0 + k) for k in range(DMA_ISSUE_GROUP)]
        for k in range(DMA_ISSUE_GROUP):
            make_copy(r0 + k, slots[k]).start(priority=k % 2)
        return c

    lax.fori_loop(0, n // DMA_ISSUE_GROUP, group, 0)


def _dispatch_kernel(pos_ref, zero_tile_ref, src_ref, dst_ref, zero_ref, sem, pad_sem, *, tm):
    base = pl.program_id(0) * tm
    tile_rows = zero_ref.shape[0]

    @pl.when(pl.program_id(0) == 0)
    def _():
        zero_ref[...] = jnp.zeros_like(zero_ref)

        def fill(k):
            start = pl.multiple_of(zero_tile_ref[k] * tile_rows, tile_rows)
            return pltpu.make_async_copy(zero_ref, dst_ref.at[pl.ds(start, tile_rows)], pad_sem)

        for k in range(zero_tile_ref.shape[0]):
            @pl.when(zero_tile_ref[k] >= 0)
            def _(k=k):
                fill(k).start()
        for k in range(zero_tile_ref.shape[0]):
            @pl.when(zero_tile_ref[k] >= 0)
            def _(k=k):
                fill(k).wait()

    _issue_slab_copies(
        tm,
        lambda r, slot: pltpu.make_async_copy(src_ref.at[r], dst_ref.at[slot], sem),
        lambda r: pos_ref[base + r])
    pltpu.make_async_copy(src_ref, dst_ref.at[pl.ds(0, tm)], sem).wait()


def _dispatch(pos, zero_tiles, slabs, n_sorted, tm, tile_rows):
    t = slabs.shape[0]
    return pl.pallas_call(
        functools.partial(_dispatch_kernel, tm=tm),
        grid_spec=pltpu.PrefetchScalarGridSpec(
            num_scalar_prefetch=2,
            grid=(t // tm,),
            in_specs=[pl.BlockSpec((tm, SUBLANES, LANES), lambda i, *_: (i, 0, 0))],
            out_specs=pl.BlockSpec(memory_space=pl.ANY),
            scratch_shapes=[pltpu.VMEM((tile_rows, SUBLANES, LANES), slabs.dtype),
                            pltpu.SemaphoreType.DMA(()), pltpu.SemaphoreType.DMA(())]),
        out_shape=jax.ShapeDtypeStruct((n_sorted,) + slabs.shape[1:], slabs.dtype),
        compiler_params=_cparams("arbitrary"),
        name="moe_dispatch",
    )(pos, zero_tiles, slabs)


def _moe_kernel(ea_ref, eb_ref, valid_ref, slabs_ref, w1a_ref, w3a_ref, w2a_ref,
                w1b_ref, w3b_ref, w2b_ref, y_ref):
    del ea_ref, eb_ref
    i = pl.program_id(0)
    tm = slabs_ref.shape[0]

    @pl.when(valid_ref[i] != 0)
    def _():
        packed = slabs_ref[:, 0:H2_SLAB_ROWS, :].reshape(tm, H2_SLAB_ROWS * LANES)
        hi, lo = _unpack_bf16_pair(packed)
        t = jnp.concatenate([hi.astype(_BF), lo.astype(_BF)], axis=1)
        route = pltpu.bitcast(slabs_ref[:, ROUTE_SLAB_ROW, :], _F32)
        y = None
        for slot, (w1_ref, w3_ref, w2_ref) in enumerate(((w1a_ref, w3a_ref, w2a_ref),
                                                          (w1b_ref, w3b_ref, w2b_ref))):
            a = jnp.dot(t, w1_ref[0], preferred_element_type=_F32)
            u = jnp.dot(t, w3_ref[0], preferred_element_type=_F32)
            hid = ((a * _sigmoid(a)) * u * route[:, slot:slot + 1]).astype(_BF)
            part = jnp.dot(hid, w2_ref[0], preferred_element_type=_F32)
            y = part if y is None else y + part
        y_ref[...] = y.reshape(tm, SUBLANES, LANES)

    @pl.when(valid_ref[i] == 0)
    def _():
        y_ref[...] = jnp.zeros_like(y_ref)


def _moe(tile_ea, tile_eb, tile_valid, slabs_sorted, w1, w3, w2, tm):
    n_sorted = slabs_sorted.shape[0]
    d, hdn = w1.shape[1], w1.shape[2]
    assert d == SUBLANES * LANES
    pick = lambda which, a, b_: pl.BlockSpec(
        (1, a, b_), lambda i, ea, eb, valid: ((ea, eb)[which][i], 0, 0))
    slab_tile = pl.BlockSpec((tm, SUBLANES, LANES), lambda i, ea, eb, valid: (i, 0, 0))
    return pl.pallas_call(
        _moe_kernel,
        grid_spec=pltpu.PrefetchScalarGridSpec(
            num_scalar_prefetch=3,
            grid=(n_sorted // tm,),
            in_specs=[slab_tile,
                      pick(0, d, hdn), pick(0, d, hdn), pick(0, hdn, d),
                      pick(1, d, hdn), pick(1, d, hdn), pick(1, hdn, d)],
            out_specs=slab_tile),
        out_shape=jax.ShapeDtypeStruct((n_sorted, SUBLANES, LANES), _F32),
        compiler_params=_cparams("arbitrary"),
        name="moe_experts",
    )(tile_ea, tile_eb, tile_valid, slabs_sorted, w1, w3, w2, w1, w3, w2)


def _combine_kernel(pos_ref, y_hbm_ref, x1_ref, mod_ref, fg_ref, o_ref, ybuf_ref, sems, *, tm):
    i = pl.program_id(0)
    n_steps = pl.num_programs(0)

    def issue(tile, slot):
        _issue_slab_copies(
            tm,
            lambda r, src: pltpu.make_async_copy(y_hbm_ref.at[src], ybuf_ref.at[slot, r],
                                                 sems.at[slot]),
            lambda r: pos_ref[tile * tm + r])

    @pl.when(i == 0)
    def _():
        issue(0, 0)

    @pl.when(i + 1 < n_steps)
    def _():
        issue(i + 1, (i + 1) % 2)

    slot = i % 2
    pltpu.make_async_copy(y_hbm_ref.at[pl.ds(0, tm)], ybuf_ref.at[slot], sems.at[slot]).wait()
    y = ybuf_ref[slot].reshape(tm, SUBLANES * LANES)
    xo = x1_ref[...] + mod_ref[0, 5:6, :] * y
    o_ref[...] = _rms(xo) * fg_ref[...]


def _combine(pos, y_sorted, x1, mod3, final_g, seq, tm):
    t, d = x1.shape
    tiles_per_batch = seq // tm
    return pl.pallas_call(
        functools.partial(_combine_kernel, tm=tm),
        grid_spec=pltpu.PrefetchScalarGridSpec(
            num_scalar_prefetch=1,
            grid=(t // tm,),
            in_specs=[pl.BlockSpec(memory_space=pl.ANY),
                      pl.BlockSpec((tm, d), lambda i, pos: (i, 0)),
                      pl.BlockSpec((1, 6, d), lambda i, pos: (i // tiles_per_batch, 0, 0)),
                      pl.BlockSpec((1, d), lambda i, pos: (0, 0))],
            out_specs=pl.BlockSpec((tm, d), lambda i, pos: (i, 0)),
            scratch_shapes=[pltpu.VMEM((2, tm, SUBLANES, LANES), _F32),
                            pltpu.SemaphoreType.DMA((2,))]),
        out_shape=jax.ShapeDtypeStruct((t, d), _F32),
        compiler_params=_cparams("arbitrary"),
        name="moe_combine",
    )(pos, y_sorted, x1, mod3, final_g)


def _routing_tables(route_cls, route_rank, counts, n_tokens, tm):
    sizes = ((counts + (tm - 1)) // tm) * tm
    ends = jnp.cumsum(sizes)
    starts = ends - sizes
    n_tiles = n_tokens // tm + N_CLASSES
    tile_start = jnp.arange(n_tiles, dtype=jnp.int32) * tm
    pos = route_rank
    tile_cls = jnp.zeros((n_tiles,), jnp.int32)
    for c in range(N_CLASSES):
        pos = pos + jnp.where(route_cls == c, starts[c], 0)
        tile_cls = tile_cls + (tile_start >= ends[c]).astype(jnp.int32)
    tile_cls = jnp.minimum(tile_cls, N_CLASSES - 1)
    tile_valid = (tile_start < ends[-1]).astype(jnp.int32)
    pairs = [(a, b_) for a in range(EXPERTS_PER_GROUP) for b_ in range(a + 1, EXPERTS_PER_GROUP)]
    grp, pair = tile_cls // PAIRS_PER_GROUP, tile_cls % PAIRS_PER_GROUP
    tile_ea = grp * EXPERTS_PER_GROUP
    tile_eb = grp * EXPERTS_PER_GROUP
    for k, (a, b_) in enumerate(pairs):
        tile_ea = tile_ea + jnp.where(pair == k, a, 0)
        tile_eb = tile_eb + jnp.where(pair == k, b_, 0)
    last_tile = jnp.where(sizes > 0, ends // tm - 1, -1)
    tail = ends[-1] // tm + jnp.arange(N_CLASSES, dtype=jnp.int32)
    zero_tiles = jnp.concatenate([last_tile, jnp.where(tail < n_tiles, tail, -1)]).astype(jnp.int32)
    return (pos.astype(jnp.int32), zero_tiles, tile_ea.astype(jnp.int32),
            tile_eb.astype(jnp.int32), tile_valid, n_tiles)


def _rope_tables(n):
    inv = (1.0 / (ROPE_BASE ** (np.arange(ROT_FREQS, dtype=np.float32) / ROT_FREQS))).astype(np.float32)
    pos = np.arange(n)
    row = (pos // GRID_W).astype(np.float32)[:, None] * inv[None, :]
    col = (pos % GRID_W).astype(np.float32)[:, None] * inv[None, :]
    cos64 = np.concatenate([np.cos(row), np.cos(row), np.cos(col), np.cos(col)], axis=1)
    sin64 = np.concatenate([-np.sin(row), np.sin(row), -np.sin(col), np.sin(col)], axis=1)
    tile = lambda a: np.tile(a.astype(np.float32), (1, LANES // HEAD_DIM))
    return jnp.asarray(tile(cos64)), jnp.asarray(tile(sin64))


def _dft_cos_sin(n):
    k = np.arange(n, dtype=np.int64)
    ang = (2.0 * np.pi / n) * ((k[:, None] * k[None, :]) % n).astype(np.float64)
    return np.cos(ang), np.sin(ang)


def _dft_tables(n):
    c_ch, s_ch = _dft_cos_sin(F_GROUP_DIM)
    c_seq, s_seq = _dft_cos_sin(n >> FNET_LEVELS)
    cs_ch = np.concatenate([c_ch, -s_ch], axis=1).astype(np.float32)
    cs_seq = np.concatenate([c_seq, s_seq], axis=1).astype(np.float32)
    twiddles = []
    for level in range(FNET_LEVELS):
        m = n >> level
        ang = (2.0 * np.pi / m) * np.arange(m // 2, dtype=np.float64)[:, None]
        for tab in (np.cos(ang), np.sin(ang)):
            twiddles.append(jnp.asarray(np.broadcast_to(tab, (m // 2, F_W)).astype(np.float32)))
    return jnp.asarray(cs_ch.astype(_BF)), jnp.asarray(cs_seq.astype(_BF)), twiddles


def kernel(x, c, ctx, c_ctx, w_mod, b_mod, norm1_g, norm2_g, w_in, lam_q1, lam_k1, lam_q2, lam_k2,
           subln_g, w_attn_out, w_four_out, w_out, w_router_group, b_router_group, w_router_expert,
           b_router_expert, w_exp_gate, w_exp_up, w_exp_down, final_g):
    b, n, d = x.shape
    assert w_mod.shape[0] == 1, "depth-1 stack"
    assert b + 1 <= MOD_ROWS

    cc = jnp.concatenate([c, c_ctx[None, :], jnp.zeros((MOD_ROWS - b - 1, d), _F32)], axis=0)
    mod3 = _modulation(cc, w_mod[0], b_mod).reshape(MOD_ROWS, 6, d)

    lam = (jnp.exp(jnp.sum(lam_q1[0] * lam_k1[0])) - jnp.exp(jnp.sum(lam_q2[0] * lam_k2[0]))
           + LAM_INIT).reshape(1).astype(_F32)

    w = w_in[0]
    scale = HEAD_DIM ** -0.5 * math.log2(math.e)
    w_lat = jnp.concatenate([w[:, REF_Q:REF_K] * scale, w[:, REF_K:REF_F],
                             w[:, REF_GA:REF_END], w[:, REF_F:REF_GA]], axis=1).astype(_BF)
    w_ctx = w[:, REF_K:REF_F].astype(_BF)
    lat_kinds = ("rope",) * ((P_V - P_Q) // PROJ_CW) + ("plain",) * ((P_GA - P_V) // PROJ_CW) \
        + ("sigmoid",) * ((P_F - P_GA) // PROJ_CW) + ("plain",) * ((P_W - P_F) // PROJ_CW)
    ctx_kinds = ("plain",) * (w_ctx.shape[1] // PROJ_CW)

    cos_t, sin_t = _rope_tables(n)
    p = _in_projection(x, mod3, lambda bi: bi, norm1_g, w_lat, cos_t, sin_t, lat_kinds, PROJ_TM)
    kvc = _in_projection(ctx, mod3, lambda bi: b, norm1_g, w_ctx, cos_t, sin_t, ctx_kinds,
                         ctx.shape[1])

    heads = _attention(lam, p, kvc, subln_g, ATTN_TQ)
    cs_ch, cs_seq, twiddles = _dft_tables(n)
    four = _fnet(p, cs_ch, cs_seq, twiddles)

    w_r = jnp.concatenate([w_router_expert[0], w_router_group[0],
                           jnp.zeros((d, ROUTER_W - N_EXPERTS - N_GROUPS), _F32)], axis=1)
    b_r = jnp.concatenate([b_router_expert[0], b_router_group[0],
                           jnp.zeros((ROUTER_W - N_EXPERTS - N_GROUPS,), _F32)])[None, :]
    wr_hi = w_r.astype(_BF)
    wr_lo = (w_r - wr_hi.astype(_F32)).astype(_BF)
    x1, rows, rec, counts = _merge(heads, four, p, x, mod3, norm2_g, w_attn_out[0].astype(_BF),
                                   w_four_out[0].astype(_BF), w_out[0].astype(_BF),
                                   jnp.concatenate([wr_hi, wr_lo], axis=1), b_r, MERGE_TM)

    t = b * n
    slabs = rows.reshape(t, SUBLANES, LANES)
    rec = rec.reshape(t // MERGE_TM, SUBLANES, MERGE_TM)
    pos, zero_tiles, tile_ea, tile_eb, tile_valid, n_tiles = _routing_tables(
        rec[:, 2, :].reshape(t).astype(jnp.int32), rec[:, 3, :].reshape(t).astype(jnp.int32),
        counts[0, :N_CLASSES].astype(jnp.int32), t, MOE_TM)

    slabs_sorted = _dispatch(pos, zero_tiles, slabs, n_tiles * MOE_TM, DISPATCH_TM, MOE_TM)
    y_sorted = _moe(tile_ea, tile_eb, tile_valid, slabs_sorted, w_exp_gate[0].astype(_BF),
                    w_exp_up[0].astype(_BF), w_exp_down[0].astype(_BF), MOE_TM)
    out = _combine(pos, y_sorted, x1.reshape(t, d), mod3, final_g[None, :], n, COMBINE_TM)
    return out.reshape(b, n, d)
```

```python
import functools
import math

import jax
import jax.numpy as jnp
import numpy as np
from jax import lax
from jax.experimental import pallas as pl
from jax.experimental.pallas import tpu as pltpu

D_MODEL = 1024
GRID_W = 64
EPS = 1e-6
N_HEADS = 8
HEAD_DIM = 64
HEAD_W = 2 * HEAD_DIM
ROT_FREQS = HEAD_DIM // 4
ROPE_BASE = 10000.0
F_GROUPS = 4
F_GROUP_DIM = 128
F_W = F_GROUPS * F_GROUP_DIM
N_GROUPS = 4
EXPERTS_PER_GROUP = 4
LOG2_EXPERTS_PER_GROUP = 2
N_EXPERTS = N_GROUPS * EXPERTS_PER_GROUP
PAIRS_PER_GROUP = EXPERTS_PER_GROUP * (EXPERTS_PER_GROUP - 1) // 2
N_CLASSES = N_GROUPS * PAIRS_PER_GROUP
EXPERT_HIDDEN = 512
LAM_INIT = 0.8 - 0.6 * math.exp(-0.3 * 0)

REF_Q, REF_K, REF_V, REF_F, REF_GA, REF_GF, REF_END = 0, 1024, 2048, 3072, 3584, 4608, 5632
P_Q, P_K, P_V, P_GA, P_GF, P_F, P_W = 0, 1024, 2048, 3072, 4096, 5120, 5632

LANES = 128
SUBLANES = 8
VMEM_LIMIT_BYTES = 56 * 1024 * 1024

PROJ_TM = 1024
PROJ_CW = 512
ATTN_TQ = 256
ATTN_SUM_ROWS = 16
ATTN_SCORE_BUFS = 3
ATTN_HEADS_PER_STEP = 2
FNET_LEVELS = 2
FNET_CHUNK = 256
MERGE_TM = 512
MOE_TM = 512
COMBINE_TM = 512
DISPATCH_TM = 4096
DMA_ISSUE_GROUP = 32
MOD_ROWS = 40
ROUTER_W = LANES
GROUP_LANE0 = N_EXPERTS
H2_SLAB_ROWS = D_MODEL // 2 // LANES
ROUTE_SLAB_ROW = H2_SLAB_ROWS

_BF = jnp.bfloat16
_F32 = jnp.float32


def _cparams(*sem):
    return pltpu.CompilerParams(dimension_semantics=sem, vmem_limit_bytes=VMEM_LIMIT_BYTES)


def _const_spec(shape):
    return pl.BlockSpec(shape, lambda *_: (0,) * len(shape), pipeline_mode=pl.Buffered(1))


def _rms(x):
    return x * lax.rsqrt(jnp.mean(x * x, axis=-1, keepdims=True) + EPS)


def _sigmoid(x):
    return 1.0 / (1.0 + jnp.exp(-x))


def _mod_kernel(cc_ref, w_ref, b_ref, o_ref):
    cc = cc_ref[...]
    s = cc * _sigmoid(cc)
    o_ref[...] = jnp.dot(s, w_ref[...], preferred_element_type=_F32,
                         precision=lax.Precision.HIGHEST) + b_ref[...]


def _modulation(cc, w_mod, b_mod):
    n = w_mod.shape[1]
    bn = D_MODEL
    return pl.pallas_call(
        _mod_kernel,
        grid=(n // bn,),
        in_specs=[pl.BlockSpec((MOD_ROWS, D_MODEL), lambda j: (0, 0)),
                  pl.BlockSpec((D_MODEL, bn), lambda j: (0, j)),
                  pl.BlockSpec((1, bn), lambda j: (0, j))],
        out_specs=pl.BlockSpec((MOD_ROWS, bn), lambda j: (0, j)),
        out_shape=jax.ShapeDtypeStruct((MOD_ROWS, n), _F32),
        compiler_params=_cparams("arbitrary"),
        name="modulation",
    )(cc, w_mod, b_mod)


def _rope(acc, cos_ref, sin_ref):
    cos = cos_ref[...]
    sin = sin_ref[...]
    lane = lax.broadcasted_iota(jnp.int32, (1, LANES), 1)
    first_half = (lane % (2 * ROT_FREQS)) < ROT_FREQS
    outs = []
    for s in range(acc.shape[1] // LANES):
        xs = acc[:, s * LANES:(s + 1) * LANES]
        partner = jnp.where(first_half,
                            pltpu.roll(xs, LANES - ROT_FREQS, 1),
                            pltpu.roll(xs, ROT_FREQS, 1))
        outs.append(xs * cos + partner * sin)
    return jnp.concatenate(outs, axis=1)


def _inproj_kernel(x_ref, mod_ref, g_ref, w_ref, cos_ref, sin_ref, o_ref, *, chunk_kinds):
    x = x_ref[0]
    shift = mod_ref[0, 0:1, :]
    scale = mod_ref[0, 1:2, :]
    h = (_rms(x) * g_ref[...]) * (1.0 + scale) + shift
    hb = h.astype(_BF)
    for j, kind in enumerate(chunk_kinds):
        cols = slice(j * PROJ_CW, (j + 1) * PROJ_CW)
        acc = jnp.dot(hb, w_ref[:, cols], preferred_element_type=_F32)
        if kind == "rope":
            acc = _rope(acc, cos_ref, sin_ref)
        elif kind == "sigmoid":
            acc = _sigmoid(acc)
        o_ref[0, :, cols] = acc.astype(_BF)


def _in_projection(x, mod3, mod_row_of_batch, g, w, cos_t, sin_t, chunk_kinds, tm):
    b, n, d = x.shape
    width = w.shape[1]
    assert width == len(chunk_kinds) * PROJ_CW and n % tm == 0
    return pl.pallas_call(
        functools.partial(_inproj_kernel, chunk_kinds=chunk_kinds),
        grid=(b, n // tm),
        in_specs=[pl.BlockSpec((1, tm, d), lambda bi, i: (bi, i, 0)),
                  pl.BlockSpec((1, 6, d), lambda bi, i: (mod_row_of_batch(bi), 0, 0)),
                  pl.BlockSpec((1, d), lambda bi, i: (0, 0)),
                  _const_spec((d, width)),
                  pl.BlockSpec((tm, LANES), lambda bi, i: (i, 0)),
                  pl.BlockSpec((tm, LANES), lambda bi, i: (i, 0))],
        out_specs=pl.BlockSpec((1, tm, width), lambda bi, i: (bi, i, 0)),
        out_shape=jax.ShapeDtypeStruct((b, n, width), _BF),
        compiler_params=_cparams("parallel", "arbitrary"),
        name="in_projection",
    )(x, mod3, g, w, cos_t, sin_t)


def _attn_kernel(lam_ref, q_ref, kl_ref, vl_ref, kc_ref, vc_ref, sg_ref, o_ref,
                 kcat_ref, vt_ref, *bufs, tq):
    nc, n = kc_ref.shape[1], kl_ref.shape[1]
    for hh in range(ATTN_HEADS_PER_STEP):
        cols = slice(hh * HEAD_W, (hh + 1) * HEAD_W)
        kcat_ref[hh, 0:nc, :] = kc_ref[0, :, cols]
        kcat_ref[hh, nc:nc + n, :] = kl_ref[0, :, cols]
        vt_ref[hh, 0:HEAD_W, 0:nc] = vc_ref[0, :, cols].astype(_F32).T.astype(_BF)
        vt_ref[hh, 0:HEAD_W, nc:nc + n] = vl_ref[0, :, cols].astype(_F32).T.astype(_BF)
        vt_ref[hh, HEAD_W:, :] = jnp.ones((ATTN_SUM_ROWS, nc + n), _BF)
    lane = lax.broadcasted_iota(jnp.int32, (1, HEAD_W), 1)
    map_lanes = (lane < HEAD_DIM, lane >= HEAD_DIM)
    nt = (((1,), (1,)), ((), ()))
    lam = lam_ref[0]
    post_scale = sg_ref[...] * (1.0 - LAM_INIT)
    n_sub = n // tq

    def scores(g):
        hh, j = divmod(g, n_sub)
        q = q_ref[0, j * tq:(j + 1) * tq, hh * HEAD_W:(hh + 1) * HEAD_W]
        for mp in range(2):
            qm = jnp.where(map_lanes[mp], q, jnp.zeros_like(q))
            bufs[g % ATTN_SCORE_BUFS][mp] = lax.dot_general(kcat_ref[hh], qm, nt,
                                                            preferred_element_type=_F32)

    def finish(g):
        hh, j = divmod(g, n_sub)
        outs = []
        for mp in range(2):
            s = bufs[g % ATTN_SCORE_BUFS][mp]
            e = jnp.exp2(s - jnp.max(s, axis=0, keepdims=True)).astype(_BF)
            r = jnp.dot(vt_ref[hh], e, preferred_element_type=_F32)
            outs.append(r[:HEAD_W, :] / r[HEAD_W:HEAD_W + 1, :])
        heads = (outs[0] - lam * outs[1]).T
        o_ref[0, j * tq:(j + 1) * tq, hh * HEAD_W:(hh + 1) * HEAD_W] = (
            _rms(heads) * post_scale).astype(_BF)

    total = ATTN_HEADS_PER_STEP * n_sub
    for g in range(ATTN_SCORE_BUFS - 1):
        scores(g)
    for g in range(total):
        if g + ATTN_SCORE_BUFS - 1 < total:
            scores(g + ATTN_SCORE_BUFS - 1)
        finish(g)


def _attention(lam, p, kvc, subln_g, tq):
    b, n, _ = p.shape
    nc = kvc.shape[1]
    w = ATTN_HEADS_PER_STEP * HEAD_W
    qb, kb, vb = P_Q // w, P_K // w, P_V // w
    seq = lambda blk: pl.BlockSpec((1, n, w), lambda bi, h: (bi, 0, blk + h))
    return pl.pallas_call(
        functools.partial(_attn_kernel, tq=tq),
        grid=(b, N_HEADS // ATTN_HEADS_PER_STEP),
        in_specs=[pl.BlockSpec(memory_space=pltpu.SMEM),
                  seq(qb), seq(kb), seq(vb),
                  pl.BlockSpec((1, nc, w), lambda bi, h: (bi, 0, h)),
                  pl.BlockSpec((1, nc, w), lambda bi, h: (bi, 0, N_HEADS // ATTN_HEADS_PER_STEP + h)),
                  pl.BlockSpec((1, HEAD_W), lambda bi, h: (0, 0))],
        out_specs=seq(0),
        out_shape=jax.ShapeDtypeStruct((b, n, N_HEADS * HEAD_W), _BF),
        scratch_shapes=[pltpu.VMEM((ATTN_HEADS_PER_STEP, nc + n, HEAD_W), _BF),
                        pltpu.VMEM((ATTN_HEADS_PER_STEP, HEAD_W + ATTN_SUM_ROWS, nc + n), _BF),
                        ] + [pltpu.VMEM((2, nc + n, tq), _F32)] * ATTN_SCORE_BUFS,
        compiler_params=_cparams("parallel", "arbitrary"),
        name="diff_attention",
    )(lam, p, p, p, kvc, kvc, subln_g)


def _bit_reverse(j, bits):
    return int(format(j, "0%db" % bits)[::-1], 2) if bits else 0


def _fnet_kernel(f_ref, cs_ch_ref, cs_seq_ref, *rest, n):
    tw_refs = rest[:2 * FNET_LEVELS]
    o_ref, zr_ref, zi_ref, rhs_ref = rest[2 * FNET_LEVELS:]
    for g in range(F_GROUPS):
        cols = slice(g * F_GROUP_DIM, (g + 1) * F_GROUP_DIM)
        t = jnp.dot(f_ref[0, :, cols], cs_ch_ref[...], preferred_element_type=_F32)
        zr_ref[:, cols] = t[:, :F_GROUP_DIM]
        zi_ref[:, cols] = t[:, F_GROUP_DIM:]
    for level in range(FNET_LEVELS):
        m = n >> level
        half = m // 2
        cos_ref, sin_ref = tw_refs[2 * level], tw_refs[2 * level + 1]
        for seg in range(1 << level):
            for c0 in range(0, half, FNET_CHUNK):
                rows_t = slice(seg * m + c0, seg * m + c0 + FNET_CHUNK)
                rows_b = slice(seg * m + half + c0, seg * m + half + c0 + FNET_CHUNK)
                tr, ti = zr_ref[rows_t, :], zi_ref[rows_t, :]
                br, bi = zr_ref[rows_b, :], zi_ref[rows_b, :]
                cw, sw = cos_ref[c0:c0 + FNET_CHUNK, :], sin_ref[c0:c0 + FNET_CHUNK, :]
                dr, di = tr - br, ti - bi
                zr_ref[rows_t, :] = tr + br
                zi_ref[rows_t, :] = ti + bi
                zr_ref[rows_b, :] = dr * cw + di * sw
                zi_ref[rows_b, :] = di * cw - dr * sw
    m = n >> FNET_LEVELS
    for j in range(1 << FNET_LEVELS):
        cols = slice(_bit_reverse(j, FNET_LEVELS) * F_W, (_bit_reverse(j, FNET_LEVELS) + 1) * F_W)
        rhs_ref[0:m, cols] = zr_ref[j * m:(j + 1) * m, :].astype(_BF)
        rhs_ref[m:2 * m, cols] = zi_ref[j * m:(j + 1) * m, :].astype(_BF)
    ortho = 1.0 / math.sqrt(n * F_GROUP_DIM)
    y = jnp.dot(cs_seq_ref[...], rhs_ref[...], preferred_element_type=_F32)
    o_ref[0] = (y * ortho).reshape(m, 1 << FNET_LEVELS, F_W).reshape(n, F_W).astype(_BF)


def _fnet(p, cs_ch, cs_seq, twiddles):
    b, n, _ = p.shape
    m = n >> FNET_LEVELS
    wide = F_W << FNET_LEVELS
    return pl.pallas_call(
        functools.partial(_fnet_kernel, n=n),
        grid=(b,),
        in_specs=[pl.BlockSpec((1, n, F_W), lambda bi: (bi, 0, P_F // F_W)),
                  _const_spec((F_GROUP_DIM, 2 * F_GROUP_DIM)),
                  _const_spec((m, 2 * m))] + [_const_spec(t.shape) for t in twiddles],
        out_specs=pl.BlockSpec((1, n, F_W), lambda bi: (bi, 0, 0)),
        out_shape=jax.ShapeDtypeStruct((b, n, F_W), _BF),
        scratch_shapes=[pltpu.VMEM((n, F_W), _F32), pltpu.VMEM((n, F_W), _F32),
                        pltpu.VMEM((2 * m, wide), _BF)],
        compiler_params=_cparams("arbitrary"),
        name="fnet_dft",
    )(p, cs_ch, cs_seq, *twiddles)


def _pack_bf16_pair(hi, lo):
    hi_bits = pltpu.bitcast(hi.astype(_BF).astype(_F32), jnp.uint32)
    lo_bits = pltpu.bitcast(lo.astype(_BF).astype(_F32), jnp.uint32)
    return hi_bits | lax.shift_right_logical(lo_bits, jnp.uint32(16))


def _unpack_bf16_pair(packed):
    hi = pltpu.bitcast(packed & jnp.uint32(0xFFFF0000), _F32)
    lo = pltpu.bitcast(lax.shift_left(packed, jnp.uint32(16)), _F32)
    return hi, lo


def _route(logits, carry):
    rows = logits.shape[0]
    lane = lax.broadcasted_iota(jnp.int32, logits.shape, 1)
    neg = jnp.float32(-jnp.inf)
    big = jnp.int32(ROUTER_W)

    def first_argmax(v):
        m = jnp.max(v, axis=-1, keepdims=True)
        idx = jnp.min(jnp.where(v == m, lane, big), axis=-1, keepdims=True)
        return m, idx

    lg = jnp.where((lane >= GROUP_LANE0) & (lane < GROUP_LANE0 + N_GROUPS), logits, neg)
    mg, ig = first_argmax(lg)
    w_grp = 1.0 / jnp.sum(jnp.exp(lg - mg), axis=-1, keepdims=True)
    g_sel = ig - GROUP_LANE0
    le = jnp.where((lane < N_EXPERTS)
                   & (jnp.right_shift(lane, LOG2_EXPERTS_PER_GROUP) == g_sel), logits, neg)
    v1, i1 = first_argmax(le)
    le2 = jnp.where(lane == i1, neg, le)
    v2, i2 = first_argmax(le2)
    e2 = jnp.exp(v2 - v1)
    w1 = w_grp / (1.0 + e2)
    w2 = w_grp * e2 / (1.0 + e2)

    first_is_lower = i1 < i2
    gate_a = jnp.where(first_is_lower, w1, w2)
    gate_b = jnp.where(first_is_lower, w2, w1)
    la = jnp.minimum(i1, i2) - g_sel * EXPERTS_PER_GROUP
    lb = jnp.maximum(i1, i2) - g_sel * EXPERTS_PER_GROUP
    pair = jnp.right_shift(la * (2 * EXPERTS_PER_GROUP - 1 - la), 1) + lb - la - 1
    cls = g_sel * PAIRS_PER_GROUP + pair

    onehot = (lane == cls).astype(_F32)
    r_i = lax.broadcasted_iota(jnp.int32, (rows, rows), 0)
    c_i = lax.broadcasted_iota(jnp.int32, (rows, rows), 1)
    earlier = (c_i < r_i).astype(_BF)
    before = jnp.dot(earlier, onehot.astype(_BF), preferred_element_type=_F32) + carry
    rank = jnp.sum(onehot * before, axis=-1, keepdims=True)
    new_carry = carry + jnp.sum(onehot, axis=0, keepdims=True)

    route = (jnp.where(lane == 0, gate_a, 0.0) + jnp.where(lane == 1, gate_b, 0.0)
             + jnp.where(lane == 2, cls.astype(_F32), 0.0) + jnp.where(lane == 3, rank, 0.0))
    return route, new_carry


def _merge_kernel(hd_ref, fo_ref, ga_ref, gf_ref, x_ref, mod_ref, g2n_ref, wao_ref, wfo_ref,
                  wo_ref, wrhl_ref, br_ref, x1_ref, rows_ref, rec_ref, counts_ref, carry_ref):
    @pl.when((pl.program_id(0) == 0) & (pl.program_id(1) == 0))
    def _():
        carry_ref[...] = jnp.zeros_like(carry_ref)

    a = jnp.dot(hd_ref[0], wao_ref[...], preferred_element_type=_F32)
    ff = jnp.dot(fo_ref[0], wfo_ref[...], preferred_element_type=_F32)
    y = ga_ref[0].astype(_F32) * a + gf_ref[0].astype(_F32) * ff
    mix = jnp.dot(y.astype(_BF), wo_ref[...], preferred_element_type=_F32)
    x1 = x_ref[0] + mod_ref[0, 2:3, :] * mix
    x1_ref[0] = x1
    h2 = (_rms(x1) * g2n_ref[...]) * (1.0 + mod_ref[0, 4:5, :]) + mod_ref[0, 3:4, :]
    h2_hi = h2.astype(_BF)
    h2_lo = (h2 - h2_hi.astype(_F32)).astype(_BF)
    hi_both = jnp.dot(h2_hi, wrhl_ref[...], preferred_element_type=_F32)
    logits = (hi_both[:, :ROUTER_W] + hi_both[:, ROUTER_W:]
              + jnp.dot(h2_lo, wrhl_ref[:, :ROUTER_W], preferred_element_type=_F32)) + br_ref[...]
    route, new_carry = _route(logits, carry_ref[...])
    carry_ref[...] = new_carry
    counts_ref[...] = new_carry
    rec_ref[0, 0] = route.T[0:SUBLANES, :]
    half = D_MODEL // 2
    tm = h2.shape[0]
    packed = _pack_bf16_pair(h2[:, :half], h2[:, half:])
    rows_ref[0, :, 0:H2_SLAB_ROWS, :] = packed.reshape(tm, H2_SLAB_ROWS, LANES)
    rows_ref[0, :, ROUTE_SLAB_ROW:ROUTE_SLAB_ROW + 1, :] = (
        pltpu.bitcast(route, jnp.uint32).reshape(tm, 1, LANES))
    rows_ref[0, :, ROUTE_SLAB_ROW + 1:SUBLANES, :] = jnp.zeros(
        (tm, SUBLANES - ROUTE_SLAB_ROW - 1, LANES), jnp.uint32)


def _merge(heads, four, p, x, mod3, norm2_g, w_ao, w_fo, w_o, wr_hilo, b_r, tm):
    b, n, d = x.shape
    tok = lambda w: pl.BlockSpec((1, tm, w), lambda bi, i: (bi, i, 0))
    return pl.pallas_call(
        _merge_kernel,
        grid=(b, n // tm),
        in_specs=[tok(d), tok(F_W),
                  pl.BlockSpec((1, tm, d), lambda bi, i: (bi, i, P_GA // D_MODEL)),
                  pl.BlockSpec((1, tm, d), lambda bi, i: (bi, i, P_GF // D_MODEL)),
                  tok(d),
                  pl.BlockSpec((1, 6, d), lambda bi, i: (bi, 0, 0)),
                  pl.BlockSpec((1, d), lambda bi, i: (0, 0)),
                  _const_spec((d, d)), _const_spec((F_W, d)), _const_spec((d, d)),
                  _const_spec((d, 2 * ROUTER_W)),
                  pl.BlockSpec((1, ROUTER_W), lambda bi, i: (0, 0))],
        out_specs=[tok(d),
                   pl.BlockSpec((1, tm, SUBLANES, LANES), lambda bi, i: (bi, i, 0, 0)),
                   pl.BlockSpec((1, 1, SUBLANES, tm), lambda bi, i: (bi, i, 0, 0)),
                   pl.BlockSpec((1, ROUTER_W), lambda bi, i: (0, 0))],
        out_shape=[jax.ShapeDtypeStruct((b, n, d), _F32),
                   jax.ShapeDtypeStruct((b, n, SUBLANES, LANES), jnp.uint32),
                   jax.ShapeDtypeStruct((b, n // tm, SUBLANES, tm), _F32),
                   jax.ShapeDtypeStruct((1, ROUTER_W), _F32)],
        scratch_shapes=[pltpu.VMEM((1, ROUTER_W), _F32)],
        compiler_params=_cparams("arbitrary", "arbitrary"),
        name="merge_router",
    )(heads, four, p, p, x, mod3, norm2_g, w_ao, w_fo, w_o, wr_hilo, b_r)


def _issue_slab_copies(n, make_copy, slot_of):
    def group(g, c):
        r0 = g * DMA_ISSUE_GROUP
        slots = [slot_of(r0 + k) for k in range(DMA_ISSUE_GROUP)]
        for k in range(DMA_ISSUE_GROUP):
            make_copy(r0 + k, slots[k]).start(priority=k % 2)
        return c

    lax.fori_loop(0, n // DMA_ISSUE_GROUP, group, 0)


def _dispatch_kernel(pos_ref, zero_tile_ref, src_ref, dst_ref, zero_ref, sem, pad_sem, *, tm):
    base = pl.program_id(0) * tm
    tile_rows = zero_ref.shape[0]

    @pl.when(pl.program_id(0) == 0)
    def _():
        zero_ref[...] = jnp.zeros_like(zero_ref)

        def fill(k):
            start = pl.multiple_of(zero_tile_ref[k] * tile_rows, tile_rows)
            return pltpu.make_async_copy(zero_ref, dst_ref.at[pl.ds(start, tile_rows)], pad_sem)

        for k in range(zero_tile_ref.shape[0]):
            @pl.when(zero_tile_ref[k] >= 0)
            def _(k=k):
                fill(k).start()
        for k in range(zero_tile_ref.shape[0]):
            @pl.when(zero_tile_ref[k] >= 0)
            def _(k=k):
                fill(k).wait()

    _issue_slab_copies(
        tm,
        lambda r, slot: pltpu.make_async_copy(src_ref.at[r], dst_ref.at[slot], sem),
        lambda r: pos_ref[base + r])
    pltpu.make_async_copy(src_ref, dst_ref.at[pl.ds(0, tm)], sem).wait()


def _dispatch(pos, zero_tiles, slabs, n_sorted, tm, tile_rows):
    t = slabs.shape[0]
    return pl.pallas_call(
        functools.partial(_dispatch_kernel, tm=tm),
        grid_spec=pltpu.PrefetchScalarGridSpec(
            num_scalar_prefetch=2,
            grid=(t // tm,),
            in_specs=[pl.BlockSpec((tm, SUBLANES, LANES), lambda i, *_: (i, 0, 0))],
            out_specs=pl.BlockSpec(memory_space=pl.ANY),
            scratch_shapes=[pltpu.VMEM((tile_rows, SUBLANES, LANES), slabs.dtype),
                            pltpu.SemaphoreType.DMA(()), pltpu.SemaphoreType.DMA(())]),
        out_shape=jax.ShapeDtypeStruct((n_sorted,) + slabs.shape[1:], slabs.dtype),
        compiler_params=_cparams("arbitrary"),
        name="moe_dispatch",
    )(pos, zero_tiles, slabs)


def _moe_kernel(ea_ref, eb_ref, valid_ref, slabs_ref, w1a_ref, w3a_ref, w2a_ref,
                w1b_ref, w3b_ref, w2b_ref, y_ref):
    del ea_ref, eb_ref
    i = pl.program_id(0)
    tm = slabs_ref.shape[0]

    @pl.when(valid_ref[i] != 0)
    def _():
        packed = slabs_ref[:, 0:H2_SLAB_ROWS, :].reshape(tm, H2_SLAB_ROWS * LANES)
        hi, lo = _unpack_bf16_pair(packed)
        t = jnp.concatenate([hi.astype(_BF), lo.astype(_BF)], axis=1)
        route = pltpu.bitcast(slabs_ref[:, ROUTE_SLAB_ROW, :], _F32)
        y = None
        for slot, (w1_ref, w3_ref, w2_ref) in enumerate(((w1a_ref, w3a_ref, w2a_ref),
                                                          (w1b_ref, w3b_ref, w2b_ref))):
            a = jnp.dot(t, w1_ref[0], preferred_element_type=_F32)
            u = jnp.dot(t, w3_ref[0], preferred_element_type=_F32)
            hid = ((a * _sigmoid(a)) * u * route[:, slot:slot + 1]).astype(_BF)
            part = jnp.dot(hid, w2_ref[0], preferred_element_type=_F32)
            y = part if y is None else y + part
        y_ref[...] = y.reshape(tm, SUBLANES, LANES)

    @pl.when(valid_ref[i] == 0)
    def _():
        y_ref[...] = jnp.zeros_like(y_ref)


def _moe(tile_ea, tile_eb, tile_valid, slabs_sorted, w1, w3, w2, tm):
    n_sorted = slabs_sorted.shape[0]
    d, hdn = w1.shape[1], w1.shape[2]
    assert d == SUBLANES * LANES
    pick = lambda which, a, b_: pl.BlockSpec(
        (1, a, b_), lambda i, ea, eb, valid: ((ea, eb)[which][i], 0, 0))
    slab_tile = pl.BlockSpec((tm, SUBLANES, LANES), lambda i, ea, eb, valid: (i, 0, 0))
    return pl.pallas_call(
        _moe_kernel,
        grid_spec=pltpu.PrefetchScalarGridSpec(
            num_scalar_prefetch=3,
            grid=(n_sorted // tm,),
            in_specs=[slab_tile,
                      pick(0, d, hdn), pick(0, d, hdn), pick(0, hdn, d),
                      pick(1, d, hdn), pick(1, d, hdn), pick(1, hdn, d)],
            out_specs=slab_tile),
        out_shape=jax.ShapeDtypeStruct((n_sorted, SUBLANES, LANES), _F32),
        compiler_params=_cparams("arbitrary"),
        name="moe_experts",
    )(tile_ea, tile_eb, tile_valid, slabs_sorted, w1, w3, w2, w1, w3, w2)


def _combine_kernel(pos_ref, y_hbm_ref, x1_ref, mod_ref, fg_ref, o_ref, ybuf_ref, sems, *, tm):
    i = pl.program_id(0)
    n_steps = pl.num_programs(0)

    def issue(tile, slot):
        _issue_slab_copies(
            tm,
            lambda r, src: pltpu.make_async_copy(y_hbm_ref.at[src], ybuf_ref.at[slot, r],
                                                 sems.at[slot]),
            lambda r: pos_ref[tile * tm + r])

    @pl.when(i == 0)
    def _():
        issue(0, 0)

    @pl.when(i + 1 < n_steps)
    def _():
        issue(i + 1, (i + 1) % 2)

    slot = i % 2
    pltpu.make_async_copy(y_hbm_ref.at[pl.ds(0, tm)], ybuf_ref.at[slot], sems.at[slot]).wait()
    y = ybuf_ref[slot].reshape(tm, SUBLANES * LANES)
    xo = x1_ref[...] + mod_ref[0, 5:6, :] * y
    o_ref[...] = _rms(xo) * fg_ref[...]


def _combine(pos, y_sorted, x1, mod3, final_g, seq, tm):
    t, d = x1.shape
    tiles_per_batch = seq // tm
    return pl.pallas_call(
        functools.partial(_combine_kernel, tm=tm),
        grid_spec=pltpu.PrefetchScalarGridSpec(
            num_scalar_prefetch=1,
            grid=(t // tm,),
            in_specs=[pl.BlockSpec(memory_space=pl.ANY),
                      pl.BlockSpec((tm, d), lambda i, pos: (i, 0)),
                      pl.BlockSpec((1, 6, d), lambda i, pos: (i // tiles_per_batch, 0, 0)),
                      pl.BlockSpec((1, d), lambda i, pos: (0, 0))],
            out_specs=pl.BlockSpec((tm, d), lambda i, pos: (i, 0)),
            scratch_shapes=[pltpu.VMEM((2, tm, SUBLANES, LANES), _F32),
                            pltpu.SemaphoreType.DMA((2,))]),
        out_shape=jax.ShapeDtypeStruct((t, d), _F32),
        compiler_params=_cparams("arbitrary"),
        name="moe_combine",
    )(pos, y_sorted, x1, mod3, final_g)


def _routing_tables(route_cls, route_rank, counts, n_tokens, tm):
    sizes = ((counts + (tm - 1)) // tm) * tm
    ends = jnp.cumsum(sizes)
    starts = ends - sizes
    n_tiles = n_tokens // tm + N_CLASSES
    tile_start = jnp.arange(n_tiles, dtype=jnp.int32) * tm
    pos = route_rank
    tile_cls = jnp.zeros((n_tiles,), jnp.int32)
    for c in range(N_CLASSES):
        pos = pos + jnp.where(route_cls == c, starts[c], 0)
        tile_cls = tile_cls + (tile_start >= ends[c]).astype(jnp.int32)
    tile_cls = jnp.minimum(tile_cls, N_CLASSES - 1)
    tile_valid = (tile_start < ends[-1]).astype(jnp.int32)
    pairs = [(a, b_) for a in range(EXPERTS_PER_GROUP) for b_ in range(a + 1, EXPERTS_PER_GROUP)]
    grp, pair = tile_cls // PAIRS_PER_GROUP, tile_cls % PAIRS_PER_GROUP
    tile_ea = grp * EXPERTS_PER_GROUP
    tile_eb = grp * EXPERTS_PER_GROUP
    for k, (a, b_) in enumerate(pairs):
        tile_ea = tile_ea + jnp.where(pair == k, a, 0)
        tile_eb = tile_eb + jnp.where(pair == k, b_, 0)
    last_tile = jnp.where(sizes > 0, ends // tm - 1, -1)
    tail = ends[-1] // tm + jnp.arange(N_CLASSES, dtype=jnp.int32)
    zero_tiles = jnp.concatenate([last_tile, jnp.where(tail < n_tiles, tail, -1)]).astype(jnp.int32)
    return (pos.astype(jnp.int32), zero_tiles, tile_ea.astype(jnp.int32),
            tile_eb.astype(jnp.int32), tile_valid, n_tiles)


def _rope_tables(n):
    inv = (1.0 / (ROPE_BASE ** (np.arange(ROT_FREQS, dtype=np.float32) / ROT_FREQS))).astype(np.float32)
    pos = np.arange(n)
    row = (pos // GRID_W).astype(np.float32)[:, None] * inv[None, :]
    col = (pos % GRID_W).astype(np.float32)[:, None] * inv[None, :]
    cos64 = np.concatenate([np.cos(row), np.cos(row), np.cos(col), np.cos(col)], axis=1)
    sin64 = np.concatenate([-np.sin(row), np.sin(row), -np.sin(col), np.sin(col)], axis=1)
    tile = lambda a: np.tile(a.astype(np.float32), (1, LANES // HEAD_DIM))
    return jnp.asarray(tile(cos64)), jnp.asarray(tile(sin64))


def _dft_cos_sin(n):
    k = np.arange(n, dtype=np.int64)
    ang = (2.0 * np.pi / n) * ((k[:, None] * k[None, :]) % n).astype(np.float64)
    return np.cos(ang), np.sin(ang)


def _dft_tables(n):
    c_ch, s_ch = _dft_cos_sin(F_GROUP_DIM)
    c_seq, s_seq = _dft_cos_sin(n >> FNET_LEVELS)
    cs_ch = np.concatenate([c_ch, -s_ch], axis=1).astype(np.float32)
    cs_seq = np.concatenate([c_seq, s_seq], axis=1).astype(np.float32)
    twiddles = []
    for level in range(FNET_LEVELS):
        m = n >> level
        ang = (2.0 * np.pi / m) * np.arange(m // 2, dtype=np.float64)[:, None]
        for tab in (np.cos(ang), np.sin(ang)):
            twiddles.append(jnp.asarray(np.broadcast_to(tab, (m // 2, F_W)).astype(np.float32)))
    return jnp.asarray(cs_ch.astype(_BF)), jnp.asarray(cs_seq.astype(_BF)), twiddles


def kernel(x, c, ctx, c_ctx, w_mod, b_mod, norm1_g, norm2_g, w_in, lam_q1, lam_k1, lam_q2, lam_k2,
           subln_g, w_attn_out, w_four_out, w_out, w_router_group, b_router_group, w_router_expert,
           b_router_expert, w_exp_gate, w_exp_up, w_exp_down, final_g):
    b, n, d = x.shape
    assert w_mod.shape[0] == 1, "depth-1 stack"
    assert b + 1 <= MOD_ROWS

    cc = jnp.concatenate([c, c_ctx[None, :], jnp.zeros((MOD_ROWS - b - 1, d), _F32)], axis=0)
    mod3 = _modulation(cc, w_mod[0], b_mod).reshape(MOD_ROWS, 6, d)

    lam = (jnp.exp(jnp.sum(lam_q1[0] * lam_k1[0])) - jnp.exp(jnp.sum(lam_q2[0] * lam_k2[0]))
           + LAM_INIT).reshape(1).astype(_F32)

    w = w_in[0]
    scale = HEAD_DIM ** -0.5 * math.log2(math.e)
    w_lat = jnp.concatenate([w[:, REF_Q:REF_K] * scale, w[:, REF_K:REF_F],
                             w[:, REF_GA:REF_END], w[:, REF_F:REF_GA]], axis=1).astype(_BF)
    w_ctx = w[:, REF_K:REF_F].astype(_BF)
    lat_kinds = ("rope",) * ((P_V - P_Q) // PROJ_CW) + ("plain",) * ((P_GA - P_V) // PROJ_CW) \
        + ("sigmoid",) * ((P_F - P_GA) // PROJ_CW) + ("plain",) * ((P_W - P_F) // PROJ_CW)
    ctx_kinds = ("plain",) * (w_ctx.shape[1] // PROJ_CW)

    cos_t, sin_t = _rope_tables(n)
    p = _in_projection(x, mod3, lambda bi: bi, norm1_g, w_lat, cos_t, sin_t, lat_kinds, PROJ_TM)
    kvc = _in_projection(ctx, mod3, lambda bi: b, norm1_g, w_ctx, cos_t, sin_t, ctx_kinds,
                         ctx.shape[1])

    heads = _attention(lam, p, kvc, subln_g, ATTN_TQ)
    cs_ch, cs_seq, twiddles = _dft_tables(n)
    four = _fnet(p, cs_ch, cs_seq, twiddles)

    w_r = jnp.concatenate([w_router_expert[0], w_router_group[0],
                           jnp.zeros((d, ROUTER_W - N_EXPERTS - N_GROUPS), _F32)], axis=1)
    b_r = jnp.concatenate([b_router_expert[0], b_router_group[0],
                           jnp.zeros((ROUTER_W - N_EXPERTS - N_GROUPS,), _F32)])[None, :]
    wr_hi = w_r.astype(_BF)
    wr_lo = (w_r - wr_hi.astype(_F32)).astype(_BF)
    x1, rows, rec, counts = _merge(heads, four, p, x, mod3, norm2_g, w_attn_out[0].astype(_BF),
                                   w_four_out[0].astype(_BF), w_out[0].astype(_BF),
                                   jnp.concatenate([wr_hi, wr_lo], axis=1), b_r, MERGE_TM)

    t = b * n
    slabs = rows.reshape(t, SUBLANES, LANES)
    rec = rec.reshape(t // MERGE_TM, SUBLANES, MERGE_TM)
    pos, zero_tiles, tile_ea, tile_eb, tile_valid, n_tiles = _routing_tables(
        rec[:, 2, :].reshape(t).astype(jnp.int32), rec[:, 3, :].reshape(t).astype(jnp.int32),
        counts[0, :N_CLASSES].astype(jnp.int32), t, MOE_TM)

    slabs_sorted = _dispatch(pos, zero_tiles, slabs, n_tiles * MOE_TM, DISPATCH_TM, MOE_TM)
    y_sorted = _moe(tile_ea, tile_eb, tile_valid, slabs_sorted, w_exp_gate[0].astype(_BF),
                    w_exp_up[0].astype(_BF), w_exp_down[0].astype(_BF), MOE_TM)
    out = _combine(pos, y_sorted, x1.reshape(t, d), mod3, final_g[None, :], n, COMBINE_TM)
    return out.reshape(b, n, d)
```
